```python
import math
import jax
import jax.numpy as jnp
from jax import lax
import numpy as np

D_MODEL = 1024
BATCH = 8
SEQ = 2048
DEPTH = 1
DEC_BATCH = 128
DEC_SEQ = 8
PAST_LEN = 8192
PAGE_SIZE = 128

SSM_HEADS = 8
SSM_HEAD_DIM = 64
SSM_WIDTH = SSM_HEADS * SSM_HEAD_DIM
SSM_GROUPS = 2
SSM_HPG = SSM_HEADS // SSM_GROUPS
D_STATE = 128
CONV_WIDTH = 4
CONV_DIM = SSM_WIDTH + 2 * SSM_GROUPS * D_STATE
SSD_CHUNK = 128
ATT_HEADS = 8
KV_HEADS = 2
GQA = ATT_HEADS // KV_HEADS
HEAD_DIM = 64
ATT_WIDTH = ATT_HEADS * HEAD_DIM
KV_COLS = KV_HEADS * 2 * HEAD_DIM
D_MIX = SSM_WIDTH + ATT_WIDTH
D_CMP = 16
L_CMP = 2 * D_CMP
CMP_HID = 64
L_SEL = 64
TOP_N = 16
WINDOW = 512
Q_BLOCK = 128
N_BRANCH = 3
FORCED_SCORE = 1e4
N_BUCKETS = 32
MAX_DISTANCE = 128
D_FF = 4 * D_MODEL
ALPHA = (2 * DEPTH) ** 0.25
BETA = (8 * DEPTH) ** -0.25
ATT_SCALE = HEAD_DIM ** -0.5
EPS = 1e-5
N_IN = SSM_WIDTH + CONV_DIM + SSM_HEADS + ATT_WIDTH + 3 * KV_COLS + N_BRANCH * ATT_HEADS

kernel_name = 'hymba_ssd_nsa_decode_step'


def layer_norm(x, g, b):
    xf = x.astype(jnp.float32)
    mu = jnp.mean(xf, -1, keepdims=True)
    var = jnp.mean(jnp.square(xf - mu), -1, keepdims=True)
    return ((xf - mu) * lax.rsqrt(var + EPS) * g + b).astype(x.dtype)


def rms_norm(x, g):
    xf = x.astype(jnp.float32)
    return xf * lax.rsqrt(jnp.mean(jnp.square(xf), -1, keepdims=True) + EPS) * g


def masked_softmax(s, mask, axis):
    s = jnp.where(mask, s, -jnp.inf)
    m = jnp.max(s, axis=axis, keepdims=True)
    e = jnp.exp(s - jnp.where(jnp.isfinite(m), m, 0.0))
    return e / jnp.maximum(jnp.sum(e, axis=axis, keepdims=True), 1e-30)


def t5_bucket(dist):
    d = jnp.maximum(dist, 0)
    exact = N_BUCKETS // 2
    far = exact + (jnp.log(jnp.maximum(d, 1).astype(jnp.float32) / exact)
                   / math.log(MAX_DISTANCE / exact) * (N_BUCKETS - exact)).astype(jnp.int32)
    return jnp.where(d < exact, d, jnp.minimum(far, N_BUCKETS - 1))


def rel_bias(dist, tbl):
    return tbl[t5_bucket(dist)].astype(jnp.float32).reshape(dist.shape + (KV_HEADS, GQA))


def project(x, w_in):
    B, T, _ = x.shape
    u = jnp.einsum('btd,de->bte', x, w_in)
    sizes = (SSM_WIDTH, CONV_DIM, SSM_HEADS, ATT_WIDTH, KV_COLS, KV_COLS, KV_COLS)
    z, xbc, dt, q, kv_c, kv_s, kv_w, gates = jnp.split(u, np.cumsum(sizes).tolist(), axis=-1)
    kv_shape = (B, T, KV_HEADS, 2, HEAD_DIM)
    return (z, xbc, dt, q.reshape(B, T, KV_HEADS, GQA, HEAD_DIM),
            kv_c.reshape(kv_shape), kv_s.reshape(kv_shape), kv_w.reshape(kv_shape),
            jax.nn.sigmoid(gates.reshape(B, T, N_BRANCH, KV_HEADS, GQA)))


def causal_conv(xbc, hist, conv_w, conv_b):
    xp = jnp.concatenate([hist.astype(xbc.dtype), xbc], axis=1)
    y = lax.conv_general_dilated(xp, conv_w[:, None, :].astype(xbc.dtype), (1,), 'VALID',
                                 dimension_numbers=('NWC', 'WIO', 'NWC'),
                                 feature_group_count=CONV_DIM)
    return jax.nn.silu(y + conv_b), xp[:, xp.shape[1] - (CONV_WIDTH - 1):]


def segsum(a):
    n = a.shape[-1]
    cs = jnp.cumsum(a, axis=-1)
    return jnp.where(jnp.tril(jnp.ones((n, n), bool)), cs[..., :, None] - cs[..., None, :], -jnp.inf)


def ssd(x, dt, A, bm, cm, h0):
    B, T = x.shape[:2]
    l = min(SSD_CHUNK, T)
    pad = (-T) % l
    if pad:
        x, dt, bm, cm = (jnp.pad(t, [(0, 0), (0, pad)] + [(0, 0)] * (t.ndim - 2)) for t in (x, dt, bm, cm))
    c = (T + pad) // l
    xd = (x * dt[..., None]).reshape(B, c, l, SSM_GROUPS, SSM_HPG, SSM_HEAD_DIM)
    a_cs = jnp.cumsum((dt * A).reshape(B, c, l, SSM_GROUPS, SSM_HPG), axis=2)
    bm = bm.reshape(B, c, l, SSM_GROUPS, D_STATE)
    cm = cm.reshape(B, c, l, SSM_GROUPS, D_STATE)
    causal = jnp.tril(jnp.ones((l, l), bool))[:, :, None, None]
    seg = jnp.where(causal, a_cs[:, :, :, None] - a_cs[:, :, None, :], -jnp.inf)
    cb = jnp.einsum('bclgn,bcsgn->bclsg', cm, bm)
    y_diag = jnp.einsum('bclsgr,bcsgrp->bclgrp', cb[..., None] * jnp.exp(seg), xd)
    states = jnp.einsum('bclgn,bclgrp->bcgrpn', bm, xd * jnp.exp(a_cs[:, :, -1:] - a_cs)[..., None])
    states = jnp.concatenate([h0[:, None], states], axis=1)
    tot = jnp.pad(jnp.moveaxis(a_cs[:, :, -1], 1, -1), [(0, 0), (0, 0), (0, 0), (1, 0)])
    states = jnp.einsum('bgrzc,bcgrpn->bzgrpn', jnp.exp(segsum(tot)), states)
    y_off = jnp.einsum('bclgn,bcgrpn->bclgrp', cm, states[:, :-1]) * jnp.exp(a_cs)[..., None]
    y = (y_diag + y_off).reshape(B, c * l, SSM_GROUPS, SSM_HPG, SSM_HEAD_DIM)[:, :T]
    return y, states[:, -1]


def ssm_branch(z, xbc, dt_raw, conv_hist, h0, conv_w, conv_b, dt_bias, a_log, d_skip, norm_g):
    B, T, _ = z.shape
    f32 = jnp.float32
    xbc, conv_new = causal_conv(xbc, conv_hist, conv_w, conv_b)
    xs, bm, cm = jnp.split(xbc, [SSM_WIDTH, SSM_WIDTH + SSM_GROUPS * D_STATE], axis=-1)
    xs = xs.reshape(B, T, SSM_GROUPS, SSM_HPG, SSM_HEAD_DIM).astype(f32)
    bm = bm.reshape(B, T, SSM_GROUPS, D_STATE).astype(f32)
    cm = cm.reshape(B, T, SSM_GROUPS, D_STATE).astype(f32)
    dt = jax.nn.softplus((dt_raw + dt_bias).astype(f32)).reshape(B, T, SSM_GROUPS, SSM_HPG)
    A = -jnp.exp(a_log.astype(f32)).reshape(SSM_GROUPS, SSM_HPG)
    h0 = h0.astype(f32).reshape(B, SSM_GROUPS, SSM_HPG, SSM_HEAD_DIM, D_STATE)
    y, h_final = ssd(xs, dt, A, bm, cm, h0)
    y = y + d_skip.astype(f32).reshape(SSM_GROUPS, SSM_HPG)[..., None] * xs
    y = y * jax.nn.silu(z.astype(f32)).reshape(B, T, SSM_GROUPS, SSM_HPG, SSM_HEAD_DIM)
    y = rms_norm(y.reshape(B, T, SSM_GROUPS, SSM_HPG * SSM_HEAD_DIM),
                 norm_g.reshape(SSM_GROUPS, SSM_HPG * SSM_HEAD_DIM))
    return (y.reshape(B, T, SSM_WIDTH), conv_new,
            h_final.reshape(B, SSM_HEADS, SSM_HEAD_DIM, D_STATE))


def dense_attend(q, k, v, dist, mask, tbl):
    s = jnp.einsum('...qhgd,...khd->...qkhg', q, k).astype(jnp.float32) * ATT_SCALE + rel_bias(dist, tbl)
    p = masked_softmax(s, mask[..., None, None], axis=-3)
    return jnp.einsum('...qkhg,...khd->...qhgd', p, v.astype(jnp.float32)), p


def cmp_hidden(kv, w1):
    B, T = kv.shape[:2]
    sub = kv.reshape(B, T // D_CMP, D_CMP, KV_HEADS, 2, HEAD_DIM)
    return jnp.einsum('bnjhed,ijedf->bnihef', sub, w1)


def cmp_finish(hid, b1, w2, b2):
    pre = hid[:, :-1, 0] + hid[:, 1:, 1] + b1
    return jnp.einsum('bnhef,efd->bnhed', jax.nn.gelu(pre), w2) + b2


def cmp_attention(q, kv_cmp, q_pos, tbl):
    n = kv_cmp.shape[1]
    end = jnp.arange(n) * D_CMP + (L_CMP - 1)
    dist = q_pos[:, None] - end[None, :]
    o, p = dense_attend(q, kv_cmp[..., 0, :], kv_cmp[..., 1, :], dist, dist >= 0, tbl)
    return o, p.sum(-1)


def select_blocks(p_grp, q_pos, n_slc):
    n_cmp = p_grp.shape[2]
    i = np.arange(n_cmp)[:, None]
    j = np.arange(n_slc)[None, :]
    overlap = ((i * D_CMP < (j + 1) * L_SEL) & (i * D_CMP + L_CMP > j * L_SEL)).astype(np.float32)
    score = jnp.einsum('bqkh,kj->bhqj', p_grp, jnp.asarray(overlap, p_grp.dtype))
    jj = jnp.arange(n_slc)[None, :]
    cur = (q_pos // L_SEL)[:, None]
    visible = jj * L_SEL <= q_pos[:, None]
    forced = (jj == 0) | (jj == cur) | (jj == cur - 1)
    score = jnp.where(visible, jnp.where(forced, FORCED_SCORE, score), -1.0)
    top, idx = lax.top_k(score, min(TOP_N, n_slc))
    return idx, top >= 0.0


def sel_core(q, k, v, k_pos, valid, q_pos, tbl):
    dist = q_pos[None, None, :, None] - k_pos
    mask = valid & (dist >= 0)
    tbl_h = jnp.transpose(tbl.reshape(N_BUCKETS, KV_HEADS, GQA), (1, 0, 2))
    bias = tbl_h[jnp.arange(KV_HEADS)[None, :, None, None], t5_bucket(dist)].astype(jnp.float32)
    s = jnp.einsum('bqhgd,bhqnd->bhqng', q, k).astype(jnp.float32) * ATT_SCALE + bias
    p = masked_softmax(s, mask[..., None], axis=3)
    return jnp.einsum('bhqng,bhqnd->bqhgd', p, v.astype(jnp.float32))


def sel_prompt(q, kv_s, idx, valid, tbl):
    B, T = q.shape[:2]
    nb, n_slc, k = T // Q_BLOCK, T // L_SEL, idx.shape[-1]
    kvb = kv_s.reshape(B, n_slc, L_SEL, KV_HEADS, 2, HEAD_DIM).transpose(0, 3, 1, 2, 4, 5)
    b_i = jnp.arange(B)[:, None, None, None]
    h_i = jnp.arange(KV_HEADS)[None, :, None, None]
    offs = jnp.arange(L_SEL)

    def block(args):
        qi, ii, vi, pi = args
        g = kvb[b_i, h_i, ii].reshape(B, KV_HEADS, Q_BLOCK, k * L_SEL, 2, HEAD_DIM)
        k_pos = (ii[..., None] * L_SEL + offs).reshape(B, KV_HEADS, Q_BLOCK, k * L_SEL)
        return sel_core(qi, g[..., 0, :], g[..., 1, :], k_pos, jnp.repeat(vi, L_SEL, axis=-1), pi, tbl)

    xs = (q.reshape(B, nb, Q_BLOCK, KV_HEADS, GQA, HEAD_DIM).swapaxes(0, 1),
          idx.reshape(B, KV_HEADS, nb, Q_BLOCK, k).transpose(2, 0, 1, 3, 4),
          valid.reshape(B, KV_HEADS, nb, Q_BLOCK, k).transpose(2, 0, 1, 3, 4),
          jnp.arange(T).reshape(nb, Q_BLOCK))
    o = lax.map(block, xs)
    return o.swapaxes(0, 1).reshape(B, T, KV_HEADS, GQA, HEAD_DIM)


def sel_sample(q, kv_new, pool, page_table, idx, valid, q_pos, tbl):
    B, T = q.shape[:2]
    past = page_table.shape[1] * PAGE_SIZE
    n_rows = pool.shape[0] * PAGE_SIZE
    k = idx.shape[-1]
    src = jnp.concatenate([pool.reshape(n_rows, KV_HEADS, 2, HEAD_DIM),
                           kv_new.reshape(B * T, KV_HEADS, 2, HEAD_DIM).astype(pool.dtype)], axis=0)
    pos = (idx[..., None] * L_SEL + jnp.arange(L_SEL)).reshape(B, KV_HEADS, T, k * L_SEL)
    b_i = jnp.arange(B)[:, None, None, None]
    pc = jnp.minimum(pos, past - 1)
    row_past = page_table[b_i, pc // PAGE_SIZE] * PAGE_SIZE + pc % PAGE_SIZE
    row_new = n_rows + b_i * T + jnp.clip(pos - past, 0, T - 1)
    rows = jnp.where(pos < past, row_past, row_new)
    g = src[rows, jnp.arange(KV_HEADS)[None, :, None, None]]
    return sel_core(q, g[..., 0, :], g[..., 1, :], pos, jnp.repeat(valid, L_SEL, axis=-1), q_pos, tbl)


def win_prompt(q, kv, tbl):
    B, T = q.shape[:2]
    nb, nlead = T // Q_BLOCK, WINDOW // Q_BLOCK
    nk = (nlead + 1) * Q_BLOCK
    pad = jnp.zeros((B, WINDOW) + kv.shape[2:], kv.dtype)
    kvb = jnp.concatenate([pad, kv], axis=1).reshape(B, nb + nlead, Q_BLOCK, KV_HEADS, 2, HEAD_DIM)
    band = jnp.concatenate([kvb[:, i:i + nb] for i in range(nlead + 1)], axis=2)
    qb = q.reshape(B, nb, Q_BLOCK, KV_HEADS, GQA, HEAD_DIM)
    q_pos = jnp.arange(T).reshape(nb, Q_BLOCK)
    k_pos = jnp.arange(nb)[:, None] * Q_BLOCK - WINDOW + jnp.arange(nk)[None, :]
    dist = q_pos[:, :, None] - k_pos[:, None, :]
    mask = (k_pos[:, None, :] >= 0) & (dist >= 0) & (dist < WINDOW)
    o, _ = dense_attend(qb, band[..., 0, :], band[..., 1, :], dist, mask, tbl)
    return o.reshape(B, T, KV_HEADS, GQA, HEAD_DIM)


def win_sample(q, kv_new, buf, q_pos, past, tbl):
    w, T = buf.shape[1], kv_new.shape[1]
    kv = jnp.concatenate([buf, kv_new.astype(buf.dtype)], axis=1)
    k_pos = jnp.concatenate([past - w + jnp.arange(w), past + jnp.arange(T)])
    dist = q_pos[:, None] - k_pos[None, :]
    o, _ = dense_attend(q, kv[..., 0, :], kv[..., 1, :], dist, (dist >= 0) & (dist < WINDOW), tbl)
    return o, kv[:, kv.shape[1] - w:]


def combine(y_ssm, o_c, o_s, o_w, gates, att_g, w_out):
    B, T = y_ssm.shape[:2]
    o = gates[:, :, 0, ..., None] * o_c + gates[:, :, 1, ..., None] * o_s + gates[:, :, 2, ..., None] * o_w
    o = rms_norm(o.reshape(B, T, ATT_WIDTH), att_g)
    return jnp.concatenate([y_ssm, o], axis=-1).astype(w_out.dtype) @ w_out


def mix_prompt(x, tbl, lw):
    (w_in, conv_w, conv_b, dt_bias, a_log, d_skip, ssm_g, cw1, cb1, cw2, cb2, att_g, w_out) = lw
    B, T, _ = x.shape
    z, xbc, dt, q, kv_c, kv_s, kv_w, gates = project(x, w_in)
    conv_hist = jnp.zeros((B, CONV_WIDTH - 1, CONV_DIM), x.dtype)
    h0 = jnp.zeros((B, SSM_HEADS, SSM_HEAD_DIM, D_STATE), jnp.float32)
    y_ssm, conv_new, h_new = ssm_branch(z, xbc, dt, conv_hist, h0, conv_w, conv_b, dt_bias, a_log, d_skip, ssm_g)
    q_pos = jnp.arange(T)
    o_c, p_c = cmp_attention(q, cmp_finish(cmp_hidden(kv_c, cw1), cb1, cw2, cb2), q_pos, tbl)
    idx, valid = select_blocks(p_c, q_pos, T // L_SEL)
    o_s = sel_prompt(q, kv_s, idx, valid, tbl)
    o_w = win_prompt(q, kv_w, tbl)
    y = combine(y_ssm, o_c, o_s, o_w, gates, att_g, w_out)
    w = min(WINDOW, T)
    return y, (kv_c, kv_s, kv_w[:, T - w:], conv_new, h_new)


def mix_sample(x, cache_cmp, cache_slc, cache_win, st_conv, st_ssm, page_table, tbl, lw):
    (w_in, conv_w, conv_b, dt_bias, a_log, d_skip, ssm_g, cw1, cb1, cw2, cb2, att_g, w_out) = lw
    B, T, _ = x.shape
    past = page_table.shape[1] * PAGE_SIZE
    z, xbc, dt, q, kv_c, kv_s, kv_w, gates = project(x, w_in)
    y_ssm, conv_new, h_new = ssm_branch(z, xbc, dt, st_conv, st_ssm, conv_w, conv_b, dt_bias, a_log, d_skip, ssm_g)
    q_pos = past + jnp.arange(T)
    kv_past = cache_cmp[page_table].reshape(B, past, KV_HEADS, 2, HEAD_DIM)
    n_new = (T // D_CMP) * D_CMP
    hid = jnp.concatenate([cmp_hidden(kv_past, cw1), cmp_hidden(kv_c[:, :n_new].astype(kv_past.dtype), cw1)], axis=1)
    o_c, p_c = cmp_attention(q, cmp_finish(hid, cb1, cw2, cb2), q_pos, tbl)
    idx, valid = select_blocks(p_c, q_pos, -(-(past + T) // L_SEL))
    o_s = sel_sample(q, kv_s, cache_slc, page_table, idx, valid, q_pos, tbl)
    o_w, win_new = win_sample(q, kv_w, cache_win, q_pos, past, tbl)
    y = combine(y_ssm, o_c, o_s, o_w, gates, att_g, w_out)
    return y, (kv_c, kv_s, win_new, conv_new, h_new)


def post_norm_block(x, mix, ln1_g, ln1_b, w_up, w_down, ln2_g, ln2_b):
    h = layer_norm(ALPHA * x + mix.astype(x.dtype), ln1_g, ln1_b)
    f = jnp.square(jax.nn.relu(h @ w_up)) @ w_down
    return layer_norm(ALPHA * h + f.astype(h.dtype), ln2_g, ln2_b)


def setup_inputs(seed: int = 0) -> dict:
    key = jax.random.key(seed)
    ks = iter(jax.random.split(key, 40))
    f32 = jnp.float32

    def nrm(shape, scale=1.0):
        return jax.random.normal(next(ks), shape, f32) * scale

    n_pages = PAST_LEN // PAGE_SIZE
    n_used = DEC_BATCH * n_pages
    n_phys = n_used + n_used // 4
    w_buf = min(WINDOW, PAST_LEN)
    page_table = jax.random.permutation(next(ks), n_phys)[:n_used].reshape(DEC_BATCH, n_pages).astype(jnp.int32)
    dt0 = jnp.exp(jax.random.uniform(next(ks), (DEPTH, SSM_HEADS), f32, math.log(1e-3), math.log(1e-1)))
    dt_bias = dt0 + jnp.log(-jnp.expm1(-dt0))
    a_log = jnp.log(jax.random.uniform(next(ks), (DEPTH, SSM_HEADS), f32, 1.0, 16.0))
    return {
        'x_prompt': nrm((BATCH, SEQ, D_MODEL)),
        'x_sample': nrm((DEC_BATCH, DEC_SEQ, D_MODEL)),
        'cache_cmp_kv': nrm((DEPTH, n_phys, PAGE_SIZE, KV_HEADS, 2, HEAD_DIM)),
        'cache_slc_kv': nrm((DEPTH, n_phys, PAGE_SIZE, KV_HEADS, 2, HEAD_DIM)),
        'cache_win_kv': nrm((DEPTH, DEC_BATCH, w_buf, KV_HEADS, 2, HEAD_DIM)),
        'state_conv': nrm((DEPTH, DEC_BATCH, CONV_WIDTH - 1, CONV_DIM)),
        'state_ssm': nrm((DEPTH, DEC_BATCH, SSM_HEADS, SSM_HEAD_DIM, D_STATE), 0.5),
        'page_table': page_table,
        'rel_bias_table': nrm((N_BUCKETS, ATT_HEADS), 0.3),
        'emb_ln_g': 1.0 + nrm((D_MODEL,), 0.02),
        'emb_ln_b': nrm((D_MODEL,), 0.02),
        'w_in': nrm((DEPTH, D_MODEL, N_IN), D_MODEL ** -0.5),
        'conv_w': nrm((DEPTH, CONV_WIDTH, CONV_DIM), CONV_WIDTH ** -0.5),
        'conv_b': nrm((DEPTH, CONV_DIM), 0.02),
        'dt_bias': dt_bias,
        'a_log': a_log,
        'd_skip': 1.0 + nrm((DEPTH, SSM_HEADS), 0.02),
        'ssm_norm_g': 1.0 + nrm((DEPTH, SSM_WIDTH), 0.02),
        'cmp_w1': nrm((DEPTH, 2, D_CMP, 2, HEAD_DIM, CMP_HID), (L_CMP * HEAD_DIM) ** -0.5),
        'cmp_b1': nrm((DEPTH, 2, CMP_HID), 0.02),
        'cmp_w2': nrm((DEPTH, 2, CMP_HID, HEAD_DIM), CMP_HID ** -0.5),
        'cmp_b2': nrm((DEPTH, 2, HEAD_DIM), 0.02),
        'att_norm_g': 1.0 + nrm((DEPTH, ATT_WIDTH), 0.02),
        'w_out': nrm((DEPTH, D_MIX, D_MODEL), D_MIX ** -0.5 * BETA),
        'ln1_g': 1.0 + nrm((DEPTH, D_MODEL), 0.02),
        'ln1_b': nrm((DEPTH, D_MODEL), 0.02),
        'w_up': nrm((DEPTH, D_MODEL, D_FF), D_MODEL ** -0.5),
        'w_down': nrm((DEPTH, D_FF, D_MODEL), D_FF ** -0.5 * BETA),
        'ln2_g': 1.0 + nrm((DEPTH, D_MODEL), 0.02),
        'ln2_b': nrm((DEPTH, D_MODEL), 0.02),
    }


def reference(x_prompt, x_sample, cache_cmp_kv, cache_slc_kv, cache_win_kv, state_conv, state_ssm,
              page_table, rel_bias_table, emb_ln_g, emb_ln_b, w_in, conv_w, conv_b, dt_bias, a_log,
              d_skip, ssm_norm_g, cmp_w1, cmp_b1, cmp_w2, cmp_b2, att_norm_g, w_out, ln1_g, ln1_b,
              w_up, w_down, ln2_g, ln2_b):
    xp = layer_norm(x_prompt, emb_ln_g, emb_ln_b)
    xs = layer_norm(x_sample, emb_ln_g, emb_ln_b)
    st_p, st_s = [], []
    for l in range(DEPTH):
        lw = (w_in[l], conv_w[l], conv_b[l], dt_bias[l], a_log[l], d_skip[l], ssm_norm_g[l],
              cmp_w1[l], cmp_b1[l], cmp_w2[l], cmp_b2[l], att_norm_g[l], w_out[l])
        mp, sp = mix_prompt(xp, rel_bias_table, lw)
        ms, ss = mix_sample(xs, cache_cmp_kv[l], cache_slc_kv[l], cache_win_kv[l], state_conv[l],
                            state_ssm[l], page_table, rel_bias_table, lw)
        xp = post_norm_block(xp, mp, ln1_g[l], ln1_b[l], w_up[l], w_down[l], ln2_g[l], ln2_b[l])
        xs = post_norm_block(xs, ms, ln1_g[l], ln1_b[l], w_up[l], w_down[l], ln2_g[l], ln2_b[l])
        st_p.append(sp)
        st_s.append(ss)
    cmp_kv_prompt = jnp.stack([s[0] for s in st_p])
    slc_kv_prompt = jnp.stack([s[1] for s in st_p])
    win_kv_prompt = jnp.stack([s[2] for s in st_p])
    conv_prompt = jnp.stack([s[3] for s in st_p])
    ssm_prompt = jnp.stack([s[4] for s in st_p])
    cmp_kv_sample = jnp.stack([s[0] for s in st_s])
    slc_kv_sample = jnp.stack([s[1] for s in st_s])
    win_kv_sample = jnp.stack([s[2] for s in st_s])
    conv_sample = jnp.stack([s[3] for s in st_s])
    ssm_sample = jnp.stack([s[4] for s in st_s])
    return (xp, xs, cmp_kv_prompt, slc_kv_prompt, win_kv_prompt, conv_prompt, ssm_prompt,
            cmp_kv_sample, slc_kv_sample, win_kv_sample, conv_sample, ssm_sample)
```

```python
import functools
import math

import numpy as np
import jax
import jax.numpy as jnp
from jax import lax
from jax.experimental import pallas as pl
from jax.experimental.pallas import tpu as pltpu

F32 = jnp.float32
BF16 = jnp.bfloat16
HIGHEST = lax.Precision.HIGHEST

D_MODEL = 1024
SSM_HEADS = 8
SSM_HEAD_DIM = 64
SSM_WIDTH = SSM_HEADS * SSM_HEAD_DIM
SSM_GROUPS = 2
D_STATE = 128
CONV_WIDTH = 4
CONV_DIM = SSM_WIDTH + 2 * SSM_GROUPS * D_STATE
SSD_CHUNK = 128
ATT_HEADS = 8
KV_HEADS = 2
GQA = ATT_HEADS // KV_HEADS
HEAD_DIM = 64
KV_COLS = KV_HEADS * 2 * HEAD_DIM
D_CMP = 16
L_CMP = 2 * D_CMP
CMP_HID = 64
L_SEL = 64
TOP_N = 16
WINDOW = 512
Q_BLOCK = 128
N_BRANCH = 3
FORCED_SCORE = 1e4
N_BUCKETS = 32
MAX_DISTANCE = 128
D_FF = 4 * D_MODEL
DEPTH = 1
ALPHA = (2 * DEPTH) ** 0.25
ATT_SCALE = HEAD_DIM ** -0.5
EPS = 1e-5
PAGE_SIZE = 128

LANES = 128
Q_PAD = ATT_HEADS * LANES
NEG = -1e30
MASKED_BELOW = -1e29
VMEM_LIMIT = 48 * 1024 * 1024

_OFF_Z, _OFF_XBC, _OFF_Q = 0, SSM_WIDTH, SSM_WIDTH + CONV_DIM
_OFF_KVC = _OFF_Q + Q_PAD
_OFF_KVS = _OFF_KVC + KV_COLS
_OFF_KVW = _OFF_KVS + KV_COLS
_OFF_SM = _OFF_KVW + KV_COLS
_N_PROJ = _OFF_SM + LANES


def _cparams(sem):
    return pltpu.CompilerParams(dimension_semantics=sem, vmem_limit_bytes=VMEM_LIMIT)


def _row_tile(n, preferred):
    tm = min(n, preferred)
    assert n % tm == 0 and tm % 8 == 0
    return tm


def _dot(a, b, precision=None):
    return jnp.dot(a, b, preferred_element_type=F32, precision=precision)


def _dot_nt(a, b):
    return lax.dot_general(a, b, (((1,), (1,)), ((), ())), preferred_element_type=F32)


def _layer_norm(x, g, b):
    mu = jnp.mean(x, -1, keepdims=True)
    xc = x - mu
    var = jnp.mean(xc * xc, -1, keepdims=True)
    return xc * lax.rsqrt(var + EPS) * g + b


def _sigmoid(x):
    return 1.0 / (1.0 + jnp.exp(-x))


def _softplus(x):
    return jnp.maximum(x, 0.0) + jnp.log(1.0 + jnp.exp(-jnp.abs(x)))


def _gelu_tanh(x):
    c = math.sqrt(2.0 / math.pi)
    return 0.5 * x * (1.0 + jnp.tanh(c * (x + 0.044715 * (x * x * x))))


def _proj_kernel(x_ref, g_ref, b_ref, w_ref, z_ref, xbc_ref, q_ref, kvc_ref, kvs_ref, kvw_ref, sm_ref):
    xn = _layer_norm(x_ref[...], g_ref[...], b_ref[...]).astype(BF16)

    def mm(lo, hi):
        return _dot(xn, w_ref[:, lo:hi])

    z_ref[...] = mm(_OFF_Z, _OFF_XBC)
    xbc_ref[...] = mm(_OFF_XBC, _OFF_Q)
    q_ref[...] = mm(_OFF_Q, _OFF_KVC).astype(q_ref.dtype)
    kvc_ref[...] = mm(_OFF_KVC, _OFF_KVS)
    kvs_ref[...] = mm(_OFF_KVS, _OFF_KVW)
    kvw_ref[...] = mm(_OFF_KVW, _OFF_SM)
    sm_ref[...] = mm(_OFF_SM, _N_PROJ)


def _proj(x2d, g, b, w, q_dtype, tm):
    n = x2d.shape[0]
    tm = _row_tile(n, tm)
    widths = [(SSM_WIDTH, F32), (CONV_DIM, F32), (Q_PAD, q_dtype), (KV_COLS, F32), (KV_COLS, F32),
              (KV_COLS, F32), (LANES, F32)]
    row = lambda i: (i, 0)
    fixed = lambda i: (0, 0)
    return pl.pallas_call(
        _proj_kernel,
        grid=(n // tm,),
        in_specs=[pl.BlockSpec((tm, D_MODEL), row), pl.BlockSpec((1, D_MODEL), fixed),
                  pl.BlockSpec((1, D_MODEL), fixed), pl.BlockSpec((D_MODEL, _N_PROJ), fixed)],
        out_specs=[pl.BlockSpec((tm, wd), row) for wd, _ in widths],
        out_shape=[jax.ShapeDtypeStruct((n, wd), dt) for wd, dt in widths],
        compiler_params=_cparams(("parallel",)),
        name="proj",
    )(x2d, g, b, w)


def _ssm_kernel(z_ref, xbc_ref, sm_ref, hist_ref, h0_ref, cw_ref, cb_ref, dtb_ref, alog_ref, dskip_ref,
                ng_ref, e_ref, et_ref, y_ref, hfin_ref, xext, state, *, tb, l, nc):
    c = pl.program_id(1)

    @pl.when(c == 0)
    def _():
        xext[0:8, :] = jnp.zeros((8, CONV_DIM), F32)
        xext[8 - (CONV_WIDTH - 1):8, :] = hist_ref[0]
        if tb < l:
            xext[8 + tb:8 + l, :] = jnp.zeros((l - tb, CONV_DIM), F32)
        state[...] = h0_ref[0].reshape(SSM_WIDTH, D_STATE)

    xext[8:8 + tb, :] = xbc_ref[0]
    conv = cb_ref[...]
    for k in range(CONV_WIDTH):
        lo = 8 - (CONV_WIDTH - 1) + k
        conv = conv + cw_ref[k:k + 1, :] * xext[lo:lo + l, :]
    xc = conv * _sigmoid(conv)
    xext[0:8, :] = xext[tb:tb + 8, :]

    dt = _softplus(sm_ref[0] + dtb_ref[...])
    if tb < l:
        dt = jnp.concatenate([dt, jnp.zeros((l - tb, LANES), F32)], axis=0)
    a = dt * (-jnp.exp(alog_ref[...]))
    ri = lax.broadcasted_iota(jnp.int32, (l, l), 0)
    ci = lax.broadcasted_iota(jnp.int32, (l, l), 1)
    tril = ri >= ci
    a_cs = _dot(jnp.where(tril, 1.0, 0.0), a, HIGHEST)
    a_cs_t = a_cs.T
    e = e_ref[...]
    dtx = _dot(dt, e, HIGHEST)
    eax = _dot(jnp.exp(a_cs[:tb]), e, HIGHEST)
    decx = _dot(jnp.exp(a_cs[l - 1:l, :] - a_cs), e, HIGHEST)
    tot = jnp.broadcast_to(jnp.exp(a_cs_t[:, l - 1:l]), (LANES, LANES))
    rtot = _dot(et_ref[...], tot, HIGHEST)

    xs = xc[:, :SSM_WIDTH]
    xd = xs * dtx
    xw = xd * decx
    lane = lax.broadcasted_iota(jnp.int32, (tb, LANES), 1)
    tril_q = tril[:tb]
    y_pairs = []
    for g in range(SSM_GROUPS):
        bg = xc[:, SSM_WIDTH + g * D_STATE:SSM_WIDTH + (g + 1) * D_STATE].astype(BF16)
        c_lo = SSM_WIDTH + SSM_GROUPS * D_STATE + g * D_STATE
        cg = xc[:tb, c_lo:c_lo + D_STATE].astype(BF16)
        cb = _dot_nt(cg, bg)
        for k in range(2):
            pair = 2 * g + k
            lo = pair * LANES
            xd_pair = xd[:, lo:lo + LANES].astype(BF16)
            ys = []
            for r2 in range(2):
                h = 2 * pair + r2
                seg = a_cs[:tb, h:h + 1] - a_cs_t[h:h + 1, :]
                lm = jnp.where(tril_q, jnp.exp(jnp.where(tril_q, seg, 0.0)), 0.0)
                ys.append(_dot((cb * lm).astype(BF16), xd_pair))
            y_diag = jnp.where(lane < SSM_HEAD_DIM, ys[0], ys[1])
            sp = state[lo:lo + LANES, :]
            y_off = _dot_nt(cg, sp.astype(BF16)) * eax[:, lo:lo + LANES]
            y_pairs.append(y_diag + y_off)
            upd = _dot(xw[:, lo:lo + LANES].T.astype(BF16), bg)
            state[lo:lo + LANES, :] = sp * rtot[lo:lo + LANES, :] + upd
    y = jnp.concatenate(y_pairs, axis=1) + dskip_ref[...] * xs[:tb]
    zz = z_ref[0]
    y = y * (zz * _sigmoid(zz))
    gw = SSM_WIDTH // SSM_GROUPS
    outs = []
    for g in range(SSM_GROUPS):
        yg = y[:, g * gw:(g + 1) * gw]
        ms = jnp.mean(yg * yg, -1, keepdims=True)
        outs.append(yg * lax.rsqrt(ms + EPS) * ng_ref[:, g * gw:(g + 1) * gw])
    y_ref[0] = jnp.concatenate(outs, axis=1).astype(y_ref.dtype)

    @pl.when(c == nc - 1)
    def _():
        hfin_ref[0] = state[...].reshape(SSM_HEADS, SSM_HEAD_DIM, D_STATE)


def _ssm(z, xbc, sm, hist, h0, conv_w, conv_b, dt_bias, a_log, d_skip, norm_g):
    bsz, t, _ = z.shape
    l = SSD_CHUNK
    tb = min(l, t)
    assert t % tb == 0 and tb % 8 == 0 and t >= CONV_WIDTH - 1
    nc = t // tb
    pad8 = lambda v: jnp.pad(v.reshape(1, SSM_HEADS), ((0, 0), (0, LANES - SSM_HEADS)))
    expand = np.zeros((LANES, SSM_WIDTH), np.float32)
    for h in range(SSM_HEADS):
        expand[h, h * SSM_HEAD_DIM:(h + 1) * SSM_HEAD_DIM] = 1.0
    blk = lambda b, c: (b, c, 0)
    per_b3 = lambda b, c: (b, 0, 0)
    per_b4 = lambda b, c: (b, 0, 0, 0)
    fixed = lambda b, c: (0, 0)
    return pl.pallas_call(
        functools.partial(_ssm_kernel, tb=tb, l=l, nc=nc),
        grid=(bsz, nc),
        in_specs=[pl.BlockSpec((1, tb, SSM_WIDTH), blk), pl.BlockSpec((1, tb, CONV_DIM), blk),
                  pl.BlockSpec((1, tb, LANES), blk),
                  pl.BlockSpec((1, CONV_WIDTH - 1, CONV_DIM), per_b3),
                  pl.BlockSpec((1, SSM_HEADS, SSM_HEAD_DIM, D_STATE), per_b4),
                  pl.BlockSpec((CONV_WIDTH, CONV_DIM), fixed), pl.BlockSpec((1, CONV_DIM), fixed),
                  pl.BlockSpec((1, LANES), fixed), pl.BlockSpec((1, LANES), fixed),
                  pl.BlockSpec((1, SSM_WIDTH), fixed), pl.BlockSpec((1, SSM_WIDTH), fixed),
                  pl.BlockSpec((LANES, SSM_WIDTH), fixed), pl.BlockSpec((SSM_WIDTH, LANES), fixed)],
        out_specs=[pl.BlockSpec((1, tb, SSM_WIDTH), blk),
                   pl.BlockSpec((1, SSM_HEADS, SSM_HEAD_DIM, D_STATE), per_b4)],
        out_shape=[jax.ShapeDtypeStruct((bsz, t, SSM_WIDTH), BF16),
                   jax.ShapeDtypeStruct((bsz, SSM_HEADS, SSM_HEAD_DIM, D_STATE), F32)],
        scratch_shapes=[pltpu.VMEM((8 + l, CONV_DIM), F32), pltpu.VMEM((SSM_WIDTH, D_STATE), F32)],
        compiler_params=_cparams(("parallel", "arbitrary")),
        name="ssm",
    )(z, xbc, sm, hist, h0, conv_w, conv_b.reshape(1, CONV_DIM), pad8(dt_bias), pad8(a_log),
      jnp.repeat(d_skip, SSM_HEAD_DIM).reshape(1, SSM_WIDTH), norm_g.reshape(1, SSM_WIDTH),
      jnp.asarray(expand), jnp.asarray(expand.T))


def _compress(x, h, w1_ref, b1_ref, w2_ref, b2_ref):
    n_sub = x.shape[0]
    xh = jnp.concatenate([x[:, j * KV_COLS + h * LANES:j * KV_COLS + (h + 1) * LANES] for j in range(D_CMP)],
                         axis=1).astype(BF16)
    hid = _dot(xh, w1_ref[...])
    pre = hid[:, :LANES] + pltpu.roll(hid[:, LANES:], n_sub - 1, 0) + b1_ref[...]
    return _dot(_gelu_tanh(pre).astype(BF16), w2_ref[...]) + b2_ref[...]


def _select(score, qpos, n_slc):
    jj = lax.broadcasted_iota(jnp.int32, score.shape, 1)
    cur = qpos // L_SEL
    visible = jj * L_SEL <= qpos
    forced = (jj == 0) | (jj == cur) | (jj == cur - 1)
    sc = jnp.where(visible, jnp.where(forced, FORCED_SCORE, score), -1.0)
    sc = jnp.where(jj < n_slc, sc, -2.0)
    rank = jnp.zeros(score.shape, F32)
    for k in range(n_slc):
        ck = sc[:, k:k + 1]
        beats = (ck > sc) | ((ck == sc) & (jj > k))
        rank = rank + jnp.where(beats, 1.0, 0.0)
    return jnp.where((rank < min(TOP_N, n_slc)) & (sc >= 0.0), 1.0, 0.0)


def _softmax_rows(s):
    m = jnp.max(s, -1, keepdims=True)
    e = jnp.where(s > MASKED_BELOW, jnp.exp(s - m), 0.0)
    return e / jnp.maximum(jnp.sum(e, -1, keepdims=True), 1e-30)


def _flash_step(carry, s, kv):
    m, l, acc = carry
    m2 = jnp.maximum(m, jnp.max(s, -1, keepdims=True))
    e = jnp.where(s > MASKED_BELOW, jnp.exp(s - m2), 0.0)
    alpha = jnp.exp(m - m2)
    return (m2, alpha * l + jnp.sum(e, -1, keepdims=True),
            alpha * acc + _dot(e.astype(BF16), kv))


def _flash_init(rows):
    return (jnp.full((rows, 1), NEG, F32), jnp.zeros((rows, 1), F32), jnp.zeros((rows, LANES), F32))


def _flash_out(carry):
    _, l, acc = carry
    return acc / jnp.maximum(l, 1e-30)


def _cmp_prompt_kernel(kvc_ref, q_ref, bias_ref, w1_ref, b1_ref, w2_ref, b2_ref, ov_ref, o_ref, sel_ref,
                       kvcmp, *, n_slc):
    qb = pl.program_id(1)

    @pl.when(qb == 0)
    def _():
        for h in range(KV_HEADS):
            kvcmp[h] = _compress(kvc_ref.at[0], h, w1_ref, b1_ref, w2_ref, b2_ref).astype(BF16)

    qpos = qb * Q_BLOCK + lax.broadcasted_iota(jnp.int32, (Q_BLOCK, 1), 0)
    for h in range(KV_HEADS):
        kv = kvcmp[h]
        pg = jnp.zeros((Q_BLOCK, kv.shape[0]), F32)
        for g in range(GQA):
            hd = h * GQA + g
            qg = q_ref[0, :, hd * LANES:(hd + 1) * LANES]
            p = _softmax_rows(_dot_nt(qg, kv) * ATT_SCALE + bias_ref[hd])
            o_ref[0, :, hd * LANES:(hd + 1) * LANES] = _dot(p.astype(BF16), kv)
            pg = pg + p
        sel_ref[0, h] = _select(_dot(pg, ov_ref[...], HIGHEST), qpos, n_slc)


def _cmp_prompt(kvc, q, bias_c, cmp_w, overlap):
    bsz, t, _ = kvc.shape
    n_sub = t // D_CMP
    nqb = t // Q_BLOCK
    n_slc = t // L_SEL
    assert n_sub == LANES and n_slc <= LANES
    w1, b1, w2, b2 = cmp_w
    fixed = lambda b, i: (0, 0)
    return pl.pallas_call(
        functools.partial(_cmp_prompt_kernel, n_slc=n_slc),
        grid=(bsz, nqb),
        in_specs=[pl.BlockSpec((1, n_sub, D_CMP * KV_COLS), lambda b, i: (b, 0, 0)),
                  pl.BlockSpec((1, Q_BLOCK, Q_PAD), lambda b, i: (b, i, 0)),
                  pl.BlockSpec((ATT_HEADS, Q_BLOCK, n_sub), lambda b, i: (0, i, 0)),
                  pl.BlockSpec(w1.shape, fixed), pl.BlockSpec(b1.shape, fixed),
                  pl.BlockSpec(w2.shape, fixed), pl.BlockSpec(b2.shape, fixed),
                  pl.BlockSpec(overlap.shape, fixed)],
        out_specs=[pl.BlockSpec((1, Q_BLOCK, Q_PAD), lambda b, i: (b, i, 0)),
                   pl.BlockSpec((1, KV_HEADS, Q_BLOCK, LANES), lambda b, i: (b, 0, i, 0))],
        out_shape=[jax.ShapeDtypeStruct((bsz, t, Q_PAD), F32),
                   jax.ShapeDtypeStruct((bsz, KV_HEADS, t, LANES), F32)],
        scratch_shapes=[pltpu.VMEM((KV_HEADS, n_sub, LANES), BF16)],
        compiler_params=_cparams(("parallel", "arbitrary")),
        name="cmp_prompt",
    )(kvc.reshape(bsz, n_sub, D_CMP * KV_COLS), q, bias_c, w1, b1, w2, b2, overlap)


def _nsa_prompt_kernel(q_ref, kvs_ref, kvw_ref, sel_ref, ex_ref, selt_ref, wint_ref, os_ref, ow_ref,
                       kvsb, kvwb, selx):
    qb = pl.program_id(1)

    @pl.when(qb == 0)
    def _():
        kvsb[...] = kvs_ref[0].astype(BF16)
        kvwb[...] = kvw_ref[0].astype(BF16)

    nlead = WINDOW // Q_BLOCK
    for h in range(KV_HEADS):
        hl = h * LANES
        selx[...] = (_dot(sel_ref[0, h].astype(BF16), ex_ref[...]) - 1.0) * (-NEG)
        for g in range(GQA):
            hd = h * GQA + g
            qg = q_ref[0, :, hd * LANES:(hd + 1) * LANES]

            def tile(carry, kt, bias, kvbuf):
                off = pl.multiple_of(kt * Q_BLOCK, Q_BLOCK)
                kv = kvbuf[pl.ds(off, Q_BLOCK), hl:hl + LANES]
                return carry, _dot_nt(qg, kv) * ATT_SCALE + bias, kv, off

            def far_body(kt, carry):
                carry, s, kv, off = tile(carry, kt, selt_ref[2, hd], kvsb)
                return _flash_step(carry, s + selx[:, pl.ds(off, Q_BLOCK)], kv)

            carry = lax.fori_loop(0, jnp.maximum(qb - 1, 0), far_body, _flash_init(Q_BLOCK))
            for r in (1, 0):
                kt = jnp.maximum(qb - r, 0)
                gone = jnp.where(qb - r >= 0, 0.0, NEG)
                carry, s, kv, off = tile(carry, kt, selt_ref[r, hd], kvsb)
                carry = _flash_step(carry, s + selx[:, pl.ds(off, Q_BLOCK)] + gone, kv)
            os_ref[0, :, hd * LANES:(hd + 1) * LANES] = _flash_out(carry)

            carry = _flash_init(Q_BLOCK)
            for r in range(nlead, -1, -1):
                kt = jnp.maximum(qb - r, 0)
                gone = jnp.where(qb - r >= 0, 0.0, NEG)
                carry, s, kv, off = tile(carry, kt, wint_ref[r, hd], kvwb)
                carry = _flash_step(carry, s + gone, kv)
            ow_ref[0, :, hd * LANES:(hd + 1) * LANES] = _flash_out(carry)


def _nsa_prompt(q, kvs, kvw, sel, expand_sel, sel_tiles, win_tiles):
    bsz, t, _ = kvs.shape
    nqb = t // Q_BLOCK
    qblk = lambda b, i: (b, i, 0)
    per_b = lambda b, i: (b, 0, 0)
    fixed2 = lambda b, i: (0, 0)
    fixed4 = lambda b, i: (0, 0, 0, 0)
    return pl.pallas_call(
        _nsa_prompt_kernel,
        grid=(bsz, nqb),
        in_specs=[pl.BlockSpec((1, Q_BLOCK, Q_PAD), qblk), pl.BlockSpec((1, t, KV_COLS), per_b),
                  pl.BlockSpec((1, t, KV_COLS), per_b),
                  pl.BlockSpec((1, KV_HEADS, Q_BLOCK, LANES), lambda b, i: (b, 0, i, 0)),
                  pl.BlockSpec(expand_sel.shape, fixed2), pl.BlockSpec(sel_tiles.shape, fixed4),
                  pl.BlockSpec(win_tiles.shape, fixed4)],
        out_specs=[pl.BlockSpec((1, Q_BLOCK, Q_PAD), qblk), pl.BlockSpec((1, Q_BLOCK, Q_PAD), qblk)],
        out_shape=[jax.ShapeDtypeStruct((bsz, t, Q_PAD), F32)] * 2,
        scratch_shapes=[pltpu.VMEM((t, KV_COLS), BF16), pltpu.VMEM((t, KV_COLS), BF16),
                        pltpu.VMEM((Q_BLOCK, t), F32)],
        compiler_params=_cparams(("parallel", "arbitrary")),
        name="nsa_prompt",
    )(q, kvs, kvw, sel, expand_sel, sel_tiles, win_tiles)


def _stack_q(q_ref, h):
    return jnp.concatenate([q_ref[0, :, (h * GQA + g) * LANES:(h * GQA + g + 1) * LANES] for g in range(GQA)],
                           axis=0).astype(BF16)


def _cmp_sample_kernel(pt_ref, *refs, n_pages, n_slc, past, t):
    pages = refs[:n_pages]
    (q_ref, bias_ref, w1_ref, b1_ref, w2_ref, b2_ref, ov_ref, o_ref, sel_ref, xbuf) = refs[n_pages:]
    sub = PAGE_SIZE // D_CMP
    for k in range(n_pages):
        xbuf[k * sub:(k + 1) * sub, :] = pages[k][0]
    qpos = past + lax.broadcasted_iota(jnp.int32, (t, 1), 0)
    for h in range(KV_HEADS):
        kv = _compress(xbuf, h, w1_ref, b1_ref, w2_ref, b2_ref).astype(BF16)
        p = _softmax_rows(_dot_nt(_stack_q(q_ref, h), kv) * ATT_SCALE + bias_ref[h])
        o = _dot(p.astype(BF16), kv)
        pg = jnp.zeros((t, kv.shape[0]), F32)
        for g in range(GQA):
            hd = h * GQA + g
            o_ref[0, :, hd * LANES:(hd + 1) * LANES] = o[g * t:(g + 1) * t]
            pg = pg + p[g * t:(g + 1) * t]
        sel_ref[0, h] = _select(_dot(pg, ov_ref[...], HIGHEST), qpos, n_slc)


def _cmp_sample(cache_cmp, page_table, q, bias_cs, cmp_w, overlap, n_slc):
    bsz, n_pages = page_table.shape
    t = q.shape[1]
    n_phys = cache_cmp.shape[0]
    sub = PAGE_SIZE // D_CMP
    n_sub = n_pages * sub
    w1, b1, w2, b2 = cmp_w
    width = overlap.shape[1]
    fixed2 = lambda b, pt: (0, 0)
    page_specs = [pl.BlockSpec((1, sub, D_CMP * KV_COLS), lambda b, pt, k=k: (pt[b, k], 0, 0))
                  for k in range(n_pages)]
    pages = cache_cmp.reshape(n_phys, sub, D_CMP * KV_COLS)
    grid_spec = pltpu.PrefetchScalarGridSpec(
        num_scalar_prefetch=1,
        grid=(bsz,),
        in_specs=page_specs + [
            pl.BlockSpec((1, t, Q_PAD), lambda b, pt: (b, 0, 0)),
            pl.BlockSpec(bias_cs.shape, lambda b, pt: (0, 0, 0)),
            pl.BlockSpec(w1.shape, fixed2), pl.BlockSpec(b1.shape, fixed2),
            pl.BlockSpec(w2.shape, fixed2), pl.BlockSpec(b2.shape, fixed2),
            pl.BlockSpec(overlap.shape, fixed2)],
        out_specs=[pl.BlockSpec((1, t, Q_PAD), lambda b, pt: (b, 0, 0)),
                   pl.BlockSpec((1, KV_HEADS, t, width), lambda b, pt: (b, 0, 0, 0))],
        scratch_shapes=[pltpu.VMEM((n_sub, D_CMP * KV_COLS), F32)])
    return pl.pallas_call(
        functools.partial(_cmp_sample_kernel, n_pages=n_pages, n_slc=n_slc, past=n_pages * PAGE_SIZE, t=t),
        grid_spec=grid_spec,
        out_shape=[jax.ShapeDtypeStruct((bsz, t, Q_PAD), F32),
                   jax.ShapeDtypeStruct((bsz, KV_HEADS, t, width), F32)],
        compiler_params=_cparams(("parallel",)),
        name="cmp_sample",
    )(page_table, *([pages] * n_pages), q, bias_cs, w1, b1, w2, b2, overlap)


def _joint_attend(qh, s_past, kv_past, s_new, kv_new):
    m = jnp.maximum(jnp.max(s_past, -1, keepdims=True), jnp.max(s_new, -1, keepdims=True))
    e_past = jnp.where(s_past > MASKED_BELOW, jnp.exp(s_past - m), 0.0)
    e_new = jnp.where(s_new > MASKED_BELOW, jnp.exp(s_new - m), 0.0)
    den = jnp.sum(e_past, -1, keepdims=True) + jnp.sum(e_new, -1, keepdims=True)
    acc = _dot(e_past.astype(BF16), kv_past) + _dot(e_new.astype(BF16), kv_new)
    return acc / jnp.maximum(den, 1e-30)


def _pad_new_rows(new_ref, t):
    return jnp.concatenate([new_ref[0], jnp.zeros((LANES - t, KV_COLS), F32)], axis=0).astype(BF16)


def _sel_sample_kernel(pt_ref, *refs, n_pages, t):
    pages = refs[:n_pages]
    (new_ref, q_ref, sel_ref, bias_ref, biasn_ref, o_ref, kvb, mask) = refs[n_pages:]
    for k in range(n_pages):
        kvb[k * PAGE_SIZE:(k + 1) * PAGE_SIZE, :] = pages[k][0].astype(BF16)
    kv_new = _pad_new_rows(new_ref, t)
    lane = lax.broadcasted_iota(jnp.int32, (GQA * t, LANES), 1)
    per_tile = LANES // L_SEL
    for h in range(KV_HEADS):
        hl = h * LANES
        sel = sel_ref[0, h]
        sel4 = jnp.concatenate([sel] * GQA, axis=0)

        def tile_mask(k):
            cols = [jnp.broadcast_to(sel4[:, per_tile * k + i:per_tile * k + i + 1], (GQA * t, LANES))
                    for i in range(per_tile)]
            m = cols[-1]
            for i in range(per_tile - 2, -1, -1):
                m = jnp.where(lane < (i + 1) * L_SEL, cols[i], m)
            return (m - 1.0) * (-NEG)

        for k in range(n_pages):
            mask[:, k * LANES:(k + 1) * LANES] = tile_mask(k)
        qh = _stack_q(q_ref, h)
        kvp = kvb[:, hl:hl + LANES]
        s_past = _dot_nt(qh, kvp) * ATT_SCALE + bias_ref[h] + mask[...]
        kvn = kv_new[:, hl:hl + LANES]
        s_new = _dot_nt(qh, kvn) * ATT_SCALE + biasn_ref[h] + tile_mask(n_pages)
        o = _joint_attend(qh, s_past, kvp, s_new, kvn)
        for g in range(GQA):
            hd = h * GQA + g
            o_ref[0, :, hd * LANES:(hd + 1) * LANES] = o[g * t:(g + 1) * t]


def _sel_sample(cache_slc, page_table, kvs_new, q, sel, bias_past, bias_new):
    bsz, n_pages = page_table.shape
    t = q.shape[1]
    n_phys = cache_slc.shape[0]
    past = n_pages * PAGE_SIZE
    width = sel.shape[-1]
    assert (n_pages + 1) * (LANES // L_SEL) <= width
    page_specs = [pl.BlockSpec((1, PAGE_SIZE, KV_COLS), lambda b, pt, k=k: (pt[b, k], 0, 0))
                  for k in range(n_pages)]
    pages = cache_slc.reshape(n_phys, PAGE_SIZE, KV_COLS)
    per_b = lambda b, pt: (b, 0, 0)
    fixed3 = lambda b, pt: (0, 0, 0)
    grid_spec = pltpu.PrefetchScalarGridSpec(
        num_scalar_prefetch=1,
        grid=(bsz,),
        in_specs=page_specs + [
            pl.BlockSpec((1, t, KV_COLS), per_b), pl.BlockSpec((1, t, Q_PAD), per_b),
            pl.BlockSpec((1, KV_HEADS, t, width), lambda b, pt: (b, 0, 0, 0)),
            pl.BlockSpec(bias_past.shape, fixed3), pl.BlockSpec(bias_new.shape, fixed3)],
        out_specs=pl.BlockSpec((1, t, Q_PAD), per_b),
        scratch_shapes=[pltpu.VMEM((past, KV_COLS), BF16), pltpu.VMEM((GQA * t, past), F32)])
    return pl.pallas_call(
        functools.partial(_sel_sample_kernel, n_pages=n_pages, t=t),
        grid_spec=grid_spec,
        out_shape=jax.ShapeDtypeStruct((bsz, t, Q_PAD), F32),
        compiler_params=_cparams(("parallel",)),
        name="sel_sample",
    )(page_table, *([pages] * n_pages), kvs_new, q, sel, bias_past, bias_new)


def _win_sample_kernel(buf_ref, new_ref, q_ref, bias_ref, biasn_ref, o_ref, win_ref, *, t):
    buf = buf_ref[0]
    w = buf.shape[0]
    win_ref[0, 0:w - t, :] = buf[t:, :]
    win_ref[0, w - t:w, :] = new_ref[0]
    kvb = buf.astype(BF16)
    kv_new = _pad_new_rows(new_ref, t)
    for h in range(KV_HEADS):
        hl = h * LANES
        qh = _stack_q(q_ref, h)
        kvp = kvb[:, hl:hl + LANES]
        kvn = kv_new[:, hl:hl + LANES]
        s_past = _dot_nt(qh, kvp) * ATT_SCALE + bias_ref[h]
        s_new = _dot_nt(qh, kvn) * ATT_SCALE + biasn_ref[h]
        o = _joint_attend(qh, s_past, kvp, s_new, kvn)
        for g in range(GQA):
            hd = h * GQA + g
            o_ref[0, :, hd * LANES:(hd + 1) * LANES] = o[g * t:(g + 1) * t]


def _win_sample(cache_win, kvw_new, q, bias_buf, bias_new):
    bsz, w, _ = cache_win.shape
    t = q.shape[1]
    per_b = lambda b: (b, 0, 0)
    fixed3 = lambda b: (0, 0, 0)
    return pl.pallas_call(
        functools.partial(_win_sample_kernel, t=t),
        grid=(bsz,),
        in_specs=[pl.BlockSpec((1, w, KV_COLS), per_b), pl.BlockSpec((1, t, KV_COLS), per_b),
                  pl.BlockSpec((1, t, Q_PAD), per_b),
                  pl.BlockSpec(bias_buf.shape, fixed3), pl.BlockSpec(bias_new.shape, fixed3)],
        out_specs=[pl.BlockSpec((1, t, Q_PAD), per_b), pl.BlockSpec((1, w, KV_COLS), per_b)],
        out_shape=[jax.ShapeDtypeStruct((bsz, t, Q_PAD), F32),
                   jax.ShapeDtypeStruct((bsz, w, KV_COLS), F32)],
        compiler_params=_cparams(("parallel",)),
        name="win_sample",
    )(cache_win, kvw_new, q, bias_buf, bias_new)


def _combine_kernel(x_ref, y_ref, oc_ref, os_ref, ow_ref, sm_ref, eg_ref, eb_ref, ag_ref, wo_ref, g1_ref,
                    b1_ref, h_ref):
    xn = _layer_norm(x_ref[...], eg_ref[...], eb_ref[...])
    gates = _sigmoid(sm_ref[...])
    rows = x_ref.shape[0]
    lane = lax.broadcasted_iota(jnp.int32, (rows, LANES), 1)
    heads = []
    ss = jnp.zeros((rows, 1), F32)
    for hd in range(ATT_HEADS):
        sl = slice(hd * LANES, (hd + 1) * LANES)
        o = jnp.zeros((rows, LANES), F32)
        for br, ref in enumerate((oc_ref, os_ref, ow_ref)):
            col = SSM_HEADS + br * ATT_HEADS + hd
            o = o + gates[:, col:col + 1] * ref[:, sl]
        o = jnp.where(lane >= HEAD_DIM, o, 0.0)
        ss = ss + jnp.sum(o * o, -1, keepdims=True)
        heads.append(o)
    rms = lax.rsqrt(ss / (ATT_HEADS * HEAD_DIM) + EPS)
    att = jnp.concatenate(heads, axis=1) * rms * ag_ref[...]
    cat = jnp.concatenate([y_ref[...], att.astype(BF16)], axis=1)
    mix = _dot(cat, wo_ref[...])
    h_ref[...] = _layer_norm(ALPHA * xn + mix, g1_ref[...], b1_ref[...])


def _combine(x2d, y, oc, os_, ow, sm, eg, eb, ag_pad, wo_pad, g1, b1, tm):
    n = x2d.shape[0]
    tm = _row_tile(n, tm)
    row = lambda i: (i, 0)
    fixed = lambda i: (0, 0)
    vec = pl.BlockSpec((1, D_MODEL), fixed)
    return pl.pallas_call(
        _combine_kernel,
        grid=(n // tm,),
        in_specs=[pl.BlockSpec((tm, D_MODEL), row), pl.BlockSpec((tm, SSM_WIDTH), row),
                  pl.BlockSpec((tm, Q_PAD), row), pl.BlockSpec((tm, Q_PAD), row),
                  pl.BlockSpec((tm, Q_PAD), row), pl.BlockSpec((tm, LANES), row),
                  vec, vec, pl.BlockSpec((1, Q_PAD), fixed),
                  pl.BlockSpec((SSM_WIDTH + Q_PAD, D_MODEL), fixed), vec, vec],
        out_specs=pl.BlockSpec((tm, D_MODEL), row),
        out_shape=jax.ShapeDtypeStruct((n, D_MODEL), F32),
        compiler_params=_cparams(("parallel",)),
        name="combine",
    )(x2d, y, oc, os_, ow, sm, eg, eb, ag_pad, wo_pad, g1, b1)


def _ffn_kernel(h_ref, wu_ref, wd_ref, g_ref, b_ref, o_ref, acc, *, nk):
    k = pl.program_id(1)
    h = h_ref[...]
    u = jnp.maximum(_dot(h.astype(BF16), wu_ref[...]), 0.0)
    part = _dot((u * u).astype(BF16), wd_ref[...])

    @pl.when(k == 0)
    def _():
        acc[...] = part

    @pl.when(k > 0)
    def _():
        acc[...] = acc[...] + part

    @pl.when(k == nk - 1)
    def _():
        o_ref[...] = _layer_norm(ALPHA * h + acc[...], g_ref[...], b_ref[...])


def _ffn(h2d, wu, wd, g, b, tm, tf):
    n = h2d.shape[0]
    tm = _row_tile(n, tm)
    nk = D_FF // tf
    vec = pl.BlockSpec((1, D_MODEL), lambda i, k: (0, 0))
    return pl.pallas_call(
        functools.partial(_ffn_kernel, nk=nk),
        grid=(n // tm, nk),
        in_specs=[pl.BlockSpec((tm, D_MODEL), lambda i, k: (i, 0)),
                  pl.BlockSpec((D_MODEL, tf), lambda i, k: (0, k)),
                  pl.BlockSpec((tf, D_MODEL), lambda i, k: (k, 0)), vec, vec],
        out_specs=pl.BlockSpec((tm, D_MODEL), lambda i, k: (i, 0)),
        out_shape=jax.ShapeDtypeStruct((n, D_MODEL), F32),
        scratch_shapes=[pltpu.VMEM((tm, D_MODEL), F32)],
        compiler_params=_cparams(("parallel", "arbitrary")),
        name="ffn",
    )(h2d, wu, wd, g, b)


def _bucket_np(dist):
    d = np.maximum(dist, 0)
    exact = N_BUCKETS // 2
    far = exact + (np.log(np.maximum(d, 1).astype(np.float32) / np.float32(exact))
                   / np.float32(math.log(MAX_DISTANCE / exact)) * (N_BUCKETS - exact)).astype(np.int32)
    return np.where(d < exact, d, np.minimum(far, N_BUCKETS - 1)).astype(np.int32)


def _bias_table(tbl, dist, mask):
    b = jnp.moveaxis(tbl[jnp.asarray(_bucket_np(dist))], -1, 0)
    return jnp.where(jnp.asarray(mask)[None], b, NEG).astype(F32)


def _stack_gt(tab, t):
    return tab.reshape(KV_HEADS, GQA * t, tab.shape[-1])


def _overlap(n_cmp_pad, n_cmp, width, n_slc):
    i = np.arange(n_cmp_pad)[:, None]
    j = np.arange(width)[None, :]
    ov = (i * D_CMP < (j + 1) * L_SEL) & (i * D_CMP + L_CMP > j * L_SEL) & (i < n_cmp) & (j < n_slc)
    return jnp.asarray(ov.astype(np.float32))


def _prep_cmp_weights(w1, b1, w2, b2):
    eye = jnp.eye(2, dtype=F32)
    w1r = (w1[:, :, :, :, None, :] * eye[None, None, :, None, :, None]).transpose(1, 2, 3, 0, 4, 5)
    w1r = w1r.reshape(D_CMP * 2 * HEAD_DIM, 2 * 2 * CMP_HID).astype(BF16)
    w2r = (w2[:, :, None, :] * eye[:, None, :, None]).reshape(2 * CMP_HID, 2 * HEAD_DIM).astype(BF16)
    return w1r, b1.reshape(1, 2 * CMP_HID), w2r, b2.reshape(1, 2 * HEAD_DIM)


def _prep_w_in(w_in):
    sizes = (SSM_WIDTH, CONV_DIM, SSM_HEADS, ATT_HEADS * HEAD_DIM, KV_COLS, KV_COLS, KV_COLS)
    z, xbc, dt, q, kvc, kvs, kvw, gates = jnp.split(w_in, np.cumsum(sizes).tolist(), axis=1)
    q_pad = jnp.pad(q.reshape(D_MODEL, ATT_HEADS, HEAD_DIM), ((0, 0), (0, 0), (0, LANES - HEAD_DIM)))
    small = jnp.concatenate([dt, gates], axis=1)
    small = jnp.pad(small, ((0, 0), (0, LANES - small.shape[1])))
    return jnp.concatenate([z, xbc, q_pad.reshape(D_MODEL, Q_PAD), kvc, kvs, kvw, small], axis=1).astype(BF16)


def _pad_heads_rows(w):
    n = w.shape[1]
    return jnp.pad(w.reshape(ATT_HEADS, HEAD_DIM, n), ((0, 0), (LANES - HEAD_DIM, 0), (0, 0))).reshape(Q_PAD, n)


def kernel(x_prompt, x_sample, cache_cmp_kv, cache_slc_kv, cache_win_kv, state_conv, state_ssm, page_table,
           rel_bias_table, emb_ln_g, emb_ln_b, w_in, conv_w, conv_b, dt_bias, a_log, d_skip, ssm_norm_g,
           cmp_w1, cmp_b1, cmp_w2, cmp_b2, att_norm_g, w_out, ln1_g, ln1_b, w_up, w_down, ln2_g, ln2_b):
    assert w_in.shape[0] == DEPTH
    bp, tp, _ = x_prompt.shape
    bs, ts, _ = x_sample.shape
    n_pages = page_table.shape[1]
    past = n_pages * PAGE_SIZE
    w_buf = cache_win_kv.shape[2]
    assert ts < D_CMP and w_buf == WINDOW and past >= WINDOW and tp >= WINDOW
    tbl = rel_bias_table
    vec = lambda v: v.reshape(1, -1)

    w_proj = _prep_w_in(w_in[0])
    cmp_w = _prep_cmp_weights(cmp_w1[0], cmp_b1[0], cmp_w2[0], cmp_b2[0])
    wo_pad = jnp.concatenate([w_out[0][:SSM_WIDTH], _pad_heads_rows(w_out[0][SSM_WIDTH:])], axis=0).astype(BF16)
    ag_pad = _pad_heads_rows(att_norm_g[0].reshape(-1, 1)).reshape(1, Q_PAD)
    wu = w_up[0].astype(BF16)
    wd = w_down[0].astype(BF16)
    eg, eb = vec(emb_ln_g), vec(emb_ln_b)

    def trunk_tail(x2d, y, oc, os_, ow, sm, tm):
        h = _combine(x2d, y, oc, os_, ow, sm, eg, eb, ag_pad, wo_pad, vec(ln1_g[0]), vec(ln1_b[0]), tm)
        return _ffn(h, wu, wd, vec(ln2_g[0]), vec(ln2_b[0]), 1024, D_FF // 4)

    ssm_w = (conv_w[0], conv_b[0], dt_bias[0], a_log[0], d_skip[0], ssm_norm_g[0])

    xp2 = x_prompt.reshape(bp * tp, D_MODEL)
    z, xbc, q, kvc, kvs, kvw, sm = _proj(xp2, eg, eb, w_proj, BF16, 512)
    r3 = lambda a: a.reshape(bp, tp, a.shape[-1])
    xbc3 = r3(xbc)
    y_ssm, h_new = _ssm(r3(z), xbc3, r3(sm), jnp.zeros((bp, CONV_WIDTH - 1, CONV_DIM), F32),
                        jnp.zeros((bp, SSM_HEADS, SSM_HEAD_DIM, D_STATE), F32), *ssm_w)
    n_sub = tp // D_CMP
    n_cmp = n_sub - 1
    n_slc = tp // L_SEL
    qpos = np.arange(tp)
    dist_c = qpos[:, None] - (np.arange(n_sub) * D_CMP + L_CMP - 1)[None, :]
    bias_c = _bias_table(tbl, dist_c, (dist_c >= 0) & (np.arange(n_sub) < n_cmp)[None, :])
    oc, sel = _cmp_prompt(r3(kvc), r3(q), bias_c, cmp_w, _overlap(n_sub, n_cmp, LANES, n_slc))
    ii = np.arange(Q_BLOCK)[:, None] - np.arange(Q_BLOCK)[None, :]
    sel_tiles = jnp.stack([_bias_table(tbl, ii + Q_BLOCK * r, ii + Q_BLOCK * r >= 0) for r in range(3)])
    nlead = WINDOW // Q_BLOCK
    win_tiles = jnp.stack([_bias_table(tbl, ii + Q_BLOCK * r, (ii + Q_BLOCK * r >= 0) & (ii + Q_BLOCK * r < WINDOW))
                           for r in range(nlead + 1)])
    expand_sel = jnp.asarray((np.arange(LANES)[:, None] == (np.arange(tp) // L_SEL)[None, :]).astype(np.float32),
                             BF16)
    os_, ow = _nsa_prompt(r3(q), r3(kvs), r3(kvw), sel, expand_sel, sel_tiles, win_tiles)
    f2 = lambda a: a.reshape(bp * tp, a.shape[-1])
    y_prompt = trunk_tail(xp2, f2(y_ssm), f2(oc), f2(os_), f2(ow), sm, 256).reshape(bp, tp, D_MODEL)
    kv5 = lambda a, b, t: a.reshape(1, b, t, KV_HEADS, 2, HEAD_DIM)
    w = min(WINDOW, tp)
    prompt_state = (kv5(kvc, bp, tp), kv5(kvs, bp, tp), kv5(r3(kvw)[:, tp - w:], bp, w),
                    xbc3[:, tp - (CONV_WIDTH - 1):][None], h_new[None])

    xs2 = x_sample.reshape(bs * ts, D_MODEL)
    z, xbc, q, kvc, kvs, kvw, sm = _proj(xs2, eg, eb, w_proj, F32, 512)
    r3 = lambda a: a.reshape(bs, ts, a.shape[-1])
    xbc3 = r3(xbc)
    y_ssm, h_new = _ssm(r3(z), xbc3, r3(sm), state_conv[0], state_ssm[0], *ssm_w)
    n_sub = past // D_CMP
    n_cmp = n_sub - 1
    n_slc = -(-(past + ts) // L_SEL)
    width = -(-n_slc // LANES) * LANES
    qpos = past + np.arange(ts)
    dist_c = qpos[:, None] - (np.arange(n_sub) * D_CMP + L_CMP - 1)[None, :]
    bias_cs = _stack_gt(_bias_table(tbl, dist_c, (dist_c >= 0) & (np.arange(n_sub) < n_cmp)[None, :]), ts)
    oc, sel = _cmp_sample(cache_cmp_kv[0], page_table, r3(q), bias_cs, cmp_w,
                          _overlap(n_sub, n_cmp, width, n_slc), n_slc)
    dist_p = qpos[:, None] - np.arange(past)[None, :]
    dist_n = np.arange(ts)[:, None] - np.arange(LANES)[None, :]
    new_ok = (dist_n >= 0) & (np.arange(LANES) < ts)[None, :]
    bias_new = _stack_gt(_bias_table(tbl, dist_n, new_ok), ts)
    os_ = _sel_sample(cache_slc_kv[0], page_table, r3(kvs), r3(q), sel,
                      _stack_gt(_bias_table(tbl, dist_p, dist_p >= 0), ts), bias_new)
    dist_w = qpos[:, None] - (past - w_buf + np.arange(w_buf))[None, :]
    bias_wb = _stack_gt(_bias_table(tbl, dist_w, (dist_w >= 0) & (dist_w < WINDOW)), ts)
    ow, win_new = _win_sample(cache_win_kv[0].reshape(bs, w_buf, KV_COLS), r3(kvw), r3(q), bias_wb, bias_new)
    f2 = lambda a: a.reshape(bs * ts, a.shape[-1])
    y_sample = trunk_tail(xs2, f2(y_ssm), f2(oc), f2(os_), f2(ow), sm, 256).reshape(bs, ts, D_MODEL)
    sample_state = (kv5(kvc, bs, ts), kv5(kvs, bs, ts), kv5(win_new, bs, w_buf),
                    xbc3[:, ts - (CONV_WIDTH - 1):][None], h_new[None])

    return (y_prompt, y_sample) + prompt_state + sample_state
```

```python
import functools
import math

import numpy as np
import jax
import jax.numpy as jnp
from jax import lax
from jax.experimental import pallas as pl
from jax.experimental.pallas import tpu as pltpu

F32 = jnp.float32
BF16 = jnp.bfloat16
HIGHEST = lax.Precision.HIGHEST

D_MODEL = 1024
SSM_HEADS = 8
SSM_HEAD_DIM = 64
SSM_WIDTH = SSM_HEADS * SSM_HEAD_DIM
SSM_GROUPS = 2
D_STATE = 128
CONV_WIDTH = 4
CONV_DIM = SSM_WIDTH + 2 * SSM_GROUPS * D_STATE
SSD_CHUNK = 128
ATT_HEADS = 8
KV_HEADS = 2
GQA = ATT_HEADS // KV_HEADS
HEAD_DIM = 64
ATT_WIDTH = ATT_HEADS * HEAD_DIM
KV_COLS = KV_HEADS * 2 * HEAD_DIM
D_CMP = 16
L_CMP = 2 * D_CMP
CMP_HID = 64
L_SEL = 64
TOP_N = 16
WINDOW = 512
Q_BLOCK = 128
N_BRANCH = 3
FORCED_SCORE = 1e4
N_BUCKETS = 32
MAX_DISTANCE = 128
D_FF = 4 * D_MODEL
DEPTH = 1
ALPHA = (2 * DEPTH) ** 0.25
ATT_SCALE = HEAD_DIM ** -0.5
EPS = 1e-5
PAGE_SIZE = 128

LANES = 128
NEG = -1e30
MASKED_BELOW = -1e29
VMEM_LIMIT = 48 * 1024 * 1024
KV_ROWS = 2 * HEAD_DIM
SEL_PAD = 32
NEAR_W = 32

_OFF_Z = 0
_OFF_XBC = _OFF_Z + SSM_WIDTH
_OFF_Q = _OFF_XBC + CONV_DIM
_OFF_KVC = _OFF_Q + ATT_WIDTH
_OFF_KVS = _OFF_KVC + KV_COLS
_OFF_KVW = _OFF_KVS + KV_COLS
_OFF_SM = _OFF_KVW + KV_COLS
_N_PROJ = _OFF_SM + LANES


def _cparams(sem):
    return pltpu.CompilerParams(dimension_semantics=sem, vmem_limit_bytes=VMEM_LIMIT)


def _row_tile(n, preferred):
    tm = min(n, preferred)
    assert n % tm == 0 and tm % 8 == 0
    return tm


def _dot(a, b, precision=None):
    return jnp.dot(a, b, preferred_element_type=F32, precision=precision)


def _dot_nt(a, b):
    return lax.dot_general(a, b, (((1,), (1,)), ((), ())), preferred_element_type=F32)


def _layer_norm(x, g, b):
    mu = jnp.mean(x, -1, keepdims=True)
    xc = x - mu
    var = jnp.mean(xc * xc, -1, keepdims=True)
    return xc * lax.rsqrt(var + EPS) * g + b


def _sigmoid(x):
    return 1.0 / (1.0 + jnp.exp(-x))


def _softplus(x):
    return jnp.maximum(x, 0.0) + jnp.log(1.0 + jnp.exp(-jnp.abs(x)))


def _gelu_tanh(x):
    c = math.sqrt(2.0 / math.pi)
    return 0.5 * x * (1.0 + jnp.tanh(c * (x + 0.044715 * (x * x * x))))


def _proj_kernel(x_ref, g_ref, b_ref, w_ref, z_ref, xbc_ref, q_ref, kvc_ref, *rest, feature_major):
    xn = _layer_norm(x_ref[...], g_ref[...], b_ref[...]).astype(BF16)

    def mm(lo, hi):
        return _dot(xn, w_ref[:, lo:hi])

    z_ref[...] = mm(_OFF_Z, _OFF_XBC)
    xbc_ref[...] = mm(_OFF_XBC, _OFF_Q)
    q_ref[...] = mm(_OFF_Q, _OFF_KVC).astype(q_ref.dtype)
    kvc = mm(_OFF_KVC, _OFF_KVS)
    if feature_major:
        kvct_ref, kvst_ref, kvwt_ref, sm_ref = rest
        for h in range(KV_HEADS):
            kvc_ref[h] = kvc[:, h * KV_ROWS:(h + 1) * KV_ROWS]
        kvct_ref[0] = kvc.T
        kvst_ref[0] = mm(_OFF_KVS, _OFF_KVW).T
        kvwt_ref[0] = mm(_OFF_KVW, _OFF_SM).T
    else:
        kvs_ref, kvw_ref, sm_ref = rest
        kvc_ref[...] = kvc
        kvs_ref[...] = mm(_OFF_KVS, _OFF_KVW)
        kvw_ref[...] = mm(_OFF_KVW, _OFF_SM)
    sm_ref[...] = mm(_OFF_SM, _N_PROJ)


def _proj(x2d, g, b, w, q_dtype, tm, seq=None):
    n = x2d.shape[0]
    tm = _row_tile(n, tm)
    row = lambda i: (i, 0)
    fixed = lambda i: (0, 0)
    rm = lambda wd, dt: (pl.BlockSpec((tm, wd), row), jax.ShapeDtypeStruct((n, wd), dt))
    outs = [rm(SSM_WIDTH, F32), rm(CONV_DIM, F32), rm(ATT_WIDTH, q_dtype)]
    if seq is None:
        outs += [rm(KV_COLS, F32), rm(KV_COLS, F32), rm(KV_COLS, F32)]
    else:
        outs.append((pl.BlockSpec((KV_HEADS, tm, KV_ROWS), lambda i: (0, i, 0)),
                     jax.ShapeDtypeStruct((KV_HEADS, n, KV_ROWS), F32)))
        bsz, t = seq
        assert t % tm == 0 and tm % LANES == 0
        per = t // tm
        fm = (pl.BlockSpec((1, KV_COLS, tm), lambda i: (i // per, 0, i % per)),
              jax.ShapeDtypeStruct((bsz, KV_COLS, t), F32))
        outs += [fm, fm, fm]
    outs.append(rm(LANES, F32))
    return pl.pallas_call(
        functools.partial(_proj_kernel, feature_major=seq is not None),
        grid=(n // tm,),
        in_specs=[pl.BlockSpec((tm, D_MODEL), row), pl.BlockSpec((1, D_MODEL), fixed),
                  pl.BlockSpec((1, D_MODEL), fixed), pl.BlockSpec((D_MODEL, _N_PROJ), fixed)],
        out_specs=[o[0] for o in outs],
        out_shape=[o[1] for o in outs],
        compiler_params=_cparams(("parallel",)),
        name="proj",
    )(x2d, g, b, w)


def _ssm_kernel(z_ref, xbc_ref, sm_ref, hist_ref, h0_ref, cw_ref, cb_ref, dtb_ref, alog_ref, dskip_ref,
                ng_ref, e_ref, et_ref, y_ref, hfin_ref, xext, state, *, tb, l, nc):
    c = pl.program_id(1)

    @pl.when(c == 0)
    def _():
        xext[0:8, :] = jnp.zeros((8, CONV_DIM), F32)
        xext[8 - (CONV_WIDTH - 1):8, :] = hist_ref[0]
        if tb < l:
            xext[8 + tb:8 + l, :] = jnp.zeros((l - tb, CONV_DIM), F32)
        state[...] = h0_ref[0].reshape(SSM_WIDTH, D_STATE)

    xext[8:8 + tb, :] = xbc_ref[0]
    conv = cb_ref[...]
    for k in range(CONV_WIDTH):
        lo = 8 - (CONV_WIDTH - 1) + k
        conv = conv + cw_ref[k:k + 1, :] * xext[lo:lo + l, :]
    xc = conv * _sigmoid(conv)
    xext[0:8, :] = xext[tb:tb + 8, :]

    dt = _softplus(sm_ref[0] + dtb_ref[...])
    if tb < l:
        dt = jnp.concatenate([dt, jnp.zeros((l - tb, LANES), F32)], axis=0)
    a = dt * (-jnp.exp(alog_ref[...]))
    ri = lax.broadcasted_iota(jnp.int32, (l, l), 0)
    ci = lax.broadcasted_iota(jnp.int32, (l, l), 1)
    tril = ri >= ci
    a_cs = _dot(jnp.where(tril, 1.0, 0.0), a, HIGHEST)
    a_cs_t = a_cs.T
    e = e_ref[...]
    dtx = _dot(dt, e, HIGHEST)
    eax = _dot(jnp.exp(a_cs[:tb]), e, HIGHEST)
    decx = _dot(jnp.exp(a_cs[l - 1:l, :] - a_cs), e, HIGHEST)
    tot = jnp.broadcast_to(jnp.exp(a_cs_t[:, l - 1:l]), (LANES, LANES))
    rtot = _dot(et_ref[...], tot, HIGHEST)

    xs = xc[:, :SSM_WIDTH]
    xd = xs * dtx
    xw = xd * decx
    lane = lax.broadcasted_iota(jnp.int32, (tb, LANES), 1)
    tril_q = tril[:tb]
    y_pairs = []
    for g in range(SSM_GROUPS):
        bg = xc[:, SSM_WIDTH + g * D_STATE:SSM_WIDTH + (g + 1) * D_STATE].astype(BF16)
        c_lo = SSM_WIDTH + SSM_GROUPS * D_STATE + g * D_STATE
        cg = xc[:tb, c_lo:c_lo + D_STATE].astype(BF16)
        cb = _dot_nt(cg, bg)
        for k in range(2):
            pair = 2 * g + k
            lo = pair * LANES
            xd_pair = xd[:, lo:lo + LANES].astype(BF16)
            ys = []
            for r2 in range(2):
                h = 2 * pair + r2
                seg = a_cs[:tb, h:h + 1] - a_cs_t[h:h + 1, :]
                lm = jnp.where(tril_q, jnp.exp(jnp.where(tril_q, seg, 0.0)), 0.0)
                ys.append(_dot((cb * lm).astype(BF16), xd_pair))
            y_diag = jnp.where(lane < SSM_HEAD_DIM, ys[0], ys[1])
            sp = state[lo:lo + LANES, :]
            y_off = _dot_nt(cg, sp.astype(BF16)) * eax[:, lo:lo + LANES]
            y_pairs.append(y_diag + y_off)
            upd = _dot(xw[:, lo:lo + LANES].T.astype(BF16), bg)
            state[lo:lo + LANES, :] = sp * rtot[lo:lo + LANES, :] + upd
    y = jnp.concatenate(y_pairs, axis=1) + dskip_ref[...] * xs[:tb]
    zz = z_ref[0]
    y = y * (zz * _sigmoid(zz))
    gw = SSM_WIDTH // SSM_GROUPS
    outs = []
    for g in range(SSM_GROUPS):
        yg = y[:, g * gw:(g + 1) * gw]
        ms = jnp.mean(yg * yg, -1, keepdims=True)
        outs.append(yg * lax.rsqrt(ms + EPS) * ng_ref[:, g * gw:(g + 1) * gw])
    y_ref[0] = jnp.concatenate(outs, axis=1).astype(y_ref.dtype)

    @pl.when(c == nc - 1)
    def _():
        hfin_ref[0] = state[...].reshape(SSM_HEADS, SSM_HEAD_DIM, D_STATE)


def _ssm(z, xbc, sm, hist, h0, conv_w, conv_b, dt_bias, a_log, d_skip, norm_g):
    bsz, t, _ = z.shape
    l = SSD_CHUNK
    tb = min(l, t)
    assert t % tb == 0 and tb % 8 == 0 and t >= CONV_WIDTH - 1
    nc = t // tb
    pad8 = lambda v: jnp.pad(v.reshape(1, SSM_HEADS), ((0, 0), (0, LANES - SSM_HEADS)))
    expand = np.zeros((LANES, SSM_WIDTH), np.float32)
    for h in range(SSM_HEADS):
        expand[h, h * SSM_HEAD_DIM:(h + 1) * SSM_HEAD_DIM] = 1.0
    blk = lambda b, c: (b, c, 0)
    per_b3 = lambda b, c: (b, 0, 0)
    per_b4 = lambda b, c: (b, 0, 0, 0)
    fixed = lambda b, c: (0, 0)
    return pl.pallas_call(
        functools.partial(_ssm_kernel, tb=tb, l=l, nc=nc),
        grid=(bsz, nc),
        in_specs=[pl.BlockSpec((1, tb, SSM_WIDTH), blk), pl.BlockSpec((1, tb, CONV_DIM), blk),
                  pl.BlockSpec((1, tb, LANES), blk),
                  pl.BlockSpec((1, CONV_WIDTH - 1, CONV_DIM), per_b3),
                  pl.BlockSpec((1, SSM_HEADS, SSM_HEAD_DIM, D_STATE), per_b4),
                  pl.BlockSpec((CONV_WIDTH, CONV_DIM), fixed), pl.BlockSpec((1, CONV_DIM), fixed),
                  pl.BlockSpec((1, LANES), fixed), pl.BlockSpec((1, LANES), fixed),
                  pl.BlockSpec((1, SSM_WIDTH), fixed), pl.BlockSpec((1, SSM_WIDTH), fixed),
                  pl.BlockSpec((LANES, SSM_WIDTH), fixed), pl.BlockSpec((SSM_WIDTH, LANES), fixed)],
        out_specs=[pl.BlockSpec((1, tb, SSM_WIDTH), blk),
                   pl.BlockSpec((1, SSM_HEADS, SSM_HEAD_DIM, D_STATE), per_b4)],
        out_shape=[jax.ShapeDtypeStruct((bsz, t, SSM_WIDTH), BF16),
                   jax.ShapeDtypeStruct((bsz, SSM_HEADS, SSM_HEAD_DIM, D_STATE), F32)],
        scratch_shapes=[pltpu.VMEM((8 + l, CONV_DIM), F32), pltpu.VMEM((SSM_WIDTH, D_STATE), F32)],
        compiler_params=_cparams(("parallel", "arbitrary")),
        name="ssm",
    )(z, xbc, sm, hist, h0, conv_w, conv_b.reshape(1, CONV_DIM), pad8(dt_bias), pad8(a_log),
      jnp.repeat(d_skip, SSM_HEAD_DIM).reshape(1, SSM_WIDTH), norm_g.reshape(1, SSM_WIDTH),
      jnp.asarray(expand), jnp.asarray(expand.T))


def _compress(load_rows, h, n_sub, cw):
    w1_ref, b1_ref, w2k_ref, b2k_ref, w2v_ref, b2v_ref = cw
    hid = jnp.zeros((n_sub, 2 * KV_ROWS), F32)
    for j in range(D_CMP):
        hid = hid + _dot(load_rows(h, j).astype(BF16), w1_ref[j])
    pre = hid[:, :KV_ROWS] + pltpu.roll(hid[:, KV_ROWS:], n_sub - 1, 0) + b1_ref[...]
    act = _gelu_tanh(pre).astype(BF16)
    return _dot(act, w2k_ref[...]) + b2k_ref[...], _dot(act, w2v_ref[...]) + b2v_ref[...]


def _select(score, qpos, n_slc):
    jj = lax.broadcasted_iota(jnp.int32, score.shape, 1)
    cur = qpos // L_SEL
    visible = jj * L_SEL <= qpos
    forced = (jj == 0) | (jj == cur) | (jj == cur - 1)
    sc = jnp.where(visible, jnp.where(forced, FORCED_SCORE, score), -1.0)
    sc = jnp.where(jj < n_slc, sc, -2.0)
    rank = jnp.zeros(score.shape, F32)
    for k in range(n_slc):
        ck = sc[:, k:k + 1]
        beats = (ck > sc) | ((ck == sc) & (jj > k))
        rank = rank + jnp.where(beats, 1.0, 0.0)
    return jnp.where((rank < min(TOP_N, n_slc)) & (sc >= 0.0), 1.0, 0.0)


def _softmax_rows(s):
    m = jnp.max(s, -1, keepdims=True)
    e = jnp.where(s > MASKED_BELOW, jnp.exp(s - m), 0.0)
    return e / jnp.maximum(jnp.sum(e, -1, keepdims=True), 1e-30)


def _stack_heads(q_ref, h):
    return jnp.concatenate([q_ref[0, :, (h * GQA + g) * HEAD_DIM:(h * GQA + g + 1) * HEAD_DIM]
                            for g in range(GQA)], axis=0)


def _unstack_heads(o_ref, h, o, rows):
    for g in range(GQA):
        hd = h * GQA + g
        o_ref[0, :, hd * HEAD_DIM:(hd + 1) * HEAD_DIM] = o[g * rows:(g + 1) * rows]


def _stack_tiles(ref, lead, h):
    return jnp.concatenate([ref[lead + (h * GQA + g,)] for g in range(GQA)], axis=0)


def _cmp_prompt_kernel(kvc_ref, q_ref, near_ref, far_ref, w1_ref, b1_ref, w2k_ref, b2k_ref, w2v_ref, b2v_ref,
                       ov_ref, o_ref, sel_ref, kc, vc, *, n_sub, n_cmp, n_slc):
    qb = pl.program_id(1)

    @pl.when(qb == 0)
    def _():
        load = lambda h, j: kvc_ref[h, pl.ds(j, n_sub, stride=D_CMP), :]
        for h in range(KV_HEADS):
            k, v = _compress(load, h, n_sub, (w1_ref, b1_ref, w2k_ref, b2k_ref, w2v_ref, b2v_ref))
            kc[h] = k.astype(BF16)
            vc[h] = v.astype(BF16)

    row = lax.broadcasted_iota(jnp.int32, (Q_BLOCK, n_sub), 0)
    blk = lax.broadcasted_iota(jnp.int32, (Q_BLOCK, n_sub), 1)
    qpos = qb * Q_BLOCK + row
    visible = (blk * D_CMP + (L_CMP - 1) <= qpos) & (blk < n_cmp)
    near_lo = qb * (Q_BLOCK // D_CMP) - (MAX_DISTANCE + L_CMP - 1) // D_CMP
    mi = lax.broadcasted_iota(jnp.int32, (NEAR_W, n_sub), 0)
    ni = lax.broadcasted_iota(jnp.int32, (NEAR_W, n_sub), 1)
    place = jnp.where(ni - mi == near_lo, 1.0, 0.0)
    for h in range(KV_HEADS):
        bias = jnp.concatenate(
            [jnp.where(blk >= near_lo, _dot(near_ref[h * GQA + g], place, HIGHEST),
                       far_ref[h * GQA + g:h * GQA + g + 1, :]) for g in range(GQA)], axis=0)
        s = _dot_nt(_stack_heads(q_ref, h), kc[h]) + bias
        p = _softmax_rows(jnp.where(jnp.concatenate([visible] * GQA, axis=0), s, NEG))
        _unstack_heads(o_ref, h, _dot(p.astype(BF16), vc[h]), Q_BLOCK)
        pg = p[0:Q_BLOCK]
        for g in range(1, GQA):
            pg = pg + p[g * Q_BLOCK:(g + 1) * Q_BLOCK]
        sel_ref[0, h] = _select(_dot(pg, ov_ref[...], HIGHEST), qpos[:, 0:1], n_slc)


def _cmp_prompt(kvc, q, near, far, cmp_w, overlap):
    bsz, t, _ = q.shape
    n_sub = t // D_CMP
    n_cmp = n_sub - 1
    nqb = t // Q_BLOCK
    n_slc = t // L_SEL
    assert n_sub == LANES and n_slc <= LANES
    fixed = lambda a: pl.BlockSpec(a.shape, lambda b, i, _n=a.ndim: (0,) * _n)
    return pl.pallas_call(
        functools.partial(_cmp_prompt_kernel, n_sub=n_sub, n_cmp=n_cmp, n_slc=n_slc),
        grid=(bsz, nqb),
        in_specs=[pl.BlockSpec((KV_HEADS, t, KV_ROWS), lambda b, i: (0, b, 0)),
                  pl.BlockSpec((1, Q_BLOCK, ATT_WIDTH), lambda b, i: (b, i, 0)),
                  fixed(near), fixed(far)] + [fixed(a) for a in cmp_w] + [fixed(overlap)],
        out_specs=[pl.BlockSpec((1, Q_BLOCK, ATT_WIDTH), lambda b, i: (b, i, 0)),
                   pl.BlockSpec((1, KV_HEADS, Q_BLOCK, LANES), lambda b, i: (b, 0, i, 0))],
        out_shape=[jax.ShapeDtypeStruct((bsz, t, ATT_WIDTH), F32),
                   jax.ShapeDtypeStruct((bsz, KV_HEADS, t, LANES), F32)],
        scratch_shapes=[pltpu.VMEM((KV_HEADS, n_sub, HEAD_DIM), BF16),
                        pltpu.VMEM((KV_HEADS, n_sub, HEAD_DIM), BF16)],
        compiler_params=_cparams(("parallel", "arbitrary")),
        name="cmp_prompt",
    )(kvc, q, near, far, *cmp_w, overlap)


def _nsa_prompt_kernel(q_ref, kvs_ref, kvw_ref, sel_ref, eneg_ref, selt_ref, far_ref, wint_ref, os_ref, ow_ref,
                       kaug, vs, kw, vw, s_buf, mrun, lrun, acc):
    qb = pl.program_id(1)
    rows = GQA * Q_BLOCK

    @pl.when(qb == 0)
    def _():
        for h in range(KV_HEADS):
            lo = h * KV_ROWS
            kaug[h, 0:HEAD_DIM, :] = kvs_ref[0, lo:lo + HEAD_DIM, :].astype(BF16)
            kaug[h, HEAD_DIM:HEAD_DIM + SEL_PAD, :] = eneg_ref[...]
            vs[h] = kvs_ref[0, lo + HEAD_DIM:lo + KV_ROWS, :].astype(BF16)
            kw[h] = kvw_ref[0, lo:lo + HEAD_DIM, :].astype(BF16)
            vw[h] = kvw_ref[0, lo + HEAD_DIM:lo + KV_ROWS, :].astype(BF16)

    def tile(i):
        if isinstance(i, int):
            return pl.ds(i * Q_BLOCK, Q_BLOCK)
        return pl.ds(pl.multiple_of(i * Q_BLOCK, Q_BLOCK), Q_BLOCK)

    def score(qa, k_ref, h, kt, slot, bias):
        s = _dot(qa, k_ref[h, :, tile(kt)]) + bias
        s_buf[:, tile(slot)] = s
        mrun[...] = jnp.maximum(mrun[...], s)

    def accumulate(m, v_ref, h, kt, slot):
        p = jnp.exp(s_buf[:, tile(slot)] - m)
        lrun[...] = lrun[...] + p
        acc[...] = acc[...] + _dot_nt(p.astype(BF16), v_ref[h, :, tile(kt)])

    def reset():
        mrun[...] = jnp.full((rows, LANES), NEG, F32)
        lrun[...] = jnp.zeros((rows, LANES), F32)
        acc[...] = jnp.zeros((rows, HEAD_DIM), F32)

    def finish(o_ref, h):
        _unstack_heads(o_ref, h, acc[...] / jnp.sum(lrun[...], -1, keepdims=True), Q_BLOCK)

    nlead = WINDOW // Q_BLOCK
    for h in range(KV_HEADS):
        q4 = _stack_heads(q_ref, h)
        notsel = (1.0 - sel_ref[0, h][:, 0:SEL_PAD]).astype(BF16)
        qa = jnp.concatenate([q4, jnp.concatenate([notsel] * GQA, axis=0)], axis=1)

        reset()
        far = jnp.concatenate([jnp.broadcast_to(far_ref[h * GQA + g:h * GQA + g + 1, :], (Q_BLOCK, LANES))
                               for g in range(GQA)], axis=0)

        def far_body(kt, carry):
            score(qa, kaug, h, kt, kt, far)
            return carry

        lax.fori_loop(0, jnp.maximum(qb - 1, 0), far_body, 0)

        @pl.when(qb >= 1)
        def _():
            score(qa, kaug, h, qb - 1, qb - 1, _stack_tiles(selt_ref, (1,), h))

        score(qa, kaug, h, qb, qb, _stack_tiles(selt_ref, (0,), h))
        m = jnp.max(mrun[...], -1, keepdims=True)

        def pv_body(kt, carry):
            accumulate(m, vs, h, kt, kt)
            return carry

        lax.fori_loop(0, qb + 1, pv_body, 0)
        finish(os_ref, h)

        reset()
        for r in range(nlead, -1, -1):
            @pl.when(qb >= r)
            def _(r=r):
                score(q4, kw, h, qb - r, r, _stack_tiles(wint_ref, (r,), h))
        m = jnp.max(mrun[...], -1, keepdims=True)
        for r in range(nlead, -1, -1):
            @pl.when(qb >= r)
            def _(r=r):
                accumulate(m, vw, h, qb - r, r)
        finish(ow_ref, h)


def _nsa_prompt(q, kvs_t, kvw_t, sel, eneg, sel_tiles, far, win_tiles):
    bsz, _, t = kvs_t.shape
    nqb = t // Q_BLOCK
    qblk = lambda b, i: (b, i, 0)
    per_b = lambda b, i: (b, 0, 0)
    fixed = lambda a: pl.BlockSpec(a.shape, lambda b, i, _n=a.ndim: (0,) * _n)
    rows = GQA * Q_BLOCK
    return pl.pallas_call(
        _nsa_prompt_kernel,
        grid=(bsz, nqb),
        in_specs=[pl.BlockSpec((1, Q_BLOCK, ATT_WIDTH), qblk), pl.BlockSpec((1, KV_COLS, t), per_b),
                  pl.BlockSpec((1, KV_COLS, t), per_b),
                  pl.BlockSpec((1, KV_HEADS, Q_BLOCK, LANES), lambda b, i: (b, 0, i, 0)),
                  fixed(eneg), fixed(sel_tiles), fixed(far), fixed(win_tiles)],
        out_specs=[pl.BlockSpec((1, Q_BLOCK, ATT_WIDTH), qblk), pl.BlockSpec((1, Q_BLOCK, ATT_WIDTH), qblk)],
        out_shape=[jax.ShapeDtypeStruct((bsz, t, ATT_WIDTH), F32)] * 2,
        scratch_shapes=[pltpu.VMEM((KV_HEADS, HEAD_DIM + SEL_PAD, t), BF16),
                        pltpu.VMEM((KV_HEADS, HEAD_DIM, t), BF16),
                        pltpu.VMEM((KV_HEADS, HEAD_DIM, t), BF16),
                        pltpu.VMEM((KV_HEADS, HEAD_DIM, t), BF16),
                        pltpu.VMEM((rows, t), F32), pltpu.VMEM((rows, LANES), F32),
                        pltpu.VMEM((rows, LANES), F32), pltpu.VMEM((rows, HEAD_DIM), F32)],
        compiler_params=_cparams(("parallel", "arbitrary")),
        name="nsa_prompt",
    )(q, kvs_t, kvw_t, sel, eneg, sel_tiles, far, win_tiles)


def _cmp_sample_kernel(pt_ref, *refs, n_pages, n_slc, past, t):
    pages = refs[:n_pages]
    (q_ref, bias_ref, w1_ref, b1_ref, w2k_ref, b2k_ref, w2v_ref, b2v_ref, ov_ref, o_ref, sel_ref,
     xbuf) = refs[n_pages:]
    for k in range(n_pages):
        for h in range(KV_HEADS):
            xbuf[h, k * PAGE_SIZE:(k + 1) * PAGE_SIZE, :] = pages[k][0, h * KV_ROWS:(h + 1) * KV_ROWS, :].T
    n_sub = n_pages * PAGE_SIZE // D_CMP
    load = lambda h, j: xbuf[h, pl.ds(j, n_sub, stride=D_CMP), :]
    qpos = past + lax.broadcasted_iota(jnp.int32, (t, 1), 0)
    for h in range(KV_HEADS):
        k, v = _compress(load, h, n_sub, (w1_ref, b1_ref, w2k_ref, b2k_ref, w2v_ref, b2v_ref))
        qh = _stack_heads(q_ref, h).astype(BF16)
        p = _softmax_rows(_dot_nt(qh, k.astype(BF16)) + bias_ref[h])
        _unstack_heads(o_ref, h, _dot(p.astype(BF16), v.astype(BF16)), t)
        pg = p[0:t]
        for g in range(1, GQA):
            pg = pg + p[g * t:(g + 1) * t]
        sel_ref[0, h] = _select(_dot(pg, ov_ref[...], HIGHEST), qpos, n_slc)


def _page_specs(n_pages):
    return [pl.BlockSpec((1, KV_COLS, PAGE_SIZE), lambda b, pt, k=k: (pt[b, k], 0, 0)) for k in range(n_pages)]


def _cmp_sample(pages, page_table, q, bias_cs, cmp_w, overlap, n_slc):
    bsz, n_pages = page_table.shape
    t = q.shape[1]
    width = overlap.shape[1]
    fixed = lambda a: pl.BlockSpec(a.shape, lambda b, pt, _n=a.ndim: (0,) * _n)
    grid_spec = pltpu.PrefetchScalarGridSpec(
        num_scalar_prefetch=1,
        grid=(bsz,),
        in_specs=_page_specs(n_pages) + [pl.BlockSpec((1, t, ATT_WIDTH), lambda b, pt: (b, 0, 0)), fixed(bias_cs)]
        + [fixed(a) for a in cmp_w] + [fixed(overlap)],
        out_specs=[pl.BlockSpec((1, t, ATT_WIDTH), lambda b, pt: (b, 0, 0)),
                   pl.BlockSpec((1, KV_HEADS, t, width), lambda b, pt: (b, 0, 0, 0))],
        scratch_shapes=[pltpu.VMEM((KV_HEADS, n_pages * PAGE_SIZE, KV_ROWS), F32)])
    return pl.pallas_call(
        functools.partial(_cmp_sample_kernel, n_pages=n_pages, n_slc=n_slc, past=n_pages * PAGE_SIZE, t=t),
        grid_spec=grid_spec,
        out_shape=[jax.ShapeDtypeStruct((bsz, t, ATT_WIDTH), F32),
                   jax.ShapeDtypeStruct((bsz, KV_HEADS, t, width), F32)],
        compiler_params=_cparams(("parallel",)),
        name="cmp_sample",
    )(page_table, *([pages] * n_pages), q, bias_cs, *cmp_w, overlap)


def _joint_attend(s_past, vt_past, s_new, v_new):
    m = jnp.maximum(jnp.max(s_past, -1, keepdims=True), jnp.max(s_new, -1, keepdims=True))
    e_past = jnp.exp(s_past - m)
    e_new = jnp.exp(s_new - m)
    den = jnp.sum(e_past, -1, keepdims=True) + jnp.sum(e_new, -1, keepdims=True)
    acc = _dot_nt(e_past.astype(BF16), vt_past) + _dot(e_new.astype(BF16), v_new)
    return acc / den


def _pad_new_rows(new_ref, t):
    return jnp.concatenate([new_ref[0], jnp.zeros((LANES - t, KV_COLS), F32)], axis=0)


def _sel_sample_kernel(pt_ref, *refs, n_pages, t):
    pages = refs[:n_pages]
    (new_ref, q_ref, sel_ref, far_ref, near_ref, biasn_ref, o_ref, kvb, mask) = refs[n_pages:]
    for k in range(n_pages):
        kvb[:, k * PAGE_SIZE:(k + 1) * PAGE_SIZE] = pages[k][0].astype(BF16)
    kv_new = _pad_new_rows(new_ref, t).astype(BF16)
    rows = GQA * t
    lane = lax.broadcasted_iota(jnp.int32, (rows, LANES), 1)
    per_tile = LANES // L_SEL
    for h in range(KV_HEADS):
        lo = h * KV_ROWS
        sel4 = jnp.concatenate([sel_ref[0, h]] * GQA, axis=0)

        def tile_mask(k):
            cols = [jnp.broadcast_to(sel4[:, per_tile * k + i:per_tile * k + i + 1], (rows, LANES))
                    for i in range(per_tile)]
            m = cols[-1]
            for i in range(per_tile - 2, -1, -1):
                m = jnp.where(lane < (i + 1) * L_SEL, cols[i], m)
            return (m - 1.0) * (-NEG)

        far = jnp.concatenate([jnp.broadcast_to(far_ref[h * GQA + g:h * GQA + g + 1, :], (t, LANES))
                               for g in range(GQA)], axis=0)
        for k in range(n_pages - 1):
            mask[:, k * LANES:(k + 1) * LANES] = tile_mask(k) + far
        mask[:, (n_pages - 1) * LANES:n_pages * LANES] = tile_mask(n_pages - 1) + near_ref[h]
        qh = _stack_heads(q_ref, h).astype(BF16)
        s_past = _dot(qh, kvb[lo:lo + HEAD_DIM, :]) + mask[...]
        s_new = _dot_nt(qh, kv_new[:, lo:lo + HEAD_DIM]) + biasn_ref[h] + tile_mask(n_pages)
        o = _joint_attend(s_past, kvb[lo + HEAD_DIM:lo + KV_ROWS, :], s_new, kv_new[:, lo + HEAD_DIM:lo + KV_ROWS])
        _unstack_heads(o_ref, h, o, t)


def _sel_sample(pages, page_table, kvs_new, q, sel, far, near, bias_new):
    bsz, n_pages = page_table.shape
    t = q.shape[1]
    past = n_pages * PAGE_SIZE
    width = sel.shape[-1]
    assert (n_pages + 1) * (LANES // L_SEL) <= width
    per_b = lambda b, pt: (b, 0, 0)
    fixed = lambda a: pl.BlockSpec(a.shape, lambda b, pt, _n=a.ndim: (0,) * _n)
    grid_spec = pltpu.PrefetchScalarGridSpec(
        num_scalar_prefetch=1,
        grid=(bsz,),
        in_specs=_page_specs(n_pages) + [
            pl.BlockSpec((1, t, KV_COLS), per_b), pl.BlockSpec((1, t, ATT_WIDTH), per_b),
            pl.BlockSpec((1, KV_HEADS, t, width), lambda b, pt: (b, 0, 0, 0)),
            fixed(far), fixed(near), fixed(bias_new)],
        out_specs=pl.BlockSpec((1, t, ATT_WIDTH), per_b),
        scratch_shapes=[pltpu.VMEM((KV_COLS, past), BF16), pltpu.VMEM((GQA * t, past), F32)])
    return pl.pallas_call(
        functools.partial(_sel_sample_kernel, n_pages=n_pages, t=t),
        grid_spec=grid_spec,
        out_shape=jax.ShapeDtypeStruct((bsz, t, ATT_WIDTH), F32),
        compiler_params=_cparams(("parallel",)),
        name="sel_sample",
    )(page_table, *([pages] * n_pages), kvs_new, q, sel, far, near, bias_new)


def _win_sample_kernel(buf_ref, new_ref, q_ref, bias_ref, biasn_ref, o_ref, win_ref, *, t):
    buf = buf_ref[0]
    w = buf.shape[1]
    new = _pad_new_rows(new_ref, t)
    shifted = pltpu.roll(buf, w - t, 1)
    tail = pltpu.roll(new.T, LANES - t, 1)
    lane = lax.broadcasted_iota(jnp.int32, (KV_COLS, LANES), 1)
    win_ref[0, :, 0:w - LANES] = shifted[:, 0:w - LANES]
    win_ref[0, :, w - LANES:w] = jnp.where(lane >= LANES - t, tail, shifted[:, w - LANES:w])
    kvb = buf.astype(BF16)
    kv_new = new.astype(BF16)
    for h in range(KV_HEADS):
        lo = h * KV_ROWS
        qh = _stack_heads(q_ref, h).astype(BF16)
        s_past = _dot(qh, kvb[lo:lo + HEAD_DIM, :]) + bias_ref[h]
        s_new = _dot_nt(qh, kv_new[:, lo:lo + HEAD_DIM]) + biasn_ref[h]
        o = _joint_attend(s_past, kvb[lo + HEAD_DIM:lo + KV_ROWS, :], s_new, kv_new[:, lo + HEAD_DIM:lo + KV_ROWS])
        _unstack_heads(o_ref, h, o, t)


def _win_sample(buf_t, kvw_new, q, bias_buf, bias_new):
    bsz, _, w = buf_t.shape
    t = q.shape[1]
    per_b = lambda b: (b, 0, 0)
    fixed = lambda a: pl.BlockSpec(a.shape, lambda b, _n=a.ndim: (0,) * _n)
    return pl.pallas_call(
        functools.partial(_win_sample_kernel, t=t),
        grid=(bsz,),
        in_specs=[pl.BlockSpec((1, KV_COLS, w), per_b), pl.BlockSpec((1, t, KV_COLS), per_b),
                  pl.BlockSpec((1, t, ATT_WIDTH), per_b), fixed(bias_buf), fixed(bias_new)],
        out_specs=[pl.BlockSpec((1, t, ATT_WIDTH), per_b), pl.BlockSpec((1, KV_COLS, w), per_b)],
        out_shape=[jax.ShapeDtypeStruct((bsz, t, ATT_WIDTH), F32),
                   jax.ShapeDtypeStruct((bsz, KV_COLS, w), F32)],
        compiler_params=_cparams(("parallel",)),
        name="win_sample",
    )(buf_t, kvw_new, q, bias_buf, bias_new)


def _combine_kernel(x_ref, y_ref, oc_ref, os_ref, ow_ref, sm_ref, eg_ref, eb_ref, ex_ref, ag_ref, wo_ref, g1_ref,
                    b1_ref, h_ref):
    xn = _layer_norm(x_ref[...], eg_ref[...], eb_ref[...])
    gates = _sigmoid(sm_ref[...])
    o = jnp.zeros(oc_ref.shape, F32)
    for br, ref in enumerate((oc_ref, os_ref, ow_ref)):
        o = o + _dot(gates, ex_ref[br], HIGHEST) * ref[...]
    rms = lax.rsqrt(jnp.mean(o * o, -1, keepdims=True) + EPS)
    att = (o * rms * ag_ref[...]).astype(BF16)
    mix = _dot(jnp.concatenate([y_ref[...], att], axis=1), wo_ref[...])
    h_ref[...] = _layer_norm(ALPHA * xn + mix, g1_ref[...], b1_ref[...])


def _combine(x2d, y, oc, os_, ow, sm, eg, eb, gate_expand, ag, wo, g1, b1, tm):
    n = x2d.shape[0]
    tm = _row_tile(n, tm)
    row = lambda i: (i, 0)
    fixed = lambda a: pl.BlockSpec(a.shape, lambda i, _n=a.ndim: (0,) * _n)
    att = pl.BlockSpec((tm, ATT_WIDTH), row)
    return pl.pallas_call(
        _combine_kernel,
        grid=(n // tm,),
        in_specs=[pl.BlockSpec((tm, D_MODEL), row), pl.BlockSpec((tm, SSM_WIDTH), row), att, att, att,
                  pl.BlockSpec((tm, LANES), row), fixed(eg), fixed(eb), fixed(gate_expand), fixed(ag), fixed(wo),
                  fixed(g1), fixed(b1)],
        out_specs=pl.BlockSpec((tm, D_MODEL), row),
        out_shape=jax.ShapeDtypeStruct((n, D_MODEL), F32),
        compiler_params=_cparams(("parallel",)),
        name="combine",
    )(x2d, y, oc, os_, ow, sm, eg, eb, gate_expand, ag, wo, g1, b1)


def _ffn_kernel(h_ref, wu_ref, wd_ref, g_ref, b_ref, o_ref, acc, *, nk):
    k = pl.program_id(1)
    h = h_ref[...]
    u = jnp.maximum(_dot(h.astype(BF16), wu_ref[...]), 0.0)
    part = _dot((u * u).astype(BF16), wd_ref[...])

    @pl.when(k == 0)
    def _():
        acc[...] = part

    @pl.when(k > 0)
    def _():
        acc[...] = acc[...] + part

    @pl.when(k == nk - 1)
    def _():
        o_ref[...] = _layer_norm(ALPHA * h + acc[...], g_ref[...], b_ref[...])


def _ffn(h2d, wu, wd, g, b, tm, tf):
    n = h2d.shape[0]
    tm = _row_tile(n, tm)
    nk = D_FF // tf
    vec = pl.BlockSpec((1, D_MODEL), lambda i, k: (0, 0))
    return pl.pallas_call(
        functools.partial(_ffn_kernel, nk=nk),
        grid=(n // tm, nk),
        in_specs=[pl.BlockSpec((tm, D_MODEL), lambda i, k: (i, 0)),
                  pl.BlockSpec((D_MODEL, tf), lambda i, k: (0, k)),
                  pl.BlockSpec((tf, D_MODEL), lambda i, k: (k, 0)), vec, vec],
        out_specs=pl.BlockSpec((tm, D_MODEL), lambda i, k: (i, 0)),
        out_shape=jax.ShapeDtypeStruct((n, D_MODEL), F32),
        scratch_shapes=[pltpu.VMEM((tm, D_MODEL), F32)],
        compiler_params=_cparams(("parallel", "arbitrary")),
        name="ffn",
    )(h2d, wu, wd, g, b)


def _bucket_np(dist):
    d = np.maximum(dist, 0)
    exact = N_BUCKETS // 2
    far = exact + (np.log(np.maximum(d, 1).astype(np.float32) / np.float32(exact))
                   / np.float32(math.log(MAX_DISTANCE / exact)) * (N_BUCKETS - exact)).astype(np.int32)
    return np.where(d < exact, d, np.minimum(far, N_BUCKETS - 1)).astype(np.int32)


def _bias_lookup(tbl, dist, mask=None):
    dist = np.asarray(dist)
    onehot = np.eye(N_BUCKETS, dtype=np.float32)[_bucket_np(dist).reshape(-1)]
    b = jnp.dot(jnp.asarray(onehot), tbl, precision=HIGHEST).T.reshape((ATT_HEADS,) + dist.shape)
    return b if mask is None else jnp.where(jnp.asarray(mask)[None], b, NEG)


def _toeplitz_tile(tbl, offset, mask):
    period = 2 * Q_BLOCK
    k = np.arange(period)
    vals = _bias_lookup(tbl, offset - np.where(k < Q_BLOCK, k, k - period))
    tiled = jnp.tile(vals, (1, Q_BLOCK))[:, :Q_BLOCK * (period - 1)]
    t = tiled.reshape(ATT_HEADS, Q_BLOCK, period - 1)[:, :, :Q_BLOCK]
    return jnp.where(jnp.asarray(mask)[None], t, NEG)


def _stack_gt(tab, t):
    return tab.reshape(KV_HEADS, GQA * t, tab.shape[-1])


def _far_rows(tbl):
    return jnp.broadcast_to(tbl[N_BUCKETS - 1][:, None], (ATT_HEADS, LANES))


def _overlap(n_cmp_pad, n_cmp, width, n_slc):
    i = np.arange(n_cmp_pad)[:, None]
    j = np.arange(width)[None, :]
    ov = (i * D_CMP < (j + 1) * L_SEL) & (i * D_CMP + L_CMP > j * L_SEL) & (i < n_cmp) & (j < n_slc)
    return jnp.asarray(ov.astype(np.float32))


def _prep_cmp_weights(w1, b1, w2, b2):
    eye = jnp.eye(2, dtype=F32)
    w1r = (w1[:, :, :, :, None, :] * eye[None, None, :, None, :, None]).transpose(1, 2, 3, 0, 4, 5)
    w1r = w1r.reshape(D_CMP, KV_ROWS, 2 * KV_ROWS).astype(BF16)
    zero = jnp.zeros((CMP_HID, HEAD_DIM), F32)
    w2k = jnp.concatenate([w2[0], zero], axis=0).astype(BF16)
    w2v = jnp.concatenate([zero, w2[1]], axis=0).astype(BF16)
    return (w1r, b1.reshape(1, 2 * CMP_HID), w2k, b2[0].reshape(1, HEAD_DIM), w2v, b2[1].reshape(1, HEAD_DIM))


def _prep_w_in(w_in):
    sizes = (SSM_WIDTH, CONV_DIM, SSM_HEADS, ATT_WIDTH, KV_COLS, KV_COLS, KV_COLS)
    z, xbc, dt, q, kvc, kvs, kvw, gates = jnp.split(w_in, np.cumsum(sizes).tolist(), axis=1)
    small = jnp.concatenate([dt, gates], axis=1)
    small = jnp.pad(small, ((0, 0), (0, LANES - small.shape[1])))
    return jnp.concatenate([z, xbc, q * ATT_SCALE, kvc, kvs, kvw, small], axis=1).astype(BF16)


def _gate_expand():
    ex = np.zeros((N_BRANCH, LANES, ATT_WIDTH), np.float32)
    for br in range(N_BRANCH):
        for hd in range(ATT_HEADS):
            ex[br, SSM_HEADS + br * ATT_HEADS + hd, hd * HEAD_DIM:(hd + 1) * HEAD_DIM] = 1.0
    return jnp.asarray(ex)


def _feature_major(a):
    lead = a.shape[:-4]
    rows = a.shape[-4]
    return jnp.moveaxis(a.reshape(lead + (rows, KV_COLS)), -2, -1)


def _row_major6(a_t):
    bsz, _, rows = a_t.shape
    return jnp.moveaxis(a_t, 1, 2).reshape(1, bsz, rows, KV_HEADS, 2, HEAD_DIM)


def kernel(x_prompt, x_sample, cache_cmp_kv, cache_slc_kv, cache_win_kv, state_conv, state_ssm, page_table,
           rel_bias_table, emb_ln_g, emb_ln_b, w_in, conv_w, conv_b, dt_bias, a_log, d_skip, ssm_norm_g,
           cmp_w1, cmp_b1, cmp_w2, cmp_b2, att_norm_g, w_out, ln1_g, ln1_b, w_up, w_down, ln2_g, ln2_b):
    assert w_in.shape[0] == DEPTH
    bp, tp, _ = x_prompt.shape
    bs, ts, _ = x_sample.shape
    n_pages = page_table.shape[1]
    past = n_pages * PAGE_SIZE
    w_buf = cache_win_kv.shape[2]
    assert ts < D_CMP and ts % 8 == 0 and w_buf == WINDOW and past >= WINDOW and tp >= WINDOW
    tbl = rel_bias_table
    vec = lambda v: v.reshape(1, -1)

    w_proj = _prep_w_in(w_in[0])
    cmp_w = _prep_cmp_weights(cmp_w1[0], cmp_b1[0], cmp_w2[0], cmp_b2[0])
    wo = w_out[0].astype(BF16)
    wu = w_up[0].astype(BF16)
    wd = w_down[0].astype(BF16)
    eg, eb = vec(emb_ln_g), vec(emb_ln_b)
    gate_expand = _gate_expand()
    far = _far_rows(tbl)

    def trunk_tail(x2d, y, oc, os_, ow, sm):
        h = _combine(x2d, y, oc, os_, ow, sm, eg, eb, gate_expand, vec(att_norm_g[0]), wo, vec(ln1_g[0]),
                     vec(ln1_b[0]), 256)
        return _ffn(h, wu, wd, vec(ln2_g[0]), vec(ln2_b[0]), 1024, D_FF // 4)

    ssm_w = (conv_w[0], conv_b[0], dt_bias[0], a_log[0], d_skip[0], ssm_norm_g[0])

    xp2 = x_prompt.reshape(bp * tp, D_MODEL)
    z, xbc, q, kvc, kvc_t, kvs_t, kvw_t, sm = _proj(xp2, eg, eb, w_proj, BF16, 512, seq=(bp, tp))
    r3 = lambda a: a.reshape(bp, tp, a.shape[-1])
    xbc3 = r3(xbc)
    y_ssm, h_new = _ssm(r3(z), xbc3, r3(sm), jnp.zeros((bp, CONV_WIDTH - 1, CONV_DIM), F32),
                        jnp.zeros((bp, SSM_HEADS, SSM_HEAD_DIM, D_STATE), F32), *ssm_w)
    n_sub = tp // D_CMP
    n_slc = tp // L_SEL
    band = (np.arange(Q_BLOCK)[:, None] - D_CMP * np.arange(NEAR_W)[None, :]
            + D_CMP * ((MAX_DISTANCE + L_CMP - 1) // D_CMP) - (L_CMP - 1))
    oc, sel = _cmp_prompt(kvc, r3(q), _bias_lookup(tbl, band), far, cmp_w,
                          _overlap(n_sub, n_sub - 1, LANES, n_slc))
    ii = np.arange(Q_BLOCK)[:, None] - np.arange(Q_BLOCK)[None, :]
    sel_tiles = jnp.stack([_toeplitz_tile(tbl, Q_BLOCK * r, ii + Q_BLOCK * r >= 0) for r in range(2)])
    nlead = WINDOW // Q_BLOCK
    win_tiles = jnp.stack([_toeplitz_tile(tbl, Q_BLOCK * r, (ii + Q_BLOCK * r >= 0) & (ii + Q_BLOCK * r < WINDOW))
                           for r in range(nlead + 1)])
    assert n_slc <= SEL_PAD
    eneg = jnp.asarray(np.where(np.arange(SEL_PAD)[:, None] == (np.arange(tp) // L_SEL)[None, :], NEG, 0.0), BF16)
    os_, ow = _nsa_prompt(r3(q), kvs_t, kvw_t, sel, eneg, sel_tiles, far, win_tiles)
    f2 = lambda a: a.reshape(bp * tp, a.shape[-1])
    y_prompt = trunk_tail(xp2, f2(y_ssm), f2(oc), f2(os_), f2(ow), sm).reshape(bp, tp, D_MODEL)
    w = min(WINDOW, tp)
    prompt_state = (_row_major6(kvc_t), _row_major6(kvs_t), _row_major6(kvw_t[:, :, tp - w:]),
                    xbc3[:, tp - (CONV_WIDTH - 1):][None], h_new[None])

    xs2 = x_sample.reshape(bs * ts, D_MODEL)
    z, xbc, q, kvc, kvs, kvw, sm = _proj(xs2, eg, eb, w_proj, F32, 512)
    r3 = lambda a: a.reshape(bs, ts, a.shape[-1])
    xbc3 = r3(xbc)
    y_ssm, h_new = _ssm(r3(z), xbc3, r3(sm), state_conv[0], state_ssm[0], *ssm_w)
    n_sub = past // D_CMP
    n_cmp = n_sub - 1
    n_slc = -(-(past + ts) // L_SEL)
    width = -(-n_slc // LANES) * LANES
    qpos = past + np.arange(ts)
    dist_c = qpos[:, None] - (np.arange(n_sub) * D_CMP + L_CMP - 1)[None, :]
    bias_cs = _stack_gt(_bias_lookup(tbl, dist_c, (dist_c >= 0) & (np.arange(n_sub) < n_cmp)[None, :]), ts)
    oc, sel = _cmp_sample(_feature_major(cache_cmp_kv[0]), page_table, r3(q), bias_cs, cmp_w,
                          _overlap(n_sub, n_cmp, width, n_slc), n_slc)
    dist_last = qpos[:, None] - (past - PAGE_SIZE + np.arange(PAGE_SIZE))[None, :]
    dist_n = np.arange(ts)[:, None] - np.arange(LANES)[None, :]
    bias_new = _stack_gt(_bias_lookup(tbl, dist_n, (dist_n >= 0) & (np.arange(LANES) < ts)[None, :]), ts)
    os_ = _sel_sample(_feature_major(cache_slc_kv[0]), page_table, r3(kvs), r3(q), sel, far,
                      _stack_gt(_bias_lookup(tbl, dist_last), ts), bias_new)
    dist_w = qpos[:, None] - (past - w_buf + np.arange(w_buf))[None, :]
    bias_wb = _stack_gt(_bias_lookup(tbl, dist_w, (dist_w >= 0) & (dist_w < WINDOW)), ts)
    ow, win_new_t = _win_sample(_feature_major(cache_win_kv[0]), r3(kvw), r3(q), bias_wb, bias_new)
    f2 = lambda a: a.reshape(bs * ts, a.shape[-1])
    y_sample = trunk_tail(xs2, f2(y_ssm), f2(oc), f2(os_), f2(ow), sm).reshape(bs, ts, D_MODEL)
    kv6 = lambda a: a.reshape(1, bs, ts, KV_HEADS, 2, HEAD_DIM)
    sample_state = (kv6(kvc), kv6(kvs), _row_major6(win_new_t),
                    xbc3[:, ts - (CONV_WIDTH - 1):][None], h_new[None])

    return (y_prompt, y_sample) + prompt_state + sample_state
```

```python
import functools
import math

import numpy as np
import jax
import jax.numpy as jnp
from jax import lax
from jax.experimental import pallas as pl
from jax.experimental.pallas import tpu as pltpu

F32 = jnp.float32
BF16 = jnp.bfloat16
HIGHEST = lax.Precision.HIGHEST

D_MODEL = 1024
SSM_HEADS = 8
SSM_HEAD_DIM = 64
SSM_WIDTH = SSM_HEADS * SSM_HEAD_DIM
SSM_GROUPS = 2
D_STATE = 128
CONV_WIDTH = 4
CONV_DIM = SSM_WIDTH + 2 * SSM_GROUPS * D_STATE
SSD_CHUNK = 128
ATT_HEADS = 8
KV_HEADS = 2
GQA = ATT_HEADS // KV_HEADS
HEAD_DIM = 64
ATT_WIDTH = ATT_HEADS * HEAD_DIM
KV_COLS = KV_HEADS * 2 * HEAD_DIM
D_CMP = 16
L_CMP = 2 * D_CMP
CMP_HID = 64
L_SEL = 64
TOP_N = 16
WINDOW = 512
Q_BLOCK = 128
N_BRANCH = 3
FORCED_SCORE = 1e4
N_BUCKETS = 32
MAX_DISTANCE = 128
D_FF = 4 * D_MODEL
DEPTH = 1
ALPHA = (2 * DEPTH) ** 0.25
ATT_SCALE = HEAD_DIM ** -0.5
EPS = 1e-5
PAGE_SIZE = 128

LANES = 128
NEG = -1e30
MASKED_BELOW = -1e29
VMEM_LIMIT = 48 * 1024 * 1024
KV_ROWS = 2 * HEAD_DIM
SEL_PAD = 32
NEAR_W = 32
SEL_CHUNK = 512

_OFF_Z = 0
_OFF_XBC = _OFF_Z + SSM_WIDTH
_OFF_Q = _OFF_XBC + CONV_DIM
_OFF_KVC = _OFF_Q + ATT_WIDTH
_OFF_KVS = _OFF_KVC + KV_COLS
_OFF_KVW = _OFF_KVS + KV_COLS
_OFF_SM = _OFF_KVW + KV_COLS
_N_PROJ = _OFF_SM + LANES


def _cparams(sem):
    return pltpu.CompilerParams(dimension_semantics=sem, vmem_limit_bytes=VMEM_LIMIT)


def _row_tile(n, preferred):
    tm = min(n, preferred)
    assert n % tm == 0 and tm % 8 == 0
    return tm


def _dot(a, b, precision=None):
    return jnp.dot(a, b, preferred_element_type=F32, precision=precision)


def _dot_nt(a, b):
    return lax.dot_general(a, b, (((1,), (1,)), ((), ())), preferred_element_type=F32)


def _layer_norm(x, g, b):
    mu = jnp.mean(x, -1, keepdims=True)
    xc = x - mu
    var = jnp.mean(xc * xc, -1, keepdims=True)
    return xc * lax.rsqrt(var + EPS) * g + b


def _sigmoid(x):
    return 1.0 / (1.0 + jnp.exp(-x))


def _softplus(x):
    return jnp.maximum(x, 0.0) + jnp.log(1.0 + jnp.exp(-jnp.abs(x)))


def _gelu_tanh(x):
    c = math.sqrt(2.0 / math.pi)
    return 0.5 * x * (1.0 + jnp.tanh(c * (x + 0.044715 * (x * x * x))))


def _proj_kernel(x_ref, g_ref, b_ref, w_ref, z_ref, xbc_ref, q_ref, kvc_ref, *rest, feature_major):
    xn = _layer_norm(x_ref[...], g_ref[...], b_ref[...]).astype(BF16)

    def mm(lo, hi):
        return _dot(xn, w_ref[:, lo:hi])

    z_ref[...] = mm(_OFF_Z, _OFF_XBC)
    xbc_ref[...] = mm(_OFF_XBC, _OFF_Q)
    q_ref[...] = mm(_OFF_Q, _OFF_KVC).astype(q_ref.dtype)
    kvc = mm(_OFF_KVC, _OFF_KVS)
    if feature_major:
        kvct_ref, kvst_ref, kvwt_ref, sm_ref = rest
        for h in range(KV_HEADS):
            kvc_ref[h] = kvc[:, h * KV_ROWS:(h + 1) * KV_ROWS]
        kvct_ref[0] = kvc.T
        kvst_ref[0] = mm(_OFF_KVS, _OFF_KVW).T
        kvwt_ref[0] = mm(_OFF_KVW, _OFF_SM).T
    else:
        kvs_ref, kvw_ref, sm_ref = rest
        kvc_ref[...] = kvc
        kvs_ref[...] = mm(_OFF_KVS, _OFF_KVW)
        kvw_ref[...] = mm(_OFF_KVW, _OFF_SM)
    sm_ref[...] = mm(_OFF_SM, _N_PROJ)


def _proj(x2d, g, b, w, q_dtype, tm, seq=None):
    n = x2d.shape[0]
    tm = _row_tile(n, tm)
    row = lambda i: (i, 0)
    fixed = lambda i: (0, 0)
    rm = lambda wd, dt: (pl.BlockSpec((tm, wd), row), jax.ShapeDtypeStruct((n, wd), dt))
    outs = [rm(SSM_WIDTH, F32), rm(CONV_DIM, F32), rm(ATT_WIDTH, q_dtype)]
    if seq is None:
        outs += [rm(KV_COLS, F32), rm(KV_COLS, F32), rm(KV_COLS, F32)]
    else:
        outs.append((pl.BlockSpec((KV_HEADS, tm, KV_ROWS), lambda i: (0, i, 0)),
                     jax.ShapeDtypeStruct((KV_HEADS, n, KV_ROWS), F32)))
        bsz, t = seq
        assert t % tm == 0 and tm % LANES == 0
        per = t // tm
        fm = (pl.BlockSpec((1, KV_COLS, tm), lambda i: (i // per, 0, i % per)),
              jax.ShapeDtypeStruct((bsz, KV_COLS, t), F32))
        outs += [fm, fm, fm]
    outs.append(rm(LANES, F32))
    return pl.pallas_call(
        functools.partial(_proj_kernel, feature_major=seq is not None),
        grid=(n // tm,),
        in_specs=[pl.BlockSpec((tm, D_MODEL), row), pl.BlockSpec((1, D_MODEL), fixed),
                  pl.BlockSpec((1, D_MODEL), fixed), pl.BlockSpec((D_MODEL, _N_PROJ), fixed)],
        out_specs=[o[0] for o in outs],
        out_shape=[o[1] for o in outs],
        compiler_params=_cparams(("parallel",)),
        name="proj",
    )(x2d, g, b, w)


def _ssm_kernel(z_ref, xbc_ref, sm_ref, hist_ref, h0_ref, cw_ref, cb_ref, dtb_ref, alog_ref, dskip_ref,
                ng_ref, e_ref, et_ref, y_ref, hfin_ref, xext, state, *, tb, l, nc):
    c = pl.program_id(1)

    @pl.when(c == 0)
    def _():
        xext[0:8, :] = jnp.zeros((8, CONV_DIM), F32)
        xext[8 - (CONV_WIDTH - 1):8, :] = hist_ref[0]
        if tb < l:
            xext[8 + tb:8 + l, :] = jnp.zeros((l - tb, CONV_DIM), F32)
        state[...] = h0_ref[0].reshape(SSM_WIDTH, D_STATE)

    xext[8:8 + tb, :] = xbc_ref[0]
    conv = cb_ref[...]
    for k in range(CONV_WIDTH):
        lo = 8 - (CONV_WIDTH - 1) + k
        conv = conv + cw_ref[k:k + 1, :] * xext[lo:lo + l, :]
    xc = conv * _sigmoid(conv)
    xext[0:8, :] = xext[tb:tb + 8, :]

    dt = _softplus(sm_ref[0] + dtb_ref[...])
    if tb < l:
        dt = jnp.concatenate([dt, jnp.zeros((l - tb, LANES), F32)], axis=0)
    a = dt * (-jnp.exp(alog_ref[...]))
    ri = lax.broadcasted_iota(jnp.int32, (l, l), 0)
    ci = lax.broadcasted_iota(jnp.int32, (l, l), 1)
    tril = ri >= ci
    a_cs = _dot(jnp.where(tril, 1.0, 0.0), a, HIGHEST)
    a_cs_t = a_cs.T
    e = e_ref[...]
    dtx = _dot(dt, e, HIGHEST)
    eax = _dot(jnp.exp(a_cs[:tb]), e, HIGHEST)
    decx = _dot(jnp.exp(a_cs[l - 1:l, :] - a_cs), e, HIGHEST)
    tot = jnp.broadcast_to(jnp.exp(a_cs_t[:, l - 1:l]), (LANES, LANES))
    rtot = _dot(et_ref[...], tot, HIGHEST)

    xs = xc[:, :SSM_WIDTH]
    xd = xs * dtx
    xw = xd * decx
    lane = lax.broadcasted_iota(jnp.int32, (tb, LANES), 1)
    tril_q = tril[:tb]
    y_pairs = []
    for g in range(SSM_GROUPS):
        bg = xc[:, SSM_WIDTH + g * D_STATE:SSM_WIDTH + (g + 1) * D_STATE].astype(BF16)
        c_lo = SSM_WIDTH + SSM_GROUPS * D_STATE + g * D_STATE
        cg = xc[:tb, c_lo:c_lo + D_STATE].astype(BF16)
        cb = _dot_nt(cg, bg)
        for k in range(2):
            pair = 2 * g + k
            lo = pair * LANES
            xd_pair = xd[:, lo:lo + LANES].astype(BF16)
            ys = []
            for r2 in range(2):
                h = 2 * pair + r2
                seg = a_cs[:tb, h:h + 1] - a_cs_t[h:h + 1, :]
                lm = jnp.where(tril_q, jnp.exp(jnp.where(tril_q, seg, 0.0)), 0.0)
                ys.append(_dot((cb * lm).astype(BF16), xd_pair))
            y_diag = jnp.where(lane < SSM_HEAD_DIM, ys[0], ys[1])
            sp = state[lo:lo + LANES, :]
            y_off = _dot_nt(cg, sp.astype(BF16)) * eax[:, lo:lo + LANES]
            y_pairs.append(y_diag + y_off)
            upd = _dot(xw[:, lo:lo + LANES].T.astype(BF16), bg)
            state[lo:lo + LANES, :] = sp * rtot[lo:lo + LANES, :] + upd
    y = jnp.concatenate(y_pairs, axis=1) + dskip_ref[...] * xs[:tb]
    zz = z_ref[0]
    y = y * (zz * _sigmoid(zz))
    gw = SSM_WIDTH // SSM_GROUPS
    outs = []
    for g in range(SSM_GROUPS):
        yg = y[:, g * gw:(g + 1) * gw]
        ms = jnp.mean(yg * yg, -1, keepdims=True)
        outs.append(yg * lax.rsqrt(ms + EPS) * ng_ref[:, g * gw:(g + 1) * gw])
    y_ref[0] = jnp.concatenate(outs, axis=1).astype(y_ref.dtype)

    @pl.when(c == nc - 1)
    def _():
        hfin_ref[0] = state[...].reshape(SSM_HEADS, SSM_HEAD_DIM, D_STATE)


def _ssm(z, xbc, sm, hist, h0, conv_w, conv_b, dt_bias, a_log, d_skip, norm_g):
    bsz, t, _ = z.shape
    l = SSD_CHUNK
    tb = min(l, t)
    assert t % tb == 0 and tb % 8 == 0 and t >= CONV_WIDTH - 1
    nc = t // tb
    pad8 = lambda v: jnp.pad(v.reshape(1, SSM_HEADS), ((0, 0), (0, LANES - SSM_HEADS)))
    expand = np.zeros((LANES, SSM_WIDTH), np.float32)
    for h in range(SSM_HEADS):
        expand[h, h * SSM_HEAD_DIM:(h + 1) * SSM_HEAD_DIM] = 1.0
    blk = lambda b, c: (b, c, 0)
    per_b3 = lambda b, c: (b, 0, 0)
    per_b4 = lambda b, c: (b, 0, 0, 0)
    fixed = lambda b, c: (0, 0)
    return pl.pallas_call(
        functools.partial(_ssm_kernel, tb=tb, l=l, nc=nc),
        grid=(bsz, nc),
        in_specs=[pl.BlockSpec((1, tb, SSM_WIDTH), blk), pl.BlockSpec((1, tb, CONV_DIM), blk),
                  pl.BlockSpec((1, tb, LANES), blk),
                  pl.BlockSpec((1, CONV_WIDTH - 1, CONV_DIM), per_b3),
                  pl.BlockSpec((1, SSM_HEADS, SSM_HEAD_DIM, D_STATE), per_b4),
                  pl.BlockSpec((CONV_WIDTH, CONV_DIM), fixed), pl.BlockSpec((1, CONV_DIM), fixed),
                  pl.BlockSpec((1, LANES), fixed), pl.BlockSpec((1, LANES), fixed),
                  pl.BlockSpec((1, SSM_WIDTH), fixed), pl.BlockSpec((1, SSM_WIDTH), fixed),
                  pl.BlockSpec((LANES, SSM_WIDTH), fixed), pl.BlockSpec((SSM_WIDTH, LANES), fixed)],
        out_specs=[pl.BlockSpec((1, tb, SSM_WIDTH), blk),
                   pl.BlockSpec((1, SSM_HEADS, SSM_HEAD_DIM, D_STATE), per_b4)],
        out_shape=[jax.ShapeDtypeStruct((bsz, t, SSM_WIDTH), BF16),
                   jax.ShapeDtypeStruct((bsz, SSM_HEADS, SSM_HEAD_DIM, D_STATE), F32)],
        scratch_shapes=[pltpu.VMEM((8 + l, CONV_DIM), F32), pltpu.VMEM((SSM_WIDTH, D_STATE), F32)],
        compiler_params=_cparams(("parallel", "arbitrary")),
        name="ssm",
    )(z, xbc, sm, hist, h0, conv_w, conv_b.reshape(1, CONV_DIM), pad8(dt_bias), pad8(a_log),
      jnp.repeat(d_skip, SSM_HEAD_DIM).reshape(1, SSM_WIDTH), norm_g.reshape(1, SSM_WIDTH),
      jnp.asarray(expand), jnp.asarray(expand.T))


def _compress(load_rows, h, n_sub, cw):
    w1_ref, b1_ref, w2k_ref, b2k_ref, w2v_ref, b2v_ref = cw
    hid = jnp.zeros((n_sub, 2 * KV_ROWS), F32)
    for j in range(D_CMP):
        hid = hid + _dot(load_rows(h, j).astype(BF16), w1_ref[j])
    pre = hid[:, :KV_ROWS] + pltpu.roll(hid[:, KV_ROWS:], n_sub - 1, 0) + b1_ref[...]
    act = _gelu_tanh(pre).astype(BF16)
    return _dot(act, w2k_ref[...]) + b2k_ref[...], _dot(act, w2v_ref[...]) + b2v_ref[...]


def _select(score, qpos, n_slc):
    jj = lax.broadcasted_iota(jnp.int32, score.shape, 1)
    cur = qpos // L_SEL
    visible = jj * L_SEL <= qpos
    forced = (jj == 0) | (jj == cur) | (jj == cur - 1)
    sc = jnp.where(visible, jnp.where(forced, FORCED_SCORE, score), -1.0)
    sc = jnp.where(jj < n_slc, sc, -2.0)
    rank = jnp.zeros(score.shape, F32)
    for k in range(n_slc):
        ck = sc[:, k:k + 1]
        beats = (ck > sc) | ((ck == sc) & (jj > k))
        rank = rank + jnp.where(beats, 1.0, 0.0)
    return jnp.where((rank < min(TOP_N, n_slc)) & (sc >= 0.0), 1.0, 0.0)


def _softmax_rows(s):
    m = jnp.max(s, -1, keepdims=True)
    e = jnp.where(s > MASKED_BELOW, jnp.exp(s - m), 0.0)
    return e / jnp.maximum(jnp.sum(e, -1, keepdims=True), 1e-30)


def _stack_heads(q_ref, h):
    return jnp.concatenate([q_ref[0, :, (h * GQA + g) * HEAD_DIM:(h * GQA + g + 1) * HEAD_DIM]
                            for g in range(GQA)], axis=0)


def _unstack_heads(o_ref, h, o, rows):
    for g in range(GQA):
        hd = h * GQA + g
        o_ref[0, :, hd * HEAD_DIM:(hd + 1) * HEAD_DIM] = o[g * rows:(g + 1) * rows]


def _stack_tiles(ref, lead, h):
    return jnp.concatenate([ref[lead + (h * GQA + g,)] for g in range(GQA)], axis=0)


def _cmp_prompt_kernel(kvc_ref, q_ref, near_ref, far_ref, w1_ref, b1_ref, w2k_ref, b2k_ref, w2v_ref, b2v_ref,
                       ov_ref, o_ref, sel_ref, kc, vc, *, n_sub, n_cmp, n_slc):
    qb = pl.program_id(1)

    @pl.when(qb == 0)
    def _():
        load = lambda h, j: kvc_ref[h, pl.ds(j, n_sub, stride=D_CMP), :]
        for h in range(KV_HEADS):
            k, v = _compress(load, h, n_sub, (w1_ref, b1_ref, w2k_ref, b2k_ref, w2v_ref, b2v_ref))
            kc[h] = k.astype(BF16)
            vc[h] = v.astype(BF16)

    row = lax.broadcasted_iota(jnp.int32, (Q_BLOCK, n_sub), 0)
    blk = lax.broadcasted_iota(jnp.int32, (Q_BLOCK, n_sub), 1)
    qpos = qb * Q_BLOCK + row
    visible = (blk * D_CMP + (L_CMP - 1) <= qpos) & (blk < n_cmp)
    near_lo = qb * (Q_BLOCK // D_CMP) - (MAX_DISTANCE + L_CMP - 1) // D_CMP
    mi = lax.broadcasted_iota(jnp.int32, (NEAR_W, n_sub), 0)
    ni = lax.broadcasted_iota(jnp.int32, (NEAR_W, n_sub), 1)
    place = jnp.where(ni - mi == near_lo, 1.0, 0.0)
    for h in range(KV_HEADS):
        bias = jnp.concatenate(
            [jnp.where(blk >= near_lo, _dot(near_ref[h * GQA + g], place, HIGHEST),
                       far_ref[h * GQA + g:h * GQA + g + 1, :]) for g in range(GQA)], axis=0)
        s = _dot_nt(_stack_heads(q_ref, h), kc[h]) + bias
        p = _softmax_rows(jnp.where(jnp.concatenate([visible] * GQA, axis=0), s, NEG))
        _unstack_heads(o_ref, h, _dot(p.astype(BF16), vc[h]), Q_BLOCK)
        pg = p[0:Q_BLOCK]
        for g in range(1, GQA):
            pg = pg + p[g * Q_BLOCK:(g + 1) * Q_BLOCK]
        sel_ref[0, h] = _select(_dot(pg, ov_ref[...], HIGHEST), qpos[:, 0:1], n_slc)


def _cmp_prompt(kvc, q, near, far, cmp_w, overlap):
    bsz, t, _ = q.shape
    n_sub = t // D_CMP
    n_cmp = n_sub - 1
    nqb = t // Q_BLOCK
    n_slc = t // L_SEL
    assert n_sub == LANES and n_slc <= LANES
    fixed = lambda a: pl.BlockSpec(a.shape, lambda b, i, _n=a.ndim: (0,) * _n)
    return pl.pallas_call(
        functools.partial(_cmp_prompt_kernel, n_sub=n_sub, n_cmp=n_cmp, n_slc=n_slc),
        grid=(bsz, nqb),
        in_specs=[pl.BlockSpec((KV_HEADS, t, KV_ROWS), lambda b, i: (0, b, 0)),
                  pl.BlockSpec((1, Q_BLOCK, ATT_WIDTH), lambda b, i: (b, i, 0)),
                  fixed(near), fixed(far)] + [fixed(a) for a in cmp_w] + [fixed(overlap)],
        out_specs=[pl.BlockSpec((1, Q_BLOCK, ATT_WIDTH), lambda b, i: (b, i, 0)),
                   pl.BlockSpec((1, KV_HEADS, Q_BLOCK, LANES), lambda b, i: (b, 0, i, 0))],
        out_shape=[jax.ShapeDtypeStruct((bsz, t, ATT_WIDTH), F32),
                   jax.ShapeDtypeStruct((bsz, KV_HEADS, t, LANES), F32)],
        scratch_shapes=[pltpu.VMEM((KV_HEADS, n_sub, HEAD_DIM), BF16),
                        pltpu.VMEM((KV_HEADS, n_sub, HEAD_DIM), BF16)],
        compiler_params=_cparams(("parallel", "arbitrary")),
        name="cmp_prompt",
    )(kvc, q, near, far, *cmp_w, overlap)


def _nsa_prompt_kernel(q_ref, kvs_ref, kvw_ref, sel_ref, eneg_ref, near_ref, far_ref, winb_ref, os_ref, ow_ref,
                       kaug, vsa, kwp, vwa, s_buf, mrun, acc, *, t):
    qb = pl.program_id(1)
    rows = GQA * Q_BLOCK
    near_w = 2 * Q_BLOCK
    win_w = WINDOW + Q_BLOCK

    @pl.when(qb == 0)
    def _():
        ones_row = jnp.where(lax.broadcasted_iota(jnp.int32, (KV_ROWS - HEAD_DIM, t), 0) == 0, 1.0, 0.0).astype(BF16)
        for h in range(KV_HEADS):
            lo = h * KV_ROWS
            kaug[h, :, 0:Q_BLOCK] = jnp.zeros((HEAD_DIM + SEL_PAD, Q_BLOCK), BF16)
            kaug[h, 0:HEAD_DIM, Q_BLOCK:] = kvs_ref[0, lo:lo + HEAD_DIM, :].astype(BF16)
            kaug[h, HEAD_DIM:, Q_BLOCK:] = eneg_ref[...]
            vsa[h, :, 0:Q_BLOCK] = jnp.zeros((KV_ROWS, Q_BLOCK), BF16)
            vsa[h, 0:HEAD_DIM, Q_BLOCK:] = kvs_ref[0, lo + HEAD_DIM:lo + KV_ROWS, :].astype(BF16)
            vsa[h, HEAD_DIM:, Q_BLOCK:] = ones_row
            kwp[h, :, 0:WINDOW] = jnp.zeros((HEAD_DIM, WINDOW), BF16)
            kwp[h, :, WINDOW:] = kvw_ref[0, lo:lo + HEAD_DIM, :].astype(BF16)
            vwa[h, :, 0:WINDOW] = jnp.zeros((KV_ROWS, WINDOW), BF16)
            vwa[h, 0:HEAD_DIM, WINDOW:] = kvw_ref[0, lo + HEAD_DIM:lo + KV_ROWS, :].astype(BF16)
            vwa[h, HEAD_DIM:, WINDOW:] = ones_row

    def normalise(a):
        return a[:, 0:HEAD_DIM] / a[:, HEAD_DIM:HEAD_DIM + 1]

    def tile_max(s):
        m = s[:, 0:LANES]
        for i in range(1, s.shape[1] // LANES):
            m = jnp.maximum(m, s[:, i * LANES:(i + 1) * LANES])
        return m

    start = pl.multiple_of(qb * Q_BLOCK, Q_BLOCK)
    n_chunk = t // SEL_CHUNK
    tiles_per_chunk = SEL_CHUNK // Q_BLOCK
    for h in range(KV_HEADS):
        q4 = _stack_heads(q_ref, h)
        notsel = (1.0 - sel_ref[0, h][:, 0:SEL_PAD]).astype(BF16)
        qa = jnp.concatenate([q4, jnp.concatenate([notsel] * GQA, axis=0)], axis=1)

        far = jnp.concatenate([far_ref[h]] * tiles_per_chunk, axis=1)
        mrun[...] = jnp.full((rows, LANES), NEG, F32)
        for c in range(n_chunk):
            @pl.when(c * tiles_per_chunk < qb - 1)
            def _(c=c):
                key = c * SEL_CHUNK + lax.broadcasted_iota(jnp.int32, (1, SEL_CHUNK), 1)
                late = jnp.where(key < (qb - 1) * Q_BLOCK, 0.0, NEG)
                lo = Q_BLOCK + c * SEL_CHUNK
                s = _dot(qa, kaug[h, :, lo:lo + SEL_CHUNK]) + far + late
                s_buf[:, c * SEL_CHUNK:(c + 1) * SEL_CHUNK] = s
                mrun[...] = jnp.maximum(mrun[...], tile_max(s))

        first = jnp.where(lax.broadcasted_iota(jnp.int32, (1, near_w), 1) < Q_BLOCK,
                          jnp.where(qb >= 1, 0.0, NEG), 0.0)
        s_near = _dot(qa, kaug[h, :, pl.ds(start, near_w)]) + near_ref[h] + first
        m = jnp.max(jnp.maximum(mrun[...], tile_max(s_near)), -1, keepdims=True)
        acc[...] = _dot_nt(jnp.exp(s_near - m).astype(BF16), vsa[h, :, pl.ds(start, near_w)])
        for c in range(n_chunk):
            @pl.when(c * tiles_per_chunk < qb - 1)
            def _(c=c):
                lo = Q_BLOCK + c * SEL_CHUNK
                p = jnp.exp(s_buf[:, c * SEL_CHUNK:(c + 1) * SEL_CHUNK] - m)
                acc[...] = acc[...] + _dot_nt(p.astype(BF16), vsa[h, :, lo:lo + SEL_CHUNK])
        _unstack_heads(os_ref, h, normalise(acc[...]), Q_BLOCK)

        pad = jnp.where(lax.broadcasted_iota(jnp.int32, (1, win_w), 1) < WINDOW - qb * Q_BLOCK, NEG, 0.0)
        s_w = _dot(q4, kwp[h, :, pl.ds(start, win_w)]) + winb_ref[h] + pad
        p = jnp.exp(s_w - jnp.max(tile_max(s_w), -1, keepdims=True))
        _unstack_heads(ow_ref, h, normalise(_dot_nt(p.astype(BF16), vwa[h, :, pl.ds(start, win_w)])), Q_BLOCK)


def _nsa_prompt(q, kvs_t, kvw_t, sel, eneg, near, far, win_bias):
    bsz, _, t = kvs_t.shape
    assert t % SEL_CHUNK == 0
    nqb = t // Q_BLOCK
    qblk = lambda b, i: (b, i, 0)
    per_b = lambda b, i: (b, 0, 0)
    fixed = lambda a: pl.BlockSpec(a.shape, lambda b, i, _n=a.ndim: (0,) * _n)
    rows = GQA * Q_BLOCK
    return pl.pallas_call(
        functools.partial(_nsa_prompt_kernel, t=t),
        grid=(bsz, nqb),
        in_specs=[pl.BlockSpec((1, Q_BLOCK, ATT_WIDTH), qblk), pl.BlockSpec((1, KV_COLS, t), per_b),
                  pl.BlockSpec((1, KV_COLS, t), per_b),
                  pl.BlockSpec((1, KV_HEADS, Q_BLOCK, LANES), lambda b, i: (b, 0, i, 0)),
                  fixed(eneg), fixed(near), fixed(far), fixed(win_bias)],
        out_specs=[pl.BlockSpec((1, Q_BLOCK, ATT_WIDTH), qblk), pl.BlockSpec((1, Q_BLOCK, ATT_WIDTH), qblk)],
        out_shape=[jax.ShapeDtypeStruct((bsz, t, ATT_WIDTH), F32)] * 2,
        scratch_shapes=[pltpu.VMEM((KV_HEADS, HEAD_DIM + SEL_PAD, Q_BLOCK + t), BF16),
                        pltpu.VMEM((KV_HEADS, KV_ROWS, Q_BLOCK + t), BF16),
                        pltpu.VMEM((KV_HEADS, HEAD_DIM, WINDOW + t), BF16),
                        pltpu.VMEM((KV_HEADS, KV_ROWS, WINDOW + t), BF16),
                        pltpu.VMEM((rows, t), F32), pltpu.VMEM((rows, LANES), F32),
                        pltpu.VMEM((rows, KV_ROWS), F32)],
        compiler_params=_cparams(("parallel", "arbitrary")),
        name="nsa_prompt",
    )(q, kvs_t, kvw_t, sel, eneg, near, far, win_bias)


def _cmp_sample_kernel(pt_ref, *refs, n_pages, n_slc, past, t):
    pages = refs[:n_pages]
    (q_ref, bias_ref, perm_ref, w1_ref, b1_ref, w2k_ref, b2k_ref, w2v_ref, b2v_ref, ov_ref, o_ref, sel_ref,
     xj) = refs[n_pages:]
    sub = PAGE_SIZE // D_CMP
    perm = perm_ref[...]
    for k2 in range(n_pages // 2):
        for h in range(KV_HEADS):
            xa, xb = (_dot_nt(perm, pages[2 * k2 + i][0, h * KV_ROWS:(h + 1) * KV_ROWS, :].astype(BF16))
                      for i in range(2))
            for j in range(D_CMP):
                xj[h, j, 2 * sub * k2:2 * sub * (k2 + 1), :] = jnp.concatenate(
                    [xa[j * sub:(j + 1) * sub], xb[j * sub:(j + 1) * sub]], axis=0).astype(BF16)
    n_sub = n_pages * sub
    load = lambda h, j: xj[h, j]
    qpos = past + lax.broadcasted_iota(jnp.int32, (t, 1), 0)
    for h in range(KV_HEADS):
        k, v = _compress(load, h, n_sub, (w1_ref, b1_ref, w2k_ref, b2k_ref, w2v_ref, b2v_ref))
        qh = _stack_heads(q_ref, h).astype(BF16)
        p = _softmax_rows(_dot_nt(qh, k.astype(BF16)) + bias_ref[h])
        _unstack_heads(o_ref, h, _dot(p.astype(BF16), v.astype(BF16)), t)
        pg = p[0:t]
        for g in range(1, GQA):
            pg = pg + p[g * t:(g + 1) * t]
        sel_ref[0, h] = _select(_dot(pg, ov_ref[...], HIGHEST), qpos, n_slc)


def _page_specs(n_pages):
    return [pl.BlockSpec((1, KV_COLS, PAGE_SIZE), lambda b, pt, k=k: (pt[b, k], 0, 0)) for k in range(n_pages)]


def _cmp_sample(pages, page_table, q, bias_cs, cmp_w, overlap, n_slc):
    bsz, n_pages = page_table.shape
    t = q.shape[1]
    width = overlap.shape[1]
    assert n_pages % 2 == 0
    sub = PAGE_SIZE // D_CMP
    r = np.arange(PAGE_SIZE)
    perm = jnp.asarray((r[None, :] == (D_CMP * (r % sub) + r // sub)[:, None]).astype(np.float32), BF16)
    fixed = lambda a: pl.BlockSpec(a.shape, lambda b, pt, _n=a.ndim: (0,) * _n)
    grid_spec = pltpu.PrefetchScalarGridSpec(
        num_scalar_prefetch=1,
        grid=(bsz,),
        in_specs=_page_specs(n_pages) + [pl.BlockSpec((1, t, ATT_WIDTH), lambda b, pt: (b, 0, 0)), fixed(bias_cs),
                                         fixed(perm)]
        + [fixed(a) for a in cmp_w] + [fixed(overlap)],
        out_specs=[pl.BlockSpec((1, t, ATT_WIDTH), lambda b, pt: (b, 0, 0)),
                   pl.BlockSpec((1, KV_HEADS, t, width), lambda b, pt: (b, 0, 0, 0))],
        scratch_shapes=[pltpu.VMEM((KV_HEADS, D_CMP, n_pages * sub, KV_ROWS), BF16)])
    return pl.pallas_call(
        functools.partial(_cmp_sample_kernel, n_pages=n_pages, n_slc=n_slc, past=n_pages * PAGE_SIZE, t=t),
        grid_spec=grid_spec,
        out_shape=[jax.ShapeDtypeStruct((bsz, t, ATT_WIDTH), F32),
                   jax.ShapeDtypeStruct((bsz, KV_HEADS, t, width), F32)],
        compiler_params=_cparams(("parallel",)),
        name="cmp_sample",
    )(page_table, *([pages] * n_pages), q, bias_cs, perm, *cmp_w, overlap)


def _joint_attend(s_past, vt_past, s_new, v_new):
    m = jnp.maximum(jnp.max(s_past, -1, keepdims=True), jnp.max(s_new, -1, keepdims=True))
    e_past = jnp.exp(s_past - m)
    e_new = jnp.exp(s_new - m)
    den = jnp.sum(e_past, -1, keepdims=True) + jnp.sum(e_new, -1, keepdims=True)
    acc = _dot_nt(e_past.astype(BF16), vt_past) + _dot(e_new.astype(BF16), v_new)
    return acc / den


def _pad_new_rows(new_ref, t):
    return jnp.concatenate([new_ref[0], jnp.zeros((LANES - t, KV_COLS), F32)], axis=0)


def _sel_sample_kernel(pt_ref, *refs, n_pages, t):
    pages = refs[:n_pages]
    (new_ref, q_ref, sel_ref, far_ref, near_ref, biasn_ref, o_ref, kvb, mask) = refs[n_pages:]
    for k in range(n_pages):
        kvb[:, k * PAGE_SIZE:(k + 1) * PAGE_SIZE] = pages[k][0].astype(BF16)
    kv_new = _pad_new_rows(new_ref, t).astype(BF16)
    rows = GQA * t
    lane = lax.broadcasted_iota(jnp.int32, (rows, LANES), 1)
    per_tile = LANES // L_SEL
    for h in range(KV_HEADS):
        lo = h * KV_ROWS
        sel4 = jnp.concatenate([sel_ref[0, h]] * GQA, axis=0)

        def tile_mask(k):
            cols = [jnp.broadcast_to(sel4[:, per_tile * k + i:per_tile * k + i + 1], (rows, LANES))
                    for i in range(per_tile)]
            m = cols[-1]
            for i in range(per_tile - 2, -1, -1):
                m = jnp.where(lane < (i + 1) * L_SEL, cols[i], m)
            return (m - 1.0) * (-NEG)

        far = jnp.concatenate([jnp.broadcast_to(far_ref[h * GQA + g:h * GQA + g + 1, :], (t, LANES))
                               for g in range(GQA)], axis=0)
        for k in range(n_pages - 1):
            mask[:, k * LANES:(k + 1) * LANES] = tile_mask(k) + far
        mask[:, (n_pages - 1) * LANES:n_pages * LANES] = tile_mask(n_pages - 1) + near_ref[h]
        qh = _stack_heads(q_ref, h).astype(BF16)
        s_past = _dot(qh, kvb[lo:lo + HEAD_DIM, :]) + mask[...]
        s_new = _dot_nt(qh, kv_new[:, lo:lo + HEAD_DIM]) + biasn_ref[h] + tile_mask(n_pages)
        o = _joint_attend(s_past, kvb[lo + HEAD_DIM:lo + KV_ROWS, :], s_new, kv_new[:, lo + HEAD_DIM:lo + KV_ROWS])
        _unstack_heads(o_ref, h, o, t)


def _sel_sample(pages, page_table, kvs_new, q, sel, far, near, bias_new):
    bsz, n_pages = page_table.shape
    t = q.shape[1]
    past = n_pages * PAGE_SIZE
    width = sel.shape[-1]
    assert (n_pages + 1) * (LANES // L_SEL) <= width
    per_b = lambda b, pt: (b, 0, 0)
    fixed = lambda a: pl.BlockSpec(a.shape, lambda b, pt, _n=a.ndim: (0,) * _n)
    grid_spec = pltpu.PrefetchScalarGridSpec(
        num_scalar_prefetch=1,
        grid=(bsz,),
        in_specs=_page_specs(n_pages) + [
            pl.BlockSpec((1, t, KV_COLS), per_b), pl.BlockSpec((1, t, ATT_WIDTH), per_b),
            pl.BlockSpec((1, KV_HEADS, t, width), lambda b, pt: (b, 0, 0, 0)),
            fixed(far), fixed(near), fixed(bias_new)],
        out_specs=pl.BlockSpec((1, t, ATT_WIDTH), per_b),
        scratch_shapes=[pltpu.VMEM((KV_COLS, past), BF16), pltpu.VMEM((GQA * t, past), F32)])
    return pl.pallas_call(
        functools.partial(_sel_sample_kernel, n_pages=n_pages, t=t),
        grid_spec=grid_spec,
        out_shape=jax.ShapeDtypeStruct((bsz, t, ATT_WIDTH), F32),
        compiler_params=_cparams(("parallel",)),
        name="sel_sample",
    )(page_table, *([pages] * n_pages), kvs_new, q, sel, far, near, bias_new)


def _win_sample_kernel(buf_ref, new_ref, q_ref, bias_ref, biasn_ref, o_ref, win_ref, *, t):
    buf = buf_ref[0]
    w = buf.shape[1]
    new = _pad_new_rows(new_ref, t)
    shifted = pltpu.roll(buf, w - t, 1)
    tail = pltpu.roll(new.T, LANES - t, 1)
    lane = lax.broadcasted_iota(jnp.int32, (KV_COLS, LANES), 1)
    win_ref[0, :, 0:w - LANES] = shifted[:, 0:w - LANES]
    win_ref[0, :, w - LANES:w] = jnp.where(lane >= LANES - t, tail, shifted[:, w - LANES:w])
    kvb = buf.astype(BF16)
    kv_new = new.astype(BF16)
    for h in range(KV_HEADS):
        lo = h * KV_ROWS
        qh = _stack_heads(q_ref, h).astype(BF16)
        s_past = _dot(qh, kvb[lo:lo + HEAD_DIM, :]) + bias_ref[h]
        s_new = _dot_nt(qh, kv_new[:, lo:lo + HEAD_DIM]) + biasn_ref[h]
        o = _joint_attend(s_past, kvb[lo + HEAD_DIM:lo + KV_ROWS, :], s_new, kv_new[:, lo + HEAD_DIM:lo + KV_ROWS])
        _unstack_heads(o_ref, h, o, t)


def _win_sample(buf_t, kvw_new, q, bias_buf, bias_new):
    bsz, _, w = buf_t.shape
    t = q.shape[1]
    per_b = lambda b: (b, 0, 0)
    fixed = lambda a: pl.BlockSpec(a.shape, lambda b, _n=a.ndim: (0,) * _n)
    return pl.pallas_call(
        functools.partial(_win_sample_kernel, t=t),
        grid=(bsz,),
        in_specs=[pl.BlockSpec((1, KV_COLS, w), per_b), pl.BlockSpec((1, t, KV_COLS), per_b),
                  pl.BlockSpec((1, t, ATT_WIDTH), per_b), fixed(bias_buf), fixed(bias_new)],
        out_specs=[pl.BlockSpec((1, t, ATT_WIDTH), per_b), pl.BlockSpec((1, KV_COLS, w), per_b)],
        out_shape=[jax.ShapeDtypeStruct((bsz, t, ATT_WIDTH), F32),
                   jax.ShapeDtypeStruct((bsz, KV_COLS, w), F32)],
        compiler_params=_cparams(("parallel",)),
        name="win_sample",
    )(buf_t, kvw_new, q, bias_buf, bias_new)


def _combine_kernel(x_ref, y_ref, oc_ref, os_ref, ow_ref, sm_ref, eg_ref, eb_ref, ex_ref, ag_ref, wo_ref, g1_ref,
                    b1_ref, h_ref):
    xn = _layer_norm(x_ref[...], eg_ref[...], eb_ref[...])
    gates = _sigmoid(sm_ref[...])
    o = jnp.zeros(oc_ref.shape, F32)
    for br, ref in enumerate((oc_ref, os_ref, ow_ref)):
        o = o + _dot(gates, ex_ref[br], HIGHEST) * ref[...]
    rms = lax.rsqrt(jnp.mean(o * o, -1, keepdims=True) + EPS)
    att = (o * rms * ag_ref[...]).astype(BF16)
    mix = _dot(jnp.concatenate([y_ref[...], att], axis=1), wo_ref[...])
    h_ref[...] = _layer_norm(ALPHA * xn + mix, g1_ref[...], b1_ref[...])


def _combine(x2d, y, oc, os_, ow, sm, eg, eb, gate_expand, ag, wo, g1, b1, tm):
    n = x2d.shape[0]
    tm = _row_tile(n, tm)
    row = lambda i: (i, 0)
    fixed = lambda a: pl.BlockSpec(a.shape, lambda i, _n=a.ndim: (0,) * _n)
    att = pl.BlockSpec((tm, ATT_WIDTH), row)
    return pl.pallas_call(
        _combine_kernel,
        grid=(n // tm,),
        in_specs=[pl.BlockSpec((tm, D_MODEL), row), pl.BlockSpec((tm, SSM_WIDTH), row), att, att, att,
                  pl.BlockSpec((tm, LANES), row), fixed(eg), fixed(eb), fixed(gate_expand), fixed(ag), fixed(wo),
                  fixed(g1), fixed(b1)],
        out_specs=pl.BlockSpec((tm, D_MODEL), row),
        out_shape=jax.ShapeDtypeStruct((n, D_MODEL), F32),
        compiler_params=_cparams(("parallel",)),
        name="combine",
    )(x2d, y, oc, os_, ow, sm, eg, eb, gate_expand, ag, wo, g1, b1)


def _ffn_kernel(h_ref, wu_ref, wd_ref, g_ref, b_ref, o_ref, acc, *, nk):
    k = pl.program_id(1)
    h = h_ref[...]
    u = jnp.maximum(_dot(h.astype(BF16), wu_ref[...]), 0.0)
    part = _dot((u * u).astype(BF16), wd_ref[...])

    @pl.when(k == 0)
    def _():
        acc[...] = part

    @pl.when(k > 0)
    def _():
        acc[...] = acc[...] + part

    @pl.when(k == nk - 1)
    def _():
        o_ref[...] = _layer_norm(ALPHA * h + acc[...], g_ref[...], b_ref[...])


def _ffn(h2d, wu, wd, g, b, tm, tf):
    n = h2d.shape[0]
    tm = _row_tile(n, tm)
    nk = D_FF // tf
    vec = pl.BlockSpec((1, D_MODEL), lambda i, k: (0, 0))
    return pl.pallas_call(
        functools.partial(_ffn_kernel, nk=nk),
        grid=(n // tm, nk),
        in_specs=[pl.BlockSpec((tm, D_MODEL), lambda i, k: (i, 0)),
                  pl.BlockSpec((D_MODEL, tf), lambda i, k: (0, k)),
                  pl.BlockSpec((tf, D_MODEL), lambda i, k: (k, 0)), vec, vec],
        out_specs=pl.BlockSpec((tm, D_MODEL), lambda i, k: (i, 0)),
        out_shape=jax.ShapeDtypeStruct((n, D_MODEL), F32),
        scratch_shapes=[pltpu.VMEM((tm, D_MODEL), F32)],
        compiler_params=_cparams(("parallel", "arbitrary")),
        name="ffn",
    )(h2d, wu, wd, g, b)


def _bucket_np(dist):
    d = np.maximum(dist, 0)
    exact = N_BUCKETS // 2
    far = exact + (np.log(np.maximum(d, 1).astype(np.float32) / np.float32(exact))
                   / np.float32(math.log(MAX_DISTANCE / exact)) * (N_BUCKETS - exact)).astype(np.int32)
    return np.where(d < exact, d, np.minimum(far, N_BUCKETS - 1)).astype(np.int32)


def _bias_lookup(tbl, dist, mask=None):
    dist = np.asarray(dist)
    onehot = np.eye(N_BUCKETS, dtype=np.float32)[_bucket_np(dist).reshape(-1)]
    b = jnp.dot(jnp.asarray(onehot), tbl, precision=HIGHEST).T.reshape((ATT_HEADS,) + dist.shape)
    return b if mask is None else jnp.where(jnp.asarray(mask)[None], b, NEG)


def _toeplitz_tile(tbl, offset, mask):
    period = 2 * Q_BLOCK
    k = np.arange(period)
    vals = _bias_lookup(tbl, offset - np.where(k < Q_BLOCK, k, k - period))
    tiled = jnp.tile(vals, (1, Q_BLOCK))[:, :Q_BLOCK * (period - 1)]
    t = tiled.reshape(ATT_HEADS, Q_BLOCK, period - 1)[:, :, :Q_BLOCK]
    return jnp.where(jnp.asarray(mask)[None], t, NEG)


def _stack_gt(tab, t):
    return tab.reshape(KV_HEADS, GQA * t, tab.shape[-1])


def _far_rows(tbl):
    return jnp.broadcast_to(tbl[N_BUCKETS - 1][:, None], (ATT_HEADS, LANES))


def _overlap(n_cmp_pad, n_cmp, width, n_slc):
    i = np.arange(n_cmp_pad)[:, None]
    j = np.arange(width)[None, :]
    ov = (i * D_CMP < (j + 1) * L_SEL) & (i * D_CMP + L_CMP > j * L_SEL) & (i < n_cmp) & (j < n_slc)
    return jnp.asarray(ov.astype(np.float32))


def _prep_cmp_weights(w1, b1, w2, b2):
    eye = jnp.eye(2, dtype=F32)
    w1r = (w1[:, :, :, :, None, :] * eye[None, None, :, None, :, None]).transpose(1, 2, 3, 0, 4, 5)
    w1r = w1r.reshape(D_CMP, KV_ROWS, 2 * KV_ROWS).astype(BF16)
    zero = jnp.zeros((CMP_HID, HEAD_DIM), F32)
    w2k = jnp.concatenate([w2[0], zero], axis=0).astype(BF16)
    w2v = jnp.concatenate([zero, w2[1]], axis=0).astype(BF16)
    return (w1r, b1.reshape(1, 2 * CMP_HID), w2k, b2[0].reshape(1, HEAD_DIM), w2v, b2[1].reshape(1, HEAD_DIM))


def _prep_w_in(w_in):
    sizes = (SSM_WIDTH, CONV_DIM, SSM_HEADS, ATT_WIDTH, KV_COLS, KV_COLS, KV_COLS)
    z, xbc, dt, q, kvc, kvs, kvw, gates = jnp.split(w_in, np.cumsum(sizes).tolist(), axis=1)
    small = jnp.concatenate([dt, gates], axis=1)
    small = jnp.pad(small, ((0, 0), (0, LANES - small.shape[1])))
    return jnp.concatenate([z, xbc, q * ATT_SCALE, kvc, kvs, kvw, small], axis=1).astype(BF16)


def _gate_expand():
    ex = np.zeros((N_BRANCH, LANES, ATT_WIDTH), np.float32)
    for br in range(N_BRANCH):
        for hd in range(ATT_HEADS):
            ex[br, SSM_HEADS + br * ATT_HEADS + hd, hd * HEAD_DIM:(hd + 1) * HEAD_DIM] = 1.0
    return jnp.asarray(ex)


def _feature_major(a):
    lead = a.shape[:-4]
    rows = a.shape[-4]
    return jnp.moveaxis(a.reshape(lead + (rows, KV_COLS)), -2, -1)


def _row_major6(a_t):
    bsz, _, rows = a_t.shape
    return jnp.moveaxis(a_t, 1, 2).reshape(1, bsz, rows, KV_HEADS, 2, HEAD_DIM)


def kernel(x_prompt, x_sample, cache_cmp_kv, cache_slc_kv, cache_win_kv, state_conv, state_ssm, page_table,
           rel_bias_table, emb_ln_g, emb_ln_b, w_in, conv_w, conv_b, dt_bias, a_log, d_skip, ssm_norm_g,
           cmp_w1, cmp_b1, cmp_w2, cmp_b2, att_norm_g, w_out, ln1_g, ln1_b, w_up, w_down, ln2_g, ln2_b):
    assert w_in.shape[0] == DEPTH
    bp, tp, _ = x_prompt.shape
    bs, ts, _ = x_sample.shape
    n_pages = page_table.shape[1]
    past = n_pages * PAGE_SIZE
    w_buf = cache_win_kv.shape[2]
    assert ts < D_CMP and ts % 8 == 0 and w_buf == WINDOW and past >= WINDOW and tp >= WINDOW
    tbl = rel_bias_table
    vec = lambda v: v.reshape(1, -1)

    w_proj = _prep_w_in(w_in[0])
    cmp_w = _prep_cmp_weights(cmp_w1[0], cmp_b1[0], cmp_w2[0], cmp_b2[0])
    wo = w_out[0].astype(BF16)
    wu = w_up[0].astype(BF16)
    wd = w_down[0].astype(BF16)
    eg, eb = vec(emb_ln_g), vec(emb_ln_b)
    gate_expand = _gate_expand()
    far = _far_rows(tbl)

    def trunk_tail(x2d, y, oc, os_, ow, sm):
        h = _combine(x2d, y, oc, os_, ow, sm, eg, eb, gate_expand, vec(att_norm_g[0]), wo, vec(ln1_g[0]),
                     vec(ln1_b[0]), 256)
        return _ffn(h, wu, wd, vec(ln2_g[0]), vec(ln2_b[0]), 1024, D_FF // 4)

    ssm_w = (conv_w[0], conv_b[0], dt_bias[0], a_log[0], d_skip[0], ssm_norm_g[0])

    xp2 = x_prompt.reshape(bp * tp, D_MODEL)
    z, xbc, q, kvc, kvc_t, kvs_t, kvw_t, sm = _proj(xp2, eg, eb, w_proj, BF16, 512, seq=(bp, tp))
    r3 = lambda a: a.reshape(bp, tp, a.shape[-1])
    xbc3 = r3(xbc)
    y_ssm, h_new = _ssm(r3(z), xbc3, r3(sm), jnp.zeros((bp, CONV_WIDTH - 1, CONV_DIM), F32),
                        jnp.zeros((bp, SSM_HEADS, SSM_HEAD_DIM, D_STATE), F32), *ssm_w)
    n_sub = tp // D_CMP
    n_slc = tp // L_SEL
    band = (np.arange(Q_BLOCK)[:, None] - D_CMP * np.arange(NEAR_W)[None, :]
            + D_CMP * ((MAX_DISTANCE + L_CMP - 1) // D_CMP) - (L_CMP - 1))
    oc, sel = _cmp_prompt(kvc, r3(q), _bias_lookup(tbl, band), far, cmp_w,
                          _overlap(n_sub, n_sub - 1, LANES, n_slc))
    ii = np.arange(Q_BLOCK)[:, None] - np.arange(Q_BLOCK)[None, :]
    tiles_gq = lambda rs, ok: jnp.concatenate(
        [_toeplitz_tile(tbl, Q_BLOCK * r, ok(ii + Q_BLOCK * r)) for r in rs], axis=2).reshape(
            KV_HEADS, GQA * Q_BLOCK, len(rs) * Q_BLOCK)
    near = tiles_gq((1, 0), lambda d: d >= 0)
    win_bias = tiles_gq(range(WINDOW // Q_BLOCK, -1, -1), lambda d: (d >= 0) & (d < WINDOW))
    far_gq = jnp.repeat(far, Q_BLOCK, axis=0).reshape(KV_HEADS, GQA * Q_BLOCK, LANES)
    assert n_slc <= SEL_PAD
    eneg = jnp.asarray(np.where(np.arange(SEL_PAD)[:, None] == (np.arange(tp) // L_SEL)[None, :], NEG, 0.0), BF16)
    os_, ow = _nsa_prompt(r3(q), kvs_t, kvw_t, sel, eneg, near, far_gq, win_bias)
    f2 = lambda a: a.reshape(bp * tp, a.shape[-1])
    y_prompt = trunk_tail(xp2, f2(y_ssm), f2(oc), f2(os_), f2(ow), sm).reshape(bp, tp, D_MODEL)
    w = min(WINDOW, tp)
    prompt_state = (_row_major6(kvc_t), _row_major6(kvs_t), _row_major6(kvw_t[:, :, tp - w:]),
                    xbc3[:, tp - (CONV_WIDTH - 1):][None], h_new[None])

    xs2 = x_sample.reshape(bs * ts, D_MODEL)
    z, xbc, q, kvc, kvs, kvw, sm = _proj(xs2, eg, eb, w_proj, F32, 512)
    r3 = lambda a: a.reshape(bs, ts, a.shape[-1])
    xbc3 = r3(xbc)
    y_ssm, h_new = _ssm(r3(z), xbc3, r3(sm), state_conv[0], state_ssm[0], *ssm_w)
    n_sub = past // D_CMP
    n_cmp = n_sub - 1
    n_slc = -(-(past + ts) // L_SEL)
    width = -(-n_slc // LANES) * LANES
    qpos = past + np.arange(ts)
    dist_c = qpos[:, None] - (np.arange(n_sub) * D_CMP + L_CMP - 1)[None, :]
    bias_cs = _stack_gt(_bias_lookup(tbl, dist_c, (dist_c >= 0) & (np.arange(n_sub) < n_cmp)[None, :]), ts)
    oc, sel = _cmp_sample(_feature_major(cache_cmp_kv[0]), page_table, r3(q), bias_cs, cmp_w,
                          _overlap(n_sub, n_cmp, width, n_slc), n_slc)
    dist_last = qpos[:, None] - (past - PAGE_SIZE + np.arange(PAGE_SIZE))[None, :]
    dist_n = np.arange(ts)[:, None] - np.arange(LANES)[None, :]
    bias_new = _stack_gt(_bias_lookup(tbl, dist_n, (dist_n >= 0) & (np.arange(LANES) < ts)[None, :]), ts)
    os_ = _sel_sample(_feature_major(cache_slc_kv[0]), page_table, r3(kvs), r3(q), sel, far,
                      _stack_gt(_bias_lookup(tbl, dist_last), ts), bias_new)
    dist_w = qpos[:, None] - (past - w_buf + np.arange(w_buf))[None, :]
    bias_wb = _stack_gt(_bias_lookup(tbl, dist_w, (dist_w >= 0) & (dist_w < WINDOW)), ts)
    ow, win_new_t = _win_sample(_feature_major(cache_win_kv[0]), r3(kvw), r3(q), bias_wb, bias_new)
    f2 = lambda a: a.reshape(bs * ts, a.shape[-1])
    y_sample = trunk_tail(xs2, f2(y_ssm), f2(oc), f2(os_), f2(ow), sm).reshape(bs, ts, D_MODEL)
    kv6 = lambda a: a.reshape(1, bs, ts, KV_HEADS, 2, HEAD_DIM)
    sample_state = (kv6(kvc), kv6(kvs), _row_major6(win_new_t),
                    xbc3[:, ts - (CONV_WIDTH - 1):][None], h_new[None])

    return (y_prompt, y_sample) + prompt_state + sample_state
```

```python
import functools
import math

import numpy as np
import jax
import jax.numpy as jnp
from jax import lax
from jax.experimental import pallas as pl
from jax.experimental.pallas import tpu as pltpu

F32 = jnp.float32
BF16 = jnp.bfloat16
HIGHEST = lax.Precision.HIGHEST

D_MODEL = 1024
SSM_HEADS = 8
SSM_HEAD_DIM = 64
SSM_WIDTH = SSM_HEADS * SSM_HEAD_DIM
SSM_GROUPS = 2
D_STATE = 128
CONV_WIDTH = 4
CONV_DIM = SSM_WIDTH + 2 * SSM_GROUPS * D_STATE
SSD_CHUNK = 128
ATT_HEADS = 8
KV_HEADS = 2
GQA = ATT_HEADS // KV_HEADS
HEAD_DIM = 64
ATT_WIDTH = ATT_HEADS * HEAD_DIM
KV_COLS = KV_HEADS * 2 * HEAD_DIM
D_CMP = 16
L_CMP = 2 * D_CMP
CMP_HID = 64
L_SEL = 64
TOP_N = 16
WINDOW = 512
Q_BLOCK = 128
N_BRANCH = 3
FORCED_SCORE = 1e4
N_BUCKETS = 32
MAX_DISTANCE = 128
D_FF = 4 * D_MODEL
DEPTH = 1
ALPHA = (2 * DEPTH) ** 0.25
ATT_SCALE = HEAD_DIM ** -0.5
EPS = 1e-5
PAGE_SIZE = 128

LANES = 128
NEG = -1e30
MASKED_BELOW = -1e29
VMEM_LIMIT = 48 * 1024 * 1024
KV_ROWS = 2 * HEAD_DIM
SEL_PAD = 32
NEAR_W = 32
SEL_CHUNK = 512

_OFF_Z = 0
_OFF_XBC = _OFF_Z + SSM_WIDTH
_OFF_Q = _OFF_XBC + CONV_DIM
_OFF_KVC = _OFF_Q + ATT_WIDTH
_OFF_KVS = _OFF_KVC + KV_COLS
_OFF_KVW = _OFF_KVS + KV_COLS
_OFF_SM = _OFF_KVW + KV_COLS
_N_PROJ = _OFF_SM + LANES


def _cparams(sem):
    return pltpu.CompilerParams(dimension_semantics=sem, vmem_limit_bytes=VMEM_LIMIT)


def _row_tile(n, preferred):
    tm = min(n, preferred)
    assert n % tm == 0 and tm % 8 == 0
    return tm


def _dot(a, b, precision=None):
    return jnp.dot(a, b, preferred_element_type=F32, precision=precision)


def _dot_nt(a, b):
    return lax.dot_general(a, b, (((1,), (1,)), ((), ())), preferred_element_type=F32)


def _layer_norm(x, g, b):
    mu = jnp.mean(x, -1, keepdims=True)
    xc = x - mu
    var = jnp.mean(xc * xc, -1, keepdims=True)
    return xc * lax.rsqrt(var + EPS) * g + b


def _sigmoid(x):
    return 1.0 / (1.0 + jnp.exp(-x))


def _softplus(x):
    return jnp.maximum(x, 0.0) + jnp.log(1.0 + jnp.exp(-jnp.abs(x)))


def _gelu_tanh(x):
    c = math.sqrt(2.0 / math.pi)
    return 0.5 * x * (1.0 + jnp.tanh(c * (x + 0.044715 * (x * x * x))))


def _proj_kernel(x_ref, g_ref, b_ref, w_ref, z_ref, xbc_ref, q_ref, kvc_ref, *rest, feature_major):
    xn = _layer_norm(x_ref[...], g_ref[...], b_ref[...]).astype(BF16)

    def mm(lo, hi):
        return _dot(xn, w_ref[:, lo:hi])

    z_ref[...] = mm(_OFF_Z, _OFF_XBC)
    xbc_ref[...] = mm(_OFF_XBC, _OFF_Q)
    q_ref[...] = mm(_OFF_Q, _OFF_KVC).astype(q_ref.dtype)
    kvc = mm(_OFF_KVC, _OFF_KVS)
    if feature_major:
        kvct_ref, kvst_ref, kvwt_ref, sm_ref = rest
        for h in range(KV_HEADS):
            kvc_ref[h] = kvc[:, h * KV_ROWS:(h + 1) * KV_ROWS]
        kvct_ref[0] = kvc.T
        kvst_ref[0] = mm(_OFF_KVS, _OFF_KVW).T
        kvwt_ref[0] = mm(_OFF_KVW, _OFF_SM).T
    else:
        kvs_ref, kvw_ref, sm_ref = rest
        kvc_ref[...] = kvc
        kvs_ref[...] = mm(_OFF_KVS, _OFF_KVW)
        kvw_ref[...] = mm(_OFF_KVW, _OFF_SM)
    sm_ref[...] = mm(_OFF_SM, _N_PROJ)


def _proj(x2d, g, b, w, q_dtype, tm, seq=None):
    n = x2d.shape[0]
    tm = _row_tile(n, tm)
    row = lambda i: (i, 0)
    fixed = lambda i: (0, 0)
    rm = lambda wd, dt: (pl.BlockSpec((tm, wd), row), jax.ShapeDtypeStruct((n, wd), dt))
    outs = [rm(SSM_WIDTH, F32), rm(CONV_DIM, F32), rm(ATT_WIDTH, q_dtype)]
    if seq is None:
        outs += [rm(KV_COLS, F32), rm(KV_COLS, F32), rm(KV_COLS, F32)]
    else:
        outs.append((pl.BlockSpec((KV_HEADS, tm, KV_ROWS), lambda i: (0, i, 0)),
                     jax.ShapeDtypeStruct((KV_HEADS, n, KV_ROWS), F32)))
        bsz, t = seq
        assert t % tm == 0 and tm % LANES == 0
        per = t // tm
        fm = (pl.BlockSpec((1, KV_COLS, tm), lambda i: (i // per, 0, i % per)),
              jax.ShapeDtypeStruct((bsz, KV_COLS, t), F32))
        outs += [fm, fm, fm]
    outs.append(rm(LANES, F32))
    return pl.pallas_call(
        functools.partial(_proj_kernel, feature_major=seq is not None),
        grid=(n // tm,),
        in_specs=[pl.BlockSpec((tm, D_MODEL), row), pl.BlockSpec((1, D_MODEL), fixed),
                  pl.BlockSpec((1, D_MODEL), fixed), pl.BlockSpec((D_MODEL, _N_PROJ), fixed)],
        out_specs=[o[0] for o in outs],
        out_shape=[o[1] for o in outs],
        compiler_params=_cparams(("parallel",)),
        name="proj",
    )(x2d, g, b, w)


def _ssm_kernel(z_ref, xbc_ref, sm_ref, hist_ref, h0_ref, cw_ref, cb_ref, dtb_ref, alog_ref, dskip_ref,
                ng_ref, e_ref, et_ref, y_ref, hfin_ref, xext, state, *, tb, l, nc):
    c = pl.program_id(1)

    @pl.when(c == 0)
    def _():
        xext[0:8, :] = jnp.zeros((8, CONV_DIM), F32)
        xext[8 - (CONV_WIDTH - 1):8, :] = hist_ref[0]
        if tb < l:
            xext[8 + tb:8 + l, :] = jnp.zeros((l - tb, CONV_DIM), F32)
        state[...] = h0_ref[0].reshape(SSM_WIDTH, D_STATE)

    xext[8:8 + tb, :] = xbc_ref[0]
    conv = cb_ref[...]
    for k in range(CONV_WIDTH):
        lo = 8 - (CONV_WIDTH - 1) + k
        conv = conv + cw_ref[k:k + 1, :] * xext[lo:lo + l, :]
    xc = conv * _sigmoid(conv)
    xext[0:8, :] = xext[tb:tb + 8, :]

    dt = _softplus(sm_ref[0] + dtb_ref[...])
    if tb < l:
        dt = jnp.concatenate([dt, jnp.zeros((l - tb, LANES), F32)], axis=0)
    a = dt * (-jnp.exp(alog_ref[...]))
    ri = lax.broadcasted_iota(jnp.int32, (l, l), 0)
    ci = lax.broadcasted_iota(jnp.int32, (l, l), 1)
    tril = ri >= ci
    a_cs = _dot(jnp.where(tril, 1.0, 0.0), a, HIGHEST)
    a_cs_t = a_cs.T
    e = e_ref[...]
    dtx = _dot(dt, e, HIGHEST)
    eax = _dot(jnp.exp(a_cs[:tb]), e, HIGHEST)
    decx = _dot(jnp.exp(a_cs[l - 1:l, :] - a_cs), e, HIGHEST)
    tot = jnp.broadcast_to(jnp.exp(a_cs_t[:, l - 1:l]), (LANES, LANES))
    rtot = _dot(et_ref[...], tot, HIGHEST)

    xs = xc[:, :SSM_WIDTH]
    xd = xs * dtx
    xw = xd * decx
    lane = lax.broadcasted_iota(jnp.int32, (tb, LANES), 1)
    tril_q = tril[:tb]
    y_pairs = []
    for g in range(SSM_GROUPS):
        bg = xc[:, SSM_WIDTH + g * D_STATE:SSM_WIDTH + (g + 1) * D_STATE].astype(BF16)
        c_lo = SSM_WIDTH + SSM_GROUPS * D_STATE + g * D_STATE
        cg = xc[:tb, c_lo:c_lo + D_STATE].astype(BF16)
        cb = _dot_nt(cg, bg)
        for k in range(2):
            pair = 2 * g + k
            lo = pair * LANES
            xd_pair = xd[:, lo:lo + LANES].astype(BF16)
            ys = []
            for r2 in range(2):
                h = 2 * pair + r2
                seg = a_cs[:tb, h:h + 1] - a_cs_t[h:h + 1, :]
                lm = jnp.where(tril_q, jnp.exp(jnp.where(tril_q, seg, 0.0)), 0.0)
                ys.append(_dot((cb * lm).astype(BF16), xd_pair))
            y_diag = jnp.where(lane < SSM_HEAD_DIM, ys[0], ys[1])
            sp = state[lo:lo + LANES, :]
            y_off = _dot_nt(cg, sp.astype(BF16)) * eax[:, lo:lo + LANES]
            y_pairs.append(y_diag + y_off)
            upd = _dot(xw[:, lo:lo + LANES].T.astype(BF16), bg)
            state[lo:lo + LANES, :] = sp * rtot[lo:lo + LANES, :] + upd
    y = jnp.concatenate(y_pairs, axis=1) + dskip_ref[...] * xs[:tb]
    zz = z_ref[0]
    y = y * (zz * _sigmoid(zz))
    gw = SSM_WIDTH // SSM_GROUPS
    outs = []
    for g in range(SSM_GROUPS):
        yg = y[:, g * gw:(g + 1) * gw]
        ms = jnp.mean(yg * yg, -1, keepdims=True)
        outs.append(yg * lax.rsqrt(ms + EPS) * ng_ref[:, g * gw:(g + 1) * gw])
    y_ref[0] = jnp.concatenate(outs, axis=1).astype(y_ref.dtype)

    @pl.when(c == nc - 1)
    def _():
        hfin_ref[0] = state[...].reshape(SSM_HEADS, SSM_HEAD_DIM, D_STATE)


def _ssm(z, xbc, sm, hist, h0, conv_w, conv_b, dt_bias, a_log, d_skip, norm_g):
    bsz, t, _ = z.shape
    l = SSD_CHUNK
    tb = min(l, t)
    assert t % tb == 0 and tb % 8 == 0 and t >= CONV_WIDTH - 1
    nc = t // tb
    pad8 = lambda v: jnp.pad(v.reshape(1, SSM_HEADS), ((0, 0), (0, LANES - SSM_HEADS)))
    expand = np.zeros((LANES, SSM_WIDTH), np.float32)
    for h in range(SSM_HEADS):
        expand[h, h * SSM_HEAD_DIM:(h + 1) * SSM_HEAD_DIM] = 1.0
    blk = lambda b, c: (b, c, 0)
    per_b3 = lambda b, c: (b, 0, 0)
    per_b4 = lambda b, c: (b, 0, 0, 0)
    fixed = lambda b, c: (0, 0)
    return pl.pallas_call(
        functools.partial(_ssm_kernel, tb=tb, l=l, nc=nc),
        grid=(bsz, nc),
        in_specs=[pl.BlockSpec((1, tb, SSM_WIDTH), blk), pl.BlockSpec((1, tb, CONV_DIM), blk),
                  pl.BlockSpec((1, tb, LANES), blk),
                  pl.BlockSpec((1, CONV_WIDTH - 1, CONV_DIM), per_b3),
                  pl.BlockSpec((1, SSM_HEADS, SSM_HEAD_DIM, D_STATE), per_b4),
                  pl.BlockSpec((CONV_WIDTH, CONV_DIM), fixed), pl.BlockSpec((1, CONV_DIM), fixed),
                  pl.BlockSpec((1, LANES), fixed), pl.BlockSpec((1, LANES), fixed),
                  pl.BlockSpec((1, SSM_WIDTH), fixed), pl.BlockSpec((1, SSM_WIDTH), fixed),
                  pl.BlockSpec((LANES, SSM_WIDTH), fixed), pl.BlockSpec((SSM_WIDTH, LANES), fixed)],
        out_specs=[pl.BlockSpec((1, tb, SSM_WIDTH), blk),
                   pl.BlockSpec((1, SSM_HEADS, SSM_HEAD_DIM, D_STATE), per_b4)],
        out_shape=[jax.ShapeDtypeStruct((bsz, t, SSM_WIDTH), BF16),
                   jax.ShapeDtypeStruct((bsz, SSM_HEADS, SSM_HEAD_DIM, D_STATE), F32)],
        scratch_shapes=[pltpu.VMEM((8 + l, CONV_DIM), F32), pltpu.VMEM((SSM_WIDTH, D_STATE), F32)],
        compiler_params=_cparams(("parallel", "arbitrary")),
        name="ssm",
    )(z, xbc, sm, hist, h0, conv_w, conv_b.reshape(1, CONV_DIM), pad8(dt_bias), pad8(a_log),
      jnp.repeat(d_skip, SSM_HEAD_DIM).reshape(1, SSM_WIDTH), norm_g.reshape(1, SSM_WIDTH),
      jnp.asarray(expand), jnp.asarray(expand.T))


def _compress(load_pair, h, n_sub, cw):
    w1_ref, b1_ref, w2_ref, b2_ref = cw
    hid = jnp.zeros((n_sub, 2 * KV_ROWS), F32)
    for jp in range(D_CMP // 2):
        hid = hid + _dot(load_pair(h, jp), w1_ref[jp])
    pre = hid[:, :KV_ROWS] + pltpu.roll(hid[:, KV_ROWS:], n_sub - 1, 0) + b1_ref[...]
    return _dot(_gelu_tanh(pre).astype(BF16), w2_ref[...]) + b2_ref[...]


def _pad_q(q):
    return jnp.concatenate([q, jnp.zeros(q.shape, q.dtype)], axis=1)


def _select(score, qpos, n_slc):
    jj = lax.broadcasted_iota(jnp.int32, score.shape, 1)
    cur = qpos // L_SEL
    visible = jj * L_SEL <= qpos
    forced = (jj == 0) | (jj == cur) | (jj == cur - 1)
    sc = jnp.where(visible, jnp.where(forced, FORCED_SCORE, score), -1.0)
    sc = jnp.where(jj < n_slc, sc, -2.0)
    rank = jnp.zeros(score.shape, F32)
    lane = lax.broadcasted_iota(jnp.int32, (1, score.shape[1]), 1)
    for k in range(n_slc):
        ck = sc[:, k:k + 1]
        tie = jnp.where(lane > k, 1.0, 0.0)
        rank = rank + jnp.where(ck > sc, 1.0, jnp.where(ck == sc, tie, 0.0))
    return jnp.where(rank < min(TOP_N, n_slc), jnp.where(sc >= 0.0, 1.0, 0.0), 0.0)


def _softmax_rows(s):
    m = jnp.max(s, -1, keepdims=True)
    e = jnp.where(s > MASKED_BELOW, jnp.exp(s - m), 0.0)
    return e / jnp.maximum(jnp.sum(e, -1, keepdims=True), 1e-30)


def _stack_heads(q_ref, h):
    return jnp.concatenate([q_ref[0, :, (h * GQA + g) * HEAD_DIM:(h * GQA + g + 1) * HEAD_DIM]
                            for g in range(GQA)], axis=0)


def _unstack_heads(o_ref, h, o, rows):
    for g in range(GQA):
        hd = h * GQA + g
        o_ref[0, :, hd * HEAD_DIM:(hd + 1) * HEAD_DIM] = o[g * rows:(g + 1) * rows]


def _cmp_prompt_kernel(kvc_ref, q_ref, near_ref, far_ref, w1_ref, b1_ref, w2_ref, b2_ref, ov_ref, o_ref, sel_ref,
                       kvcmp, *, n_sub, n_cmp, n_slc):
    qb = pl.program_id(1)

    @pl.when(qb == 0)
    def _():
        rows = lambda h, j: kvc_ref[h, pl.ds(j, n_sub, stride=D_CMP), :].astype(BF16)
        load = lambda h, jp: jnp.concatenate([rows(h, 2 * jp), rows(h, 2 * jp + 1)], axis=1)
        for h in range(KV_HEADS):
            kvcmp[h] = _compress(load, h, n_sub, (w1_ref, b1_ref, w2_ref, b2_ref)).astype(BF16)

    row = lax.broadcasted_iota(jnp.int32, (Q_BLOCK, n_sub), 0)
    blk = lax.broadcasted_iota(jnp.int32, (Q_BLOCK, n_sub), 1)
    qpos = qb * Q_BLOCK + row
    visible = (blk * D_CMP + (L_CMP - 1) <= qpos) & (blk < n_cmp)
    near_lo = qb * (Q_BLOCK // D_CMP) - (MAX_DISTANCE + L_CMP - 1) // D_CMP
    shift = lax.rem(near_lo + n_sub, n_sub)
    for h in range(KV_HEADS):
        bias = jnp.concatenate(
            [jnp.where(blk >= near_lo, pltpu.roll(near_ref[h * GQA + g], shift, 1),
                       far_ref[h * GQA + g:h * GQA + g + 1, :]) for g in range(GQA)], axis=0)
        kv = kvcmp[h]
        s = _dot_nt(_pad_q(_stack_heads(q_ref, h)), kv) + bias
        p = _softmax_rows(jnp.where(jnp.concatenate([visible] * GQA, axis=0), s, NEG))
        _unstack_heads(o_ref, h, _dot(p.astype(BF16), kv)[:, HEAD_DIM:], Q_BLOCK)
        pg = p[0:Q_BLOCK]
        for g in range(1, GQA):
            pg = pg + p[g * Q_BLOCK:(g + 1) * Q_BLOCK]
        sel_ref[0, h] = _select(_dot(pg, ov_ref[...], HIGHEST), qpos[:, 0:1], n_slc)


def _cmp_prompt(kvc, q, near, far, cmp_w, overlap):
    bsz, t, _ = q.shape
    n_sub = t // D_CMP
    n_cmp = n_sub - 1
    nqb = t // Q_BLOCK
    n_slc = t // L_SEL
    assert n_sub == LANES and n_slc <= LANES
    fixed = lambda a: pl.BlockSpec(a.shape, lambda b, i, _n=a.ndim: (0,) * _n)
    return pl.pallas_call(
        functools.partial(_cmp_prompt_kernel, n_sub=n_sub, n_cmp=n_cmp, n_slc=n_slc),
        grid=(bsz, nqb),
        in_specs=[pl.BlockSpec((KV_HEADS, t, KV_ROWS), lambda b, i: (0, b, 0)),
                  pl.BlockSpec((1, Q_BLOCK, ATT_WIDTH), lambda b, i: (b, i, 0)),
                  fixed(near), fixed(far)] + [fixed(a) for a in cmp_w] + [fixed(overlap)],
        out_specs=[pl.BlockSpec((1, Q_BLOCK, ATT_WIDTH), lambda b, i: (b, i, 0)),
                   pl.BlockSpec((1, KV_HEADS, Q_BLOCK, LANES), lambda b, i: (b, 0, i, 0))],
        out_shape=[jax.ShapeDtypeStruct((bsz, t, ATT_WIDTH), F32),
                   jax.ShapeDtypeStruct((bsz, KV_HEADS, t, LANES), F32)],
        scratch_shapes=[pltpu.VMEM((KV_HEADS, n_sub, KV_ROWS), BF16)],
        compiler_params=_cparams(("parallel", "arbitrary")),
        name="cmp_prompt",
    )(kvc, q, near, far, *cmp_w, overlap)


def _nsa_prompt_kernel(q_ref, kvs_ref, kvw_ref, sel_ref, eneg_ref, near_ref, far_ref, winb_ref, os_ref, ow_ref,
                       kaug, vsa, kwp, vwa, s_buf, mrun, acc, *, t):
    qb = pl.program_id(1)
    rows = GQA * Q_BLOCK
    near_w = 2 * Q_BLOCK
    win_w = WINDOW + Q_BLOCK

    @pl.when(qb == 0)
    def _():
        ones_row = jnp.where(lax.broadcasted_iota(jnp.int32, (KV_ROWS - HEAD_DIM, t), 0) == 0, 1.0, 0.0).astype(BF16)
        for h in range(KV_HEADS):
            lo = h * KV_ROWS
            kaug[h, :, 0:Q_BLOCK] = jnp.zeros((HEAD_DIM + SEL_PAD, Q_BLOCK), BF16)
            kaug[h, 0:HEAD_DIM, Q_BLOCK:] = kvs_ref[0, lo:lo + HEAD_DIM, :].astype(BF16)
            kaug[h, HEAD_DIM:, Q_BLOCK:] = eneg_ref[...]
            vsa[h, :, 0:Q_BLOCK] = jnp.zeros((KV_ROWS, Q_BLOCK), BF16)
            vsa[h, 0:HEAD_DIM, Q_BLOCK:] = kvs_ref[0, lo + HEAD_DIM:lo + KV_ROWS, :].astype(BF16)
            vsa[h, HEAD_DIM:, Q_BLOCK:] = ones_row
            kwp[h, :, 0:WINDOW] = jnp.zeros((HEAD_DIM, WINDOW), BF16)
            kwp[h, :, WINDOW:] = kvw_ref[0, lo:lo + HEAD_DIM, :].astype(BF16)
            vwa[h, :, 0:WINDOW] = jnp.zeros((KV_ROWS, WINDOW), BF16)
            vwa[h, 0:HEAD_DIM, WINDOW:] = kvw_ref[0, lo + HEAD_DIM:lo + KV_ROWS, :].astype(BF16)
            vwa[h, HEAD_DIM:, WINDOW:] = ones_row

    def normalise(a):
        return a[:, 0:HEAD_DIM] / a[:, HEAD_DIM:HEAD_DIM + 1]

    def tile_max(s):
        m = s[:, 0:LANES]
        for i in range(1, s.shape[1] // LANES):
            m = jnp.maximum(m, s[:, i * LANES:(i + 1) * LANES])
        return m

    start = pl.multiple_of(qb * Q_BLOCK, Q_BLOCK)
    n_chunk = t // SEL_CHUNK
    tiles_per_chunk = SEL_CHUNK // Q_BLOCK
    for h in range(KV_HEADS):
        q4 = _stack_heads(q_ref, h)
        notsel = (1.0 - sel_ref[0, h][:, 0:SEL_PAD]).astype(BF16)
        qa = jnp.concatenate([q4, jnp.concatenate([notsel] * GQA, axis=0)], axis=1)

        far = jnp.concatenate([far_ref[h]] * tiles_per_chunk, axis=1)
        mrun[...] = jnp.full((rows, LANES), NEG, F32)
        for c in range(n_chunk):
            @pl.when(c * tiles_per_chunk < qb - 1)
            def _(c=c):
                key = c * SEL_CHUNK + lax.broadcasted_iota(jnp.int32, (1, SEL_CHUNK), 1)
                late = jnp.where(key < (qb - 1) * Q_BLOCK, 0.0, NEG)
                lo = Q_BLOCK + c * SEL_CHUNK
                s = _dot(qa, kaug[h, :, lo:lo + SEL_CHUNK]) + far + late
                s_buf[:, c * SEL_CHUNK:(c + 1) * SEL_CHUNK] = s
                mrun[...] = jnp.maximum(mrun[...], tile_max(s))

        first = jnp.where(lax.broadcasted_iota(jnp.int32, (1, near_w), 1) < Q_BLOCK,
                          jnp.where(qb >= 1, 0.0, NEG), 0.0)
        s_near = _dot(qa, kaug[h, :, pl.ds(start, near_w)]) + near_ref[h] + first
        m = jnp.max(jnp.maximum(mrun[...], tile_max(s_near)), -1, keepdims=True)
        acc[...] = _dot_nt(jnp.exp(s_near - m).astype(BF16), vsa[h, :, pl.ds(start, near_w)])
        for c in range(n_chunk):
            @pl.when(c * tiles_per_chunk < qb - 1)
            def _(c=c):
                lo = Q_BLOCK + c * SEL_CHUNK
                p = jnp.exp(s_buf[:, c * SEL_CHUNK:(c + 1) * SEL_CHUNK] - m)
                acc[...] = acc[...] + _dot_nt(p.astype(BF16), vsa[h, :, lo:lo + SEL_CHUNK])
        _unstack_heads(os_ref, h, normalise(acc[...]), Q_BLOCK)

        pad = jnp.where(lax.broadcasted_iota(jnp.int32, (1, win_w), 1) < WINDOW - qb * Q_BLOCK, NEG, 0.0)
        s_w = _dot(q4, kwp[h, :, pl.ds(start, win_w)]) + winb_ref[h] + pad
        p = jnp.exp(s_w - jnp.max(tile_max(s_w), -1, keepdims=True))
        _unstack_heads(ow_ref, h, normalise(_dot_nt(p.astype(BF16), vwa[h, :, pl.ds(start, win_w)])), Q_BLOCK)


def _nsa_prompt(q, kvs_t, kvw_t, sel, eneg, near, far, win_bias):
    bsz, _, t = kvs_t.shape
    assert t % SEL_CHUNK == 0
    nqb = t // Q_BLOCK
    qblk = lambda b, i: (b, i, 0)
    per_b = lambda b, i: (b, 0, 0)
    fixed = lambda a: pl.BlockSpec(a.shape, lambda b, i, _n=a.ndim: (0,) * _n)
    rows = GQA * Q_BLOCK
    return pl.pallas_call(
        functools.partial(_nsa_prompt_kernel, t=t),
        grid=(bsz, nqb),
        in_specs=[pl.BlockSpec((1, Q_BLOCK, ATT_WIDTH), qblk), pl.BlockSpec((1, KV_COLS, t), per_b),
                  pl.BlockSpec((1, KV_COLS, t), per_b),
                  pl.BlockSpec((1, KV_HEADS, Q_BLOCK, LANES), lambda b, i: (b, 0, i, 0)),
                  fixed(eneg), fixed(near), fixed(far), fixed(win_bias)],
        out_specs=[pl.BlockSpec((1, Q_BLOCK, ATT_WIDTH), qblk), pl.BlockSpec((1, Q_BLOCK, ATT_WIDTH), qblk)],
        out_shape=[jax.ShapeDtypeStruct((bsz, t, ATT_WIDTH), F32)] * 2,
        scratch_shapes=[pltpu.VMEM((KV_HEADS, HEAD_DIM + SEL_PAD, Q_BLOCK + t), BF16),
                        pltpu.VMEM((KV_HEADS, KV_ROWS, Q_BLOCK + t), BF16),
                        pltpu.VMEM((KV_HEADS, HEAD_DIM, WINDOW + t), BF16),
                        pltpu.VMEM((KV_HEADS, KV_ROWS, WINDOW + t), BF16),
                        pltpu.VMEM((rows, t), F32), pltpu.VMEM((rows, LANES), F32),
                        pltpu.VMEM((rows, KV_ROWS), F32)],
        compiler_params=_cparams(("parallel", "arbitrary")),
        name="nsa_prompt",
    )(q, kvs_t, kvw_t, sel, eneg, near, far, win_bias)


def _cmp_sample_kernel(pt_ref, *refs, n_pages, n_slc, past, t):
    pages = refs[:n_pages]
    (q_ref, bias_ref, perm_ref, w1_ref, b1_ref, w2_ref, b2_ref, ov_ref, o_ref, sel_ref, xj) = refs[n_pages:]
    sub = PAGE_SIZE // D_CMP
    perm = perm_ref[...]
    per_tile = LANES // (2 * sub)
    for k2 in range(n_pages // 2):
        pair = jnp.concatenate([pages[2 * k2][0].astype(BF16), pages[2 * k2 + 1][0].astype(BF16)], axis=1)
        y = _dot(pair, perm)
        for h in range(KV_HEADS):
            for c in range(2 * PAGE_SIZE // LANES):
                xt = y[h * KV_ROWS:(h + 1) * KV_ROWS, c * LANES:(c + 1) * LANES].T
                for i in range(per_tile):
                    j = c * per_tile + i
                    xj[h, j // 2, 2 * sub * k2:2 * sub * (k2 + 1), (j % 2) * KV_ROWS:(j % 2 + 1) * KV_ROWS] = (
                        xt[2 * sub * i:2 * sub * (i + 1)].astype(BF16))
    n_sub = n_pages * sub
    load = lambda h, jp: xj[h, jp]
    qpos = past + lax.broadcasted_iota(jnp.int32, (t, 1), 0)
    for h in range(KV_HEADS):
        kv = _compress(load, h, n_sub, (w1_ref, b1_ref, w2_ref, b2_ref)).astype(BF16)
        qh = _pad_q(_stack_heads(q_ref, h).astype(BF16))
        p = _softmax_rows(_dot_nt(qh, kv) + bias_ref[h])
        _unstack_heads(o_ref, h, _dot(p.astype(BF16), kv)[:, HEAD_DIM:], t)
        pg = p[0:t]
        for g in range(1, GQA):
            pg = pg + p[g * t:(g + 1) * t]
        sel_ref[0, h] = _select(_dot(pg, ov_ref[...], HIGHEST), qpos, n_slc)


def _page_specs(n_pages):
    return [pl.BlockSpec((1, KV_COLS, PAGE_SIZE), lambda b, pt, k=k: (pt[b, k], 0, 0)) for k in range(n_pages)]


def _cmp_sample(pages, page_table, q, bias_cs, cmp_w, overlap, n_slc):
    bsz, n_pages = page_table.shape
    t = q.shape[1]
    width = overlap.shape[1]
    assert n_pages % 2 == 0
    sub = PAGE_SIZE // D_CMP
    col = np.arange(2 * PAGE_SIZE)
    page, row = col // PAGE_SIZE, col % PAGE_SIZE
    dest = (row % D_CMP) * 2 * sub + page * sub + row // D_CMP
    perm = jnp.asarray((dest[:, None] == col[None, :]).astype(np.float32), BF16)
    fixed = lambda a: pl.BlockSpec(a.shape, lambda b, pt, _n=a.ndim: (0,) * _n)
    grid_spec = pltpu.PrefetchScalarGridSpec(
        num_scalar_prefetch=1,
        grid=(bsz,),
        in_specs=_page_specs(n_pages) + [pl.BlockSpec((1, t, ATT_WIDTH), lambda b, pt: (b, 0, 0)), fixed(bias_cs),
                                         fixed(perm)]
        + [fixed(a) for a in cmp_w] + [fixed(overlap)],
        out_specs=[pl.BlockSpec((1, t, ATT_WIDTH), lambda b, pt: (b, 0, 0)),
                   pl.BlockSpec((1, KV_HEADS, t, width), lambda b, pt: (b, 0, 0, 0))],
        scratch_shapes=[pltpu.VMEM((KV_HEADS, D_CMP // 2, n_pages * sub, 2 * KV_ROWS), BF16)])
    return pl.pallas_call(
        functools.partial(_cmp_sample_kernel, n_pages=n_pages, n_slc=n_slc, past=n_pages * PAGE_SIZE, t=t),
        grid_spec=grid_spec,
        out_shape=[jax.ShapeDtypeStruct((bsz, t, ATT_WIDTH), F32),
                   jax.ShapeDtypeStruct((bsz, KV_HEADS, t, width), F32)],
        compiler_params=_cparams(("parallel",)),
        name="cmp_sample",
    )(page_table, *([pages] * n_pages), q, bias_cs, perm, *cmp_w, overlap)


def _joint_attend(s_past, vt_past, s_new, v_new):
    m = jnp.maximum(jnp.max(s_past, -1, keepdims=True), jnp.max(s_new, -1, keepdims=True))
    e_past = jnp.exp(s_past - m)
    e_new = jnp.exp(s_new - m)
    den = jnp.sum(e_past, -1, keepdims=True) + jnp.sum(e_new, -1, keepdims=True)
    acc = _dot_nt(e_past.astype(BF16), vt_past) + _dot(e_new.astype(BF16), v_new)
    return acc / den


def _pad_new_rows(new_ref, t):
    return jnp.concatenate([new_ref[0], jnp.zeros((LANES - t, KV_COLS), F32)], axis=0)


def _sel_sample_kernel(pt_ref, *refs, n_pages, t):
    pages = refs[:n_pages]
    (new_ref, q_ref, sel_ref, far_ref, near_ref, biasn_ref, o_ref, kvb, mask) = refs[n_pages:]
    for k in range(n_pages):
        kvb[:, k * PAGE_SIZE:(k + 1) * PAGE_SIZE] = pages[k][0].astype(BF16)
    kv_new = _pad_new_rows(new_ref, t).astype(BF16)
    rows = GQA * t
    lane = lax.broadcasted_iota(jnp.int32, (rows, LANES), 1)
    per_tile = LANES // L_SEL
    for h in range(KV_HEADS):
        lo = h * KV_ROWS
        sel4 = jnp.concatenate([sel_ref[0, h]] * GQA, axis=0)

        def tile_mask(k):
            cols = [jnp.broadcast_to(sel4[:, per_tile * k + i:per_tile * k + i + 1], (rows, LANES))
                    for i in range(per_tile)]
            m = cols[-1]
            for i in range(per_tile - 2, -1, -1):
                m = jnp.where(lane < (i + 1) * L_SEL, cols[i], m)
            return (m - 1.0) * (-NEG)

        far = jnp.concatenate([jnp.broadcast_to(far_ref[h * GQA + g:h * GQA + g + 1, :], (t, LANES))
                               for g in range(GQA)], axis=0)
        for k in range(n_pages - 1):
            mask[:, k * LANES:(k + 1) * LANES] = tile_mask(k) + far
        mask[:, (n_pages - 1) * LANES:n_pages * LANES] = tile_mask(n_pages - 1) + near_ref[h]
        qh = _stack_heads(q_ref, h).astype(BF16)
        s_past = _dot(qh, kvb[lo:lo + HEAD_DIM, :]) + mask[...]
        s_new = _dot_nt(qh, kv_new[:, lo:lo + HEAD_DIM]) + biasn_ref[h] + tile_mask(n_pages)
        o = _joint_attend(s_past, kvb[lo + HEAD_DIM:lo + KV_ROWS, :], s_new, kv_new[:, lo + HEAD_DIM:lo + KV_ROWS])
        _unstack_heads(o_ref, h, o, t)


def _sel_sample(pages, page_table, kvs_new, q, sel, far, near, bias_new):
    bsz, n_pages = page_table.shape
    t = q.shape[1]
    past = n_pages * PAGE_SIZE
    width = sel.shape[-1]
    assert (n_pages + 1) * (LANES // L_SEL) <= width
    per_b = lambda b, pt: (b, 0, 0)
    fixed = lambda a: pl.BlockSpec(a.shape, lambda b, pt, _n=a.ndim: (0,) * _n)
    grid_spec = pltpu.PrefetchScalarGridSpec(
        num_scalar_prefetch=1,
        grid=(bsz,),
        in_specs=_page_specs(n_pages) + [
            pl.BlockSpec((1, t, KV_COLS), per_b), pl.BlockSpec((1, t, ATT_WIDTH), per_b),
            pl.BlockSpec((1, KV_HEADS, t, width), lambda b, pt: (b, 0, 0, 0)),
            fixed(far), fixed(near), fixed(bias_new)],
        out_specs=pl.BlockSpec((1, t, ATT_WIDTH), per_b),
        scratch_shapes=[pltpu.VMEM((KV_COLS, past), BF16), pltpu.VMEM((GQA * t, past), F32)])
    return pl.pallas_call(
        functools.partial(_sel_sample_kernel, n_pages=n_pages, t=t),
        grid_spec=grid_spec,
        out_shape=jax.ShapeDtypeStruct((bsz, t, ATT_WIDTH), F32),
        compiler_params=_cparams(("parallel",)),
        name="sel_sample",
    )(page_table, *([pages] * n_pages), kvs_new, q, sel, far, near, bias_new)


def _win_sample_kernel(buf_ref, new_ref, q_ref, bias_ref, biasn_ref, o_ref, win_ref, *, t):
    buf = buf_ref[0]
    w = buf.shape[1]
    new = _pad_new_rows(new_ref, t)
    shifted = pltpu.roll(buf, w - t, 1)
    tail = pltpu.roll(new.T, LANES - t, 1)
    lane = lax.broadcasted_iota(jnp.int32, (KV_COLS, LANES), 1)
    win_ref[0, :, 0:w - LANES] = shifted[:, 0:w - LANES]
    win_ref[0, :, w - LANES:w] = jnp.where(lane >= LANES - t, tail, shifted[:, w - LANES:w])
    kvb = buf.astype(BF16)
    kv_new = new.astype(BF16)
    for h in range(KV_HEADS):
        lo = h * KV_ROWS
        qh = _stack_heads(q_ref, h).astype(BF16)
        s_past = _dot(qh, kvb[lo:lo + HEAD_DIM, :]) + bias_ref[h]
        s_new = _dot_nt(qh, kv_new[:, lo:lo + HEAD_DIM]) + biasn_ref[h]
        o = _joint_attend(s_past, kvb[lo + HEAD_DIM:lo + KV_ROWS, :], s_new, kv_new[:, lo + HEAD_DIM:lo + KV_ROWS])
        _unstack_heads(o_ref, h, o, t)


def _win_sample(buf_t, kvw_new, q, bias_buf, bias_new):
    bsz, _, w = buf_t.shape
    t = q.shape[1]
    per_b = lambda b: (b, 0, 0)
    fixed = lambda a: pl.BlockSpec(a.shape, lambda b, _n=a.ndim: (0,) * _n)
    return pl.pallas_call(
        functools.partial(_win_sample_kernel, t=t),
        grid=(bsz,),
        in_specs=[pl.BlockSpec((1, KV_COLS, w), per_b), pl.BlockSpec((1, t, KV_COLS), per_b),
                  pl.BlockSpec((1, t, ATT_WIDTH), per_b), fixed(bias_buf), fixed(bias_new)],
        out_specs=[pl.BlockSpec((1, t, ATT_WIDTH), per_b), pl.BlockSpec((1, KV_COLS, w), per_b)],
        out_shape=[jax.ShapeDtypeStruct((bsz, t, ATT_WIDTH), F32),
                   jax.ShapeDtypeStruct((bsz, KV_COLS, w), F32)],
        compiler_params=_cparams(("parallel",)),
        name="win_sample",
    )(buf_t, kvw_new, q, bias_buf, bias_new)


def _combine_kernel(x_ref, y_ref, oc_ref, os_ref, ow_ref, sm_ref, eg_ref, eb_ref, ex_ref, ag_ref, wo_ref, g1_ref,
                    b1_ref, h_ref):
    xn = _layer_norm(x_ref[...], eg_ref[...], eb_ref[...])
    gates = _sigmoid(sm_ref[...])
    o = jnp.zeros(oc_ref.shape, F32)
    for br, ref in enumerate((oc_ref, os_ref, ow_ref)):
        o = o + _dot(gates, ex_ref[br], HIGHEST) * ref[...]
    rms = lax.rsqrt(jnp.mean(o * o, -1, keepdims=True) + EPS)
    att = (o * rms * ag_ref[...]).astype(BF16)
    mix = _dot(jnp.concatenate([y_ref[...], att], axis=1), wo_ref[...])
    h_ref[...] = _layer_norm(ALPHA * xn + mix, g1_ref[...], b1_ref[...])


def _combine(x2d, y, oc, os_, ow, sm, eg, eb, gate_expand, ag, wo, g1, b1, tm):
    n = x2d.shape[0]
    tm = _row_tile(n, tm)
    row = lambda i: (i, 0)
    fixed = lambda a: pl.BlockSpec(a.shape, lambda i, _n=a.ndim: (0,) * _n)
    att = pl.BlockSpec((tm, ATT_WIDTH), row)
    return pl.pallas_call(
        _combine_kernel,
        grid=(n // tm,),
        in_specs=[pl.BlockSpec((tm, D_MODEL), row), pl.BlockSpec((tm, SSM_WIDTH), row), att, att, att,
                  pl.BlockSpec((tm, LANES), row), fixed(eg), fixed(eb), fixed(gate_expand), fixed(ag), fixed(wo),
                  fixed(g1), fixed(b1)],
        out_specs=pl.BlockSpec((tm, D_MODEL), row),
        out_shape=jax.ShapeDtypeStruct((n, D_MODEL), F32),
        compiler_params=_cparams(("parallel",)),
        name="combine",
    )(x2d, y, oc, os_, ow, sm, eg, eb, gate_expand, ag, wo, g1, b1)


def _ffn_kernel(h_ref, wu_ref, wd_ref, g_ref, b_ref, o_ref, acc, *, nk):
    k = pl.program_id(1)
    h = h_ref[...]
    u = jnp.maximum(_dot(h.astype(BF16), wu_ref[...]), 0.0)
    part = _dot((u * u).astype(BF16), wd_ref[...])

    @pl.when(k == 0)
    def _():
        acc[...] = part

    @pl.when(k > 0)
    def _():
        acc[...] = acc[...] + part

    @pl.when(k == nk - 1)
    def _():
        o_ref[...] = _layer_norm(ALPHA * h + acc[...], g_ref[...], b_ref[...])


def _ffn(h2d, wu, wd, g, b, tm, tf):
    n = h2d.shape[0]
    tm = _row_tile(n, tm)
    nk = D_FF // tf
    vec = pl.BlockSpec((1, D_MODEL), lambda i, k: (0, 0))
    return pl.pallas_call(
        functools.partial(_ffn_kernel, nk=nk),
        grid=(n // tm, nk),
        in_specs=[pl.BlockSpec((tm, D_MODEL), lambda i, k: (i, 0)),
                  pl.BlockSpec((D_MODEL, tf), lambda i, k: (0, k)),
                  pl.BlockSpec((tf, D_MODEL), lambda i, k: (k, 0)), vec, vec],
        out_specs=pl.BlockSpec((tm, D_MODEL), lambda i, k: (i, 0)),
        out_shape=jax.ShapeDtypeStruct((n, D_MODEL), F32),
        scratch_shapes=[pltpu.VMEM((tm, D_MODEL), F32)],
        compiler_params=_cparams(("parallel", "arbitrary")),
        name="ffn",
    )(h2d, wu, wd, g, b)


def _bucket_np(dist):
    d = np.maximum(dist, 0)
    exact = N_BUCKETS // 2
    far = exact + (np.log(np.maximum(d, 1).astype(np.float32) / np.float32(exact))
                   / np.float32(math.log(MAX_DISTANCE / exact)) * (N_BUCKETS - exact)).astype(np.int32)
    return np.where(d < exact, d, np.minimum(far, N_BUCKETS - 1)).astype(np.int32)


def _bias_lookup(tbl, dist, mask=None):
    dist = np.asarray(dist)
    onehot = np.eye(N_BUCKETS, dtype=np.float32)[_bucket_np(dist).reshape(-1)]
    b = jnp.dot(jnp.asarray(onehot), tbl, precision=HIGHEST).T.reshape((ATT_HEADS,) + dist.shape)
    return b if mask is None else jnp.where(jnp.asarray(mask)[None], b, NEG)


def _toeplitz_tile(tbl, offset, mask):
    period = 2 * Q_BLOCK
    k = np.arange(period)
    vals = _bias_lookup(tbl, offset - np.where(k < Q_BLOCK, k, k - period))
    tiled = jnp.tile(vals, (1, Q_BLOCK))[:, :Q_BLOCK * (period - 1)]
    t = tiled.reshape(ATT_HEADS, Q_BLOCK, period - 1)[:, :, :Q_BLOCK]
    return jnp.where(jnp.asarray(mask)[None], t, NEG)


def _stack_gt(tab, t):
    return tab.reshape(KV_HEADS, GQA * t, tab.shape[-1])


def _far_rows(tbl):
    return jnp.broadcast_to(tbl[N_BUCKETS - 1][:, None], (ATT_HEADS, LANES))


def _overlap(n_cmp_pad, n_cmp, width, n_slc):
    i = np.arange(n_cmp_pad)[:, None]
    j = np.arange(width)[None, :]
    ov = (i * D_CMP < (j + 1) * L_SEL) & (i * D_CMP + L_CMP > j * L_SEL) & (i < n_cmp) & (j < n_slc)
    return jnp.asarray(ov.astype(np.float32))


def _prep_cmp_weights(w1, b1, w2, b2):
    eye = jnp.eye(2, dtype=F32)
    w1r = (w1[:, :, :, :, None, :] * eye[None, None, :, None, :, None]).transpose(1, 2, 3, 0, 4, 5)
    w1r = w1r.reshape(D_CMP // 2, 2 * KV_ROWS, 2 * KV_ROWS).astype(BF16)
    w2r = (w2[:, :, None, :] * eye[:, None, :, None]).reshape(2 * CMP_HID, KV_ROWS).astype(BF16)
    return (w1r, b1.reshape(1, 2 * CMP_HID), w2r, b2.reshape(1, KV_ROWS))


def _prep_w_in(w_in):
    sizes = (SSM_WIDTH, CONV_DIM, SSM_HEADS, ATT_WIDTH, KV_COLS, KV_COLS, KV_COLS)
    z, xbc, dt, q, kvc, kvs, kvw, gates = jnp.split(w_in, np.cumsum(sizes).tolist(), axis=1)
    small = jnp.concatenate([dt, gates], axis=1)
    small = jnp.pad(small, ((0, 0), (0, LANES - small.shape[1])))
    return jnp.concatenate([z, xbc, q * ATT_SCALE, kvc, kvs, kvw, small], axis=1).astype(BF16)


def _gate_expand():
    ex = np.zeros((N_BRANCH, LANES, ATT_WIDTH), np.float32)
    for br in range(N_BRANCH):
        for hd in range(ATT_HEADS):
            ex[br, SSM_HEADS + br * ATT_HEADS + hd, hd * HEAD_DIM:(hd + 1) * HEAD_DIM] = 1.0
    return jnp.asarray(ex)


def _feature_major(a):
    lead = a.shape[:-4]
    rows = a.shape[-4]
    return jnp.moveaxis(a.reshape(lead + (rows, KV_COLS)), -2, -1)


def _row_major6(a_t):
    bsz, _, rows = a_t.shape
    return jnp.moveaxis(a_t, 1, 2).reshape(1, bsz, rows, KV_HEADS, 2, HEAD_DIM)


def kernel(x_prompt, x_sample, cache_cmp_kv, cache_slc_kv, cache_win_kv, state_conv, state_ssm, page_table,
           rel_bias_table, emb_ln_g, emb_ln_b, w_in, conv_w, conv_b, dt_bias, a_log, d_skip, ssm_norm_g,
           cmp_w1, cmp_b1, cmp_w2, cmp_b2, att_norm_g, w_out, ln1_g, ln1_b, w_up, w_down, ln2_g, ln2_b):
    assert w_in.shape[0] == DEPTH
    bp, tp, _ = x_prompt.shape
    bs, ts, _ = x_sample.shape
    n_pages = page_table.shape[1]
    past = n_pages * PAGE_SIZE
    w_buf = cache_win_kv.shape[2]
    assert ts < D_CMP and ts % 8 == 0 and w_buf == WINDOW and past >= WINDOW and tp >= WINDOW
    tbl = rel_bias_table
    vec = lambda v: v.reshape(1, -1)

    w_proj = _prep_w_in(w_in[0])
    cmp_w = _prep_cmp_weights(cmp_w1[0], cmp_b1[0], cmp_w2[0], cmp_b2[0])
    wo = w_out[0].astype(BF16)
    wu = w_up[0].astype(BF16)
    wd = w_down[0].astype(BF16)
    eg, eb = vec(emb_ln_g), vec(emb_ln_b)
    gate_expand = _gate_expand()
    far = _far_rows(tbl)

    def trunk_tail(x2d, y, oc, os_, ow, sm):
        h = _combine(x2d, y, oc, os_, ow, sm, eg, eb, gate_expand, vec(att_norm_g[0]), wo, vec(ln1_g[0]),
                     vec(ln1_b[0]), 256)
        return _ffn(h, wu, wd, vec(ln2_g[0]), vec(ln2_b[0]), 1024, D_FF // 4)

    ssm_w = (conv_w[0], conv_b[0], dt_bias[0], a_log[0], d_skip[0], ssm_norm_g[0])

    xp2 = x_prompt.reshape(bp * tp, D_MODEL)
    z, xbc, q, kvc, kvc_t, kvs_t, kvw_t, sm = _proj(xp2, eg, eb, w_proj, BF16, 512, seq=(bp, tp))
    r3 = lambda a: a.reshape(bp, tp, a.shape[-1])
    xbc3 = r3(xbc)
    y_ssm, h_new = _ssm(r3(z), xbc3, r3(sm), jnp.zeros((bp, CONV_WIDTH - 1, CONV_DIM), F32),
                        jnp.zeros((bp, SSM_HEADS, SSM_HEAD_DIM, D_STATE), F32), *ssm_w)
    n_sub = tp // D_CMP
    n_slc = tp // L_SEL
    band = (np.arange(Q_BLOCK)[:, None] - D_CMP * np.arange(NEAR_W)[None, :]
            + D_CMP * ((MAX_DISTANCE + L_CMP - 1) // D_CMP) - (L_CMP - 1))
    near_c = jnp.pad(_bias_lookup(tbl, band), ((0, 0), (0, 0), (0, n_sub - NEAR_W)))
    oc, sel = _cmp_prompt(kvc, r3(q), near_c, far, cmp_w,
                          _overlap(n_sub, n_sub - 1, LANES, n_slc))
    ii = np.arange(Q_BLOCK)[:, None] - np.arange(Q_BLOCK)[None, :]
    tiles_gq = lambda rs, ok: jnp.concatenate(
        [_toeplitz_tile(tbl, Q_BLOCK * r, ok(ii + Q_BLOCK * r)) for r in rs], axis=2).reshape(
            KV_HEADS, GQA * Q_BLOCK, len(rs) * Q_BLOCK)
    near = tiles_gq((1, 0), lambda d: d >= 0)
    win_bias = tiles_gq(range(WINDOW // Q_BLOCK, -1, -1), lambda d: (d >= 0) & (d < WINDOW))
    far_gq = jnp.repeat(far, Q_BLOCK, axis=0).reshape(KV_HEADS, GQA * Q_BLOCK, LANES)
    assert n_slc <= SEL_PAD
    eneg = jnp.asarray(np.where(np.arange(SEL_PAD)[:, None] == (np.arange(tp) // L_SEL)[None, :], NEG, 0.0), BF16)
    os_, ow = _nsa_prompt(r3(q), kvs_t, kvw_t, sel, eneg, near, far_gq, win_bias)
    f2 = lambda a: a.reshape(bp * tp, a.shape[-1])
    y_prompt = trunk_tail(xp2, f2(y_ssm), f2(oc), f2(os_), f2(ow), sm).reshape(bp, tp, D_MODEL)
    w = min(WINDOW, tp)
    prompt_state = (_row_major6(kvc_t), _row_major6(kvs_t), _row_major6(kvw_t[:, :, tp - w:]),
                    xbc3[:, tp - (CONV_WIDTH - 1):][None], h_new[None])

    xs2 = x_sample.reshape(bs * ts, D_MODEL)
    z, xbc, q, kvc, kvs, kvw, sm = _proj(xs2, eg, eb, w_proj, F32, 512)
    r3 = lambda a: a.reshape(bs, ts, a.shape[-1])
    xbc3 = r3(xbc)
    y_ssm, h_new = _ssm(r3(z), xbc3, r3(sm), state_conv[0], state_ssm[0], *ssm_w)
    n_sub = past // D_CMP
    n_cmp = n_sub - 1
    n_slc = -(-(past + ts) // L_SEL)
    width = -(-n_slc // LANES) * LANES
    qpos = past + np.arange(ts)
    dist_c = qpos[:, None] - (np.arange(n_sub) * D_CMP + L_CMP - 1)[None, :]
    bias_cs = _stack_gt(_bias_lookup(tbl, dist_c, (dist_c >= 0) & (np.arange(n_sub) < n_cmp)[None, :]), ts)
    oc, sel = _cmp_sample(_feature_major(cache_cmp_kv[0]), page_table, r3(q), bias_cs, cmp_w,
                          _overlap(n_sub, n_cmp, width, n_slc), n_slc)
    dist_last = qpos[:, None] - (past - PAGE_SIZE + np.arange(PAGE_SIZE))[None, :]
    dist_n = np.arange(ts)[:, None] - np.arange(LANES)[None, :]
    bias_new = _stack_gt(_bias_lookup(tbl, dist_n, (dist_n >= 0) & (np.arange(LANES) < ts)[None, :]), ts)
    os_ = _sel_sample(_feature_major(cache_slc_kv[0]), page_table, r3(kvs), r3(q), sel, far,
                      _stack_gt(_bias_lookup(tbl, dist_last), ts), bias_new)
    dist_w = qpos[:, None] - (past - w_buf + np.arange(w_buf))[None, :]
    bias_wb = _stack_gt(_bias_lookup(tbl, dist_w, (dist_w >= 0) & (dist_w < WINDOW)), ts)
    ow, win_new_t = _win_sample(_feature_major(cache_win_kv[0]), r3(kvw), r3(q), bias_wb, bias_new)
    f2 = lambda a: a.reshape(bs * ts, a.shape[-1])
    y_sample = trunk_tail(xs2, f2(y_ssm), f2(oc), f2(os_), f2(ow), sm).reshape(bs, ts, D_MODEL)
    kv6 = lambda a: a.reshape(1, bs, ts, KV_HEADS, 2, HEAD_DIM)
    sample_state = (kv6(kvc), kv6(kvs), _row_major6(win_new_t),
                    xbc3[:, ts - (CONV_WIDTH - 1):][None], h_new[None])

    return (y_prompt, y_sample) + prompt_state + sample_state
```

```python
import functools
import math

import numpy as np
import jax
import jax.numpy as jnp
from jax import lax
from jax.experimental import pallas as pl
from jax.experimental.pallas import tpu as pltpu

F32 = jnp.float32
BF16 = jnp.bfloat16
HIGHEST = lax.Precision.HIGHEST

D_MODEL = 1024
SSM_HEADS = 8
SSM_HEAD_DIM = 64
SSM_WIDTH = SSM_HEADS * SSM_HEAD_DIM
SSM_GROUPS = 2
D_STATE = 128
CONV_WIDTH = 4
CONV_DIM = SSM_WIDTH + 2 * SSM_GROUPS * D_STATE
SSD_CHUNK = 128
ATT_HEADS = 8
KV_HEADS = 2
GQA = ATT_HEADS // KV_HEADS
HEAD_DIM = 64
ATT_WIDTH = ATT_HEADS * HEAD_DIM
KV_COLS = KV_HEADS * 2 * HEAD_DIM
D_CMP = 16
L_CMP = 2 * D_CMP
CMP_HID = 64
L_SEL = 64
TOP_N = 16
WINDOW = 512
Q_BLOCK = 128
N_BRANCH = 3
FORCED_SCORE = 1e4
N_BUCKETS = 32
MAX_DISTANCE = 128
D_FF = 4 * D_MODEL
DEPTH = 1
ALPHA = (2 * DEPTH) ** 0.25
ATT_SCALE = HEAD_DIM ** -0.5
EPS = 1e-5
PAGE_SIZE = 128

LANES = 128
NEG = -1e30
MASKED_BELOW = -1e29
VMEM_LIMIT = 48 * 1024 * 1024
KV_ROWS = 2 * HEAD_DIM
SEL_PAD = 32
NEAR_W = 32
SEL_CHUNK = 512

_OFF_Z = 0
_OFF_XBC = _OFF_Z + SSM_WIDTH
_OFF_Q = _OFF_XBC + CONV_DIM
_OFF_KVC = _OFF_Q + ATT_WIDTH
_OFF_KVS = _OFF_KVC + KV_COLS
_OFF_KVW = _OFF_KVS + KV_COLS
_OFF_SM = _OFF_KVW + KV_COLS
_N_PROJ = _OFF_SM + LANES


def _cparams(sem):
    return pltpu.CompilerParams(dimension_semantics=sem, vmem_limit_bytes=VMEM_LIMIT)


def _row_tile(n, preferred):
    tm = min(n, preferred)
    assert n % tm == 0 and tm % 8 == 0
    return tm


def _dot(a, b, precision=None):
    return jnp.dot(a, b, preferred_element_type=F32, precision=precision)


def _dot_nt(a, b):
    return lax.dot_general(a, b, (((1,), (1,)), ((), ())), preferred_element_type=F32)


def _layer_norm(x, g, b):
    mu = jnp.mean(x, -1, keepdims=True)
    xc = x - mu
    var = jnp.mean(xc * xc, -1, keepdims=True)
    return xc * lax.rsqrt(var + EPS) * g + b


def _sigmoid(x):
    return 1.0 / (1.0 + jnp.exp(-x))


def _softplus(x):
    return jnp.maximum(x, 0.0) + jnp.log(1.0 + jnp.exp(-jnp.abs(x)))


def _gelu_tanh(x):
    c = math.sqrt(2.0 / math.pi)
    return 0.5 * x * (1.0 + jnp.tanh(c * (x + 0.044715 * (x * x * x))))


def _proj_kernel(x_ref, g_ref, b_ref, w_ref, z_ref, xbc_ref, q_ref, kvc_ref, *rest, feature_major):
    xn = _layer_norm(x_ref[...], g_ref[...], b_ref[...]).astype(BF16)

    def mm(lo, hi):
        return _dot(xn, w_ref[:, lo:hi])

    z_ref[...] = mm(_OFF_Z, _OFF_XBC)
    xbc_ref[...] = mm(_OFF_XBC, _OFF_Q)
    q_ref[...] = mm(_OFF_Q, _OFF_KVC).astype(q_ref.dtype)
    kvc = mm(_OFF_KVC, _OFF_KVS)
    if feature_major:
        kvct_ref, kvst_ref, kvwt_ref, sm_ref = rest
        for h in range(KV_HEADS):
            kvc_ref[h] = kvc[:, h * KV_ROWS:(h + 1) * KV_ROWS]
        kvct_ref[0] = kvc.T
        kvst_ref[0] = mm(_OFF_KVS, _OFF_KVW).T
        kvwt_ref[0] = mm(_OFF_KVW, _OFF_SM).T
    else:
        kvs_ref, kvw_ref, sm_ref = rest
        kvc_ref[...] = kvc
        kvs_ref[...] = mm(_OFF_KVS, _OFF_KVW)
        kvw_ref[...] = mm(_OFF_KVW, _OFF_SM)
    sm_ref[...] = mm(_OFF_SM, _N_PROJ)


def _proj(x2d, g, b, w, q_dtype, tm, seq=None):
    n = x2d.shape[0]
    tm = _row_tile(n, tm)
    row = lambda i: (i, 0)
    fixed = lambda i: (0, 0)
    rm = lambda wd, dt: (pl.BlockSpec((tm, wd), row), jax.ShapeDtypeStruct((n, wd), dt))
    outs = [rm(SSM_WIDTH, F32), rm(CONV_DIM, F32), rm(ATT_WIDTH, q_dtype)]
    if seq is None:
        outs += [rm(KV_COLS, F32), rm(KV_COLS, F32), rm(KV_COLS, F32)]
    else:
        outs.append((pl.BlockSpec((KV_HEADS, tm, KV_ROWS), lambda i: (0, i, 0)),
                     jax.ShapeDtypeStruct((KV_HEADS, n, KV_ROWS), F32)))
        bsz, t = seq
        assert t % tm == 0 and tm % LANES == 0
        per = t // tm
        fm = (pl.BlockSpec((1, KV_COLS, tm), lambda i: (i // per, 0, i % per)),
              jax.ShapeDtypeStruct((bsz, KV_COLS, t), F32))
        outs += [fm, fm, fm]
    outs.append(rm(LANES, F32))
    return pl.pallas_call(
        functools.partial(_proj_kernel, feature_major=seq is not None),
        grid=(n // tm,),
        in_specs=[pl.BlockSpec((tm, D_MODEL), row), pl.BlockSpec((1, D_MODEL), fixed),
                  pl.BlockSpec((1, D_MODEL), fixed), pl.BlockSpec((D_MODEL, _N_PROJ), fixed)],
        out_specs=[o[0] for o in outs],
        out_shape=[o[1] for o in outs],
        compiler_params=_cparams(("parallel",)),
        name="proj",
    )(x2d, g, b, w)


def _ssm_kernel(z_ref, xbc_ref, sm_ref, hist_ref, h0_ref, cw_ref, cb_ref, dtb_ref, alog_ref, dskip_ref,
                ng_ref, e_ref, et_ref, y_ref, hfin_ref, xext, state, *, tb, l, nc):
    c = pl.program_id(1)

    @pl.when(c == 0)
    def _():
        xext[0:8, :] = jnp.zeros((8, CONV_DIM), F32)
        xext[8 - (CONV_WIDTH - 1):8, :] = hist_ref[0]
        if tb < l:
            xext[8 + tb:8 + l, :] = jnp.zeros((l - tb, CONV_DIM), F32)
        state[...] = h0_ref[0].reshape(SSM_WIDTH, D_STATE)

    xext[8:8 + tb, :] = xbc_ref[0]
    conv = cb_ref[...]
    for k in range(CONV_WIDTH):
        lo = 8 - (CONV_WIDTH - 1) + k
        conv = conv + cw_ref[k:k + 1, :] * xext[lo:lo + l, :]
    xc = conv * _sigmoid(conv)
    xext[0:8, :] = xext[tb:tb + 8, :]

    dt = _softplus(sm_ref[0] + dtb_ref[...])
    if tb < l:
        dt = jnp.concatenate([dt, jnp.zeros((l - tb, LANES), F32)], axis=0)
    a = dt * (-jnp.exp(alog_ref[...]))
    ri = lax.broadcasted_iota(jnp.int32, (l, l), 0)
    ci = lax.broadcasted_iota(jnp.int32, (l, l), 1)
    tril = ri >= ci
    a_cs = _dot(jnp.where(tril, 1.0, 0.0), a, HIGHEST)
    a_cs_t = a_cs.T
    e = e_ref[...]
    dtx = _dot(dt, e, HIGHEST)
    eax = _dot(jnp.exp(a_cs[:tb]), e, HIGHEST)
    decx = _dot(jnp.exp(a_cs[l - 1:l, :] - a_cs), e, HIGHEST)
    tot = jnp.broadcast_to(jnp.exp(a_cs_t[:, l - 1:l]), (LANES, LANES))
    rtot = _dot(et_ref[...], tot, HIGHEST)

    xs = xc[:, :SSM_WIDTH]
    xd = xs * dtx
    xw = xd * decx
    lane = lax.broadcasted_iota(jnp.int32, (tb, LANES), 1)
    tril_q = tril[:tb]
    y_pairs = []
    for g in range(SSM_GROUPS):
        bg = xc[:, SSM_WIDTH + g * D_STATE:SSM_WIDTH + (g + 1) * D_STATE].astype(BF16)
        c_lo = SSM_WIDTH + SSM_GROUPS * D_STATE + g * D_STATE
        cg = xc[:tb, c_lo:c_lo + D_STATE].astype(BF16)
        cb = _dot_nt(cg, bg)
        for k in range(2):
            pair = 2 * g + k
            lo = pair * LANES
            xd_pair = xd[:, lo:lo + LANES].astype(BF16)
            ys = []
            for r2 in range(2):
                h = 2 * pair + r2
                seg = a_cs[:tb, h:h + 1] - a_cs_t[h:h + 1, :]
                lm = jnp.where(tril_q, jnp.exp(jnp.where(tril_q, seg, 0.0)), 0.0)
                ys.append(_dot((cb * lm).astype(BF16), xd_pair))
            y_diag = jnp.where(lane < SSM_HEAD_DIM, ys[0], ys[1])
            sp = state[lo:lo + LANES, :]
            y_off = _dot_nt(cg, sp.astype(BF16)) * eax[:, lo:lo + LANES]
            y_pairs.append(y_diag + y_off)
            upd = _dot(xw[:, lo:lo + LANES].T.astype(BF16), bg)
            state[lo:lo + LANES, :] = sp * rtot[lo:lo + LANES, :] + upd
    y = jnp.concatenate(y_pairs, axis=1) + dskip_ref[...] * xs[:tb]
    zz = z_ref[0]
    y = y * (zz * _sigmoid(zz))
    gw = SSM_WIDTH // SSM_GROUPS
    outs = []
    for g in range(SSM_GROUPS):
        yg = y[:, g * gw:(g + 1) * gw]
        ms = jnp.mean(yg * yg, -1, keepdims=True)
        outs.append(yg * lax.rsqrt(ms + EPS) * ng_ref[:, g * gw:(g + 1) * gw])
    y_ref[0] = jnp.concatenate(outs, axis=1).astype(y_ref.dtype)

    @pl.when(c == nc - 1)
    def _():
        hfin_ref[0] = state[...].reshape(SSM_HEADS, SSM_HEAD_DIM, D_STATE)


def _ssm(z, xbc, sm, hist, h0, conv_w, conv_b, dt_bias, a_log, d_skip, norm_g):
    bsz, t, _ = z.shape
    l = SSD_CHUNK
    tb = min(l, t)
    assert t % tb == 0 and tb % 8 == 0 and t >= CONV_WIDTH - 1
    nc = t // tb
    pad8 = lambda v: jnp.pad(v.reshape(1, SSM_HEADS), ((0, 0), (0, LANES - SSM_HEADS)))
    expand = np.zeros((LANES, SSM_WIDTH), np.float32)
    for h in range(SSM_HEADS):
        expand[h, h * SSM_HEAD_DIM:(h + 1) * SSM_HEAD_DIM] = 1.0
    blk = lambda b, c: (b, c, 0)
    per_b3 = lambda b, c: (b, 0, 0)
    per_b4 = lambda b, c: (b, 0, 0, 0)
    fixed = lambda b, c: (0, 0)
    return pl.pallas_call(
        functools.partial(_ssm_kernel, tb=tb, l=l, nc=nc),
        grid=(bsz, nc),
        in_specs=[pl.BlockSpec((1, tb, SSM_WIDTH), blk), pl.BlockSpec((1, tb, CONV_DIM), blk),
                  pl.BlockSpec((1, tb, LANES), blk),
                  pl.BlockSpec((1, CONV_WIDTH - 1, CONV_DIM), per_b3),
                  pl.BlockSpec((1, SSM_HEADS, SSM_HEAD_DIM, D_STATE), per_b4),
                  pl.BlockSpec((CONV_WIDTH, CONV_DIM), fixed), pl.BlockSpec((1, CONV_DIM), fixed),
                  pl.BlockSpec((1, LANES), fixed), pl.BlockSpec((1, LANES), fixed),
                  pl.BlockSpec((1, SSM_WIDTH), fixed), pl.BlockSpec((1, SSM_WIDTH), fixed),
                  pl.BlockSpec((LANES, SSM_WIDTH), fixed), pl.BlockSpec((SSM_WIDTH, LANES), fixed)],
        out_specs=[pl.BlockSpec((1, tb, SSM_WIDTH), blk),
                   pl.BlockSpec((1, SSM_HEADS, SSM_HEAD_DIM, D_STATE), per_b4)],
        out_shape=[jax.ShapeDtypeStruct((bsz, t, SSM_WIDTH), BF16),
                   jax.ShapeDtypeStruct((bsz, SSM_HEADS, SSM_HEAD_DIM, D_STATE), F32)],
        scratch_shapes=[pltpu.VMEM((8 + l, CONV_DIM), F32), pltpu.VMEM((SSM_WIDTH, D_STATE), F32)],
        compiler_params=_cparams(("parallel", "arbitrary")),
        name="ssm",
    )(z, xbc, sm, hist, h0, conv_w, conv_b.reshape(1, CONV_DIM), pad8(dt_bias), pad8(a_log),
      jnp.repeat(d_skip, SSM_HEAD_DIM).reshape(1, SSM_WIDTH), norm_g.reshape(1, SSM_WIDTH),
      jnp.asarray(expand), jnp.asarray(expand.T))


def _compress(load_pair, h, n_sub, cw):
    w1_ref, b1_ref, w2_ref, b2_ref = cw
    hid = jnp.zeros((n_sub, 2 * KV_ROWS), F32)
    for jp in range(D_CMP // 2):
        hid = hid + _dot(load_pair(h, jp), w1_ref[jp])
    pre = hid[:, :KV_ROWS] + pltpu.roll(hid[:, KV_ROWS:], n_sub - 1, 0) + b1_ref[...]
    return _dot(_gelu_tanh(pre).astype(BF16), w2_ref[...]) + b2_ref[...]


def _pad_q(q):
    return jnp.concatenate([q, jnp.zeros(q.shape, q.dtype)], axis=1)


def _select(score, qpos, n_slc, n_visible=None):
    jj = lax.broadcasted_iota(jnp.int32, score.shape, 1)
    cur = qpos // L_SEL
    visible = jj * L_SEL <= qpos
    forced = (jj == 0) | (jj == cur) | (jj == cur - 1)
    sc = jnp.where(visible, jnp.where(forced, FORCED_SCORE, score), -1.0)
    sc = jnp.where(jj < n_slc, sc, -2.0)
    lane = lax.broadcasted_iota(jnp.int32, (1, score.shape[1]), 1)

    def count(rank, ks):
        for k in ks:
            ck = sc[:, k:k + 1]
            tie = jnp.where(lane > k, 1.0, 0.0)
            rank = rank + jnp.where(ck > sc, 1.0, jnp.where(ck == sc, tie, 0.0))
        return rank

    rank = jnp.zeros(score.shape, F32)
    group = 8
    for k0 in range(0, n_slc, group):
        ks = range(k0, min(k0 + group, n_slc))
        if n_visible is None:
            rank = count(rank, ks)
        else:
            rank = lax.cond(k0 < n_visible, functools.partial(count, ks=ks), lambda r: r, rank)
    return jnp.where(rank < min(TOP_N, n_slc), jnp.where(sc >= 0.0, 1.0, 0.0), 0.0)


def _softmax_rows(s):
    m = jnp.max(s, -1, keepdims=True)
    e = jnp.where(s > MASKED_BELOW, jnp.exp(s - m), 0.0)
    return e / jnp.maximum(jnp.sum(e, -1, keepdims=True), 1e-30)


def _stack_heads(q_ref, h, i=0):
    return jnp.concatenate([q_ref[i, :, (h * GQA + g) * HEAD_DIM:(h * GQA + g + 1) * HEAD_DIM]
                            for g in range(GQA)], axis=0)


def _unstack_heads(o_ref, h, o, rows, i=0):
    for g in range(GQA):
        hd = h * GQA + g
        o_ref[i, :, hd * HEAD_DIM:(hd + 1) * HEAD_DIM] = o[g * rows:(g + 1) * rows]


def _cmp_prompt_kernel(kvc_ref, q_ref, near_ref, far_ref, w1_ref, b1_ref, w2_ref, b2_ref, ov_ref, o_ref, sel_ref,
                       kvcmp, *, n_sub, n_cmp, n_slc):
    qb = pl.program_id(1)

    @pl.when(qb == 0)
    def _():
        rows = lambda h, j: kvc_ref[h, pl.ds(j, n_sub, stride=D_CMP), :].astype(BF16)
        load = lambda h, jp: jnp.concatenate([rows(h, 2 * jp), rows(h, 2 * jp + 1)], axis=1)
        for h in range(KV_HEADS):
            kvcmp[h] = _compress(load, h, n_sub, (w1_ref, b1_ref, w2_ref, b2_ref)).astype(BF16)

    row = lax.broadcasted_iota(jnp.int32, (Q_BLOCK, n_sub), 0)
    blk = lax.broadcasted_iota(jnp.int32, (Q_BLOCK, n_sub), 1)
    qpos = qb * Q_BLOCK + row
    visible = (blk * D_CMP + (L_CMP - 1) <= qpos) & (blk < n_cmp)
    near_lo = qb * (Q_BLOCK // D_CMP) - (MAX_DISTANCE + L_CMP - 1) // D_CMP
    shift = lax.rem(near_lo + n_sub, n_sub)
    for h in range(KV_HEADS):
        bias = jnp.concatenate(
            [jnp.where(blk >= near_lo, pltpu.roll(near_ref[h * GQA + g], shift, 1),
                       far_ref[h * GQA + g:h * GQA + g + 1, :]) for g in range(GQA)], axis=0)
        kv = kvcmp[h]
        s = _dot_nt(_pad_q(_stack_heads(q_ref, h)), kv) + bias
        p = _softmax_rows(jnp.where(jnp.concatenate([visible] * GQA, axis=0), s, NEG))
        _unstack_heads(o_ref, h, _dot(p.astype(BF16), kv)[:, HEAD_DIM:], Q_BLOCK)
        pg = p[0:Q_BLOCK]
        for g in range(1, GQA):
            pg = pg + p[g * Q_BLOCK:(g + 1) * Q_BLOCK]
        sel_ref[0, h] = _select(_dot(pg, ov_ref[...], HIGHEST), qpos[:, 0:1], n_slc,
                                n_visible=(qb + 1) * (Q_BLOCK // L_SEL))


def _cmp_prompt(kvc, q, near, far, cmp_w, overlap):
    bsz, t, _ = q.shape
    n_sub = t // D_CMP
    n_cmp = n_sub - 1
    nqb = t // Q_BLOCK
    n_slc = t // L_SEL
    assert n_sub == LANES and n_slc <= LANES
    fixed = lambda a: pl.BlockSpec(a.shape, lambda b, i, _n=a.ndim: (0,) * _n)
    return pl.pallas_call(
        functools.partial(_cmp_prompt_kernel, n_sub=n_sub, n_cmp=n_cmp, n_slc=n_slc),
        grid=(bsz, nqb),
        in_specs=[pl.BlockSpec((KV_HEADS, t, KV_ROWS), lambda b, i: (0, b, 0)),
                  pl.BlockSpec((1, Q_BLOCK, ATT_WIDTH), lambda b, i: (b, i, 0)),
                  fixed(near), fixed(far)] + [fixed(a) for a in cmp_w] + [fixed(overlap)],
        out_specs=[pl.BlockSpec((1, Q_BLOCK, ATT_WIDTH), lambda b, i: (b, i, 0)),
                   pl.BlockSpec((1, KV_HEADS, Q_BLOCK, LANES), lambda b, i: (b, 0, i, 0))],
        out_shape=[jax.ShapeDtypeStruct((bsz, t, ATT_WIDTH), F32),
                   jax.ShapeDtypeStruct((bsz, KV_HEADS, t, LANES), F32)],
        scratch_shapes=[pltpu.VMEM((KV_HEADS, n_sub, KV_ROWS), BF16)],
        compiler_params=_cparams(("parallel", "arbitrary")),
        name="cmp_prompt",
    )(kvc, q, near, far, *cmp_w, overlap)


def _nsa_prompt_kernel(q_ref, kvs_ref, kvw_ref, sel_ref, eneg_ref, near_ref, far_ref, winb_ref, os_ref, ow_ref,
                       kaug, vsa, kwp, vwa, s_buf, mrun, acc, *, t):
    qb = pl.program_id(1)
    rows = GQA * Q_BLOCK
    near_w = 2 * Q_BLOCK
    win_w = WINDOW + Q_BLOCK

    @pl.when(qb == 0)
    def _():
        ones_row = jnp.where(lax.broadcasted_iota(jnp.int32, (KV_ROWS - HEAD_DIM, t), 0) == 0, 1.0, 0.0).astype(BF16)
        for h in range(KV_HEADS):
            lo = h * KV_ROWS
            kaug[h, :, 0:Q_BLOCK] = jnp.zeros((HEAD_DIM + SEL_PAD, Q_BLOCK), BF16)
            kaug[h, 0:HEAD_DIM, Q_BLOCK:] = kvs_ref[0, lo:lo + HEAD_DIM, :].astype(BF16)
            kaug[h, HEAD_DIM:, Q_BLOCK:] = eneg_ref[...]
            vsa[h, :, 0:Q_BLOCK] = jnp.zeros((KV_ROWS, Q_BLOCK), BF16)
            vsa[h, 0:HEAD_DIM, Q_BLOCK:] = kvs_ref[0, lo + HEAD_DIM:lo + KV_ROWS, :].astype(BF16)
            vsa[h, HEAD_DIM:, Q_BLOCK:] = ones_row
            kwp[h, :, 0:WINDOW] = jnp.zeros((HEAD_DIM, WINDOW), BF16)
            kwp[h, :, WINDOW:] = kvw_ref[0, lo:lo + HEAD_DIM, :].astype(BF16)
            vwa[h, :, 0:WINDOW] = jnp.zeros((KV_ROWS, WINDOW), BF16)
            vwa[h, 0:HEAD_DIM, WINDOW:] = kvw_ref[0, lo + HEAD_DIM:lo + KV_ROWS, :].astype(BF16)
            vwa[h, HEAD_DIM:, WINDOW:] = ones_row

    def normalise(a):
        return a[:, 0:HEAD_DIM] / a[:, HEAD_DIM:HEAD_DIM + 1]

    def tile_max(s):
        m = s[:, 0:LANES]
        for i in range(1, s.shape[1] // LANES):
            m = jnp.maximum(m, s[:, i * LANES:(i + 1) * LANES])
        return m

    start = pl.multiple_of(qb * Q_BLOCK, Q_BLOCK)
    n_chunk = t // SEL_CHUNK
    tiles_per_chunk = SEL_CHUNK // Q_BLOCK
    for h in range(KV_HEADS):
        q4 = _stack_heads(q_ref, h)
        notsel = (1.0 - sel_ref[0, h][:, 0:SEL_PAD]).astype(BF16)
        qa = jnp.concatenate([q4, jnp.concatenate([notsel] * GQA, axis=0)], axis=1)

        far = jnp.concatenate([far_ref[h]] * tiles_per_chunk, axis=1)
        mrun[...] = jnp.full((rows, LANES), NEG, F32)
        for c in range(n_chunk):
            @pl.when(c * tiles_per_chunk < qb - 1)
            def _(c=c):
                key = c * SEL_CHUNK + lax.broadcasted_iota(jnp.int32, (1, SEL_CHUNK), 1)
                late = jnp.where(key < (qb - 1) * Q_BLOCK, 0.0, NEG)
                lo = Q_BLOCK + c * SEL_CHUNK
                s = _dot(qa, kaug[h, :, lo:lo + SEL_CHUNK]) + far + late
                s_buf[:, c * SEL_CHUNK:(c + 1) * SEL_CHUNK] = s
                mrun[...] = jnp.maximum(mrun[...], tile_max(s))

        first = jnp.where(lax.broadcasted_iota(jnp.int32, (1, near_w), 1) < Q_BLOCK,
                          jnp.where(qb >= 1, 0.0, NEG), 0.0)
        s_near = _dot(qa, kaug[h, :, pl.ds(start, near_w)]) + near_ref[h] + first
        m = jnp.max(jnp.maximum(mrun[...], tile_max(s_near)), -1, keepdims=True)
        acc[...] = _dot_nt(jnp.exp(s_near - m).astype(BF16), vsa[h, :, pl.ds(start, near_w)])
        for c in range(n_chunk):
            @pl.when(c * tiles_per_chunk < qb - 1)
            def _(c=c):
                lo = Q_BLOCK + c * SEL_CHUNK
                p = jnp.exp(s_buf[:, c * SEL_CHUNK:(c + 1) * SEL_CHUNK] - m)
                acc[...] = acc[...] + _dot_nt(p.astype(BF16), vsa[h, :, lo:lo + SEL_CHUNK])
        _unstack_heads(os_ref, h, normalise(acc[...]), Q_BLOCK)

        pad = jnp.where(lax.broadcasted_iota(jnp.int32, (1, win_w), 1) < WINDOW - qb * Q_BLOCK, NEG, 0.0)
        s_w = _dot(q4, kwp[h, :, pl.ds(start, win_w)]) + winb_ref[h] + pad
        p = jnp.exp(s_w - jnp.max(tile_max(s_w), -1, keepdims=True))
        _unstack_heads(ow_ref, h, normalise(_dot_nt(p.astype(BF16), vwa[h, :, pl.ds(start, win_w)])), Q_BLOCK)


def _nsa_prompt(q, kvs_t, kvw_t, sel, eneg, near, far, win_bias):
    bsz, _, t = kvs_t.shape
    assert t % SEL_CHUNK == 0
    nqb = t // Q_BLOCK
    qblk = lambda b, i: (b, i, 0)
    per_b = lambda b, i: (b, 0, 0)
    fixed = lambda a: pl.BlockSpec(a.shape, lambda b, i, _n=a.ndim: (0,) * _n)
    rows = GQA * Q_BLOCK
    return pl.pallas_call(
        functools.partial(_nsa_prompt_kernel, t=t),
        grid=(bsz, nqb),
        in_specs=[pl.BlockSpec((1, Q_BLOCK, ATT_WIDTH), qblk), pl.BlockSpec((1, KV_COLS, t), per_b),
                  pl.BlockSpec((1, KV_COLS, t), per_b),
                  pl.BlockSpec((1, KV_HEADS, Q_BLOCK, LANES), lambda b, i: (b, 0, i, 0)),
                  fixed(eneg), fixed(near), fixed(far), fixed(win_bias)],
        out_specs=[pl.BlockSpec((1, Q_BLOCK, ATT_WIDTH), qblk), pl.BlockSpec((1, Q_BLOCK, ATT_WIDTH), qblk)],
        out_shape=[jax.ShapeDtypeStruct((bsz, t, ATT_WIDTH), F32)] * 2,
        scratch_shapes=[pltpu.VMEM((KV_HEADS, HEAD_DIM + SEL_PAD, Q_BLOCK + t), BF16),
                        pltpu.VMEM((KV_HEADS, KV_ROWS, Q_BLOCK + t), BF16),
                        pltpu.VMEM((KV_HEADS, HEAD_DIM, WINDOW + t), BF16),
                        pltpu.VMEM((KV_HEADS, KV_ROWS, WINDOW + t), BF16),
                        pltpu.VMEM((rows, t), F32), pltpu.VMEM((rows, LANES), F32),
                        pltpu.VMEM((rows, KV_ROWS), F32)],
        compiler_params=_cparams(("parallel", "arbitrary")),
        name="nsa_prompt",
    )(q, kvs_t, kvw_t, sel, eneg, near, far, win_bias)


def _page_copy(pages_hbm, page, buf, sem, slot, k):
    return pltpu.make_async_copy(pages_hbm.at[page], buf.at[slot, k], sem.at[slot])


def _stream_pages(pt_ref, pages_hbm, buf, sem, n_pages):
    b = pl.program_id(0)
    slot = lax.rem(b, 2)

    def start(row, into):
        for k in range(n_pages):
            _page_copy(pages_hbm, pt_ref[row, k], buf, sem, into, k).start()

    @pl.when(b == 0)
    def _():
        start(0, 0)

    @pl.when(b + 1 < pl.num_programs(0))
    def _():
        start(b + 1, 1 - slot)

    for k in range(n_pages):
        _page_copy(pages_hbm, 0, buf, sem, slot, k).wait()
    return slot


def _cmp_sample_kernel(pt_ref, pages_hbm, q_ref, bias_ref, perm_ref, w1_ref, b1_ref, w2_ref, b2_ref, ov_ref, o_ref,
                       sel_ref, buf, sem, xj, *, n_pages, n_slc, past, t):
    slot = _stream_pages(pt_ref, pages_hbm, buf, sem, n_pages)
    sub = PAGE_SIZE // D_CMP
    perm = perm_ref[...]
    per_tile = LANES // (2 * sub)
    for k2 in range(n_pages // 2):
        pair = jnp.concatenate([buf[slot, 2 * k2].astype(BF16), buf[slot, 2 * k2 + 1].astype(BF16)], axis=1)
        y = _dot(pair, perm)
        for h in range(KV_HEADS):
            for c in range(2 * PAGE_SIZE // LANES):
                xt = y[h * KV_ROWS:(h + 1) * KV_ROWS, c * LANES:(c + 1) * LANES].T
                for i in range(per_tile):
                    j = c * per_tile + i
                    xj[h, j // 2, 2 * sub * k2:2 * sub * (k2 + 1), (j % 2) * KV_ROWS:(j % 2 + 1) * KV_ROWS] = (
                        xt[2 * sub * i:2 * sub * (i + 1)].astype(BF16))
    n_sub = n_pages * sub
    load = lambda h, jp: xj[h, jp]
    qpos = past + lax.broadcasted_iota(jnp.int32, (t, 1), 0)
    for h in range(KV_HEADS):
        kv = _compress(load, h, n_sub, (w1_ref, b1_ref, w2_ref, b2_ref)).astype(BF16)
        qh = _pad_q(_stack_heads(q_ref, h).astype(BF16))
        p = _softmax_rows(_dot_nt(qh, kv) + bias_ref[h])
        _unstack_heads(o_ref, h, _dot(p.astype(BF16), kv)[:, HEAD_DIM:], t)
        pg = p[0:t]
        for g in range(1, GQA):
            pg = pg + p[g * t:(g + 1) * t]
        sel_ref[0, h] = _select(_dot(pg, ov_ref[...], HIGHEST), qpos, n_slc)


def _page_scratch(n_pages):
    return [pltpu.VMEM((2, n_pages, KV_COLS, PAGE_SIZE), F32), pltpu.SemaphoreType.DMA((2,))]


def _cmp_sample(pages, page_table, q, bias_cs, cmp_w, overlap, n_slc):
    bsz, n_pages = page_table.shape
    t = q.shape[1]
    width = overlap.shape[1]
    assert n_pages % 2 == 0
    sub = PAGE_SIZE // D_CMP
    col = np.arange(2 * PAGE_SIZE)
    page, row = col // PAGE_SIZE, col % PAGE_SIZE
    dest = (row % D_CMP) * 2 * sub + page * sub + row // D_CMP
    perm = jnp.asarray((dest[:, None] == col[None, :]).astype(np.float32), BF16)
    fixed = lambda a: pl.BlockSpec(a.shape, lambda b, pt, _n=a.ndim: (0,) * _n)
    grid_spec = pltpu.PrefetchScalarGridSpec(
        num_scalar_prefetch=1,
        grid=(bsz,),
        in_specs=[pl.BlockSpec(memory_space=pl.ANY), pl.BlockSpec((1, t, ATT_WIDTH), lambda b, pt: (b, 0, 0)),
                  fixed(bias_cs), fixed(perm)] + [fixed(a) for a in cmp_w] + [fixed(overlap)],
        out_specs=[pl.BlockSpec((1, t, ATT_WIDTH), lambda b, pt: (b, 0, 0)),
                   pl.BlockSpec((1, KV_HEADS, t, width), lambda b, pt: (b, 0, 0, 0))],
        scratch_shapes=_page_scratch(n_pages)
        + [pltpu.VMEM((KV_HEADS, D_CMP // 2, n_pages * sub, 2 * KV_ROWS), BF16)])
    return pl.pallas_call(
        functools.partial(_cmp_sample_kernel, n_pages=n_pages, n_slc=n_slc, past=n_pages * PAGE_SIZE, t=t),
        grid_spec=grid_spec,
        out_shape=[jax.ShapeDtypeStruct((bsz, t, ATT_WIDTH), F32),
                   jax.ShapeDtypeStruct((bsz, KV_HEADS, t, width), F32)],
        compiler_params=_cparams(("arbitrary",)),
        name="cmp_sample",
    )(page_table, pages, q, bias_cs, perm, *cmp_w, overlap)


def _joint_attend(s_past, vt_past, s_new, v_new):
    m = jnp.maximum(jnp.max(s_past, -1, keepdims=True), jnp.max(s_new, -1, keepdims=True))
    e_past = jnp.exp(s_past - m)
    e_new = jnp.exp(s_new - m)
    den = jnp.sum(e_past, -1, keepdims=True) + jnp.sum(e_new, -1, keepdims=True)
    acc = _dot_nt(e_past.astype(BF16), vt_past) + _dot(e_new.astype(BF16), v_new)
    return acc / den


def _pad_new_rows(new_ref, t, i=0):
    return jnp.concatenate([new_ref[i], jnp.zeros((LANES - t, KV_COLS), F32)], axis=0)


def _sel_sample_kernel(pt_ref, pages_hbm, new_ref, q_ref, sel_ref, far_ref, near_ref, biasn_ref, o_ref, buf, sem,
                       kvb, mask, *, n_pages, t):
    slot = _stream_pages(pt_ref, pages_hbm, buf, sem, n_pages)
    for k in range(n_pages):
        kvb[:, k * PAGE_SIZE:(k + 1) * PAGE_SIZE] = buf[slot, k].astype(BF16)
    kv_new = _pad_new_rows(new_ref, t).astype(BF16)
    rows = GQA * t
    lane = lax.broadcasted_iota(jnp.int32, (rows, LANES), 1)
    per_tile = LANES // L_SEL
    for h in range(KV_HEADS):
        lo = h * KV_ROWS
        sel4 = jnp.concatenate([sel_ref[0, h]] * GQA, axis=0)

        def tile_mask(k):
            cols = [jnp.broadcast_to(sel4[:, per_tile * k + i:per_tile * k + i + 1], (rows, LANES))
                    for i in range(per_tile)]
            m = cols[-1]
            for i in range(per_tile - 2, -1, -1):
                m = jnp.where(lane < (i + 1) * L_SEL, cols[i], m)
            return (m - 1.0) * (-NEG)

        far = jnp.concatenate([jnp.broadcast_to(far_ref[h * GQA + g:h * GQA + g + 1, :], (t, LANES))
                               for g in range(GQA)], axis=0)
        for k in range(n_pages - 1):
            mask[:, k * LANES:(k + 1) * LANES] = tile_mask(k) + far
        mask[:, (n_pages - 1) * LANES:n_pages * LANES] = tile_mask(n_pages - 1) + near_ref[h]
        qh = _stack_heads(q_ref, h).astype(BF16)
        s_past = _dot(qh, kvb[lo:lo + HEAD_DIM, :]) + mask[...]
        s_new = _dot_nt(qh, kv_new[:, lo:lo + HEAD_DIM]) + biasn_ref[h] + tile_mask(n_pages)
        o = _joint_attend(s_past, kvb[lo + HEAD_DIM:lo + KV_ROWS, :], s_new, kv_new[:, lo + HEAD_DIM:lo + KV_ROWS])
        _unstack_heads(o_ref, h, o, t)


def _sel_sample(pages, page_table, kvs_new, q, sel, far, near, bias_new):
    bsz, n_pages = page_table.shape
    t = q.shape[1]
    past = n_pages * PAGE_SIZE
    width = sel.shape[-1]
    assert (n_pages + 1) * (LANES // L_SEL) <= width
    per_b = lambda b, pt: (b, 0, 0)
    fixed = lambda a: pl.BlockSpec(a.shape, lambda b, pt, _n=a.ndim: (0,) * _n)
    grid_spec = pltpu.PrefetchScalarGridSpec(
        num_scalar_prefetch=1,
        grid=(bsz,),
        in_specs=[pl.BlockSpec(memory_space=pl.ANY),
                  pl.BlockSpec((1, t, KV_COLS), per_b), pl.BlockSpec((1, t, ATT_WIDTH), per_b),
                  pl.BlockSpec((1, KV_HEADS, t, width), lambda b, pt: (b, 0, 0, 0)),
                  fixed(far), fixed(near), fixed(bias_new)],
        out_specs=pl.BlockSpec((1, t, ATT_WIDTH), per_b),
        scratch_shapes=_page_scratch(n_pages)
        + [pltpu.VMEM((KV_COLS, past), BF16), pltpu.VMEM((GQA * t, past), F32)])
    return pl.pallas_call(
        functools.partial(_sel_sample_kernel, n_pages=n_pages, t=t),
        grid_spec=grid_spec,
        out_shape=jax.ShapeDtypeStruct((bsz, t, ATT_WIDTH), F32),
        compiler_params=_cparams(("arbitrary",)),
        name="sel_sample",
    )(page_table, pages, kvs_new, q, sel, far, near, bias_new)


def _win_sample_kernel(buf_ref, new_ref, q_ref, bias_ref, biasn_ref, o_ref, win_ref, *, t, nb):
    lane = lax.broadcasted_iota(jnp.int32, (KV_COLS, LANES), 1)
    for i in range(nb):
        buf = buf_ref[i]
        w = buf.shape[1]
        new = _pad_new_rows(new_ref, t, i)
        shifted = pltpu.roll(buf, w - t, 1)
        tail = pltpu.roll(new.T, LANES - t, 1)
        win_ref[i, :, 0:w - LANES] = shifted[:, 0:w - LANES]
        win_ref[i, :, w - LANES:w] = jnp.where(lane >= LANES - t, tail, shifted[:, w - LANES:w])
        kvb = buf.astype(BF16)
        kv_new = new.astype(BF16)
        for h in range(KV_HEADS):
            lo = h * KV_ROWS
            qh = _stack_heads(q_ref, h, i).astype(BF16)
            s_past = _dot(qh, kvb[lo:lo + HEAD_DIM, :]) + bias_ref[h]
            s_new = _dot_nt(qh, kv_new[:, lo:lo + HEAD_DIM]) + biasn_ref[h]
            o = _joint_attend(s_past, kvb[lo + HEAD_DIM:lo + KV_ROWS, :], s_new,
                              kv_new[:, lo + HEAD_DIM:lo + KV_ROWS])
            _unstack_heads(o_ref, h, o, t, i)


def _win_sample(buf_t, kvw_new, q, bias_buf, bias_new):
    bsz, _, w = buf_t.shape
    t = q.shape[1]
    nb = 4 if bsz % 4 == 0 else 1
    per_b = lambda b: (b, 0, 0)
    fixed = lambda a: pl.BlockSpec(a.shape, lambda b, _n=a.ndim: (0,) * _n)
    return pl.pallas_call(
        functools.partial(_win_sample_kernel, t=t, nb=nb),
        grid=(bsz // nb,),
        in_specs=[pl.BlockSpec((nb, KV_COLS, w), per_b), pl.BlockSpec((nb, t, KV_COLS), per_b),
                  pl.BlockSpec((nb, t, ATT_WIDTH), per_b), fixed(bias_buf), fixed(bias_new)],
        out_specs=[pl.BlockSpec((nb, t, ATT_WIDTH), per_b), pl.BlockSpec((nb, KV_COLS, w), per_b)],
        out_shape=[jax.ShapeDtypeStruct((bsz, t, ATT_WIDTH), F32),
                   jax.ShapeDtypeStruct((bsz, KV_COLS, w), F32)],
        compiler_params=_cparams(("parallel",)),
        name="win_sample",
    )(buf_t, kvw_new, q, bias_buf, bias_new)


def _combine_kernel(x_ref, y_ref, oc_ref, os_ref, ow_ref, sm_ref, eg_ref, eb_ref, ex_ref, ag_ref, wo_ref, g1_ref,
                    b1_ref, h_ref):
    xn = _layer_norm(x_ref[...], eg_ref[...], eb_ref[...])
    gates = _sigmoid(sm_ref[...])
    g_hi = gates.astype(BF16)
    g_lo = (gates - g_hi.astype(F32)).astype(BF16)
    o = jnp.zeros(oc_ref.shape, F32)
    for br, ref in enumerate((oc_ref, os_ref, ow_ref)):
        o = o + (_dot(g_hi, ex_ref[br]) + _dot(g_lo, ex_ref[br])) * ref[...]
    rms = lax.rsqrt(jnp.mean(o * o, -1, keepdims=True) + EPS)
    att = (o * rms * ag_ref[...]).astype(BF16)
    mix = _dot(jnp.concatenate([y_ref[...], att], axis=1), wo_ref[...])
    h_ref[...] = _layer_norm(ALPHA * xn + mix, g1_ref[...], b1_ref[...])


def _combine(x2d, y, oc, os_, ow, sm, eg, eb, gate_expand, ag, wo, g1, b1, tm):
    n = x2d.shape[0]
    tm = _row_tile(n, tm)
    row = lambda i: (i, 0)
    fixed = lambda a: pl.BlockSpec(a.shape, lambda i, _n=a.ndim: (0,) * _n)
    att = pl.BlockSpec((tm, ATT_WIDTH), row)
    return pl.pallas_call(
        _combine_kernel,
        grid=(n // tm,),
        in_specs=[pl.BlockSpec((tm, D_MODEL), row), pl.BlockSpec((tm, SSM_WIDTH), row), att, att, att,
                  pl.BlockSpec((tm, LANES), row), fixed(eg), fixed(eb), fixed(gate_expand), fixed(ag), fixed(wo),
                  fixed(g1), fixed(b1)],
        out_specs=pl.BlockSpec((tm, D_MODEL), row),
        out_shape=jax.ShapeDtypeStruct((n, D_MODEL), F32),
        compiler_params=_cparams(("parallel",)),
        name="combine",
    )(x2d, y, oc, os_, ow, sm, eg, eb, gate_expand, ag, wo, g1, b1)


def _ffn_kernel(h_ref, wu_ref, wd_ref, g_ref, b_ref, o_ref, acc, *, nk):
    k = pl.program_id(1)
    h = h_ref[...]
    u = jnp.maximum(_dot(h.astype(BF16), wu_ref[...]), 0.0)
    part = _dot((u * u).astype(BF16), wd_ref[...])

    @pl.when(k == 0)
    def _():
        acc[...] = part

    @pl.when(k > 0)
    def _():
        acc[...] = acc[...] + part

    @pl.when(k == nk - 1)
    def _():
        o_ref[...] = _layer_norm(ALPHA * h + acc[...], g_ref[...], b_ref[...])


def _ffn(h2d, wu, wd, g, b, tm, tf):
    n = h2d.shape[0]
    tm = _row_tile(n, tm)
    nk = D_FF // tf
    vec = pl.BlockSpec((1, D_MODEL), lambda i, k: (0, 0))
    return pl.pallas_call(
        functools.partial(_ffn_kernel, nk=nk),
        grid=(n // tm, nk),
        in_specs=[pl.BlockSpec((tm, D_MODEL), lambda i, k: (i, 0)),
                  pl.BlockSpec((D_MODEL, tf), lambda i, k: (0, k)),
                  pl.BlockSpec((tf, D_MODEL), lambda i, k: (k, 0)), vec, vec],
        out_specs=pl.BlockSpec((tm, D_MODEL), lambda i, k: (i, 0)),
        out_shape=jax.ShapeDtypeStruct((n, D_MODEL), F32),
        scratch_shapes=[pltpu.VMEM((tm, D_MODEL), F32)],
        compiler_params=_cparams(("parallel", "arbitrary")),
        name="ffn",
    )(h2d, wu, wd, g, b)


def _bucket_np(dist):
    d = np.maximum(dist, 0)
    exact = N_BUCKETS // 2
    far = exact + (np.log(np.maximum(d, 1).astype(np.float32) / np.float32(exact))
                   / np.float32(math.log(MAX_DISTANCE / exact)) * (N_BUCKETS - exact)).astype(np.int32)
    return np.where(d < exact, d, np.minimum(far, N_BUCKETS - 1)).astype(np.int32)


def _bias_lookup(tbl, dist, mask=None):
    dist = np.asarray(dist)
    onehot = np.eye(N_BUCKETS, dtype=np.float32)[_bucket_np(dist).reshape(-1)]
    b = jnp.dot(jnp.asarray(onehot), tbl, precision=HIGHEST).T.reshape((ATT_HEADS,) + dist.shape)
    return b if mask is None else jnp.where(jnp.asarray(mask)[None], b, NEG)


def _toeplitz_tile(tbl, offset, mask):
    period = 2 * Q_BLOCK
    k = np.arange(period)
    vals = _bias_lookup(tbl, offset - np.where(k < Q_BLOCK, k, k - period))
    tiled = jnp.tile(vals, (1, Q_BLOCK))[:, :Q_BLOCK * (period - 1)]
    t = tiled.reshape(ATT_HEADS, Q_BLOCK, period - 1)[:, :, :Q_BLOCK]
    return jnp.where(jnp.asarray(mask)[None], t, NEG)


def _stack_gt(tab, t):
    return tab.reshape(KV_HEADS, GQA * t, tab.shape[-1])


def _far_rows(tbl):
    return jnp.broadcast_to(tbl[N_BUCKETS - 1][:, None], (ATT_HEADS, LANES))


def _overlap(n_cmp_pad, n_cmp, width, n_slc):
    i = np.arange(n_cmp_pad)[:, None]
    j = np.arange(width)[None, :]
    ov = (i * D_CMP < (j + 1) * L_SEL) & (i * D_CMP + L_CMP > j * L_SEL) & (i < n_cmp) & (j < n_slc)
    return jnp.asarray(ov.astype(np.float32))


def _prep_cmp_weights(w1, b1, w2, b2):
    eye = jnp.eye(2, dtype=F32)
    w1r = (w1[:, :, :, :, None, :] * eye[None, None, :, None, :, None]).transpose(1, 2, 3, 0, 4, 5)
    w1r = w1r.reshape(D_CMP // 2, 2 * KV_ROWS, 2 * KV_ROWS).astype(BF16)
    w2r = (w2[:, :, None, :] * eye[:, None, :, None]).reshape(2 * CMP_HID, KV_ROWS).astype(BF16)
    return (w1r, b1.reshape(1, 2 * CMP_HID), w2r, b2.reshape(1, KV_ROWS))


def _prep_w_in(w_in):
    sizes = (SSM_WIDTH, CONV_DIM, SSM_HEADS, ATT_WIDTH, KV_COLS, KV_COLS, KV_COLS)
    z, xbc, dt, q, kvc, kvs, kvw, gates = jnp.split(w_in, np.cumsum(sizes).tolist(), axis=1)
    small = jnp.concatenate([dt, gates], axis=1)
    small = jnp.pad(small, ((0, 0), (0, LANES - small.shape[1])))
    return jnp.concatenate([z, xbc, q * ATT_SCALE, kvc, kvs, kvw, small], axis=1).astype(BF16)


def _gate_expand():
    ex = np.zeros((N_BRANCH, LANES, ATT_WIDTH), np.float32)
    for br in range(N_BRANCH):
        for hd in range(ATT_HEADS):
            ex[br, SSM_HEADS + br * ATT_HEADS + hd, hd * HEAD_DIM:(hd + 1) * HEAD_DIM] = 1.0
    return jnp.asarray(ex, BF16)


def _feature_major(a):
    lead = a.shape[:-4]
    rows = a.shape[-4]
    return jnp.moveaxis(a.reshape(lead + (rows, KV_COLS)), -2, -1)


def _row_major6(a_t):
    bsz, _, rows = a_t.shape
    return jnp.moveaxis(a_t, 1, 2).reshape(1, bsz, rows, KV_HEADS, 2, HEAD_DIM)


def kernel(x_prompt, x_sample, cache_cmp_kv, cache_slc_kv, cache_win_kv, state_conv, state_ssm, page_table,
           rel_bias_table, emb_ln_g, emb_ln_b, w_in, conv_w, conv_b, dt_bias, a_log, d_skip, ssm_norm_g,
           cmp_w1, cmp_b1, cmp_w2, cmp_b2, att_norm_g, w_out, ln1_g, ln1_b, w_up, w_down, ln2_g, ln2_b):
    assert w_in.shape[0] == DEPTH
    bp, tp, _ = x_prompt.shape
    bs, ts, _ = x_sample.shape
    n_pages = page_table.shape[1]
    past = n_pages * PAGE_SIZE
    w_buf = cache_win_kv.shape[2]
    assert ts < D_CMP and ts % 8 == 0 and w_buf == WINDOW and past >= WINDOW and tp >= WINDOW
    tbl = rel_bias_table
    vec = lambda v: v.reshape(1, -1)

    w_proj = _prep_w_in(w_in[0])
    cmp_w = _prep_cmp_weights(cmp_w1[0], cmp_b1[0], cmp_w2[0], cmp_b2[0])
    wo = w_out[0].astype(BF16)
    wu = w_up[0].astype(BF16)
    wd = w_down[0].astype(BF16)
    eg, eb = vec(emb_ln_g), vec(emb_ln_b)
    gate_expand = _gate_expand()
    far = _far_rows(tbl)

    def trunk_tail(x2d, y, oc, os_, ow, sm):
        h = _combine(x2d, y, oc, os_, ow, sm, eg, eb, gate_expand, vec(att_norm_g[0]), wo, vec(ln1_g[0]),
                     vec(ln1_b[0]), 256)
        return _ffn(h, wu, wd, vec(ln2_g[0]), vec(ln2_b[0]), 1024, D_FF // 4)

    ssm_w = (conv_w[0], conv_b[0], dt_bias[0], a_log[0], d_skip[0], ssm_norm_g[0])

    xp2 = x_prompt.reshape(bp * tp, D_MODEL)
    z, xbc, q, kvc, kvc_t, kvs_t, kvw_t, sm = _proj(xp2, eg, eb, w_proj, BF16, 512, seq=(bp, tp))
    r3 = lambda a: a.reshape(bp, tp, a.shape[-1])
    xbc3 = r3(xbc)
    y_ssm, h_new = _ssm(r3(z), xbc3, r3(sm), jnp.zeros((bp, CONV_WIDTH - 1, CONV_DIM), F32),
                        jnp.zeros((bp, SSM_HEADS, SSM_HEAD_DIM, D_STATE), F32), *ssm_w)
    n_sub = tp // D_CMP
    n_slc = tp // L_SEL
    band = (np.arange(Q_BLOCK)[:, None] - D_CMP * np.arange(NEAR_W)[None, :]
            + D_CMP * ((MAX_DISTANCE + L_CMP - 1) // D_CMP) - (L_CMP - 1))
    near_c = jnp.pad(_bias_lookup(tbl, band), ((0, 0), (0, 0), (0, n_sub - NEAR_W)))
    oc, sel = _cmp_prompt(kvc, r3(q), near_c, far, cmp_w,
                          _overlap(n_sub, n_sub - 1, LANES, n_slc))
    ii = np.arange(Q_BLOCK)[:, None] - np.arange(Q_BLOCK)[None, :]
    tiles_gq = lambda rs, ok: jnp.concatenate(
        [_toeplitz_tile(tbl, Q_BLOCK * r, ok(ii + Q_BLOCK * r)) for r in rs], axis=2).reshape(
            KV_HEADS, GQA * Q_BLOCK, len(rs) * Q_BLOCK)
    near = tiles_gq((1, 0), lambda d: d >= 0)
    win_bias = tiles_gq(range(WINDOW // Q_BLOCK, -1, -1), lambda d: (d >= 0) & (d < WINDOW))
    far_gq = jnp.repeat(far, Q_BLOCK, axis=0).reshape(KV_HEADS, GQA * Q_BLOCK, LANES)
    assert n_slc <= SEL_PAD
    eneg = jnp.asarray(np.where(np.arange(SEL_PAD)[:, None] == (np.arange(tp) // L_SEL)[None, :], NEG, 0.0), BF16)
    os_, ow = _nsa_prompt(r3(q), kvs_t, kvw_t, sel, eneg, near, far_gq, win_bias)
    f2 = lambda a: a.reshape(bp * tp, a.shape[-1])
    y_prompt = trunk_tail(xp2, f2(y_ssm), f2(oc), f2(os_), f2(ow), sm).reshape(bp, tp, D_MODEL)
    w = min(WINDOW, tp)
    prompt_state = (_row_major6(kvc_t), _row_major6(kvs_t), _row_major6(kvw_t[:, :, tp - w:]),
                    xbc3[:, tp - (CONV_WIDTH - 1):][None], h_new[None])

    xs2 = x_sample.reshape(bs * ts, D_MODEL)
    z, xbc, q, kvc, kvs, kvw, sm = _proj(xs2, eg, eb, w_proj, F32, 512)
    r3 = lambda a: a.reshape(bs, ts, a.shape[-1])
    xbc3 = r3(xbc)
    y_ssm, h_new = _ssm(r3(z), xbc3, r3(sm), state_conv[0], state_ssm[0], *ssm_w)
    n_sub = past // D_CMP
    n_cmp = n_sub - 1
    n_slc = -(-(past + ts) // L_SEL)
    width = -(-n_slc // LANES) * LANES
    qpos = past + np.arange(ts)
    dist_c = qpos[:, None] - (np.arange(n_sub) * D_CMP + L_CMP - 1)[None, :]
    bias_cs = _stack_gt(_bias_lookup(tbl, dist_c, (dist_c >= 0) & (np.arange(n_sub) < n_cmp)[None, :]), ts)
    oc, sel = _cmp_sample(_feature_major(cache_cmp_kv[0]), page_table, r3(q), bias_cs, cmp_w,
                          _overlap(n_sub, n_cmp, width, n_slc), n_slc)
    dist_last = qpos[:, None] - (past - PAGE_SIZE + np.arange(PAGE_SIZE))[None, :]
    dist_n = np.arange(ts)[:, None] - np.arange(LANES)[None, :]
    bias_new = _stack_gt(_bias_lookup(tbl, dist_n, (dist_n >= 0) & (np.arange(LANES) < ts)[None, :]), ts)
    os_ = _sel_sample(_feature_major(cache_slc_kv[0]), page_table, r3(kvs), r3(q), sel, far,
                      _stack_gt(_bias_lookup(tbl, dist_last), ts), bias_new)
    dist_w = qpos[:, None] - (past - w_buf + np.arange(w_buf))[None, :]
    bias_wb = _stack_gt(_bias_lookup(tbl, dist_w, (dist_w >= 0) & (dist_w < WINDOW)), ts)
    ow, win_new_t = _win_sample(_feature_major(cache_win_kv[0]), r3(kvw), r3(q), bias_wb, bias_new)
    f2 = lambda a: a.reshape(bs * ts, a.shape[-1])
    y_sample = trunk_tail(xs2, f2(y_ssm), f2(oc), f2(os_), f2(ow), sm).reshape(bs, ts, D_MODEL)
    kv6 = lambda a: a.reshape(1, bs, ts, KV_HEADS, 2, HEAD_DIM)
    sample_state = (kv6(kvc), kv6(kvs), _row_major6(win_new_t),
                    xbc3[:, ts - (CONV_WIDTH - 1):][None], h_new[None])

    return (y_prompt, y_sample) + prompt_state + sample_state
```

```python
import functools
import math

import numpy as np
import jax
import jax.numpy as jnp
from jax import lax
from jax.experimental import pallas as pl
from jax.experimental.pallas import tpu as pltpu

F32 = jnp.float32
BF16 = jnp.bfloat16
HIGHEST = lax.Precision.HIGHEST

D_MODEL = 1024
SSM_HEADS = 8
SSM_HEAD_DIM = 64
SSM_WIDTH = SSM_HEADS * SSM_HEAD_DIM
SSM_GROUPS = 2
D_STATE = 128
CONV_WIDTH = 4
CONV_DIM = SSM_WIDTH + 2 * SSM_GROUPS * D_STATE
SSD_CHUNK = 128
ATT_HEADS = 8
KV_HEADS = 2
GQA = ATT_HEADS // KV_HEADS
HEAD_DIM = 64
ATT_WIDTH = ATT_HEADS * HEAD_DIM
KV_COLS = KV_HEADS * 2 * HEAD_DIM
D_CMP = 16
L_CMP = 2 * D_CMP
CMP_HID = 64
L_SEL = 64
TOP_N = 16
WINDOW = 512
Q_BLOCK = 128
N_BRANCH = 3
FORCED_SCORE = 1e4
N_BUCKETS = 32
MAX_DISTANCE = 128
D_FF = 4 * D_MODEL
DEPTH = 1
ALPHA = (2 * DEPTH) ** 0.25
ATT_SCALE = HEAD_DIM ** -0.5
EPS = 1e-5
PAGE_SIZE = 128

LANES = 128
NEG = -1e30
MASKED_BELOW = -1e29
VMEM_LIMIT = 48 * 1024 * 1024
KV_ROWS = 2 * HEAD_DIM
SEL_PAD = 32
NEAR_W = 32
SEL_CHUNK = 512

_OFF_Z = 0
_OFF_XBC = _OFF_Z + SSM_WIDTH
_OFF_Q = _OFF_XBC + CONV_DIM
_OFF_KVC = _OFF_Q + ATT_WIDTH
_OFF_KVS = _OFF_KVC + KV_COLS
_OFF_KVW = _OFF_KVS + KV_COLS
_OFF_SM = _OFF_KVW + KV_COLS
_N_PROJ = _OFF_SM + LANES


def _cparams(sem):
    return pltpu.CompilerParams(dimension_semantics=sem, vmem_limit_bytes=VMEM_LIMIT)


def _row_tile(n, preferred):
    tm = min(n, preferred)
    assert n % tm == 0 and tm % 8 == 0
    return tm


def _dot(a, b, precision=None):
    return jnp.dot(a, b, preferred_element_type=F32, precision=precision)


def _dot_nt(a, b):
    return lax.dot_general(a, b, (((1,), (1,)), ((), ())), preferred_element_type=F32)


def _layer_norm(x, g, b):
    mu = jnp.mean(x, -1, keepdims=True)
    xc = x - mu
    var = jnp.mean(xc * xc, -1, keepdims=True)
    return xc * lax.rsqrt(var + EPS) * g + b


def _sigmoid(x):
    return 1.0 / (1.0 + jnp.exp(-x))


def _softplus(x):
    return jnp.maximum(x, 0.0) + jnp.log(1.0 + jnp.exp(-jnp.abs(x)))


def _gelu_tanh(x):
    c = math.sqrt(2.0 / math.pi)
    return 0.5 * x * (1.0 + jnp.tanh(c * (x + 0.044715 * (x * x * x))))


def _proj_kernel(x_ref, g_ref, b_ref, w_ref, z_ref, xbc_ref, q_ref, kvc_ref, *rest, feature_major):
    xn = _layer_norm(x_ref[...], g_ref[...], b_ref[...]).astype(BF16)

    def mm(lo, hi):
        return _dot(xn, w_ref[:, lo:hi])

    z_ref[...] = mm(_OFF_Z, _OFF_XBC)
    xbc_ref[...] = mm(_OFF_XBC, _OFF_Q)
    q_ref[...] = mm(_OFF_Q, _OFF_KVC).astype(q_ref.dtype)
    kvc = mm(_OFF_KVC, _OFF_KVS)
    if feature_major:
        kvct_ref, kvst_ref, kvwt_ref, sm_ref = rest
        for h in range(KV_HEADS):
            kvc_ref[h] = kvc[:, h * KV_ROWS:(h + 1) * KV_ROWS]
        kvct_ref[0] = kvc.T
        kvst_ref[0] = mm(_OFF_KVS, _OFF_KVW).T
        kvwt_ref[0] = mm(_OFF_KVW, _OFF_SM).T
    else:
        kvs_ref, kvw_ref, sm_ref = rest
        kvc_ref[...] = kvc
        kvs_ref[...] = mm(_OFF_KVS, _OFF_KVW)
        kvw_ref[...] = mm(_OFF_KVW, _OFF_SM)
    sm_ref[...] = mm(_OFF_SM, _N_PROJ)


def _proj(x2d, g, b, w, q_dtype, tm, seq=None):
    n = x2d.shape[0]
    tm = _row_tile(n, tm)
    row = lambda i: (i, 0)
    fixed = lambda i: (0, 0)
    rm = lambda wd, dt: (pl.BlockSpec((tm, wd), row), jax.ShapeDtypeStruct((n, wd), dt))
    outs = [rm(SSM_WIDTH, F32), rm(CONV_DIM, F32), rm(ATT_WIDTH, q_dtype)]
    if seq is None:
        outs += [rm(KV_COLS, F32), rm(KV_COLS, F32), rm(KV_COLS, F32)]
    else:
        outs.append((pl.BlockSpec((KV_HEADS, tm, KV_ROWS), lambda i: (0, i, 0)),
                     jax.ShapeDtypeStruct((KV_HEADS, n, KV_ROWS), F32)))
        bsz, t = seq
        assert t % tm == 0 and tm % LANES == 0
        per = t // tm
        fm = (pl.BlockSpec((1, KV_COLS, tm), lambda i: (i // per, 0, i % per)),
              jax.ShapeDtypeStruct((bsz, KV_COLS, t), F32))
        outs += [fm, fm, fm]
    outs.append(rm(LANES, F32))
    return pl.pallas_call(
        functools.partial(_proj_kernel, feature_major=seq is not None),
        grid=(n // tm,),
        in_specs=[pl.BlockSpec((tm, D_MODEL), row), pl.BlockSpec((1, D_MODEL), fixed),
                  pl.BlockSpec((1, D_MODEL), fixed), pl.BlockSpec((D_MODEL, _N_PROJ), fixed)],
        out_specs=[o[0] for o in outs],
        out_shape=[o[1] for o in outs],
        compiler_params=_cparams(("parallel",)),
        name="proj",
    )(x2d, g, b, w)


def _ssm_kernel(z_ref, xbc_ref, sm_ref, hist_ref, h0_ref, cw_ref, cb_ref, dtb_ref, alog_ref, dskip_ref,
                ng_ref, y_ref, hfin_ref, xext, state, *, tb, l, nc):
    c = pl.program_id(1)

    @pl.when(c == 0)
    def _():
        xext[0:8, :] = jnp.zeros((8, CONV_DIM), F32)
        xext[8 - (CONV_WIDTH - 1):8, :] = hist_ref[0]
        if tb < l:
            xext[8 + tb:8 + l, :] = jnp.zeros((l - tb, CONV_DIM), F32)
        state[...] = h0_ref[0].reshape(SSM_WIDTH, D_STATE)

    xext[8:8 + tb, :] = xbc_ref[0]
    conv = cb_ref[...]
    for k in range(CONV_WIDTH):
        lo = 8 - (CONV_WIDTH - 1) + k
        conv = conv + cw_ref[k:k + 1, :] * xext[lo:lo + l, :]
    xc = conv * _sigmoid(conv)
    xext[0:8, :] = xext[tb:tb + 8, :]

    dt = _softplus(sm_ref[0] + dtb_ref[...])
    if tb < l:
        dt = jnp.concatenate([dt, jnp.zeros((l - tb, LANES), F32)], axis=0)
    a = dt * (-jnp.exp(alog_ref[...]))
    ri = lax.broadcasted_iota(jnp.int32, (l, l), 0)
    ci = lax.broadcasted_iota(jnp.int32, (l, l), 1)
    tril = ri >= ci
    a_cs = _dot(jnp.where(tril, 1.0, 0.0), a, HIGHEST)
    a_cs_t = a_cs.T
    dt_t = dt.T
    ea = jnp.exp(a_cs[:tb])
    wend = dt * jnp.exp(a_cs[l - 1:l, :] - a_cs)
    etot = jnp.exp(a_cs[l - 1:l, :])

    xs = xc[:, :SSM_WIDTH]
    lane = lax.broadcasted_iota(jnp.int32, (tb, LANES), 1)
    srow = lax.broadcasted_iota(jnp.int32, (LANES, D_STATE), 0)
    tril_q = tril[:tb]
    y_pairs = []
    for g in range(SSM_GROUPS):
        bg_f = xc[:, SSM_WIDTH + g * D_STATE:SSM_WIDTH + (g + 1) * D_STATE]
        bg = bg_f.astype(BF16)
        c_lo = SSM_WIDTH + SSM_GROUPS * D_STATE + g * D_STATE
        cg = xc[:tb, c_lo:c_lo + D_STATE].astype(BF16)
        cb = _dot_nt(cg, bg)
        for k in range(2):
            pair = 2 * g + k
            lo = pair * LANES
            h0, h1 = 2 * pair, 2 * pair + 1
            xs_pair = xs[:, lo:lo + LANES]
            xs_b = xs_pair.astype(BF16)
            xs_t = xs_pair.T
            ys, upd = [], []
            for r2, h in enumerate((h0, h1)):
                seg = a_cs[:tb, h:h + 1] - a_cs_t[h:h + 1, :]
                lm = jnp.where(tril_q, jnp.exp(jnp.where(tril_q, seg, 0.0)), 0.0) * dt_t[h:h + 1, :]
                ys.append(_dot((cb * lm).astype(BF16), xs_b))
                upd.append(_dot(xs_t[r2 * SSM_HEAD_DIM:(r2 + 1) * SSM_HEAD_DIM].astype(BF16),
                                (bg_f * wend[:, h:h + 1]).astype(BF16)))
            y_diag = jnp.where(lane < SSM_HEAD_DIM, ys[0], ys[1])
            sp = state[lo:lo + LANES, :]
            y_off = _dot_nt(cg, sp.astype(BF16)) * jnp.where(lane < SSM_HEAD_DIM, ea[:, h0:h0 + 1], ea[:, h1:h1 + 1])
            y_pairs.append(y_diag + y_off)
            keep = jnp.where(srow < SSM_HEAD_DIM, etot[:, h0:h0 + 1], etot[:, h1:h1 + 1])
            state[lo:lo + LANES, :] = sp * keep + jnp.concatenate(upd, axis=0)
    y = jnp.concatenate(y_pairs, axis=1) + dskip_ref[...] * xs[:tb]
    zz = z_ref[0]
    y = y * (zz * _sigmoid(zz))
    gw = SSM_WIDTH // SSM_GROUPS
    outs = []
    for g in range(SSM_GROUPS):
        yg = y[:, g * gw:(g + 1) * gw]
        ms = jnp.mean(yg * yg, -1, keepdims=True)
        outs.append(yg * lax.rsqrt(ms + EPS) * ng_ref[:, g * gw:(g + 1) * gw])
    y_ref[0] = jnp.concatenate(outs, axis=1).astype(y_ref.dtype)

    @pl.when(c == nc - 1)
    def _():
        hfin_ref[0] = state[...].reshape(SSM_HEADS, SSM_HEAD_DIM, D_STATE)


def _ssm(z, xbc, sm, hist, h0, conv_w, conv_b, dt_bias, a_log, d_skip, norm_g):
    bsz, t, _ = z.shape
    l = SSD_CHUNK
    tb = min(l, t)
    assert t % tb == 0 and tb % 8 == 0 and t >= CONV_WIDTH - 1
    nc = t // tb
    pad8 = lambda v: jnp.pad(v.reshape(1, SSM_HEADS), ((0, 0), (0, LANES - SSM_HEADS)))
    blk = lambda b, c: (b, c, 0)
    per_b3 = lambda b, c: (b, 0, 0)
    per_b4 = lambda b, c: (b, 0, 0, 0)
    fixed = lambda b, c: (0, 0)
    return pl.pallas_call(
        functools.partial(_ssm_kernel, tb=tb, l=l, nc=nc),
        grid=(bsz, nc),
        in_specs=[pl.BlockSpec((1, tb, SSM_WIDTH), blk), pl.BlockSpec((1, tb, CONV_DIM), blk),
                  pl.BlockSpec((1, tb, LANES), blk),
                  pl.BlockSpec((1, CONV_WIDTH - 1, CONV_DIM), per_b3),
                  pl.BlockSpec((1, SSM_HEADS, SSM_HEAD_DIM, D_STATE), per_b4),
                  pl.BlockSpec((CONV_WIDTH, CONV_DIM), fixed), pl.BlockSpec((1, CONV_DIM), fixed),
                  pl.BlockSpec((1, LANES), fixed), pl.BlockSpec((1, LANES), fixed),
                  pl.BlockSpec((1, SSM_WIDTH), fixed), pl.BlockSpec((1, SSM_WIDTH), fixed)],
        out_specs=[pl.BlockSpec((1, tb, SSM_WIDTH), blk),
                   pl.BlockSpec((1, SSM_HEADS, SSM_HEAD_DIM, D_STATE), per_b4)],
        out_shape=[jax.ShapeDtypeStruct((bsz, t, SSM_WIDTH), BF16),
                   jax.ShapeDtypeStruct((bsz, SSM_HEADS, SSM_HEAD_DIM, D_STATE), F32)],
        scratch_shapes=[pltpu.VMEM((8 + l, CONV_DIM), F32), pltpu.VMEM((SSM_WIDTH, D_STATE), F32)],
        compiler_params=_cparams(("parallel", "arbitrary")),
        name="ssm",
    )(z, xbc, sm, hist, h0, conv_w, conv_b.reshape(1, CONV_DIM), pad8(dt_bias), pad8(a_log),
      jnp.repeat(d_skip, SSM_HEAD_DIM).reshape(1, SSM_WIDTH), norm_g.reshape(1, SSM_WIDTH))


def _compress(load_pair, h, n_sub, cw):
    w1_ref, b1_ref, w2_ref, b2_ref = cw
    hid = jnp.zeros((n_sub, 2 * KV_ROWS), F32)
    for jp in range(D_CMP // 2):
        hid = hid + _dot(load_pair(h, jp), w1_ref[jp])
    pre = hid[:, :KV_ROWS] + pltpu.roll(hid[:, KV_ROWS:], n_sub - 1, 0) + b1_ref[...]
    return _dot(_gelu_tanh(pre).astype(BF16), w2_ref[...]) + b2_ref[...]


def _pad_q(q):
    return jnp.concatenate([q, jnp.zeros(q.shape, q.dtype)], axis=1)


def _select(score, qpos, n_slc, n_visible=None):
    jj = lax.broadcasted_iota(jnp.int32, score.shape, 1)
    cur = qpos // L_SEL
    visible = jj * L_SEL <= qpos
    forced = (jj == 0) | (jj == cur) | (jj == cur - 1)
    sc = jnp.where(visible, jnp.where(forced, FORCED_SCORE, score), -1.0)
    sc = jnp.where(jj < n_slc, sc, -2.0)
    lane = lax.broadcasted_iota(jnp.int32, (1, score.shape[1]), 1)

    def count(rank, ks):
        for k in ks:
            ck = sc[:, k:k + 1]
            tie = jnp.where(lane > k, 1.0, 0.0)
            rank = rank + jnp.where(ck > sc, 1.0, jnp.where(ck == sc, tie, 0.0))
        return rank

    rank = jnp.zeros(score.shape, F32)
    group = 8
    for k0 in range(0, n_slc, group):
        ks = range(k0, min(k0 + group, n_slc))
        if n_visible is None:
            rank = count(rank, ks)
        else:
            rank = lax.cond(k0 < n_visible, functools.partial(count, ks=ks), lambda r: r, rank)
    return jnp.where(rank < min(TOP_N, n_slc), jnp.where(sc >= 0.0, 1.0, 0.0), 0.0)


def _softmax_rows(s):
    m = jnp.max(s, -1, keepdims=True)
    e = jnp.where(s > MASKED_BELOW, jnp.exp(s - m), 0.0)
    return e / jnp.maximum(jnp.sum(e, -1, keepdims=True), 1e-30)


def _stack_heads(q_ref, h, i=0):
    return jnp.concatenate([q_ref[i, :, (h * GQA + g) * HEAD_DIM:(h * GQA + g + 1) * HEAD_DIM]
                            for g in range(GQA)], axis=0)


def _unstack_heads(o_ref, h, o, rows, i=0):
    for g in range(GQA):
        hd = h * GQA + g
        o_ref[i, :, hd * HEAD_DIM:(hd + 1) * HEAD_DIM] = o[g * rows:(g + 1) * rows]


def _cmp_prompt_kernel(kvc_ref, q_ref, bias_ref, w1_ref, b1_ref, w2_ref, b2_ref, ov_ref, o_ref, sel_ref,
                       kvcmp, *, n_sub, n_slc):
    qb = pl.program_id(1)

    @pl.when(qb == 0)
    def _():
        rows = lambda h, j: kvc_ref[h, pl.ds(j, n_sub, stride=D_CMP), :].astype(BF16)
        load = lambda h, jp: jnp.concatenate([rows(h, 2 * jp), rows(h, 2 * jp + 1)], axis=1)
        for h in range(KV_HEADS):
            kvcmp[h] = _compress(load, h, n_sub, (w1_ref, b1_ref, w2_ref, b2_ref)).astype(BF16)

    qpos = qb * Q_BLOCK + lax.broadcasted_iota(jnp.int32, (Q_BLOCK, 1), 0)
    for h in range(KV_HEADS):
        kv = kvcmp[h]
        s = _dot_nt(_pad_q(_stack_heads(q_ref, h)), kv) + bias_ref[h]
        p = _softmax_rows(s)
        _unstack_heads(o_ref, h, _dot(p.astype(BF16), kv)[:, HEAD_DIM:], Q_BLOCK)
        pg = p[0:Q_BLOCK]
        for g in range(1, GQA):
            pg = pg + p[g * Q_BLOCK:(g + 1) * Q_BLOCK]
        sel_ref[0, h] = _select(_dot(pg, ov_ref[...], HIGHEST), qpos, n_slc,
                                n_visible=(qb + 1) * (Q_BLOCK // L_SEL))


def _cmp_prompt(kvc, q, bias, cmp_w, overlap):
    bsz, t, _ = q.shape
    n_sub = t // D_CMP
    n_cmp = n_sub - 1
    nqb = t // Q_BLOCK
    n_slc = t // L_SEL
    assert n_sub == LANES and n_slc <= LANES
    fixed = lambda a: pl.BlockSpec(a.shape, lambda b, i, _n=a.ndim: (0,) * _n)
    return pl.pallas_call(
        functools.partial(_cmp_prompt_kernel, n_sub=n_sub, n_slc=n_slc),
        grid=(bsz, nqb),
        in_specs=[pl.BlockSpec((KV_HEADS, t, KV_ROWS), lambda b, i: (0, b, 0)),
                  pl.BlockSpec((1, Q_BLOCK, ATT_WIDTH), lambda b, i: (b, i, 0)),
                  pl.BlockSpec((KV_HEADS, GQA * Q_BLOCK, n_sub), lambda b, i: (0, i, 0))]
        + [fixed(a) for a in cmp_w] + [fixed(overlap)],
        out_specs=[pl.BlockSpec((1, Q_BLOCK, ATT_WIDTH), lambda b, i: (b, i, 0)),
                   pl.BlockSpec((1, KV_HEADS, Q_BLOCK, LANES), lambda b, i: (b, 0, i, 0))],
        out_shape=[jax.ShapeDtypeStruct((bsz, t, ATT_WIDTH), F32),
                   jax.ShapeDtypeStruct((bsz, KV_HEADS, t, LANES), F32)],
        scratch_shapes=[pltpu.VMEM((KV_HEADS, n_sub, KV_ROWS), BF16)],
        compiler_params=_cparams(("parallel", "arbitrary")),
        name="cmp_prompt",
    )(kvc, q, bias, *cmp_w, overlap)


def _nsa_prompt_kernel(q_ref, kvs_ref, kvw_ref, sel_ref, eneg_ref, near_ref, far_ref, winb_ref, os_ref, ow_ref,
                       kaug, vsa, kwp, vwa, s_buf, mrun, acc, *, t):
    qb = pl.program_id(1)
    rows = GQA * Q_BLOCK
    near_w = 2 * Q_BLOCK
    win_w = WINDOW + Q_BLOCK

    @pl.when(qb == 0)
    def _():
        ones_row = jnp.where(lax.broadcasted_iota(jnp.int32, (KV_ROWS - HEAD_DIM, t), 0) == 0, 1.0, 0.0).astype(BF16)
        for h in range(KV_HEADS):
            lo = h * KV_ROWS
            kaug[h, :, 0:Q_BLOCK] = jnp.zeros((HEAD_DIM + SEL_PAD, Q_BLOCK), BF16)
            kaug[h, 0:HEAD_DIM, Q_BLOCK:] = kvs_ref[0, lo:lo + HEAD_DIM, :].astype(BF16)
            kaug[h, HEAD_DIM:, Q_BLOCK:] = eneg_ref[...]
            vsa[h, :, 0:Q_BLOCK] = jnp.zeros((KV_ROWS, Q_BLOCK), BF16)
            vsa[h, 0:HEAD_DIM, Q_BLOCK:] = kvs_ref[0, lo + HEAD_DIM:lo + KV_ROWS, :].astype(BF16)
            vsa[h, HEAD_DIM:, Q_BLOCK:] = ones_row
            kwp[h, :, 0:WINDOW] = jnp.zeros((HEAD_DIM, WINDOW), BF16)
            kwp[h, :, WINDOW:] = kvw_ref[0, lo:lo + HEAD_DIM, :].astype(BF16)
            vwa[h, :, 0:WINDOW] = jnp.zeros((KV_ROWS, WINDOW), BF16)
            vwa[h, 0:HEAD_DIM, WINDOW:] = kvw_ref[0, lo + HEAD_DIM:lo + KV_ROWS, :].astype(BF16)
            vwa[h, HEAD_DIM:, WINDOW:] = ones_row

    def normalise(a):
        return a[:, 0:HEAD_DIM] / a[:, HEAD_DIM:HEAD_DIM + 1]

    def tile_max(s):
        m = s[:, 0:LANES]
        for i in range(1, s.shape[1] // LANES):
            m = jnp.maximum(m, s[:, i * LANES:(i + 1) * LANES])
        return m

    start = pl.multiple_of(qb * Q_BLOCK, Q_BLOCK)
    n_chunk = t // SEL_CHUNK
    tiles_per_chunk = SEL_CHUNK // Q_BLOCK
    for h in range(KV_HEADS):
        q4 = _stack_heads(q_ref, h)
        notsel = (1.0 - sel_ref[0, h][:, 0:SEL_PAD]).astype(BF16)
        qa = jnp.concatenate([q4, jnp.concatenate([notsel] * GQA, axis=0)], axis=1)

        far = jnp.concatenate([far_ref[h]] * tiles_per_chunk, axis=1)
        mrun[...] = jnp.full((rows, LANES), NEG, F32)
        for c in range(n_chunk):
            @pl.when(c * tiles_per_chunk < qb - 1)
            def _(c=c):
                key = c * SEL_CHUNK + lax.broadcasted_iota(jnp.int32, (1, SEL_CHUNK), 1)
                late = jnp.where(key < (qb - 1) * Q_BLOCK, 0.0, NEG)
                lo = Q_BLOCK + c * SEL_CHUNK
                s = _dot(qa, kaug[h, :, lo:lo + SEL_CHUNK]) + far + late
                s_buf[:, c * SEL_CHUNK:(c + 1) * SEL_CHUNK] = s
                mrun[...] = jnp.maximum(mrun[...], tile_max(s))

        first = jnp.where(lax.broadcasted_iota(jnp.int32, (1, near_w), 1) < Q_BLOCK,
                          jnp.where(qb >= 1, 0.0, NEG), 0.0)
        s_near = _dot(qa, kaug[h, :, pl.ds(start, near_w)]) + near_ref[h] + first
        m = jnp.max(jnp.maximum(mrun[...], tile_max(s_near)), -1, keepdims=True)
        acc[...] = _dot_nt(jnp.exp(s_near - m).astype(BF16), vsa[h, :, pl.ds(start, near_w)])
        for c in range(n_chunk):
            @pl.when(c * tiles_per_chunk < qb - 1)
            def _(c=c):
                lo = Q_BLOCK + c * SEL_CHUNK
                p = jnp.exp(s_buf[:, c * SEL_CHUNK:(c + 1) * SEL_CHUNK] - m)
                acc[...] = acc[...] + _dot_nt(p.astype(BF16), vsa[h, :, lo:lo + SEL_CHUNK])
        _unstack_heads(os_ref, h, normalise(acc[...]), Q_BLOCK)

        pad = jnp.where(lax.broadcasted_iota(jnp.int32, (1, win_w), 1) < WINDOW - qb * Q_BLOCK, NEG, 0.0)
        s_w = _dot(q4, kwp[h, :, pl.ds(start, win_w)]) + winb_ref[h] + pad
        p = jnp.exp(s_w - jnp.max(tile_max(s_w), -1, keepdims=True))
        _unstack_heads(ow_ref, h, normalise(_dot_nt(p.astype(BF16), vwa[h, :, pl.ds(start, win_w)])), Q_BLOCK)


def _nsa_prompt(q, kvs_t, kvw_t, sel, eneg, near, far, win_bias):
    bsz, _, t = kvs_t.shape
    assert t % SEL_CHUNK == 0
    nqb = t // Q_BLOCK
    qblk = lambda b, i: (b, i, 0)
    per_b = lambda b, i: (b, 0, 0)
    fixed = lambda a: pl.BlockSpec(a.shape, lambda b, i, _n=a.ndim: (0,) * _n)
    rows = GQA * Q_BLOCK
    return pl.pallas_call(
        functools.partial(_nsa_prompt_kernel, t=t),
        grid=(bsz, nqb),
        in_specs=[pl.BlockSpec((1, Q_BLOCK, ATT_WIDTH), qblk), pl.BlockSpec((1, KV_COLS, t), per_b),
                  pl.BlockSpec((1, KV_COLS, t), per_b),
                  pl.BlockSpec((1, KV_HEADS, Q_BLOCK, LANES), lambda b, i: (b, 0, i, 0)),
                  fixed(eneg), fixed(near), fixed(far), fixed(win_bias)],
        out_specs=[pl.BlockSpec((1, Q_BLOCK, ATT_WIDTH), qblk), pl.BlockSpec((1, Q_BLOCK, ATT_WIDTH), qblk)],
        out_shape=[jax.ShapeDtypeStruct((bsz, t, ATT_WIDTH), F32)] * 2,
        scratch_shapes=[pltpu.VMEM((KV_HEADS, HEAD_DIM + SEL_PAD, Q_BLOCK + t), BF16),
                        pltpu.VMEM((KV_HEADS, KV_ROWS, Q_BLOCK + t), BF16),
                        pltpu.VMEM((KV_HEADS, HEAD_DIM, WINDOW + t), BF16),
                        pltpu.VMEM((KV_HEADS, KV_ROWS, WINDOW + t), BF16),
                        pltpu.VMEM((rows, t), F32), pltpu.VMEM((rows, LANES), F32),
                        pltpu.VMEM((rows, KV_ROWS), F32)],
        compiler_params=_cparams(("parallel", "arbitrary")),
        name="nsa_prompt",
    )(q, kvs_t, kvw_t, sel, eneg, near, far, win_bias)


def _page_copy(pages_hbm, page, buf, sem, slot, k):
    return pltpu.make_async_copy(pages_hbm.at[page], buf.at[slot, k], sem.at[slot])


def _stream_pages(pt_ref, pages_hbm, buf, sem, n_pages):
    b = pl.program_id(0)
    slot = lax.rem(b, 2)

    def start(row, into):
        for k in range(n_pages):
            _page_copy(pages_hbm, pt_ref[row, k], buf, sem, into, k).start()

    @pl.when(b == 0)
    def _():
        start(0, 0)

    @pl.when(b + 1 < pl.num_programs(0))
    def _():
        start(b + 1, 1 - slot)

    for k in range(n_pages):
        _page_copy(pages_hbm, 0, buf, sem, slot, k).wait()
    return slot


def _cmp_sample_kernel(pt_ref, pages_hbm, q_ref, bias_ref, perm_ref, w1_ref, b1_ref, w2_ref, b2_ref, ov_ref, o_ref,
                       sel_ref, buf, sem, xj, *, n_pages, n_slc, past, t):
    slot = _stream_pages(pt_ref, pages_hbm, buf, sem, n_pages)
    sub = PAGE_SIZE // D_CMP
    perm = perm_ref[...]
    per_tile = LANES // (2 * sub)
    for k2 in range(n_pages // 2):
        pair = jnp.concatenate([buf[slot, 2 * k2].astype(BF16), buf[slot, 2 * k2 + 1].astype(BF16)], axis=1)
        y = _dot(pair, perm)
        for h in range(KV_HEADS):
            for c in range(2 * PAGE_SIZE // LANES):
                xt = y[h * KV_ROWS:(h + 1) * KV_ROWS, c * LANES:(c + 1) * LANES].T
                for i in range(per_tile):
                    j = c * per_tile + i
                    xj[h, j // 2, 2 * sub * k2:2 * sub * (k2 + 1), (j % 2) * KV_ROWS:(j % 2 + 1) * KV_ROWS] = (
                        xt[2 * sub * i:2 * sub * (i + 1)].astype(BF16))
    n_sub = n_pages * sub
    load = lambda h, jp: xj[h, jp]
    qpos = past + lax.broadcasted_iota(jnp.int32, (t, 1), 0)
    for h in range(KV_HEADS):
        kv = _compress(load, h, n_sub, (w1_ref, b1_ref, w2_ref, b2_ref)).astype(BF16)
        qh = _pad_q(_stack_heads(q_ref, h).astype(BF16))
        p = _softmax_rows(_dot_nt(qh, kv) + bias_ref[h])
        _unstack_heads(o_ref, h, _dot(p.astype(BF16), kv)[:, HEAD_DIM:], t)
        pg = p[0:t]
        for g in range(1, GQA):
            pg = pg + p[g * t:(g + 1) * t]
        sel_ref[0, h] = _select(_dot(pg, ov_ref[...], HIGHEST), qpos, n_slc)


def _page_scratch(n_pages):
    return [pltpu.VMEM((2, n_pages, KV_COLS, PAGE_SIZE), F32), pltpu.SemaphoreType.DMA((2,))]


def _cmp_sample(pages, page_table, q, bias_cs, cmp_w, overlap, n_slc):
    bsz, n_pages = page_table.shape
    t = q.shape[1]
    width = overlap.shape[1]
    assert n_pages % 2 == 0
    sub = PAGE_SIZE // D_CMP
    col = np.arange(2 * PAGE_SIZE)
    page, row = col // PAGE_SIZE, col % PAGE_SIZE
    dest = (row % D_CMP) * 2 * sub + page * sub + row // D_CMP
    perm = jnp.asarray((dest[:, None] == col[None, :]).astype(np.float32), BF16)
    fixed = lambda a: pl.BlockSpec(a.shape, lambda b, pt, _n=a.ndim: (0,) * _n)
    grid_spec = pltpu.PrefetchScalarGridSpec(
        num_scalar_prefetch=1,
        grid=(bsz,),
        in_specs=[pl.BlockSpec(memory_space=pl.ANY), pl.BlockSpec((1, t, ATT_WIDTH), lambda b, pt: (b, 0, 0)),
                  fixed(bias_cs), fixed(perm)] + [fixed(a) for a in cmp_w] + [fixed(overlap)],
        out_specs=[pl.BlockSpec((1, t, ATT_WIDTH), lambda b, pt: (b, 0, 0)),
                   pl.BlockSpec((1, KV_HEADS, t, width), lambda b, pt: (b, 0, 0, 0))],
        scratch_shapes=_page_scratch(n_pages)
        + [pltpu.VMEM((KV_HEADS, D_CMP // 2, n_pages * sub, 2 * KV_ROWS), BF16)])
    return pl.pallas_call(
        functools.partial(_cmp_sample_kernel, n_pages=n_pages, n_slc=n_slc, past=n_pages * PAGE_SIZE, t=t),
        grid_spec=grid_spec,
        out_shape=[jax.ShapeDtypeStruct((bsz, t, ATT_WIDTH), F32),
                   jax.ShapeDtypeStruct((bsz, KV_HEADS, t, width), F32)],
        compiler_params=_cparams(("arbitrary",)),
        name="cmp_sample",
    )(page_table, pages, q, bias_cs, perm, *cmp_w, overlap)


def _joint_attend(s_past, vt_past, s_new, v_new):
    m = jnp.maximum(jnp.max(s_past, -1, keepdims=True), jnp.max(s_new, -1, keepdims=True))
    e_past = jnp.exp(s_past - m)
    e_new = jnp.exp(s_new - m)
    den = jnp.sum(e_past, -1, keepdims=True) + jnp.sum(e_new, -1, keepdims=True)
    acc = _dot_nt(e_past.astype(BF16), vt_past) + _dot(e_new.astype(BF16), v_new)
    return acc / den


def _pad_new_rows(new_ref, t, i=0):
    return jnp.concatenate([new_ref[i], jnp.zeros((LANES - t, KV_COLS), F32)], axis=0)


def _sel_sample_kernel(pt_ref, pages_hbm, new_ref, q_ref, sel_ref, far_ref, near_ref, biasn_ref, o_ref, buf, sem,
                       kvb, mask, *, n_pages, t):
    slot = _stream_pages(pt_ref, pages_hbm, buf, sem, n_pages)
    for k in range(n_pages):
        kvb[:, k * PAGE_SIZE:(k + 1) * PAGE_SIZE] = buf[slot, k].astype(BF16)
    kv_new = _pad_new_rows(new_ref, t).astype(BF16)
    rows = GQA * t
    lane = lax.broadcasted_iota(jnp.int32, (rows, LANES), 1)
    per_tile = LANES // L_SEL
    for h in range(KV_HEADS):
        lo = h * KV_ROWS
        sel4 = jnp.concatenate([sel_ref[0, h]] * GQA, axis=0)

        def tile_mask(k):
            cols = [jnp.broadcast_to(sel4[:, per_tile * k + i:per_tile * k + i + 1], (rows, LANES))
                    for i in range(per_tile)]
            m = cols[-1]
            for i in range(per_tile - 2, -1, -1):
                m = jnp.where(lane < (i + 1) * L_SEL, cols[i], m)
            return (m - 1.0) * (-NEG)

        far = jnp.concatenate([jnp.broadcast_to(far_ref[h * GQA + g:h * GQA + g + 1, :], (t, LANES))
                               for g in range(GQA)], axis=0)
        for k in range(n_pages - 1):
            mask[:, k * LANES:(k + 1) * LANES] = tile_mask(k) + far
        mask[:, (n_pages - 1) * LANES:n_pages * LANES] = tile_mask(n_pages - 1) + near_ref[h]
        qh = _stack_heads(q_ref, h).astype(BF16)
        s_past = _dot(qh, kvb[lo:lo + HEAD_DIM, :]) + mask[...]
        s_new = _dot_nt(qh, kv_new[:, lo:lo + HEAD_DIM]) + biasn_ref[h] + tile_mask(n_pages)
        o = _joint_attend(s_past, kvb[lo + HEAD_DIM:lo + KV_ROWS, :], s_new, kv_new[:, lo + HEAD_DIM:lo + KV_ROWS])
        _unstack_heads(o_ref, h, o, t)


def _sel_sample(pages, page_table, kvs_new, q, sel, far, near, bias_new):
    bsz, n_pages = page_table.shape
    t = q.shape[1]
    past = n_pages * PAGE_SIZE
    width = sel.shape[-1]
    assert (n_pages + 1) * (LANES // L_SEL) <= width
    per_b = lambda b, pt: (b, 0, 0)
    fixed = lambda a: pl.BlockSpec(a.shape, lambda b, pt, _n=a.ndim: (0,) * _n)
    grid_spec = pltpu.PrefetchScalarGridSpec(
        num_scalar_prefetch=1,
        grid=(bsz,),
        in_specs=[pl.BlockSpec(memory_space=pl.ANY),
                  pl.BlockSpec((1, t, KV_COLS), per_b), pl.BlockSpec((1, t, ATT_WIDTH), per_b),
                  pl.BlockSpec((1, KV_HEADS, t, width), lambda b, pt: (b, 0, 0, 0)),
                  fixed(far), fixed(near), fixed(bias_new)],
        out_specs=pl.BlockSpec((1, t, ATT_WIDTH), per_b),
        scratch_shapes=_page_scratch(n_pages)
        + [pltpu.VMEM((KV_COLS, past), BF16), pltpu.VMEM((GQA * t, past), F32)])
    return pl.pallas_call(
        functools.partial(_sel_sample_kernel, n_pages=n_pages, t=t),
        grid_spec=grid_spec,
        out_shape=jax.ShapeDtypeStruct((bsz, t, ATT_WIDTH), F32),
        compiler_params=_cparams(("arbitrary",)),
        name="sel_sample",
    )(page_table, pages, kvs_new, q, sel, far, near, bias_new)


def _win_sample_kernel(buf_ref, new_ref, q_ref, bias_ref, biasn_ref, o_ref, win_ref, *, t, nb):
    lane = lax.broadcasted_iota(jnp.int32, (KV_COLS, LANES), 1)
    for i in range(nb):
        buf = buf_ref[i]
        w = buf.shape[1]
        new = _pad_new_rows(new_ref, t, i)
        shifted = pltpu.roll(buf, w - t, 1)
        tail = pltpu.roll(new.T, LANES - t, 1)
        win_ref[i, :, 0:w - LANES] = shifted[:, 0:w - LANES]
        win_ref[i, :, w - LANES:w] = jnp.where(lane >= LANES - t, tail, shifted[:, w - LANES:w])
        kvb = buf.astype(BF16)
        kv_new = new.astype(BF16)
        for h in range(KV_HEADS):
            lo = h * KV_ROWS
            qh = _stack_heads(q_ref, h, i).astype(BF16)
            s_past = _dot(qh, kvb[lo:lo + HEAD_DIM, :]) + bias_ref[h]
            s_new = _dot_nt(qh, kv_new[:, lo:lo + HEAD_DIM]) + biasn_ref[h]
            o = _joint_attend(s_past, kvb[lo + HEAD_DIM:lo + KV_ROWS, :], s_new,
                              kv_new[:, lo + HEAD_DIM:lo + KV_ROWS])
            _unstack_heads(o_ref, h, o, t, i)


def _win_sample(buf_t, kvw_new, q, bias_buf, bias_new):
    bsz, _, w = buf_t.shape
    t = q.shape[1]
    nb = 4 if bsz % 4 == 0 else 1
    per_b = lambda b: (b, 0, 0)
    fixed = lambda a: pl.BlockSpec(a.shape, lambda b, _n=a.ndim: (0,) * _n)
    return pl.pallas_call(
        functools.partial(_win_sample_kernel, t=t, nb=nb),
        grid=(bsz // nb,),
        in_specs=[pl.BlockSpec((nb, KV_COLS, w), per_b), pl.BlockSpec((nb, t, KV_COLS), per_b),
                  pl.BlockSpec((nb, t, ATT_WIDTH), per_b), fixed(bias_buf), fixed(bias_new)],
        out_specs=[pl.BlockSpec((nb, t, ATT_WIDTH), per_b), pl.BlockSpec((nb, KV_COLS, w), per_b)],
        out_shape=[jax.ShapeDtypeStruct((bsz, t, ATT_WIDTH), F32),
                   jax.ShapeDtypeStruct((bsz, KV_COLS, w), F32)],
        compiler_params=_cparams(("parallel",)),
        name="win_sample",
    )(buf_t, kvw_new, q, bias_buf, bias_new)


def _combine_kernel(x_ref, y_ref, oc_ref, os_ref, ow_ref, sm_ref, eg_ref, eb_ref, ex_ref, ag_ref, wo_ref, g1_ref,
                    b1_ref, h_ref):
    xn = _layer_norm(x_ref[...], eg_ref[...], eb_ref[...])
    gates = _sigmoid(sm_ref[...])
    g_hi = gates.astype(BF16)
    g_lo = (gates - g_hi.astype(F32)).astype(BF16)
    o = jnp.zeros(oc_ref.shape, F32)
    for br, ref in enumerate((oc_ref, os_ref, ow_ref)):
        o = o + (_dot(g_hi, ex_ref[br]) + _dot(g_lo, ex_ref[br])) * ref[...]
    rms = lax.rsqrt(jnp.mean(o * o, -1, keepdims=True) + EPS)
    att = (o * rms * ag_ref[...]).astype(BF16)
    mix = _dot(jnp.concatenate([y_ref[...], att], axis=1), wo_ref[...])
    h_ref[...] = _layer_norm(ALPHA * xn + mix, g1_ref[...], b1_ref[...])


def _combine(x2d, y, oc, os_, ow, sm, eg, eb, gate_expand, ag, wo, g1, b1, tm):
    n = x2d.shape[0]
    tm = _row_tile(n, tm)
    row = lambda i: (i, 0)
    fixed = lambda a: pl.BlockSpec(a.shape, lambda i, _n=a.ndim: (0,) * _n)
    att = pl.BlockSpec((tm, ATT_WIDTH), row)
    return pl.pallas_call(
        _combine_kernel,
        grid=(n // tm,),
        in_specs=[pl.BlockSpec((tm, D_MODEL), row), pl.BlockSpec((tm, SSM_WIDTH), row), att, att, att,
                  pl.BlockSpec((tm, LANES), row), fixed(eg), fixed(eb), fixed(gate_expand), fixed(ag), fixed(wo),
                  fixed(g1), fixed(b1)],
        out_specs=pl.BlockSpec((tm, D_MODEL), row),
        out_shape=jax.ShapeDtypeStruct((n, D_MODEL), F32),
        compiler_params=_cparams(("parallel",)),
        name="combine",
    )(x2d, y, oc, os_, ow, sm, eg, eb, gate_expand, ag, wo, g1, b1)


def _ffn_kernel(h_ref, wu_ref, wd_ref, g_ref, b_ref, o_ref, acc, *, nk):
    k = pl.program_id(1)
    h = h_ref[...]
    u = jnp.maximum(_dot(h.astype(BF16), wu_ref[...]), 0.0)
    part = _dot((u * u).astype(BF16), wd_ref[...])

    @pl.when(k == 0)
    def _():
        acc[...] = part

    @pl.when(k > 0)
    def _():
        acc[...] = acc[...] + part

    @pl.when(k == nk - 1)
    def _():
        o_ref[...] = _layer_norm(ALPHA * h + acc[...], g_ref[...], b_ref[...])


def _ffn(h2d, wu, wd, g, b, tm, tf):
    n = h2d.shape[0]
    tm = _row_tile(n, tm)
    nk = D_FF // tf
    vec = pl.BlockSpec((1, D_MODEL), lambda i, k: (0, 0))
    return pl.pallas_call(
        functools.partial(_ffn_kernel, nk=nk),
        grid=(n // tm, nk),
        in_specs=[pl.BlockSpec((tm, D_MODEL), lambda i, k: (i, 0)),
                  pl.BlockSpec((D_MODEL, tf), lambda i, k: (0, k)),
                  pl.BlockSpec((tf, D_MODEL), lambda i, k: (k, 0)), vec, vec],
        out_specs=pl.BlockSpec((tm, D_MODEL), lambda i, k: (i, 0)),
        out_shape=jax.ShapeDtypeStruct((n, D_MODEL), F32),
        scratch_shapes=[pltpu.VMEM((tm, D_MODEL), F32)],
        compiler_params=_cparams(("parallel", "arbitrary")),
        name="ffn",
    )(h2d, wu, wd, g, b)


def _bucket_np(dist):
    d = np.maximum(dist, 0)
    exact = N_BUCKETS // 2
    far = exact + (np.log(np.maximum(d, 1).astype(np.float32) / np.float32(exact))
                   / np.float32(math.log(MAX_DISTANCE / exact)) * (N_BUCKETS - exact)).astype(np.int32)
    return np.where(d < exact, d, np.minimum(far, N_BUCKETS - 1)).astype(np.int32)


def _bias_lookup(tbl, dist, mask=None):
    dist = np.asarray(dist)
    onehot = np.eye(N_BUCKETS, dtype=np.float32)[_bucket_np(dist).reshape(-1)]
    b = jnp.dot(jnp.asarray(onehot), tbl, precision=HIGHEST).T.reshape((ATT_HEADS,) + dist.shape)
    return b if mask is None else jnp.where(jnp.asarray(mask)[None], b, NEG)


def _toeplitz_tile(tbl, offset, mask):
    period = 2 * Q_BLOCK
    k = np.arange(period)
    vals = _bias_lookup(tbl, offset - np.where(k < Q_BLOCK, k, k - period))
    tiled = jnp.tile(vals, (1, Q_BLOCK))[:, :Q_BLOCK * (period - 1)]
    t = tiled.reshape(ATT_HEADS, Q_BLOCK, period - 1)[:, :, :Q_BLOCK]
    return jnp.where(jnp.asarray(mask)[None], t, NEG)


def _cmp_prompt_bias(tbl, t):
    n_sub = t // D_CMP
    back = (MAX_DISTANCE + L_CMP - 1) // D_CMP
    band = (np.arange(Q_BLOCK)[:, None] - D_CMP * np.arange(NEAR_W)[None, :] + D_CMP * back - (L_CMP - 1))
    near = jnp.pad(_bias_lookup(tbl, band), ((0, 0), (0, 0), (0, n_sub - NEAR_W)))
    far = tbl[N_BUCKETS - 1][:, None, None]
    blk = np.arange(n_sub)[None, :]
    tiles = []
    for qb in range(t // Q_BLOCK):
        near_lo = qb * (Q_BLOCK // D_CMP) - back
        qpos = qb * Q_BLOCK + np.arange(Q_BLOCK)[:, None]
        visible = (blk * D_CMP + (L_CMP - 1) <= qpos) & (blk < n_sub - 1)
        tile = jnp.where(jnp.asarray(blk >= near_lo)[None], jnp.roll(near, near_lo % n_sub, axis=2), far)
        tiles.append(jnp.where(jnp.asarray(visible)[None], tile, NEG))
    tiles = jnp.stack(tiles).reshape(t // Q_BLOCK, KV_HEADS, GQA * Q_BLOCK, n_sub)
    return jnp.moveaxis(tiles, 0, 1).reshape(KV_HEADS, -1, n_sub)


def _stack_gt(tab, t):
    return tab.reshape(KV_HEADS, GQA * t, tab.shape[-1])


def _far_rows(tbl):
    return jnp.broadcast_to(tbl[N_BUCKETS - 1][:, None], (ATT_HEADS, LANES))


def _overlap(n_cmp_pad, n_cmp, width, n_slc):
    i = np.arange(n_cmp_pad)[:, None]
    j = np.arange(width)[None, :]
    ov = (i * D_CMP < (j + 1) * L_SEL) & (i * D_CMP + L_CMP > j * L_SEL) & (i < n_cmp) & (j < n_slc)
    return jnp.asarray(ov.astype(np.float32))


def _prep_cmp_weights(w1, b1, w2, b2):
    eye = jnp.eye(2, dtype=F32)
    w1r = (w1[:, :, :, :, None, :] * eye[None, None, :, None, :, None]).transpose(1, 2, 3, 0, 4, 5)
    w1r = w1r.reshape(D_CMP // 2, 2 * KV_ROWS, 2 * KV_ROWS).astype(BF16)
    w2r = (w2[:, :, None, :] * eye[:, None, :, None]).reshape(2 * CMP_HID, KV_ROWS).astype(BF16)
    return (w1r, b1.reshape(1, 2 * CMP_HID), w2r, b2.reshape(1, KV_ROWS))


def _prep_w_in(w_in):
    sizes = (SSM_WIDTH, CONV_DIM, SSM_HEADS, ATT_WIDTH, KV_COLS, KV_COLS, KV_COLS)
    z, xbc, dt, q, kvc, kvs, kvw, gates = jnp.split(w_in, np.cumsum(sizes).tolist(), axis=1)
    small = jnp.concatenate([dt, gates], axis=1)
    small = jnp.pad(small, ((0, 0), (0, LANES - small.shape[1])))
    return jnp.concatenate([z, xbc, q * ATT_SCALE, kvc, kvs, kvw, small], axis=1).astype(BF16)


def _gate_expand():
    ex = np.zeros((N_BRANCH, LANES, ATT_WIDTH), np.float32)
    for br in range(N_BRANCH):
        for hd in range(ATT_HEADS):
            ex[br, SSM_HEADS + br * ATT_HEADS + hd, hd * HEAD_DIM:(hd + 1) * HEAD_DIM] = 1.0
    return jnp.asarray(ex, BF16)


def _feature_major(a):
    lead = a.shape[:-4]
    rows = a.shape[-4]
    return jnp.moveaxis(a.reshape(lead + (rows, KV_COLS)), -2, -1)


def _row_major6(a_t):
    bsz, _, rows = a_t.shape
    return jnp.moveaxis(a_t, 1, 2).reshape(1, bsz, rows, KV_HEADS, 2, HEAD_DIM)


def kernel(x_prompt, x_sample, cache_cmp_kv, cache_slc_kv, cache_win_kv, state_conv, state_ssm, page_table,
           rel_bias_table, emb_ln_g, emb_ln_b, w_in, conv_w, conv_b, dt_bias, a_log, d_skip, ssm_norm_g,
           cmp_w1, cmp_b1, cmp_w2, cmp_b2, att_norm_g, w_out, ln1_g, ln1_b, w_up, w_down, ln2_g, ln2_b):
    assert w_in.shape[0] == DEPTH
    bp, tp, _ = x_prompt.shape
    bs, ts, _ = x_sample.shape
    n_pages = page_table.shape[1]
    past = n_pages * PAGE_SIZE
    w_buf = cache_win_kv.shape[2]
    assert ts < D_CMP and ts % 8 == 0 and w_buf == WINDOW and past >= WINDOW and tp >= WINDOW
    tbl = rel_bias_table
    vec = lambda v: v.reshape(1, -1)

    w_proj = _prep_w_in(w_in[0])
    cmp_w = _prep_cmp_weights(cmp_w1[0], cmp_b1[0], cmp_w2[0], cmp_b2[0])
    wo = w_out[0].astype(BF16)
    wu = w_up[0].astype(BF16)
    wd = w_down[0].astype(BF16)
    eg, eb = vec(emb_ln_g), vec(emb_ln_b)
    gate_expand = _gate_expand()
    far = _far_rows(tbl)

    def trunk_tail(x2d, y, oc, os_, ow, sm):
        h = _combine(x2d, y, oc, os_, ow, sm, eg, eb, gate_expand, vec(att_norm_g[0]), wo, vec(ln1_g[0]),
                     vec(ln1_b[0]), 256)
        return _ffn(h, wu, wd, vec(ln2_g[0]), vec(ln2_b[0]), 1024, D_FF // 4)

    ssm_w = (conv_w[0], conv_b[0], dt_bias[0], a_log[0], d_skip[0], ssm_norm_g[0])

    xp2 = x_prompt.reshape(bp * tp, D_MODEL)
    z, xbc, q, kvc, kvc_t, kvs_t, kvw_t, sm = _proj(xp2, eg, eb, w_proj, BF16, 512, seq=(bp, tp))
    r3 = lambda a: a.reshape(bp, tp, a.shape[-1])
    xbc3 = r3(xbc)
    y_ssm, h_new = _ssm(r3(z), xbc3, r3(sm), jnp.zeros((bp, CONV_WIDTH - 1, CONV_DIM), F32),
                        jnp.zeros((bp, SSM_HEADS, SSM_HEAD_DIM, D_STATE), F32), *ssm_w)
    n_sub = tp // D_CMP
    n_slc = tp // L_SEL
    oc, sel = _cmp_prompt(kvc, r3(q), _cmp_prompt_bias(tbl, tp), cmp_w, _overlap(n_sub, n_sub - 1, LANES, n_slc))
    ii = np.arange(Q_BLOCK)[:, None] - np.arange(Q_BLOCK)[None, :]
    tiles_gq = lambda rs, ok: jnp.concatenate(
        [_toeplitz_tile(tbl, Q_BLOCK * r, ok(ii + Q_BLOCK * r)) for r in rs], axis=2).reshape(
            KV_HEADS, GQA * Q_BLOCK, len(rs) * Q_BLOCK)
    near = tiles_gq((1, 0), lambda d: d >= 0)
    win_bias = tiles_gq(range(WINDOW // Q_BLOCK, -1, -1), lambda d: (d >= 0) & (d < WINDOW))
    far_gq = jnp.repeat(far, Q_BLOCK, axis=0).reshape(KV_HEADS, GQA * Q_BLOCK, LANES)
    assert n_slc <= SEL_PAD
    eneg = jnp.asarray(np.where(np.arange(SEL_PAD)[:, None] == (np.arange(tp) // L_SEL)[None, :], NEG, 0.0), BF16)
    os_, ow = _nsa_prompt(r3(q), kvs_t, kvw_t, sel, eneg, near, far_gq, win_bias)
    f2 = lambda a: a.reshape(bp * tp, a.shape[-1])
    y_prompt = trunk_tail(xp2, f2(y_ssm), f2(oc), f2(os_), f2(ow), sm).reshape(bp, tp, D_MODEL)
    w = min(WINDOW, tp)
    prompt_state = (_row_major6(kvc_t), _row_major6(kvs_t), _row_major6(kvw_t[:, :, tp - w:]),
                    xbc3[:, tp - (CONV_WIDTH - 1):][None], h_new[None])

    xs2 = x_sample.reshape(bs * ts, D_MODEL)
    z, xbc, q, kvc, kvs, kvw, sm = _proj(xs2, eg, eb, w_proj, F32, 512)
    r3 = lambda a: a.reshape(bs, ts, a.shape[-1])
    xbc3 = r3(xbc)
    y_ssm, h_new = _ssm(r3(z), xbc3, r3(sm), state_conv[0], state_ssm[0], *ssm_w)
    n_sub = past // D_CMP
    n_cmp = n_sub - 1
    n_slc = -(-(past + ts) // L_SEL)
    width = -(-n_slc // LANES) * LANES
    qpos = past + np.arange(ts)
    dist_c = qpos[:, None] - (np.arange(n_sub) * D_CMP + L_CMP - 1)[None, :]
    bias_cs = _stack_gt(_bias_lookup(tbl, dist_c, (dist_c >= 0) & (np.arange(n_sub) < n_cmp)[None, :]), ts)
    oc, sel = _cmp_sample(_feature_major(cache_cmp_kv[0]), page_table, r3(q), bias_cs, cmp_w,
                          _overlap(n_sub, n_cmp, width, n_slc), n_slc)
    dist_last = qpos[:, None] - (past - PAGE_SIZE + np.arange(PAGE_SIZE))[None, :]
    dist_n = np.arange(ts)[:, None] - np.arange(LANES)[None, :]
    bias_new = _stack_gt(_bias_lookup(tbl, dist_n, (dist_n >= 0) & (np.arange(LANES) < ts)[None, :]), ts)
    os_ = _sel_sample(_feature_major(cache_slc_kv[0]), page_table, r3(kvs), r3(q), sel, far,
                      _stack_gt(_bias_lookup(tbl, dist_last), ts), bias_new)
    dist_w = qpos[:, None] - (past - w_buf + np.arange(w_buf))[None, :]
    bias_wb = _stack_gt(_bias_lookup(tbl, dist_w, (dist_w >= 0) & (dist_w < WINDOW)), ts)
    ow, win_new_t = _win_sample(_feature_major(cache_win_kv[0]), r3(kvw), r3(q), bias_wb, bias_new)
    f2 = lambda a: a.reshape(bs * ts, a.shape[-1])
    y_sample = trunk_tail(xs2, f2(y_ssm), f2(oc), f2(os_), f2(ow), sm).reshape(bs, ts, D_MODEL)
    kv6 = lambda a: a.reshape(1, bs, ts, KV_HEADS, 2, HEAD_DIM)
    sample_state = (kv6(kvc), kv6(kvs), _row_major6(win_new_t),
                    xbc3[:, ts - (CONV_WIDTH - 1):][None], h_new[None])

    return (y_prompt, y_sample) + prompt_state + sample_state
```

```python
import functools
import math

import numpy as np
import jax
import jax.numpy as jnp
from jax import lax
from jax.experimental import pallas as pl
from jax.experimental.pallas import tpu as pltpu

F32 = jnp.float32
BF16 = jnp.bfloat16
HIGHEST = lax.Precision.HIGHEST

D_MODEL = 1024
SSM_HEADS = 8
SSM_HEAD_DIM = 64
SSM_WIDTH = SSM_HEADS * SSM_HEAD_DIM
SSM_GROUPS = 2
D_STATE = 128
CONV_WIDTH = 4
CONV_DIM = SSM_WIDTH + 2 * SSM_GROUPS * D_STATE
SSD_CHUNK = 128
ATT_HEADS = 8
KV_HEADS = 2
GQA = ATT_HEADS // KV_HEADS
HEAD_DIM = 64
ATT_WIDTH = ATT_HEADS * HEAD_DIM
KV_COLS = KV_HEADS * 2 * HEAD_DIM
D_CMP = 16
L_CMP = 2 * D_CMP
CMP_HID = 64
L_SEL = 64
TOP_N = 16
WINDOW = 512
Q_BLOCK = 128
N_BRANCH = 3
FORCED_SCORE = 1e4
N_BUCKETS = 32
MAX_DISTANCE = 128
D_FF = 4 * D_MODEL
DEPTH = 1
ALPHA = (2 * DEPTH) ** 0.25
ATT_SCALE = HEAD_DIM ** -0.5
EPS = 1e-5
PAGE_SIZE = 128

LANES = 128
NEG = -1e30
MASKED_BELOW = -1e29
VMEM_LIMIT = 48 * 1024 * 1024
KV_ROWS = 2 * HEAD_DIM
SEL_PAD = 32
NEAR_W = 32
SEL_CHUNK = 512

_OFF_Z = 0
_OFF_XBC = _OFF_Z + SSM_WIDTH
_OFF_Q = _OFF_XBC + CONV_DIM
_OFF_KVC = _OFF_Q + ATT_WIDTH
_OFF_KVS = _OFF_KVC + KV_COLS
_OFF_KVW = _OFF_KVS + KV_COLS
_OFF_SM = _OFF_KVW + KV_COLS
_N_PROJ = _OFF_SM + LANES


def _cparams(sem):
    return pltpu.CompilerParams(dimension_semantics=sem, vmem_limit_bytes=VMEM_LIMIT)


def _row_tile(n, preferred):
    tm = min(n, preferred)
    assert n % tm == 0 and tm % 8 == 0
    return tm


def _dot(a, b, precision=None):
    return jnp.dot(a, b, preferred_element_type=F32, precision=precision)


def _dot_nt(a, b):
    return lax.dot_general(a, b, (((1,), (1,)), ((), ())), preferred_element_type=F32)


def _layer_norm(x, g, b):
    mu = jnp.mean(x, -1, keepdims=True)
    xc = x - mu
    var = jnp.mean(xc * xc, -1, keepdims=True)
    return xc * lax.rsqrt(var + EPS) * g + b


def _sigmoid(x):
    return 1.0 / (1.0 + jnp.exp(-x))


def _softplus(x):
    return jnp.maximum(x, 0.0) + jnp.log(1.0 + jnp.exp(-jnp.abs(x)))


def _gelu_tanh(x):
    c = math.sqrt(2.0 / math.pi)
    return 0.5 * x * (1.0 + jnp.tanh(c * (x + 0.044715 * (x * x * x))))


def _proj_kernel(x_ref, g_ref, b_ref, w_ref, z_ref, xbc_ref, q_ref, kvc_ref, *rest, feature_major):
    xn = _layer_norm(x_ref[...], g_ref[...], b_ref[...]).astype(BF16)

    def mm(lo, hi):
        return _dot(xn, w_ref[:, lo:hi])

    z_ref[...] = mm(_OFF_Z, _OFF_XBC)
    xbc_ref[...] = mm(_OFF_XBC, _OFF_Q)
    q_ref[...] = mm(_OFF_Q, _OFF_KVC).astype(q_ref.dtype)
    kvc = mm(_OFF_KVC, _OFF_KVS)
    if feature_major:
        kvct_ref, kvst_ref, kvwt_ref, sm_ref = rest
        for h in range(KV_HEADS):
            kvc_ref[h] = kvc[:, h * KV_ROWS:(h + 1) * KV_ROWS]
        kvct_ref[0] = kvc.T
        kvst_ref[0] = mm(_OFF_KVS, _OFF_KVW).T
        kvwt_ref[0] = mm(_OFF_KVW, _OFF_SM).T
    else:
        kvs_ref, kvw_ref, sm_ref = rest
        kvc_ref[...] = kvc
        kvs_ref[...] = mm(_OFF_KVS, _OFF_KVW)
        kvw_ref[...] = mm(_OFF_KVW, _OFF_SM)
    sm_ref[...] = mm(_OFF_SM, _N_PROJ)


def _proj(x2d, g, b, w, q_dtype, tm, seq=None):
    n = x2d.shape[0]
    tm = _row_tile(n, tm)
    row = lambda i: (i, 0)
    fixed = lambda i: (0, 0)
    rm = lambda wd, dt: (pl.BlockSpec((tm, wd), row), jax.ShapeDtypeStruct((n, wd), dt))
    outs = [rm(SSM_WIDTH, F32), rm(CONV_DIM, F32), rm(ATT_WIDTH, q_dtype)]
    if seq is None:
        outs += [rm(KV_COLS, F32), rm(KV_COLS, F32), rm(KV_COLS, F32)]
    else:
        outs.append((pl.BlockSpec((KV_HEADS, tm, KV_ROWS), lambda i: (0, i, 0)),
                     jax.ShapeDtypeStruct((KV_HEADS, n, KV_ROWS), F32)))
        bsz, t = seq
        assert t % tm == 0 and tm % LANES == 0
        per = t // tm
        fm = (pl.BlockSpec((1, KV_COLS, tm), lambda i: (i // per, 0, i % per)),
              jax.ShapeDtypeStruct((bsz, KV_COLS, t), F32))
        outs += [fm, fm, fm]
    outs.append(rm(LANES, F32))
    return pl.pallas_call(
        functools.partial(_proj_kernel, feature_major=seq is not None),
        grid=(n // tm,),
        in_specs=[pl.BlockSpec((tm, D_MODEL), row), pl.BlockSpec((1, D_MODEL), fixed),
                  pl.BlockSpec((1, D_MODEL), fixed), pl.BlockSpec((D_MODEL, _N_PROJ), fixed)],
        out_specs=[o[0] for o in outs],
        out_shape=[o[1] for o in outs],
        compiler_params=_cparams(("parallel",)),
        name="proj",
    )(x2d, g, b, w)


def _ssm_kernel(z_ref, xbc_ref, sm_ref, hist_ref, h0_ref, cw_ref, cb_ref, dtb_ref, alog_ref, dskip_ref,
                ng_ref, y_ref, hfin_ref, xext, state, *, tb, l, nc):
    c = pl.program_id(1)

    @pl.when(c == 0)
    def _():
        xext[0:8, :] = jnp.zeros((8, CONV_DIM), F32)
        xext[8 - (CONV_WIDTH - 1):8, :] = hist_ref[0]
        if tb < l:
            xext[8 + tb:8 + l, :] = jnp.zeros((l - tb, CONV_DIM), F32)
        state[...] = h0_ref[0].reshape(SSM_WIDTH, D_STATE)

    xext[8:8 + tb, :] = xbc_ref[0]
    conv = cb_ref[...]
    for k in range(CONV_WIDTH):
        lo = 8 - (CONV_WIDTH - 1) + k
        conv = conv + cw_ref[k:k + 1, :] * xext[lo:lo + l, :]
    xc = conv * _sigmoid(conv)
    xext[0:8, :] = xext[tb:tb + 8, :]

    dt = _softplus(sm_ref[0] + dtb_ref[...])
    if tb < l:
        dt = jnp.concatenate([dt, jnp.zeros((l - tb, LANES), F32)], axis=0)
    a = dt * (-jnp.exp(alog_ref[...]))
    ri = lax.broadcasted_iota(jnp.int32, (l, l), 0)
    ci = lax.broadcasted_iota(jnp.int32, (l, l), 1)
    tril = ri >= ci
    a_cs = _dot(jnp.where(tril, 1.0, 0.0), a, HIGHEST)
    a_cs_t = a_cs.T
    dt_t = dt.T
    ea = jnp.exp(a_cs[:tb])
    wend = dt * jnp.exp(a_cs[l - 1:l, :] - a_cs)
    etot = jnp.exp(a_cs[l - 1:l, :])

    xs = xc[:, :SSM_WIDTH]
    lane = lax.broadcasted_iota(jnp.int32, (tb, LANES), 1)
    srow = lax.broadcasted_iota(jnp.int32, (LANES, D_STATE), 0)
    tril_q = tril[:tb]
    y_pairs = []
    for g in range(SSM_GROUPS):
        bg_f = xc[:, SSM_WIDTH + g * D_STATE:SSM_WIDTH + (g + 1) * D_STATE]
        bg = bg_f.astype(BF16)
        c_lo = SSM_WIDTH + SSM_GROUPS * D_STATE + g * D_STATE
        cg = xc[:tb, c_lo:c_lo + D_STATE].astype(BF16)
        cb = _dot_nt(cg, bg)
        for k in range(2):
            pair = 2 * g + k
            lo = pair * LANES
            h0, h1 = 2 * pair, 2 * pair + 1
            xs_pair = xs[:, lo:lo + LANES]
            xs_b = xs_pair.astype(BF16)
            xs_t = xs_pair.T
            ys, upd = [], []
            for r2, h in enumerate((h0, h1)):
                seg = a_cs[:tb, h:h + 1] - a_cs_t[h:h + 1, :]
                lm = jnp.where(tril_q, jnp.exp(jnp.where(tril_q, seg, 0.0)), 0.0) * dt_t[h:h + 1, :]
                ys.append(_dot((cb * lm).astype(BF16), xs_b))
                upd.append(_dot(xs_t[r2 * SSM_HEAD_DIM:(r2 + 1) * SSM_HEAD_DIM].astype(BF16),
                                (bg_f * wend[:, h:h + 1]).astype(BF16)))
            y_diag = jnp.where(lane < SSM_HEAD_DIM, ys[0], ys[1])
            sp = state[lo:lo + LANES, :]
            y_off = _dot_nt(cg, sp.astype(BF16)) * jnp.where(lane < SSM_HEAD_DIM, ea[:, h0:h0 + 1], ea[:, h1:h1 + 1])
            y_pairs.append(y_diag + y_off)
            keep = jnp.where(srow < SSM_HEAD_DIM, etot[:, h0:h0 + 1], etot[:, h1:h1 + 1])
            state[lo:lo + LANES, :] = sp * keep + jnp.concatenate(upd, axis=0)
    y = jnp.concatenate(y_pairs, axis=1) + dskip_ref[...] * xs[:tb]
    zz = z_ref[0]
    y = y * (zz * _sigmoid(zz))
    gw = SSM_WIDTH // SSM_GROUPS
    outs = []
    for g in range(SSM_GROUPS):
        yg = y[:, g * gw:(g + 1) * gw]
        ms = jnp.mean(yg * yg, -1, keepdims=True)
        outs.append(yg * lax.rsqrt(ms + EPS) * ng_ref[:, g * gw:(g + 1) * gw])
    y_ref[0] = jnp.concatenate(outs, axis=1).astype(y_ref.dtype)

    @pl.when(c == nc - 1)
    def _():
        hfin_ref[0] = state[...].reshape(SSM_HEADS, SSM_HEAD_DIM, D_STATE)


def _ssm(z, xbc, sm, hist, h0, conv_w, conv_b, dt_bias, a_log, d_skip, norm_g):
    bsz, t, _ = z.shape
    l = SSD_CHUNK
    tb = min(l, t)
    assert t % tb == 0 and tb % 8 == 0 and t >= CONV_WIDTH - 1
    nc = t // tb
    pad8 = lambda v: jnp.pad(v.reshape(1, SSM_HEADS), ((0, 0), (0, LANES - SSM_HEADS)))
    blk = lambda b, c: (b, c, 0)
    per_b3 = lambda b, c: (b, 0, 0)
    per_b4 = lambda b, c: (b, 0, 0, 0)
    fixed = lambda b, c: (0, 0)
    return pl.pallas_call(
        functools.partial(_ssm_kernel, tb=tb, l=l, nc=nc),
        grid=(bsz, nc),
        in_specs=[pl.BlockSpec((1, tb, SSM_WIDTH), blk), pl.BlockSpec((1, tb, CONV_DIM), blk),
                  pl.BlockSpec((1, tb, LANES), blk),
                  pl.BlockSpec((1, CONV_WIDTH - 1, CONV_DIM), per_b3),
                  pl.BlockSpec((1, SSM_HEADS, SSM_HEAD_DIM, D_STATE), per_b4),
                  pl.BlockSpec((CONV_WIDTH, CONV_DIM), fixed), pl.BlockSpec((1, CONV_DIM), fixed),
                  pl.BlockSpec((1, LANES), fixed), pl.BlockSpec((1, LANES), fixed),
                  pl.BlockSpec((1, SSM_WIDTH), fixed), pl.BlockSpec((1, SSM_WIDTH), fixed)],
        out_specs=[pl.BlockSpec((1, tb, SSM_WIDTH), blk),
                   pl.BlockSpec((1, SSM_HEADS, SSM_HEAD_DIM, D_STATE), per_b4)],
        out_shape=[jax.ShapeDtypeStruct((bsz, t, SSM_WIDTH), BF16),
                   jax.ShapeDtypeStruct((bsz, SSM_HEADS, SSM_HEAD_DIM, D_STATE), F32)],
        scratch_shapes=[pltpu.VMEM((8 + l, CONV_DIM), F32), pltpu.VMEM((SSM_WIDTH, D_STATE), F32)],
        compiler_params=_cparams(("parallel", "arbitrary")),
        name="ssm",
    )(z, xbc, sm, hist, h0, conv_w, conv_b.reshape(1, CONV_DIM), pad8(dt_bias), pad8(a_log),
      jnp.repeat(d_skip, SSM_HEAD_DIM).reshape(1, SSM_WIDTH), norm_g.reshape(1, SSM_WIDTH))


def _compress(load_pair, h, n_sub, cw):
    w1_ref, b1_ref, w2_ref, b2_ref = cw
    hid = jnp.zeros((n_sub, 2 * KV_ROWS), F32)
    for jp in range(D_CMP // 2):
        hid = hid + _dot(load_pair(h, jp), w1_ref[jp])
    pre = hid[:, :KV_ROWS] + pltpu.roll(hid[:, KV_ROWS:], n_sub - 1, 0) + b1_ref[...]
    return _dot(_gelu_tanh(pre).astype(BF16), w2_ref[...]) + b2_ref[...]


def _pad_q(q):
    return jnp.concatenate([q, jnp.zeros(q.shape, q.dtype)], axis=1)


def _select(score, qpos, n_slc, n_visible=None):
    jj = lax.broadcasted_iota(jnp.int32, score.shape, 1)
    cur = qpos // L_SEL
    visible = jj * L_SEL <= qpos
    forced = (jj == 0) | (jj == cur) | (jj == cur - 1)
    sc = jnp.where(visible, jnp.where(forced, FORCED_SCORE, score), -1.0)
    sc = jnp.where(jj < n_slc, sc, -2.0)
    lane = lax.broadcasted_iota(jnp.int32, (1, score.shape[1]), 1)

    def count(rank, ks):
        for k in ks:
            ck = sc[:, k:k + 1]
            tie = jnp.where(lane > k, 1.0, 0.0)
            rank = rank + jnp.where(ck > sc, 1.0, jnp.where(ck == sc, tie, 0.0))
        return rank

    rank = jnp.zeros(score.shape, F32)
    group = 8
    for k0 in range(0, n_slc, group):
        ks = range(k0, min(k0 + group, n_slc))
        if n_visible is None:
            rank = count(rank, ks)
        else:
            rank = lax.cond(k0 < n_visible, functools.partial(count, ks=ks), lambda r: r, rank)
    return jnp.where(rank < min(TOP_N, n_slc), jnp.where(sc >= 0.0, 1.0, 0.0), 0.0)


def _select_blocks_on_rows(score_t, qpos, n_slc, n_visible):
    jj = lax.broadcasted_iota(jnp.int32, score_t.shape, 0)
    cur = qpos // L_SEL
    visible = jj * L_SEL <= qpos
    forced = (jj == 0) | (jj == cur) | (jj == cur - 1)
    sc = jnp.where(visible, jnp.where(forced, FORCED_SCORE, score_t), -1.0)

    def count(rank, ks):
        for k in ks:
            tie = jnp.where(jj > k, 1.0, 0.0)
            rank = rank + jnp.where(sc[k:k + 1, :] > sc, 1.0, jnp.where(sc[k:k + 1, :] == sc, tie, 0.0))
        return rank

    rank = jnp.zeros(score_t.shape, F32)
    group = 8
    for k0 in range(0, n_slc, group):
        ks = range(k0, min(k0 + group, n_slc))
        rank = lax.cond(k0 < n_visible, functools.partial(count, ks=ks), lambda r: r, rank)
    return jnp.where(rank < min(TOP_N, n_slc), jnp.where(sc >= 0.0, 1.0, 0.0), 0.0)


def _softmax_rows(s):
    m = jnp.max(s, -1, keepdims=True)
    e = jnp.where(s > MASKED_BELOW, jnp.exp(s - m), 0.0)
    return e / jnp.maximum(jnp.sum(e, -1, keepdims=True), 1e-30)


def _stack_heads(q_ref, h, i=0):
    return jnp.concatenate([q_ref[i, :, (h * GQA + g) * HEAD_DIM:(h * GQA + g + 1) * HEAD_DIM]
                            for g in range(GQA)], axis=0)


def _unstack_heads(o_ref, h, o, rows, i=0):
    for g in range(GQA):
        hd = h * GQA + g
        o_ref[i, :, hd * HEAD_DIM:(hd + 1) * HEAD_DIM] = o[g * rows:(g + 1) * rows]


def _cmp_prompt_kernel(kvc_ref, q_ref, bias_ref, w1_ref, b1_ref, w2_ref, b2_ref, ov_ref, o_ref, sel_ref,
                       kvcmp, *, n_sub, n_slc):
    qb = pl.program_id(1)

    @pl.when(qb == 0)
    def _():
        rows = lambda h, j: kvc_ref[h, pl.ds(j, n_sub, stride=D_CMP), :].astype(BF16)
        load = lambda h, jp: jnp.concatenate([rows(h, 2 * jp), rows(h, 2 * jp + 1)], axis=1)
        for h in range(KV_HEADS):
            kvcmp[h] = _compress(load, h, n_sub, (w1_ref, b1_ref, w2_ref, b2_ref)).astype(BF16)

    qpos = qb * Q_BLOCK + lax.broadcasted_iota(jnp.int32, (1, Q_BLOCK), 1)
    for h in range(KV_HEADS):
        kv = kvcmp[h]
        s = _dot_nt(_pad_q(_stack_heads(q_ref, h)), kv) + bias_ref[h]
        p = _softmax_rows(s)
        _unstack_heads(o_ref, h, _dot(p.astype(BF16), kv)[:, HEAD_DIM:], Q_BLOCK)
        pg = p[0:Q_BLOCK]
        for g in range(1, GQA):
            pg = pg + p[g * Q_BLOCK:(g + 1) * Q_BLOCK]
        score_t = _dot(pg, ov_ref[...], HIGHEST).T
        sel_t = _select_blocks_on_rows(score_t[0:n_slc], qpos, n_slc, (qb + 1) * (Q_BLOCK // L_SEL))
        sel_ref[0, h] = jnp.concatenate([sel_t, jnp.zeros((LANES - n_slc, Q_BLOCK), F32)], axis=0).T


def _cmp_prompt(kvc, q, bias, cmp_w, overlap):
    bsz, t, _ = q.shape
    n_sub = t // D_CMP
    n_cmp = n_sub - 1
    nqb = t // Q_BLOCK
    n_slc = t // L_SEL
    assert n_sub == LANES and n_slc <= LANES
    fixed = lambda a: pl.BlockSpec(a.shape, lambda b, i, _n=a.ndim: (0,) * _n)
    return pl.pallas_call(
        functools.partial(_cmp_prompt_kernel, n_sub=n_sub, n_slc=n_slc),
        grid=(bsz, nqb),
        in_specs=[pl.BlockSpec((KV_HEADS, t, KV_ROWS), lambda b, i: (0, b, 0)),
                  pl.BlockSpec((1, Q_BLOCK, ATT_WIDTH), lambda b, i: (b, i, 0)),
                  pl.BlockSpec((KV_HEADS, GQA * Q_BLOCK, n_sub), lambda b, i: (0, i, 0))]
        + [fixed(a) for a in cmp_w] + [fixed(overlap)],
        out_specs=[pl.BlockSpec((1, Q_BLOCK, ATT_WIDTH), lambda b, i: (b, i, 0)),
                   pl.BlockSpec((1, KV_HEADS, Q_BLOCK, LANES), lambda b, i: (b, 0, i, 0))],
        out_shape=[jax.ShapeDtypeStruct((bsz, t, ATT_WIDTH), F32),
                   jax.ShapeDtypeStruct((bsz, KV_HEADS, t, LANES), F32)],
        scratch_shapes=[pltpu.VMEM((KV_HEADS, n_sub, KV_ROWS), BF16)],
        compiler_params=_cparams(("parallel", "arbitrary")),
        name="cmp_prompt",
    )(kvc, q, bias, *cmp_w, overlap)


def _nsa_prompt_kernel(q_ref, kvs_ref, kvw_ref, sel_ref, eneg_ref, near_ref, far_ref, winb_ref, os_ref, ow_ref,
                       kaug, vsa, kwp, vwa, s_buf, mrun, acc, *, t):
    qb = pl.program_id(1)
    rows = GQA * Q_BLOCK
    near_w = 2 * Q_BLOCK
    win_w = WINDOW + Q_BLOCK

    @pl.when(qb == 0)
    def _():
        ones_row = jnp.where(lax.broadcasted_iota(jnp.int32, (KV_ROWS - HEAD_DIM, t), 0) == 0, 1.0, 0.0).astype(BF16)
        for h in range(KV_HEADS):
            lo = h * KV_ROWS
            kaug[h, :, 0:Q_BLOCK] = jnp.zeros((HEAD_DIM + SEL_PAD, Q_BLOCK), BF16)
            kaug[h, 0:HEAD_DIM, Q_BLOCK:] = kvs_ref[0, lo:lo + HEAD_DIM, :].astype(BF16)
            kaug[h, HEAD_DIM:, Q_BLOCK:] = eneg_ref[...]
            vsa[h, :, 0:Q_BLOCK] = jnp.zeros((KV_ROWS, Q_BLOCK), BF16)
            vsa[h, 0:HEAD_DIM, Q_BLOCK:] = kvs_ref[0, lo + HEAD_DIM:lo + KV_ROWS, :].astype(BF16)
            vsa[h, HEAD_DIM:, Q_BLOCK:] = ones_row
            kwp[h, :, 0:WINDOW] = jnp.zeros((HEAD_DIM, WINDOW), BF16)
            kwp[h, :, WINDOW:] = kvw_ref[0, lo:lo + HEAD_DIM, :].astype(BF16)
            vwa[h, :, 0:WINDOW] = jnp.zeros((KV_ROWS, WINDOW), BF16)
            vwa[h, 0:HEAD_DIM, WINDOW:] = kvw_ref[0, lo + HEAD_DIM:lo + KV_ROWS, :].astype(BF16)
            vwa[h, HEAD_DIM:, WINDOW:] = ones_row

    def normalise(a):
        return a[:, 0:HEAD_DIM] / a[:, HEAD_DIM:HEAD_DIM + 1]

    def tile_max(s):
        m = s[:, 0:LANES]
        for i in range(1, s.shape[1] // LANES):
            m = jnp.maximum(m, s[:, i * LANES:(i + 1) * LANES])
        return m

    start = pl.multiple_of(qb * Q_BLOCK, Q_BLOCK)
    n_chunk = t // SEL_CHUNK
    tiles_per_chunk = SEL_CHUNK // Q_BLOCK
    for h in range(KV_HEADS):
        q4 = _stack_heads(q_ref, h)
        notsel = (1.0 - sel_ref[0, h][:, 0:SEL_PAD]).astype(BF16)
        qa = jnp.concatenate([q4, jnp.concatenate([notsel] * GQA, axis=0)], axis=1)

        far = jnp.concatenate([far_ref[h]] * tiles_per_chunk, axis=1)
        mrun[...] = jnp.full((rows, LANES), NEG, F32)
        for c in range(n_chunk):
            @pl.when(c * tiles_per_chunk < qb - 1)
            def _(c=c):
                key = c * SEL_CHUNK + lax.broadcasted_iota(jnp.int32, (1, SEL_CHUNK), 1)
                late = jnp.where(key < (qb - 1) * Q_BLOCK, 0.0, NEG)
                lo = Q_BLOCK + c * SEL_CHUNK
                s = _dot(qa, kaug[h, :, lo:lo + SEL_CHUNK]) + far + late
                s_buf[:, c * SEL_CHUNK:(c + 1) * SEL_CHUNK] = s
                mrun[...] = jnp.maximum(mrun[...], tile_max(s))

        first = jnp.where(lax.broadcasted_iota(jnp.int32, (1, near_w), 1) < Q_BLOCK,
                          jnp.where(qb >= 1, 0.0, NEG), 0.0)
        s_near = _dot(qa, kaug[h, :, pl.ds(start, near_w)]) + near_ref[h] + first
        m = jnp.max(jnp.maximum(mrun[...], tile_max(s_near)), -1, keepdims=True)
        acc[...] = _dot_nt(jnp.exp(s_near - m).astype(BF16), vsa[h, :, pl.ds(start, near_w)])
        for c in range(n_chunk):
            @pl.when(c * tiles_per_chunk < qb - 1)
            def _(c=c):
                lo = Q_BLOCK + c * SEL_CHUNK
                p = jnp.exp(s_buf[:, c * SEL_CHUNK:(c + 1) * SEL_CHUNK] - m)
                acc[...] = acc[...] + _dot_nt(p.astype(BF16), vsa[h, :, lo:lo + SEL_CHUNK])
        _unstack_heads(os_ref, h, normalise(acc[...]), Q_BLOCK)

        pad = jnp.where(lax.broadcasted_iota(jnp.int32, (1, win_w), 1) < WINDOW - qb * Q_BLOCK, NEG, 0.0)
        s_w = _dot(q4, kwp[h, :, pl.ds(start, win_w)]) + winb_ref[h] + pad
        p = jnp.exp(s_w - jnp.max(tile_max(s_w), -1, keepdims=True))
        _unstack_heads(ow_ref, h, normalise(_dot_nt(p.astype(BF16), vwa[h, :, pl.ds(start, win_w)])), Q_BLOCK)


def _nsa_prompt(q, kvs_t, kvw_t, sel, eneg, near, far, win_bias):
    bsz, _, t = kvs_t.shape
    assert t % SEL_CHUNK == 0
    nqb = t // Q_BLOCK
    qblk = lambda b, i: (b, i, 0)
    per_b = lambda b, i: (b, 0, 0)
    fixed = lambda a: pl.BlockSpec(a.shape, lambda b, i, _n=a.ndim: (0,) * _n)
    rows = GQA * Q_BLOCK
    return pl.pallas_call(
        functools.partial(_nsa_prompt_kernel, t=t),
        grid=(bsz, nqb),
        in_specs=[pl.BlockSpec((1, Q_BLOCK, ATT_WIDTH), qblk), pl.BlockSpec((1, KV_COLS, t), per_b),
                  pl.BlockSpec((1, KV_COLS, t), per_b),
                  pl.BlockSpec((1, KV_HEADS, Q_BLOCK, LANES), lambda b, i: (b, 0, i, 0)),
                  fixed(eneg), fixed(near), fixed(far), fixed(win_bias)],
        out_specs=[pl.BlockSpec((1, Q_BLOCK, ATT_WIDTH), qblk), pl.BlockSpec((1, Q_BLOCK, ATT_WIDTH), qblk)],
        out_shape=[jax.ShapeDtypeStruct((bsz, t, ATT_WIDTH), F32)] * 2,
        scratch_shapes=[pltpu.VMEM((KV_HEADS, HEAD_DIM + SEL_PAD, Q_BLOCK + t), BF16),
                        pltpu.VMEM((KV_HEADS, KV_ROWS, Q_BLOCK + t), BF16),
                        pltpu.VMEM((KV_HEADS, HEAD_DIM, WINDOW + t), BF16),
                        pltpu.VMEM((KV_HEADS, KV_ROWS, WINDOW + t), BF16),
                        pltpu.VMEM((rows, t), F32), pltpu.VMEM((rows, LANES), F32),
                        pltpu.VMEM((rows, KV_ROWS), F32)],
        compiler_params=_cparams(("parallel", "arbitrary")),
        name="nsa_prompt",
    )(q, kvs_t, kvw_t, sel, eneg, near, far, win_bias)


def _page_copy(pages_hbm, page, buf, sem, slot, k):
    return pltpu.make_async_copy(pages_hbm.at[page], buf.at[slot, k], sem.at[slot])


def _stream_pages(pt_ref, pages_hbm, buf, sem, n_pages):
    b = pl.program_id(0)
    slot = lax.rem(b, 2)

    def start(row, into):
        for k in range(n_pages):
            _page_copy(pages_hbm, pt_ref[row, k], buf, sem, into, k).start()

    @pl.when(b == 0)
    def _():
        start(0, 0)

    @pl.when(b + 1 < pl.num_programs(0))
    def _():
        start(b + 1, 1 - slot)

    for k in range(n_pages):
        _page_copy(pages_hbm, 0, buf, sem, slot, k).wait()
    return slot


def _cmp_sample_kernel(pt_ref, pages_hbm, q_ref, bias_ref, perm_ref, w1_ref, b1_ref, w2_ref, b2_ref, ov_ref, o_ref,
                       sel_ref, buf, sem, xj, *, n_pages, n_slc, past, t):
    slot = _stream_pages(pt_ref, pages_hbm, buf, sem, n_pages)
    sub = PAGE_SIZE // D_CMP
    perm = perm_ref[...]
    per_tile = LANES // (2 * sub)
    for k2 in range(n_pages // 2):
        pair = jnp.concatenate([buf[slot, 2 * k2].astype(BF16), buf[slot, 2 * k2 + 1].astype(BF16)], axis=1)
        y = _dot(pair, perm)
        for h in range(KV_HEADS):
            for c in range(2 * PAGE_SIZE // LANES):
                xt = y[h * KV_ROWS:(h + 1) * KV_ROWS, c * LANES:(c + 1) * LANES].T
                for i in range(per_tile):
                    j = c * per_tile + i
                    xj[h, j // 2, 2 * sub * k2:2 * sub * (k2 + 1), (j % 2) * KV_ROWS:(j % 2 + 1) * KV_ROWS] = (
                        xt[2 * sub * i:2 * sub * (i + 1)].astype(BF16))
    n_sub = n_pages * sub
    load = lambda h, jp: xj[h, jp]
    qpos = past + lax.broadcasted_iota(jnp.int32, (t, 1), 0)
    for h in range(KV_HEADS):
        kv = _compress(load, h, n_sub, (w1_ref, b1_ref, w2_ref, b2_ref)).astype(BF16)
        qh = _pad_q(_stack_heads(q_ref, h).astype(BF16))
        p = _softmax_rows(_dot_nt(qh, kv) + bias_ref[h])
        _unstack_heads(o_ref, h, _dot(p.astype(BF16), kv)[:, HEAD_DIM:], t)
        pg = p[0:t]
        for g in range(1, GQA):
            pg = pg + p[g * t:(g + 1) * t]
        sel_ref[0, h] = _select(_dot(pg, ov_ref[...], HIGHEST), qpos, n_slc)


def _page_scratch(n_pages):
    return [pltpu.VMEM((2, n_pages, KV_COLS, PAGE_SIZE), F32), pltpu.SemaphoreType.DMA((2,))]


def _cmp_sample(pages, page_table, q, bias_cs, cmp_w, overlap, n_slc):
    bsz, n_pages = page_table.shape
    t = q.shape[1]
    width = overlap.shape[1]
    assert n_pages % 2 == 0
    sub = PAGE_SIZE // D_CMP
    col = np.arange(2 * PAGE_SIZE)
    page, row = col // PAGE_SIZE, col % PAGE_SIZE
    dest = (row % D_CMP) * 2 * sub + page * sub + row // D_CMP
    perm = jnp.asarray((dest[:, None] == col[None, :]).astype(np.float32), BF16)
    fixed = lambda a: pl.BlockSpec(a.shape, lambda b, pt, _n=a.ndim: (0,) * _n)
    grid_spec = pltpu.PrefetchScalarGridSpec(
        num_scalar_prefetch=1,
        grid=(bsz,),
        in_specs=[pl.BlockSpec(memory_space=pl.ANY), pl.BlockSpec((1, t, ATT_WIDTH), lambda b, pt: (b, 0, 0)),
                  fixed(bias_cs), fixed(perm)] + [fixed(a) for a in cmp_w] + [fixed(overlap)],
        out_specs=[pl.BlockSpec((1, t, ATT_WIDTH), lambda b, pt: (b, 0, 0)),
                   pl.BlockSpec((1, KV_HEADS, t, width), lambda b, pt: (b, 0, 0, 0))],
        scratch_shapes=_page_scratch(n_pages)
        + [pltpu.VMEM((KV_HEADS, D_CMP // 2, n_pages * sub, 2 * KV_ROWS), BF16)])
    return pl.pallas_call(
        functools.partial(_cmp_sample_kernel, n_pages=n_pages, n_slc=n_slc, past=n_pages * PAGE_SIZE, t=t),
        grid_spec=grid_spec,
        out_shape=[jax.ShapeDtypeStruct((bsz, t, ATT_WIDTH), F32),
                   jax.ShapeDtypeStruct((bsz, KV_HEADS, t, width), F32)],
        compiler_params=_cparams(("arbitrary",)),
        name="cmp_sample",
    )(page_table, pages, q, bias_cs, perm, *cmp_w, overlap)


def _joint_attend(s_past, vt_past, s_new, v_new):
    m = jnp.maximum(jnp.max(s_past, -1, keepdims=True), jnp.max(s_new, -1, keepdims=True))
    e_past = jnp.exp(s_past - m)
    e_new = jnp.exp(s_new - m)
    den = jnp.sum(e_past, -1, keepdims=True) + jnp.sum(e_new, -1, keepdims=True)
    acc = _dot_nt(e_past.astype(BF16), vt_past) + _dot(e_new.astype(BF16), v_new)
    return acc / den


def _pad_new_rows(new_ref, t, i=0):
    return jnp.concatenate([new_ref[i], jnp.zeros((LANES - t, KV_COLS), F32)], axis=0)


def _sel_sample_kernel(pt_ref, pages_hbm, new_ref, q_ref, sel_ref, far_ref, near_ref, biasn_ref, o_ref, buf, sem,
                       kvb, mask, *, n_pages, t):
    slot = _stream_pages(pt_ref, pages_hbm, buf, sem, n_pages)
    for k in range(n_pages):
        kvb[:, k * PAGE_SIZE:(k + 1) * PAGE_SIZE] = buf[slot, k].astype(BF16)
    kv_new = _pad_new_rows(new_ref, t).astype(BF16)
    rows = GQA * t
    lane = lax.broadcasted_iota(jnp.int32, (rows, LANES), 1)
    per_tile = LANES // L_SEL
    for h in range(KV_HEADS):
        lo = h * KV_ROWS
        sel4 = jnp.concatenate([sel_ref[0, h]] * GQA, axis=0)

        def tile_mask(k):
            cols = [jnp.broadcast_to(sel4[:, per_tile * k + i:per_tile * k + i + 1], (rows, LANES))
                    for i in range(per_tile)]
            m = cols[-1]
            for i in range(per_tile - 2, -1, -1):
                m = jnp.where(lane < (i + 1) * L_SEL, cols[i], m)
            return (m - 1.0) * (-NEG)

        far = jnp.concatenate([jnp.broadcast_to(far_ref[h * GQA + g:h * GQA + g + 1, :], (t, LANES))
                               for g in range(GQA)], axis=0)
        for k in range(n_pages - 1):
            mask[:, k * LANES:(k + 1) * LANES] = tile_mask(k) + far
        mask[:, (n_pages - 1) * LANES:n_pages * LANES] = tile_mask(n_pages - 1) + near_ref[h]
        qh = _stack_heads(q_ref, h).astype(BF16)
        s_past = _dot(qh, kvb[lo:lo + HEAD_DIM, :]) + mask[...]
        s_new = _dot_nt(qh, kv_new[:, lo:lo + HEAD_DIM]) + biasn_ref[h] + tile_mask(n_pages)
        o = _joint_attend(s_past, kvb[lo + HEAD_DIM:lo + KV_ROWS, :], s_new, kv_new[:, lo + HEAD_DIM:lo + KV_ROWS])
        _unstack_heads(o_ref, h, o, t)


def _sel_sample(pages, page_table, kvs_new, q, sel, far, near, bias_new):
    bsz, n_pages = page_table.shape
    t = q.shape[1]
    past = n_pages * PAGE_SIZE
    width = sel.shape[-1]
    assert (n_pages + 1) * (LANES // L_SEL) <= width
    per_b = lambda b, pt: (b, 0, 0)
    fixed = lambda a: pl.BlockSpec(a.shape, lambda b, pt, _n=a.ndim: (0,) * _n)
    grid_spec = pltpu.PrefetchScalarGridSpec(
        num_scalar_prefetch=1,
        grid=(bsz,),
        in_specs=[pl.BlockSpec(memory_space=pl.ANY),
                  pl.BlockSpec((1, t, KV_COLS), per_b), pl.BlockSpec((1, t, ATT_WIDTH), per_b),
                  pl.BlockSpec((1, KV_HEADS, t, width), lambda b, pt: (b, 0, 0, 0)),
                  fixed(far), fixed(near), fixed(bias_new)],
        out_specs=pl.BlockSpec((1, t, ATT_WIDTH), per_b),
        scratch_shapes=_page_scratch(n_pages)
        + [pltpu.VMEM((KV_COLS, past), BF16), pltpu.VMEM((GQA * t, past), F32)])
    return pl.pallas_call(
        functools.partial(_sel_sample_kernel, n_pages=n_pages, t=t),
        grid_spec=grid_spec,
        out_shape=jax.ShapeDtypeStruct((bsz, t, ATT_WIDTH), F32),
        compiler_params=_cparams(("arbitrary",)),
        name="sel_sample",
    )(page_table, pages, kvs_new, q, sel, far, near, bias_new)


def _win_sample_kernel(buf_ref, new_ref, q_ref, bias_ref, biasn_ref, o_ref, win_ref, *, t, nb):
    lane = lax.broadcasted_iota(jnp.int32, (KV_COLS, LANES), 1)
    for i in range(nb):
        buf = buf_ref[i]
        w = buf.shape[1]
        new = _pad_new_rows(new_ref, t, i)
        shifted = pltpu.roll(buf, w - t, 1)
        tail = pltpu.roll(new.T, LANES - t, 1)
        win_ref[i, :, 0:w - LANES] = shifted[:, 0:w - LANES]
        win_ref[i, :, w - LANES:w] = jnp.where(lane >= LANES - t, tail, shifted[:, w - LANES:w])
        kvb = buf.astype(BF16)
        kv_new = new.astype(BF16)
        for h in range(KV_HEADS):
            lo = h * KV_ROWS
            qh = _stack_heads(q_ref, h, i).astype(BF16)
            s_past = _dot(qh, kvb[lo:lo + HEAD_DIM, :]) + bias_ref[h]
            s_new = _dot_nt(qh, kv_new[:, lo:lo + HEAD_DIM]) + biasn_ref[h]
            o = _joint_attend(s_past, kvb[lo + HEAD_DIM:lo + KV_ROWS, :], s_new,
                              kv_new[:, lo + HEAD_DIM:lo + KV_ROWS])
            _unstack_heads(o_ref, h, o, t, i)


def _win_sample(buf_t, kvw_new, q, bias_buf, bias_new):
    bsz, _, w = buf_t.shape
    t = q.shape[1]
    nb = 4 if bsz % 4 == 0 else 1
    per_b = lambda b: (b, 0, 0)
    fixed = lambda a: pl.BlockSpec(a.shape, lambda b, _n=a.ndim: (0,) * _n)
    return pl.pallas_call(
        functools.partial(_win_sample_kernel, t=t, nb=nb),
        grid=(bsz // nb,),
        in_specs=[pl.BlockSpec((nb, KV_COLS, w), per_b), pl.BlockSpec((nb, t, KV_COLS), per_b),
                  pl.BlockSpec((nb, t, ATT_WIDTH), per_b), fixed(bias_buf), fixed(bias_new)],
        out_specs=[pl.BlockSpec((nb, t, ATT_WIDTH), per_b), pl.BlockSpec((nb, KV_COLS, w), per_b)],
        out_shape=[jax.ShapeDtypeStruct((bsz, t, ATT_WIDTH), F32),
                   jax.ShapeDtypeStruct((bsz, KV_COLS, w), F32)],
        compiler_params=_cparams(("parallel",)),
        name="win_sample",
    )(buf_t, kvw_new, q, bias_buf, bias_new)


def _combine_kernel(x_ref, y_ref, oc_ref, os_ref, ow_ref, sm_ref, eg_ref, eb_ref, ex_ref, ag_ref, wo_ref, g1_ref,
                    b1_ref, h_ref):
    xn = _layer_norm(x_ref[...], eg_ref[...], eb_ref[...])
    gates = _sigmoid(sm_ref[...])
    g_hi = gates.astype(BF16)
    g_lo = (gates - g_hi.astype(F32)).astype(BF16)
    o = jnp.zeros(oc_ref.shape, F32)
    for br, ref in enumerate((oc_ref, os_ref, ow_ref)):
        o = o + (_dot(g_hi, ex_ref[br]) + _dot(g_lo, ex_ref[br])) * ref[...]
    rms = lax.rsqrt(jnp.mean(o * o, -1, keepdims=True) + EPS)
    att = (o * rms * ag_ref[...]).astype(BF16)
    mix = _dot(jnp.concatenate([y_ref[...], att], axis=1), wo_ref[...])
    h_ref[...] = _layer_norm(ALPHA * xn + mix, g1_ref[...], b1_ref[...])


def _combine(x2d, y, oc, os_, ow, sm, eg, eb, gate_expand, ag, wo, g1, b1, tm):
    n = x2d.shape[0]
    tm = _row_tile(n, tm)
    row = lambda i: (i, 0)
    fixed = lambda a: pl.BlockSpec(a.shape, lambda i, _n=a.ndim: (0,) * _n)
    att = pl.BlockSpec((tm, ATT_WIDTH), row)
    return pl.pallas_call(
        _combine_kernel,
        grid=(n // tm,),
        in_specs=[pl.BlockSpec((tm, D_MODEL), row), pl.BlockSpec((tm, SSM_WIDTH), row), att, att, att,
                  pl.BlockSpec((tm, LANES), row), fixed(eg), fixed(eb), fixed(gate_expand), fixed(ag), fixed(wo),
                  fixed(g1), fixed(b1)],
        out_specs=pl.BlockSpec((tm, D_MODEL), row),
        out_shape=jax.ShapeDtypeStruct((n, D_MODEL), F32),
        compiler_params=_cparams(("parallel",)),
        name="combine",
    )(x2d, y, oc, os_, ow, sm, eg, eb, gate_expand, ag, wo, g1, b1)


def _ffn_kernel(h_ref, wu_ref, wd_ref, g_ref, b_ref, o_ref, *, tf):
    h = h_ref[...]
    hb = h.astype(BF16)
    acc = None
    for k in range(D_FF // tf):
        u = jnp.maximum(_dot(hb, wu_ref[:, k * tf:(k + 1) * tf]), 0.0)
        part = _dot((u * u).astype(BF16), wd_ref[k * tf:(k + 1) * tf, :])
        acc = part if acc is None else acc + part
    o_ref[...] = _layer_norm(ALPHA * h + acc, g_ref[...], b_ref[...])


def _ffn(h2d, wu, wd, g, b, tm, tf):
    n = h2d.shape[0]
    tm = _row_tile(n, tm)
    resident = lambda a: pl.BlockSpec(a.shape, lambda i: (0, 0), pipeline_mode=pl.Buffered(1))
    return pl.pallas_call(
        functools.partial(_ffn_kernel, tf=tf),
        grid=(n // tm,),
        in_specs=[pl.BlockSpec((tm, D_MODEL), lambda i: (i, 0)), resident(wu), resident(wd), resident(g), resident(b)],
        out_specs=pl.BlockSpec((tm, D_MODEL), lambda i: (i, 0)),
        out_shape=jax.ShapeDtypeStruct((n, D_MODEL), F32),
        compiler_params=_cparams(("parallel",)),
        name="ffn",
    )(h2d, wu, wd, g, b)


def _bucket_np(dist):
    d = np.maximum(dist, 0)
    exact = N_BUCKETS // 2
    far = exact + (np.log(np.maximum(d, 1).astype(np.float32) / np.float32(exact))
                   / np.float32(math.log(MAX_DISTANCE / exact)) * (N_BUCKETS - exact)).astype(np.int32)
    return np.where(d < exact, d, np.minimum(far, N_BUCKETS - 1)).astype(np.int32)


def _bias_lookup(tbl, dist, mask=None):
    dist = np.asarray(dist)
    onehot = np.eye(N_BUCKETS, dtype=np.float32)[_bucket_np(dist).reshape(-1)]
    b = jnp.dot(jnp.asarray(onehot), tbl, precision=HIGHEST).T.reshape((ATT_HEADS,) + dist.shape)
    return b if mask is None else jnp.where(jnp.asarray(mask)[None], b, NEG)


def _toeplitz_tile(tbl, offset, mask):
    period = 2 * Q_BLOCK
    k = np.arange(period)
    vals = _bias_lookup(tbl, offset - np.where(k < Q_BLOCK, k, k - period))
    tiled = jnp.tile(vals, (1, Q_BLOCK))[:, :Q_BLOCK * (period - 1)]
    t = tiled.reshape(ATT_HEADS, Q_BLOCK, period - 1)[:, :, :Q_BLOCK]
    return jnp.where(jnp.asarray(mask)[None], t, NEG)


def _cmp_prompt_bias(tbl, t):
    n_sub = t // D_CMP
    back = (MAX_DISTANCE + L_CMP - 1) // D_CMP
    band = (np.arange(Q_BLOCK)[:, None] - D_CMP * np.arange(NEAR_W)[None, :] + D_CMP * back - (L_CMP - 1))
    near = jnp.pad(_bias_lookup(tbl, band), ((0, 0), (0, 0), (0, n_sub - NEAR_W)))
    far = tbl[N_BUCKETS - 1][:, None, None]
    blk = np.arange(n_sub)[None, :]
    tiles = []
    for qb in range(t // Q_BLOCK):
        near_lo = qb * (Q_BLOCK // D_CMP) - back
        qpos = qb * Q_BLOCK + np.arange(Q_BLOCK)[:, None]
        visible = (blk * D_CMP + (L_CMP - 1) <= qpos) & (blk < n_sub - 1)
        tile = jnp.where(jnp.asarray(blk >= near_lo)[None], jnp.roll(near, near_lo % n_sub, axis=2), far)
        tiles.append(jnp.where(jnp.asarray(visible)[None], tile, NEG))
    tiles = jnp.stack(tiles).reshape(t // Q_BLOCK, KV_HEADS, GQA * Q_BLOCK, n_sub)
    return jnp.moveaxis(tiles, 0, 1).reshape(KV_HEADS, -1, n_sub)


def _stack_gt(tab, t):
    return tab.reshape(KV_HEADS, GQA * t, tab.shape[-1])


def _far_rows(tbl):
    return jnp.broadcast_to(tbl[N_BUCKETS - 1][:, None], (ATT_HEADS, LANES))


def _overlap(n_cmp_pad, n_cmp, width, n_slc):
    i = np.arange(n_cmp_pad)[:, None]
    j = np.arange(width)[None, :]
    ov = (i * D_CMP < (j + 1) * L_SEL) & (i * D_CMP + L_CMP > j * L_SEL) & (i < n_cmp) & (j < n_slc)
    return jnp.asarray(ov.astype(np.float32))


def _prep_cmp_weights(w1, b1, w2, b2):
    eye = jnp.eye(2, dtype=F32)
    w1r = (w1[:, :, :, :, None, :] * eye[None, None, :, None, :, None]).transpose(1, 2, 3, 0, 4, 5)
    w1r = w1r.reshape(D_CMP // 2, 2 * KV_ROWS, 2 * KV_ROWS).astype(BF16)
    w2r = (w2[:, :, None, :] * eye[:, None, :, None]).reshape(2 * CMP_HID, KV_ROWS).astype(BF16)
    return (w1r, b1.reshape(1, 2 * CMP_HID), w2r, b2.reshape(1, KV_ROWS))


def _prep_w_in(w_in):
    sizes = (SSM_WIDTH, CONV_DIM, SSM_HEADS, ATT_WIDTH, KV_COLS, KV_COLS, KV_COLS)
    z, xbc, dt, q, kvc, kvs, kvw, gates = jnp.split(w_in, np.cumsum(sizes).tolist(), axis=1)
    small = jnp.concatenate([dt, gates], axis=1)
    small = jnp.pad(small, ((0, 0), (0, LANES - small.shape[1])))
    return jnp.concatenate([z, xbc, q * ATT_SCALE, kvc, kvs, kvw, small], axis=1).astype(BF16)


def _gate_expand():
    ex = np.zeros((N_BRANCH, LANES, ATT_WIDTH), np.float32)
    for br in range(N_BRANCH):
        for hd in range(ATT_HEADS):
            ex[br, SSM_HEADS + br * ATT_HEADS + hd, hd * HEAD_DIM:(hd + 1) * HEAD_DIM] = 1.0
    return jnp.asarray(ex, BF16)


def _feature_major(a):
    lead = a.shape[:-4]
    rows = a.shape[-4]
    return jnp.moveaxis(a.reshape(lead + (rows, KV_COLS)), -2, -1)


def _row_major6(a_t):
    bsz, _, rows = a_t.shape
    return jnp.moveaxis(a_t, 1, 2).reshape(1, bsz, rows, KV_HEADS, 2, HEAD_DIM)


def kernel(x_prompt, x_sample, cache_cmp_kv, cache_slc_kv, cache_win_kv, state_conv, state_ssm, page_table,
           rel_bias_table, emb_ln_g, emb_ln_b, w_in, conv_w, conv_b, dt_bias, a_log, d_skip, ssm_norm_g,
           cmp_w1, cmp_b1, cmp_w2, cmp_b2, att_norm_g, w_out, ln1_g, ln1_b, w_up, w_down, ln2_g, ln2_b):
    assert w_in.shape[0] == DEPTH
    bp, tp, _ = x_prompt.shape
    bs, ts, _ = x_sample.shape
    n_pages = page_table.shape[1]
    past = n_pages * PAGE_SIZE
    w_buf = cache_win_kv.shape[2]
    assert ts < D_CMP and ts % 8 == 0 and w_buf == WINDOW and past >= WINDOW and tp >= WINDOW
    tbl = rel_bias_table
    vec = lambda v: v.reshape(1, -1)

    w_proj = _prep_w_in(w_in[0])
    cmp_w = _prep_cmp_weights(cmp_w1[0], cmp_b1[0], cmp_w2[0], cmp_b2[0])
    wo = w_out[0].astype(BF16)
    wu = w_up[0].astype(BF16)
    wd = w_down[0].astype(BF16)
    eg, eb = vec(emb_ln_g), vec(emb_ln_b)
    gate_expand = _gate_expand()
    far = _far_rows(tbl)

    def trunk_tail(x2d, y, oc, os_, ow, sm):
        h = _combine(x2d, y, oc, os_, ow, sm, eg, eb, gate_expand, vec(att_norm_g[0]), wo, vec(ln1_g[0]),
                     vec(ln1_b[0]), 256)
        return _ffn(h, wu, wd, vec(ln2_g[0]), vec(ln2_b[0]), 512, D_FF // 4)

    ssm_w = (conv_w[0], conv_b[0], dt_bias[0], a_log[0], d_skip[0], ssm_norm_g[0])

    xp2 = x_prompt.reshape(bp * tp, D_MODEL)
    z, xbc, q, kvc, kvc_t, kvs_t, kvw_t, sm = _proj(xp2, eg, eb, w_proj, BF16, 512, seq=(bp, tp))
    r3 = lambda a: a.reshape(bp, tp, a.shape[-1])
    xbc3 = r3(xbc)
    y_ssm, h_new = _ssm(r3(z), xbc3, r3(sm), jnp.zeros((bp, CONV_WIDTH - 1, CONV_DIM), F32),
                        jnp.zeros((bp, SSM_HEADS, SSM_HEAD_DIM, D_STATE), F32), *ssm_w)
    n_sub = tp // D_CMP
    n_slc = tp // L_SEL
    oc, sel = _cmp_prompt(kvc, r3(q), _cmp_prompt_bias(tbl, tp), cmp_w, _overlap(n_sub, n_sub - 1, LANES, n_slc))
    ii = np.arange(Q_BLOCK)[:, None] - np.arange(Q_BLOCK)[None, :]
    tiles_gq = lambda rs, ok: jnp.concatenate(
        [_toeplitz_tile(tbl, Q_BLOCK * r, ok(ii + Q_BLOCK * r)) for r in rs], axis=2).reshape(
            KV_HEADS, GQA * Q_BLOCK, len(rs) * Q_BLOCK)
    near = tiles_gq((1, 0), lambda d: d >= 0)
    win_bias = tiles_gq(range(WINDOW // Q_BLOCK, -1, -1), lambda d: (d >= 0) & (d < WINDOW))
    far_gq = jnp.repeat(far, Q_BLOCK, axis=0).reshape(KV_HEADS, GQA * Q_BLOCK, LANES)
    assert n_slc <= SEL_PAD
    eneg = jnp.asarray(np.where(np.arange(SEL_PAD)[:, None] == (np.arange(tp) // L_SEL)[None, :], NEG, 0.0), BF16)
    os_, ow = _nsa_prompt(r3(q), kvs_t, kvw_t, sel, eneg, near, far_gq, win_bias)
    f2 = lambda a: a.reshape(bp * tp, a.shape[-1])
    y_prompt = trunk_tail(xp2, f2(y_ssm), f2(oc), f2(os_), f2(ow), sm).reshape(bp, tp, D_MODEL)
    w = min(WINDOW, tp)
    prompt_state = (_row_major6(kvc_t), _row_major6(kvs_t), _row_major6(kvw_t[:, :, tp - w:]),
                    xbc3[:, tp - (CONV_WIDTH - 1):][None], h_new[None])

    xs2 = x_sample.reshape(bs * ts, D_MODEL)
    z, xbc, q, kvc, kvs, kvw, sm = _proj(xs2, eg, eb, w_proj, F32, 512)
    r3 = lambda a: a.reshape(bs, ts, a.shape[-1])
    xbc3 = r3(xbc)
    y_ssm, h_new = _ssm(r3(z), xbc3, r3(sm), state_conv[0], state_ssm[0], *ssm_w)
    n_sub = past // D_CMP
    n_cmp = n_sub - 1
    n_slc = -(-(past + ts) // L_SEL)
    width = -(-n_slc // LANES) * LANES
    qpos = past + np.arange(ts)
    dist_c = qpos[:, None] - (np.arange(n_sub) * D_CMP + L_CMP - 1)[None, :]
    bias_cs = _stack_gt(_bias_lookup(tbl, dist_c, (dist_c >= 0) & (np.arange(n_sub) < n_cmp)[None, :]), ts)
    oc, sel = _cmp_sample(_feature_major(cache_cmp_kv[0]), page_table, r3(q), bias_cs, cmp_w,
                          _overlap(n_sub, n_cmp, width, n_slc), n_slc)
    dist_last = qpos[:, None] - (past - PAGE_SIZE + np.arange(PAGE_SIZE))[None, :]
    dist_n = np.arange(ts)[:, None] - np.arange(LANES)[None, :]
    bias_new = _stack_gt(_bias_lookup(tbl, dist_n, (dist_n >= 0) & (np.arange(LANES) < ts)[None, :]), ts)
    os_ = _sel_sample(_feature_major(cache_slc_kv[0]), page_table, r3(kvs), r3(q), sel, far,
                      _stack_gt(_bias_lookup(tbl, dist_last), ts), bias_new)
    dist_w = qpos[:, None] - (past - w_buf + np.arange(w_buf))[None, :]
    bias_wb = _stack_gt(_bias_lookup(tbl, dist_w, (dist_w >= 0) & (dist_w < WINDOW)), ts)
    ow, win_new_t = _win_sample(_feature_major(cache_win_kv[0]), r3(kvw), r3(q), bias_wb, bias_new)
    f2 = lambda a: a.reshape(bs * ts, a.shape[-1])
    y_sample = trunk_tail(xs2, f2(y_ssm), f2(oc), f2(os_), f2(ow), sm).reshape(bs, ts, D_MODEL)
    kv6 = lambda a: a.reshape(1, bs, ts, KV_HEADS, 2, HEAD_DIM)
    sample_state = (kv6(kvc), kv6(kvs), _row_major6(win_new_t),
                    xbc3[:, ts - (CONV_WIDTH - 1):][None], h_new[None])

    return (y_prompt, y_sample) + prompt_state + sample_state
```

```python
import functools
import math

import numpy as np
import jax
import jax.numpy as jnp
from jax import lax
from jax.experimental import pallas as pl
from jax.experimental.pallas import tpu as pltpu

F32 = jnp.float32
BF16 = jnp.bfloat16
HIGHEST = lax.Precision.HIGHEST

D_MODEL = 1024
SSM_HEADS = 8
SSM_HEAD_DIM = 64
SSM_WIDTH = SSM_HEADS * SSM_HEAD_DIM
SSM_GROUPS = 2
D_STATE = 128
CONV_WIDTH = 4
CONV_DIM = SSM_WIDTH + 2 * SSM_GROUPS * D_STATE
SSD_CHUNK = 128
ATT_HEADS = 8
KV_HEADS = 2
GQA = ATT_HEADS // KV_HEADS
HEAD_DIM = 64
ATT_WIDTH = ATT_HEADS * HEAD_DIM
KV_COLS = KV_HEADS * 2 * HEAD_DIM
D_CMP = 16
L_CMP = 2 * D_CMP
CMP_HID = 64
L_SEL = 64
TOP_N = 16
WINDOW = 512
Q_BLOCK = 128
N_BRANCH = 3
FORCED_SCORE = 1e4
N_BUCKETS = 32
MAX_DISTANCE = 128
D_FF = 4 * D_MODEL
DEPTH = 1
ALPHA = (2 * DEPTH) ** 0.25
ATT_SCALE = HEAD_DIM ** -0.5
EPS = 1e-5
PAGE_SIZE = 128

LANES = 128
NEG = -1e30
MASKED_BELOW = -1e29
VMEM_LIMIT = 48 * 1024 * 1024
KV_ROWS = 2 * HEAD_DIM
SEL_PAD = 32
NEAR_W = 32
SEL_CHUNK = 512
PROJ_ROWS = 512
COMBINE_ROWS = 256
FFN_ROWS = 512
FFN_SLAB = D_FF // 4
ROWS_PER_STEP = 4

_OFF_Z = 0
_OFF_XBC = _OFF_Z + SSM_WIDTH
_OFF_Q = _OFF_XBC + CONV_DIM
_OFF_KVC = _OFF_Q + ATT_WIDTH
_OFF_KVS = _OFF_KVC + KV_COLS
_OFF_KVW = _OFF_KVS + KV_COLS
_OFF_SM = _OFF_KVW + KV_COLS
_N_PROJ = _OFF_SM + LANES


def _cparams(sem):
    return pltpu.CompilerParams(dimension_semantics=sem, vmem_limit_bytes=VMEM_LIMIT)


def _row_tile(n, preferred):
    tm = min(n, preferred)
    assert n % tm == 0 and tm % 8 == 0
    return tm


def _dot(a, b, precision=None):
    return jnp.dot(a, b, preferred_element_type=F32, precision=precision)


def _dot_nt(a, b):
    return lax.dot_general(a, b, (((1,), (1,)), ((), ())), preferred_element_type=F32)


def _layer_norm(x, g, b):
    mu = jnp.mean(x, -1, keepdims=True)
    xc = x - mu
    var = jnp.mean(xc * xc, -1, keepdims=True)
    return xc * lax.rsqrt(var + EPS) * g + b


def _sigmoid(x):
    return 1.0 / (1.0 + jnp.exp(-x))


def _softplus(x):
    return jnp.maximum(x, 0.0) + jnp.log(1.0 + jnp.exp(-jnp.abs(x)))


def _gelu_tanh(x):
    c = math.sqrt(2.0 / math.pi)
    return 0.5 * x * (1.0 + jnp.tanh(c * (x + 0.044715 * (x * x * x))))


def _proj_kernel(x_ref, g_ref, b_ref, w_ref, z_ref, xbc_ref, q_ref, kvc_ref, *rest, feature_major):
    xn = _layer_norm(x_ref[...], g_ref[...], b_ref[...]).astype(BF16)

    def mm(lo, hi):
        return _dot(xn, w_ref[:, lo:hi])

    z_ref[...] = mm(_OFF_Z, _OFF_XBC)
    xbc_ref[...] = mm(_OFF_XBC, _OFF_Q)
    q_ref[...] = mm(_OFF_Q, _OFF_KVC).astype(q_ref.dtype)
    kvc = mm(_OFF_KVC, _OFF_KVS)
    if feature_major:
        kvct_ref, kvst_ref, kvwt_ref, sm_ref = rest
        for h in range(KV_HEADS):
            kvc_ref[h] = kvc[:, h * KV_ROWS:(h + 1) * KV_ROWS]
        kvct_ref[0] = kvc.T
        kvst_ref[0] = mm(_OFF_KVS, _OFF_KVW).T
        kvwt_ref[0] = mm(_OFF_KVW, _OFF_SM).T
    else:
        kvs_ref, kvw_ref, sm_ref = rest
        kvc_ref[...] = kvc
        kvs_ref[...] = mm(_OFF_KVS, _OFF_KVW)
        kvw_ref[...] = mm(_OFF_KVW, _OFF_SM)
    sm_ref[...] = mm(_OFF_SM, _N_PROJ)


def _proj(x2d, g, b, w, q_dtype, tm, seq=None):
    n = x2d.shape[0]
    tm = _row_tile(n, tm)
    row = lambda i: (i, 0)
    fixed = lambda i: (0, 0)
    rm = lambda wd, dt: (pl.BlockSpec((tm, wd), row), jax.ShapeDtypeStruct((n, wd), dt))
    outs = [rm(SSM_WIDTH, F32), rm(CONV_DIM, F32), rm(ATT_WIDTH, q_dtype)]
    if seq is None:
        outs += [rm(KV_COLS, F32), rm(KV_COLS, F32), rm(KV_COLS, F32)]
    else:
        outs.append((pl.BlockSpec((KV_HEADS, tm, KV_ROWS), lambda i: (0, i, 0)),
                     jax.ShapeDtypeStruct((KV_HEADS, n, KV_ROWS), F32)))
        bsz, t = seq
        assert t % tm == 0 and tm % LANES == 0
        per = t // tm
        fm = (pl.BlockSpec((1, KV_COLS, tm), lambda i: (i // per, 0, i % per)),
              jax.ShapeDtypeStruct((bsz, KV_COLS, t), F32))
        outs += [fm, fm, fm]
    outs.append(rm(LANES, F32))
    return pl.pallas_call(
        functools.partial(_proj_kernel, feature_major=seq is not None),
        grid=(n // tm,),
        in_specs=[pl.BlockSpec((tm, D_MODEL), row), pl.BlockSpec((1, D_MODEL), fixed),
                  pl.BlockSpec((1, D_MODEL), fixed), pl.BlockSpec((D_MODEL, _N_PROJ), fixed)],
        out_specs=[o[0] for o in outs],
        out_shape=[o[1] for o in outs],
        compiler_params=_cparams(("parallel",)),
        name="proj",
    )(x2d, g, b, w)


def _ssm_kernel(z_ref, xbc_ref, sm_ref, hist_ref, h0_ref, cw_ref, cb_ref, dtb_ref, alog_ref, dskip_ref,
                ng_ref, y_ref, hfin_ref, xext, state, *, tb, l, nc, nb):
    for i in range(nb):
        _ssm_row(i, z_ref, xbc_ref, sm_ref, hist_ref, h0_ref, cw_ref, cb_ref, dtb_ref, alog_ref, dskip_ref,
                 ng_ref, y_ref, hfin_ref, xext.at[i], state.at[i], tb=tb, l=l, nc=nc)


def _ssm_row(i, z_ref, xbc_ref, sm_ref, hist_ref, h0_ref, cw_ref, cb_ref, dtb_ref, alog_ref, dskip_ref,
             ng_ref, y_ref, hfin_ref, xext, state, *, tb, l, nc):
    c = pl.program_id(1)

    @pl.when(c == 0)
    def _():
        xext[0:8, :] = jnp.zeros((8, CONV_DIM), F32)
        xext[8 - (CONV_WIDTH - 1):8, :] = hist_ref[i]
        if tb < l:
            xext[8 + tb:8 + l, :] = jnp.zeros((l - tb, CONV_DIM), F32)
        state[...] = h0_ref[i].reshape(SSM_WIDTH, D_STATE)

    xext[8:8 + tb, :] = xbc_ref[i]
    conv = cb_ref[...]
    for k in range(CONV_WIDTH):
        lo = 8 - (CONV_WIDTH - 1) + k
        conv = conv + cw_ref[k:k + 1, :] * xext[lo:lo + l, :]
    xc = conv * _sigmoid(conv)
    xext[0:8, :] = xext[tb:tb + 8, :]

    dt = _softplus(sm_ref[i] + dtb_ref[...])
    if tb < l:
        dt = jnp.concatenate([dt, jnp.zeros((l - tb, LANES), F32)], axis=0)
    a = dt * (-jnp.exp(alog_ref[...]))
    ri = lax.broadcasted_iota(jnp.int32, (l, l), 0)
    ci = lax.broadcasted_iota(jnp.int32, (l, l), 1)
    tril = ri >= ci
    a_cs = _dot(jnp.where(tril, 1.0, 0.0), a, HIGHEST)
    a_cs_t = a_cs.T
    dt_t = dt.T
    ea = jnp.exp(a_cs[:tb])
    wend = dt * jnp.exp(a_cs[l - 1:l, :] - a_cs)
    etot = jnp.exp(a_cs[l - 1:l, :])

    xs = xc[:, :SSM_WIDTH]
    lane = lax.broadcasted_iota(jnp.int32, (tb, LANES), 1)
    srow = lax.broadcasted_iota(jnp.int32, (LANES, D_STATE), 0)
    tril_q = tril[:tb]
    y_pairs = []
    for g in range(SSM_GROUPS):
        bg_f = xc[:, SSM_WIDTH + g * D_STATE:SSM_WIDTH + (g + 1) * D_STATE]
        bg = bg_f.astype(BF16)
        c_lo = SSM_WIDTH + SSM_GROUPS * D_STATE + g * D_STATE
        cg = xc[:tb, c_lo:c_lo + D_STATE].astype(BF16)
        cb = _dot_nt(cg, bg)
        for k in range(2):
            pair = 2 * g + k
            lo = pair * LANES
            h0, h1 = 2 * pair, 2 * pair + 1
            xs_pair = xs[:, lo:lo + LANES]
            xs_b = xs_pair.astype(BF16)
            xs_t = xs_pair.T
            ys, upd = [], []
            for r2, h in enumerate((h0, h1)):
                seg = a_cs[:tb, h:h + 1] - a_cs_t[h:h + 1, :]
                lm = jnp.where(tril_q, jnp.exp(jnp.where(tril_q, seg, 0.0)), 0.0) * dt_t[h:h + 1, :]
                ys.append(_dot((cb * lm).astype(BF16), xs_b))
                upd.append(_dot(xs_t[r2 * SSM_HEAD_DIM:(r2 + 1) * SSM_HEAD_DIM].astype(BF16),
                                (bg_f * wend[:, h:h + 1]).astype(BF16)))
            y_diag = jnp.where(lane < SSM_HEAD_DIM, ys[0], ys[1])
            sp = state[lo:lo + LANES, :]
            y_off = _dot_nt(cg, sp.astype(BF16)) * jnp.where(lane < SSM_HEAD_DIM, ea[:, h0:h0 + 1], ea[:, h1:h1 + 1])
            y_pairs.append(y_diag + y_off)
            keep = jnp.where(srow < SSM_HEAD_DIM, etot[:, h0:h0 + 1], etot[:, h1:h1 + 1])
            state[lo:lo + LANES, :] = sp * keep + jnp.concatenate(upd, axis=0)
    y = jnp.concatenate(y_pairs, axis=1) + dskip_ref[...] * xs[:tb]
    zz = z_ref[i]
    y = y * (zz * _sigmoid(zz))
    gw = SSM_WIDTH // SSM_GROUPS
    outs = []
    for g in range(SSM_GROUPS):
        yg = y[:, g * gw:(g + 1) * gw]
        ms = jnp.mean(yg * yg, -1, keepdims=True)
        outs.append(yg * lax.rsqrt(ms + EPS) * ng_ref[:, g * gw:(g + 1) * gw])
    y_ref[i] = jnp.concatenate(outs, axis=1).astype(y_ref.dtype)

    @pl.when(c == nc - 1)
    def _():
        hfin_ref[i] = state[...].reshape(SSM_HEADS, SSM_HEAD_DIM, D_STATE)


def _ssm(z, xbc, sm, hist, h0, conv_w, conv_b, dt_bias, a_log, d_skip, norm_g):
    bsz, t, _ = z.shape
    l = SSD_CHUNK
    tb = min(l, t)
    assert t % tb == 0 and tb % 8 == 0 and t >= CONV_WIDTH - 1
    nc = t // tb
    nb = ROWS_PER_STEP if (tb < l and bsz % ROWS_PER_STEP == 0) else 1
    pad8 = lambda v: jnp.pad(v.reshape(1, SSM_HEADS), ((0, 0), (0, LANES - SSM_HEADS)))
    blk = lambda b, c: (b, c, 0)
    per_b3 = lambda b, c: (b, 0, 0)
    per_b4 = lambda b, c: (b, 0, 0, 0)
    fixed = lambda b, c: (0, 0)
    return pl.pallas_call(
        functools.partial(_ssm_kernel, tb=tb, l=l, nc=nc, nb=nb),
        grid=(bsz // nb, nc),
        in_specs=[pl.BlockSpec((nb, tb, SSM_WIDTH), blk), pl.BlockSpec((nb, tb, CONV_DIM), blk),
                  pl.BlockSpec((nb, tb, LANES), blk),
                  pl.BlockSpec((nb, CONV_WIDTH - 1, CONV_DIM), per_b3),
                  pl.BlockSpec((nb, SSM_HEADS, SSM_HEAD_DIM, D_STATE), per_b4),
                  pl.BlockSpec((CONV_WIDTH, CONV_DIM), fixed), pl.BlockSpec((1, CONV_DIM), fixed),
                  pl.BlockSpec((1, LANES), fixed), pl.BlockSpec((1, LANES), fixed),
                  pl.BlockSpec((1, SSM_WIDTH), fixed), pl.BlockSpec((1, SSM_WIDTH), fixed)],
        out_specs=[pl.BlockSpec((nb, tb, SSM_WIDTH), blk),
                   pl.BlockSpec((nb, SSM_HEADS, SSM_HEAD_DIM, D_STATE), per_b4)],
        out_shape=[jax.ShapeDtypeStruct((bsz, t, SSM_WIDTH), BF16),
                   jax.ShapeDtypeStruct((bsz, SSM_HEADS, SSM_HEAD_DIM, D_STATE), F32)],
        scratch_shapes=[pltpu.VMEM((nb, 8 + l, CONV_DIM), F32), pltpu.VMEM((nb, SSM_WIDTH, D_STATE), F32)],
        compiler_params=_cparams(("parallel", "arbitrary")),
        name="ssm",
    )(z, xbc, sm, hist, h0, conv_w, conv_b.reshape(1, CONV_DIM), pad8(dt_bias), pad8(a_log),
      jnp.repeat(d_skip, SSM_HEAD_DIM).reshape(1, SSM_WIDTH), norm_g.reshape(1, SSM_WIDTH))


def _compress(load_pair, h, n_sub, cw):
    hid = jnp.zeros((n_sub, 2 * KV_ROWS), F32)
    for jp in range(D_CMP // 2):
        hid = hid + _dot(load_pair(h, jp), cw[0][jp])
    return _compress_finish(hid, n_sub, cw)


def _compress_finish(hid, n_sub, cw):
    _, b1_ref, w2_ref, b2_ref = cw
    pre = hid[:, :KV_ROWS] + pltpu.roll(hid[:, KV_ROWS:], n_sub - 1, 0) + b1_ref[...]
    return _dot(_gelu_tanh(pre).astype(BF16), w2_ref[...]) + b2_ref[...]


def _pad_q(q):
    return jnp.concatenate([q, jnp.zeros(q.shape, q.dtype)], axis=1)


def _select(score, qpos, n_slc):
    jj = lax.broadcasted_iota(jnp.int32, score.shape, 1)
    cur = qpos // L_SEL
    visible = jj * L_SEL <= qpos
    forced = (jj == 0) | (jj == cur) | (jj == cur - 1)
    sc = jnp.where(visible, jnp.where(forced, FORCED_SCORE, score), -1.0)
    sc = jnp.where(jj < n_slc, sc, -2.0)
    lane = lax.broadcasted_iota(jnp.int32, (1, score.shape[1]), 1)
    rank = jnp.zeros(score.shape, F32)
    for k in range(n_slc):
        ck = sc[:, k:k + 1]
        tie = jnp.where(lane > k, 1.0, 0.0)
        rank = rank + jnp.where(ck > sc, 1.0, jnp.where(ck == sc, tie, 0.0))
    return jnp.where(rank < min(TOP_N, n_slc), jnp.where(sc >= 0.0, 1.0, 0.0), 0.0)


def _select_blocks_on_rows(score_t, qpos, n_slc, n_visible):
    jj = lax.broadcasted_iota(jnp.int32, score_t.shape, 0)
    cur = qpos // L_SEL
    visible = jj * L_SEL <= qpos
    forced = (jj == 0) | (jj == cur) | (jj == cur - 1)
    sc = jnp.where(visible, jnp.where(forced, FORCED_SCORE, score_t), -1.0)

    def count(rank, ks):
        for k in ks:
            tie = jnp.where(jj > k, 1.0, 0.0)
            rank = rank + jnp.where(sc[k:k + 1, :] > sc, 1.0, jnp.where(sc[k:k + 1, :] == sc, tie, 0.0))
        return rank

    rank = jnp.zeros(score_t.shape, F32)
    group = 8
    for k0 in range(0, n_slc, group):
        ks = range(k0, min(k0 + group, n_slc))
        rank = lax.cond(k0 < n_visible, functools.partial(count, ks=ks), lambda r: r, rank)
    return jnp.where(rank < min(TOP_N, n_slc), jnp.where(sc >= 0.0, 1.0, 0.0), 0.0)


def _softmax_rows(s):
    m = jnp.max(s, -1, keepdims=True)
    e = jnp.where(s > MASKED_BELOW, jnp.exp(s - m), 0.0)
    return e / jnp.maximum(jnp.sum(e, -1, keepdims=True), 1e-30)


def _stack_heads(q_ref, h, i=0):
    return jnp.concatenate([q_ref[i, :, (h * GQA + g) * HEAD_DIM:(h * GQA + g + 1) * HEAD_DIM]
                            for g in range(GQA)], axis=0)


def _unstack_heads(o_ref, h, o, rows, i=0):
    for g in range(GQA):
        hd = h * GQA + g
        o_ref[i, :, hd * HEAD_DIM:(hd + 1) * HEAD_DIM] = o[g * rows:(g + 1) * rows]


def _cmp_prompt_kernel(kvc_ref, q_ref, bias_ref, w1_ref, b1_ref, w2_ref, b2_ref, ov_ref, o_ref, sel_ref,
                       kvcmp, *, n_sub, n_slc):
    qb = pl.program_id(1)

    @pl.when(qb == 0)
    def _():
        rows = lambda h, j: kvc_ref[h, pl.ds(j, n_sub, stride=D_CMP), :].astype(BF16)
        load = lambda h, jp: jnp.concatenate([rows(h, 2 * jp), rows(h, 2 * jp + 1)], axis=1)
        for h in range(KV_HEADS):
            kvcmp[h] = _compress(load, h, n_sub, (w1_ref, b1_ref, w2_ref, b2_ref)).astype(BF16)

    qpos = qb * Q_BLOCK + lax.broadcasted_iota(jnp.int32, (1, Q_BLOCK), 1)
    for h in range(KV_HEADS):
        kv = kvcmp[h]
        s = _dot_nt(_pad_q(_stack_heads(q_ref, h)), kv) + bias_ref[h]
        p = _softmax_rows(s)
        _unstack_heads(o_ref, h, _dot(p.astype(BF16), kv)[:, HEAD_DIM:], Q_BLOCK)
        pg = p[0:Q_BLOCK]
        for g in range(1, GQA):
            pg = pg + p[g * Q_BLOCK:(g + 1) * Q_BLOCK]
        score_t = _dot(pg, ov_ref[...], HIGHEST).T
        sel_t = _select_blocks_on_rows(score_t[0:n_slc], qpos, n_slc, (qb + 1) * (Q_BLOCK // L_SEL))
        sel_ref[0, h] = jnp.concatenate([sel_t, jnp.zeros((LANES - n_slc, Q_BLOCK), F32)], axis=0).T


def _cmp_prompt(kvc, q, bias, cmp_w, overlap):
    bsz, t, _ = q.shape
    n_sub = t // D_CMP
    nqb = t // Q_BLOCK
    n_slc = t // L_SEL
    assert n_sub == LANES and n_slc <= LANES
    fixed = lambda a: pl.BlockSpec(a.shape, lambda b, i, _n=a.ndim: (0,) * _n)
    return pl.pallas_call(
        functools.partial(_cmp_prompt_kernel, n_sub=n_sub, n_slc=n_slc),
        grid=(bsz, nqb),
        in_specs=[pl.BlockSpec((KV_HEADS, t, KV_ROWS), lambda b, i: (0, b, 0)),
                  pl.BlockSpec((1, Q_BLOCK, ATT_WIDTH), lambda b, i: (b, i, 0)),
                  pl.BlockSpec((KV_HEADS, GQA * Q_BLOCK, n_sub), lambda b, i: (0, i, 0))]
        + [fixed(a) for a in cmp_w] + [fixed(overlap)],
        out_specs=[pl.BlockSpec((1, Q_BLOCK, ATT_WIDTH), lambda b, i: (b, i, 0)),
                   pl.BlockSpec((1, KV_HEADS, Q_BLOCK, LANES), lambda b, i: (b, 0, i, 0))],
        out_shape=[jax.ShapeDtypeStruct((bsz, t, ATT_WIDTH), F32),
                   jax.ShapeDtypeStruct((bsz, KV_HEADS, t, LANES), F32)],
        scratch_shapes=[pltpu.VMEM((KV_HEADS, n_sub, KV_ROWS), BF16)],
        compiler_params=_cparams(("parallel", "arbitrary")),
        name="cmp_prompt",
    )(kvc, q, bias, *cmp_w, overlap)


def _nsa_prompt_kernel(q_ref, kvs_ref, kvw_ref, sel_ref, eneg_ref, near_ref, far_ref, winb_ref, os_ref, ow_ref,
                       kaug, vsa, kwp, vwa, s_buf, mrun, acc, *, t):
    qb = pl.program_id(1)
    rows = GQA * Q_BLOCK
    near_w = 2 * Q_BLOCK
    win_w = WINDOW + Q_BLOCK

    @pl.when(qb == 0)
    def _():
        ones_row = jnp.where(lax.broadcasted_iota(jnp.int32, (KV_ROWS - HEAD_DIM, t), 0) == 0, 1.0, 0.0).astype(BF16)
        for h in range(KV_HEADS):
            lo = h * KV_ROWS
            kaug[h, :, 0:Q_BLOCK] = jnp.zeros((HEAD_DIM + SEL_PAD, Q_BLOCK), BF16)
            kaug[h, 0:HEAD_DIM, Q_BLOCK:] = kvs_ref[0, lo:lo + HEAD_DIM, :].astype(BF16)
            kaug[h, HEAD_DIM:, Q_BLOCK:] = eneg_ref[...]
            vsa[h, :, 0:Q_BLOCK] = jnp.zeros((KV_ROWS, Q_BLOCK), BF16)
            vsa[h, 0:HEAD_DIM, Q_BLOCK:] = kvs_ref[0, lo + HEAD_DIM:lo + KV_ROWS, :].astype(BF16)
            vsa[h, HEAD_DIM:, Q_BLOCK:] = ones_row
            kwp[h, :, 0:WINDOW] = jnp.zeros((HEAD_DIM, WINDOW), BF16)
            kwp[h, :, WINDOW:] = kvw_ref[0, lo:lo + HEAD_DIM, :].astype(BF16)
            vwa[h, :, 0:WINDOW] = jnp.zeros((KV_ROWS, WINDOW), BF16)
            vwa[h, 0:HEAD_DIM, WINDOW:] = kvw_ref[0, lo + HEAD_DIM:lo + KV_ROWS, :].astype(BF16)
            vwa[h, HEAD_DIM:, WINDOW:] = ones_row

    def normalise(a):
        return a[:, 0:HEAD_DIM] / a[:, HEAD_DIM:HEAD_DIM + 1]

    def tile_max(s):
        m = s[:, 0:LANES]
        for i in range(1, s.shape[1] // LANES):
            m = jnp.maximum(m, s[:, i * LANES:(i + 1) * LANES])
        return m

    start = pl.multiple_of(qb * Q_BLOCK, Q_BLOCK)
    n_chunk = t // SEL_CHUNK
    tiles_per_chunk = SEL_CHUNK // Q_BLOCK
    for h in range(KV_HEADS):
        q4 = _stack_heads(q_ref, h)
        notsel = (1.0 - sel_ref[0, h][:, 0:SEL_PAD]).astype(BF16)
        qa = jnp.concatenate([q4, jnp.concatenate([notsel] * GQA, axis=0)], axis=1)

        far = jnp.concatenate([far_ref[h]] * tiles_per_chunk, axis=1)
        mrun[...] = jnp.full((rows, LANES), NEG, F32)
        for c in range(n_chunk):
            @pl.when(c * tiles_per_chunk < qb - 1)
            def _(c=c):
                key = c * SEL_CHUNK + lax.broadcasted_iota(jnp.int32, (1, SEL_CHUNK), 1)
                late = jnp.where(key < (qb - 1) * Q_BLOCK, 0.0, NEG)
                lo = Q_BLOCK + c * SEL_CHUNK
                s = _dot(qa, kaug[h, :, lo:lo + SEL_CHUNK]) + far + late
                s_buf[:, c * SEL_CHUNK:(c + 1) * SEL_CHUNK] = s
                mrun[...] = jnp.maximum(mrun[...], tile_max(s))

        first = jnp.where(lax.broadcasted_iota(jnp.int32, (1, near_w), 1) < Q_BLOCK,
                          jnp.where(qb >= 1, 0.0, NEG), 0.0)
        s_near = _dot(qa, kaug[h, :, pl.ds(start, near_w)]) + near_ref[h] + first
        pad = jnp.where(lax.broadcasted_iota(jnp.int32, (1, win_w), 1) < WINDOW - qb * Q_BLOCK, NEG, 0.0)
        s_w = _dot(q4, kwp[h, :, pl.ds(start, win_w)]) + winb_ref[h] + pad
        m = jnp.max(jnp.maximum(mrun[...], tile_max(s_near)), -1, keepdims=True)
        acc[...] = _dot_nt(jnp.exp(s_near - m).astype(BF16), vsa[h, :, pl.ds(start, near_w)])
        p_w = jnp.exp(s_w - jnp.max(tile_max(s_w), -1, keepdims=True))
        _unstack_heads(ow_ref, h, normalise(_dot_nt(p_w.astype(BF16), vwa[h, :, pl.ds(start, win_w)])), Q_BLOCK)
        for c in range(n_chunk):
            @pl.when(c * tiles_per_chunk < qb - 1)
            def _(c=c):
                lo = Q_BLOCK + c * SEL_CHUNK
                p = jnp.exp(s_buf[:, c * SEL_CHUNK:(c + 1) * SEL_CHUNK] - m)
                acc[...] = acc[...] + _dot_nt(p.astype(BF16), vsa[h, :, lo:lo + SEL_CHUNK])
        _unstack_heads(os_ref, h, normalise(acc[...]), Q_BLOCK)


def _nsa_prompt(q, kvs_t, kvw_t, sel, eneg, near, far, win_bias):
    bsz, _, t = kvs_t.shape
    assert t % SEL_CHUNK == 0
    nqb = t // Q_BLOCK
    qblk = lambda b, i: (b, i, 0)
    per_b = lambda b, i: (b, 0, 0)
    fixed = lambda a: pl.BlockSpec(a.shape, lambda b, i, _n=a.ndim: (0,) * _n)
    rows = GQA * Q_BLOCK
    return pl.pallas_call(
        functools.partial(_nsa_prompt_kernel, t=t),
        grid=(bsz, nqb),
        in_specs=[pl.BlockSpec((1, Q_BLOCK, ATT_WIDTH), qblk), pl.BlockSpec((1, KV_COLS, t), per_b),
                  pl.BlockSpec((1, KV_COLS, t), per_b),
                  pl.BlockSpec((1, KV_HEADS, Q_BLOCK, LANES), lambda b, i: (b, 0, i, 0)),
                  fixed(eneg), fixed(near), fixed(far), fixed(win_bias)],
        out_specs=[pl.BlockSpec((1, Q_BLOCK, ATT_WIDTH), qblk), pl.BlockSpec((1, Q_BLOCK, ATT_WIDTH), qblk)],
        out_shape=[jax.ShapeDtypeStruct((bsz, t, ATT_WIDTH), F32)] * 2,
        scratch_shapes=[pltpu.VMEM((KV_HEADS, HEAD_DIM + SEL_PAD, Q_BLOCK + t), BF16),
                        pltpu.VMEM((KV_HEADS, KV_ROWS, Q_BLOCK + t), BF16),
                        pltpu.VMEM((KV_HEADS, HEAD_DIM, WINDOW + t), BF16),
                        pltpu.VMEM((KV_HEADS, KV_ROWS, WINDOW + t), BF16),
                        pltpu.VMEM((rows, t), F32), pltpu.VMEM((rows, LANES), F32),
                        pltpu.VMEM((rows, KV_ROWS), F32)],
        compiler_params=_cparams(("parallel", "arbitrary")),
        name="nsa_prompt",
    )(q, kvs_t, kvw_t, sel, eneg, near, far, win_bias)


def _page_copy(pages_hbm, page, buf, sem, slot, k):
    return pltpu.make_async_copy(pages_hbm.at[page], buf.at[slot, k], sem.at[slot])


def _stream_pages(pt_ref, pages_hbm, buf, sem, n_pages):
    b = pl.program_id(0)
    slot = lax.rem(b, 2)

    def start(row, into):
        for k in range(n_pages):
            _page_copy(pages_hbm, pt_ref[row, k], buf, sem, into, k).start()

    @pl.when(b == 0)
    def _():
        start(0, 0)

    @pl.when(b + 1 < pl.num_programs(0))
    def _():
        start(b + 1, 1 - slot)

    for k in range(n_pages):
        _page_copy(pages_hbm, 0, buf, sem, slot, k).wait()
    return slot


def _cmp_sample_kernel(pt_ref, pages_hbm, q_ref, bias_ref, perm_ref, w1_ref, b1_ref, w2_ref, b2_ref, ov_ref, o_ref,
                       sel_ref, buf, sem, *xj, n_pages, n_slc, past, t):
    slot = _stream_pages(pt_ref, pages_hbm, buf, sem, n_pages)
    sub = PAGE_SIZE // D_CMP
    perm = perm_ref[...]
    per_tile = LANES // (2 * sub)
    n_sub = n_pages * sub
    n_pair = n_pages // 2
    cw = (w1_ref, b1_ref, w2_ref, b2_ref)
    qpos = past + lax.broadcasted_iota(jnp.int32, (t, 1), 0)

    def sort_pair(h, k2):
        lo = h * KV_ROWS
        pair = jnp.concatenate([buf[slot, 2 * k2, lo:lo + KV_ROWS, :].astype(BF16),
                                buf[slot, 2 * k2 + 1, lo:lo + KV_ROWS, :].astype(BF16)], axis=1)
        y = _dot(pair, perm)
        for c in range(2 * PAGE_SIZE // LANES):
            xt = y[:, c * LANES:(c + 1) * LANES].T
            for i in range(per_tile):
                j = c * per_tile + i
                xj[h][j // 2, 2 * sub * k2:2 * sub * (k2 + 1), (j % 2) * KV_ROWS:(j % 2 + 1) * KV_ROWS] = (
                    xt[2 * sub * i:2 * sub * (i + 1)].astype(BF16))

    def hidden_step(hid, h, jp):
        return hid + _dot(xj[h][jp], w1_ref[jp])

    def attend(h, hid):
        kv = _compress_finish(hid, n_sub, cw).astype(BF16)
        qh = _pad_q(_stack_heads(q_ref, h).astype(BF16))
        p = _softmax_rows(_dot_nt(qh, kv) + bias_ref[h])
        _unstack_heads(o_ref, h, _dot(p.astype(BF16), kv)[:, HEAD_DIM:], t)
        pg = p[0:t]
        for g in range(1, GQA):
            pg = pg + p[g * t:(g + 1) * t]
        return _dot(pg, ov_ref[...], HIGHEST)

    n_jp = D_CMP // 2
    zero = jnp.zeros((n_sub, 2 * KV_ROWS), F32)
    for k2 in range(n_pair):
        sort_pair(0, k2)
    hid0, hid1 = zero, zero
    every = n_pair // n_jp
    for k2 in range(n_pair):
        sort_pair(1, k2)
        if k2 % every == every - 1:
            hid0 = hidden_step(hid0, 0, k2 // every)
    for jp in range(n_jp // 2):
        hid1 = hidden_step(hid1, 1, jp)
    score0 = attend(0, hid0)
    for jp in range(n_jp // 2, n_jp):
        hid1 = hidden_step(hid1, 1, jp)
    sel_ref[0, 0] = _select(score0, qpos, n_slc)
    sel_ref[0, 1] = _select(attend(1, hid1), qpos, n_slc)


def _page_scratch(n_pages):
    return [pltpu.VMEM((2, n_pages, KV_COLS, PAGE_SIZE), F32), pltpu.SemaphoreType.DMA((2,))]


def _cmp_sample(pages, page_table, q, bias_cs, cmp_w, overlap, n_slc):
    bsz, n_pages = page_table.shape
    t = q.shape[1]
    width = overlap.shape[1]
    assert n_pages % 2 == 0
    sub = PAGE_SIZE // D_CMP
    col = np.arange(2 * PAGE_SIZE)
    page, row = col // PAGE_SIZE, col % PAGE_SIZE
    dest = (row % D_CMP) * 2 * sub + page * sub + row // D_CMP
    perm = jnp.asarray((dest[:, None] == col[None, :]).astype(np.float32), BF16)
    fixed = lambda a: pl.BlockSpec(a.shape, lambda b, pt, _n=a.ndim: (0,) * _n)
    grid_spec = pltpu.PrefetchScalarGridSpec(
        num_scalar_prefetch=1,
        grid=(bsz,),
        in_specs=[pl.BlockSpec(memory_space=pl.ANY), pl.BlockSpec((1, t, ATT_WIDTH), lambda b, pt: (b, 0, 0)),
                  fixed(bias_cs), fixed(perm)] + [fixed(a) for a in cmp_w] + [fixed(overlap)],
        out_specs=[pl.BlockSpec((1, t, ATT_WIDTH), lambda b, pt: (b, 0, 0)),
                   pl.BlockSpec((1, KV_HEADS, t, width), lambda b, pt: (b, 0, 0, 0))],
        scratch_shapes=_page_scratch(n_pages)
        + [pltpu.VMEM((D_CMP // 2, n_pages * sub, 2 * KV_ROWS), BF16)] * KV_HEADS)
    return pl.pallas_call(
        functools.partial(_cmp_sample_kernel, n_pages=n_pages, n_slc=n_slc, past=n_pages * PAGE_SIZE, t=t),
        grid_spec=grid_spec,
        out_shape=[jax.ShapeDtypeStruct((bsz, t, ATT_WIDTH), F32),
                   jax.ShapeDtypeStruct((bsz, KV_HEADS, t, width), F32)],
        compiler_params=_cparams(("arbitrary",)),
        name="cmp_sample",
    )(page_table, pages, q, bias_cs, perm, *cmp_w, overlap)


def _joint_attend(s_past, vt_past, s_new, v_new):
    m = jnp.maximum(jnp.max(s_past, -1, keepdims=True), jnp.max(s_new, -1, keepdims=True))
    e_past = jnp.exp(s_past - m)
    e_new = jnp.exp(s_new - m)
    den = jnp.sum(e_past, -1, keepdims=True) + jnp.sum(e_new, -1, keepdims=True)
    acc = _dot_nt(e_past.astype(BF16), vt_past) + _dot(e_new.astype(BF16), v_new)
    return acc / den


def _pad_new_rows(new_ref, t, i=0):
    return jnp.concatenate([new_ref[i], jnp.zeros((LANES - t, KV_COLS), F32)], axis=0)


def _sel_sample_kernel(pt_ref, pages_hbm, new_ref, q_ref, sel_ref, far_ref, near_ref, biasn_ref, o_ref, buf, sem,
                       kvb, mask, *, n_pages, t):
    slot = _stream_pages(pt_ref, pages_hbm, buf, sem, n_pages)
    for k in range(n_pages):
        kvb[:, k * PAGE_SIZE:(k + 1) * PAGE_SIZE] = buf[slot, k].astype(BF16)
    kv_new = _pad_new_rows(new_ref, t).astype(BF16)
    rows = GQA * t
    lane = lax.broadcasted_iota(jnp.int32, (rows, LANES), 1)
    per_tile = LANES // L_SEL
    for h in range(KV_HEADS):
        lo = h * KV_ROWS
        sel4 = jnp.concatenate([sel_ref[0, h]] * GQA, axis=0)

        def tile_mask(k):
            cols = [jnp.broadcast_to(sel4[:, per_tile * k + i:per_tile * k + i + 1], (rows, LANES))
                    for i in range(per_tile)]
            m = cols[-1]
            for i in range(per_tile - 2, -1, -1):
                m = jnp.where(lane < (i + 1) * L_SEL, cols[i], m)
            return (m - 1.0) * (-NEG)

        far = jnp.concatenate([jnp.broadcast_to(far_ref[h * GQA + g:h * GQA + g + 1, :], (t, LANES))
                               for g in range(GQA)], axis=0)
        for k in range(n_pages - 1):
            mask[:, k * LANES:(k + 1) * LANES] = tile_mask(k) + far
        mask[:, (n_pages - 1) * LANES:n_pages * LANES] = tile_mask(n_pages - 1) + near_ref[h]
        qh = _stack_heads(q_ref, h).astype(BF16)
        s_past = _dot(qh, kvb[lo:lo + HEAD_DIM, :]) + mask[...]
        s_new = _dot_nt(qh, kv_new[:, lo:lo + HEAD_DIM]) + biasn_ref[h] + tile_mask(n_pages)
        o = _joint_attend(s_past, kvb[lo + HEAD_DIM:lo + KV_ROWS, :], s_new, kv_new[:, lo + HEAD_DIM:lo + KV_ROWS])
        _unstack_heads(o_ref, h, o, t)


def _sel_sample(pages, page_table, kvs_new, q, sel, far, near, bias_new):
    bsz, n_pages = page_table.shape
    t = q.shape[1]
    past = n_pages * PAGE_SIZE
    width = sel.shape[-1]
    assert (n_pages + 1) * (LANES // L_SEL) <= width
    per_b = lambda b, pt: (b, 0, 0)
    fixed = lambda a: pl.BlockSpec(a.shape, lambda b, pt, _n=a.ndim: (0,) * _n)
    grid_spec = pltpu.PrefetchScalarGridSpec(
        num_scalar_prefetch=1,
        grid=(bsz,),
        in_specs=[pl.BlockSpec(memory_space=pl.ANY),
                  pl.BlockSpec((1, t, KV_COLS), per_b), pl.BlockSpec((1, t, ATT_WIDTH), per_b),
                  pl.BlockSpec((1, KV_HEADS, t, width), lambda b, pt: (b, 0, 0, 0)),
                  fixed(far), fixed(near), fixed(bias_new)],
        out_specs=pl.BlockSpec((1, t, ATT_WIDTH), per_b),
        scratch_shapes=_page_scratch(n_pages)
        + [pltpu.VMEM((KV_COLS, past), BF16), pltpu.VMEM((GQA * t, past), F32)])
    return pl.pallas_call(
        functools.partial(_sel_sample_kernel, n_pages=n_pages, t=t),
        grid_spec=grid_spec,
        out_shape=jax.ShapeDtypeStruct((bsz, t, ATT_WIDTH), F32),
        compiler_params=_cparams(("arbitrary",)),
        name="sel_sample",
    )(page_table, pages, kvs_new, q, sel, far, near, bias_new)


def _win_sample_kernel(buf_ref, new_ref, q_ref, bias_ref, biasn_ref, o_ref, win_ref, *, t, nb):
    lane = lax.broadcasted_iota(jnp.int32, (KV_COLS, LANES), 1)
    for i in range(nb):
        buf = buf_ref[i]
        w = buf.shape[1]
        new = _pad_new_rows(new_ref, t, i)
        shifted = pltpu.roll(buf, w - t, 1)
        tail = pltpu.roll(new.T, LANES - t, 1)
        win_ref[i, :, 0:w - LANES] = shifted[:, 0:w - LANES]
        win_ref[i, :, w - LANES:w] = jnp.where(lane >= LANES - t, tail, shifted[:, w - LANES:w])
        kvb = buf.astype(BF16)
        kv_new = new.astype(BF16)
        for h in range(KV_HEADS):
            lo = h * KV_ROWS
            qh = _stack_heads(q_ref, h, i).astype(BF16)
            s_past = _dot(qh, kvb[lo:lo + HEAD_DIM, :]) + bias_ref[h]
            s_new = _dot_nt(qh, kv_new[:, lo:lo + HEAD_DIM]) + biasn_ref[h]
            o = _joint_attend(s_past, kvb[lo + HEAD_DIM:lo + KV_ROWS, :], s_new,
                              kv_new[:, lo + HEAD_DIM:lo + KV_ROWS])
            _unstack_heads(o_ref, h, o, t, i)


def _win_sample(buf_t, kvw_new, q, bias_buf, bias_new):
    bsz, _, w = buf_t.shape
    t = q.shape[1]
    nb = ROWS_PER_STEP if bsz % ROWS_PER_STEP == 0 else 1
    per_b = lambda b: (b, 0, 0)
    fixed = lambda a: pl.BlockSpec(a.shape, lambda b, _n=a.ndim: (0,) * _n)
    return pl.pallas_call(
        functools.partial(_win_sample_kernel, t=t, nb=nb),
        grid=(bsz // nb,),
        in_specs=[pl.BlockSpec((nb, KV_COLS, w), per_b), pl.BlockSpec((nb, t, KV_COLS), per_b),
                  pl.BlockSpec((nb, t, ATT_WIDTH), per_b), fixed(bias_buf), fixed(bias_new)],
        out_specs=[pl.BlockSpec((nb, t, ATT_WIDTH), per_b), pl.BlockSpec((nb, KV_COLS, w), per_b)],
        out_shape=[jax.ShapeDtypeStruct((bsz, t, ATT_WIDTH), F32),
                   jax.ShapeDtypeStruct((bsz, KV_COLS, w), F32)],
        compiler_params=_cparams(("parallel",)),
        name="win_sample",
    )(buf_t, kvw_new, q, bias_buf, bias_new)


def _combine_kernel(x_ref, y_ref, oc_ref, os_ref, ow_ref, sm_ref, eg_ref, eb_ref, ex_ref, ag_ref, wo_ref, g1_ref,
                    b1_ref, h_ref):
    xn = _layer_norm(x_ref[...], eg_ref[...], eb_ref[...])
    gates = _sigmoid(sm_ref[...])
    g_hi = gates.astype(BF16)
    g_lo = (gates - g_hi.astype(F32)).astype(BF16)
    o = jnp.zeros(oc_ref.shape, F32)
    for br, ref in enumerate((oc_ref, os_ref, ow_ref)):
        o = o + (_dot(g_hi, ex_ref[br]) + _dot(g_lo, ex_ref[br])) * ref[...]
    rms = lax.rsqrt(jnp.mean(o * o, -1, keepdims=True) + EPS)
    att = (o * rms * ag_ref[...]).astype(BF16)
    mix = _dot(jnp.concatenate([y_ref[...], att], axis=1), wo_ref[...])
    h_ref[...] = _layer_norm(ALPHA * xn + mix, g1_ref[...], b1_ref[...])


def _combine(x2d, y, oc, os_, ow, sm, eg, eb, gate_expand, ag, wo, g1, b1, tm):
    n = x2d.shape[0]
    tm = _row_tile(n, tm)
    row = lambda i: (i, 0)
    fixed = lambda a: pl.BlockSpec(a.shape, lambda i, _n=a.ndim: (0,) * _n)
    att = pl.BlockSpec((tm, ATT_WIDTH), row)
    return pl.pallas_call(
        _combine_kernel,
        grid=(n // tm,),
        in_specs=[pl.BlockSpec((tm, D_MODEL), row), pl.BlockSpec((tm, SSM_WIDTH), row), att, att, att,
                  pl.BlockSpec((tm, LANES), row), fixed(eg), fixed(eb), fixed(gate_expand), fixed(ag), fixed(wo),
                  fixed(g1), fixed(b1)],
        out_specs=pl.BlockSpec((tm, D_MODEL), row),
        out_shape=jax.ShapeDtypeStruct((n, D_MODEL), F32),
        compiler_params=_cparams(("parallel",)),
        name="combine",
    )(x2d, y, oc, os_, ow, sm, eg, eb, gate_expand, ag, wo, g1, b1)


def _ffn_kernel(h_ref, wu_ref, wd_ref, g_ref, b_ref, o_ref, *, tf):
    h = h_ref[...]
    hb = h.astype(BF16)
    acc = None
    for k in range(D_FF // tf):
        u = jnp.maximum(_dot(hb, wu_ref[:, k * tf:(k + 1) * tf]), 0.0)
        part = _dot((u * u).astype(BF16), wd_ref[k * tf:(k + 1) * tf, :])
        acc = part if acc is None else acc + part
    o_ref[...] = _layer_norm(ALPHA * h + acc, g_ref[...], b_ref[...])


def _ffn(h2d, wu, wd, g, b, tm, tf):
    n = h2d.shape[0]
    tm = _row_tile(n, tm)
    resident = lambda a: pl.BlockSpec(a.shape, lambda i: (0, 0), pipeline_mode=pl.Buffered(1))
    return pl.pallas_call(
        functools.partial(_ffn_kernel, tf=tf),
        grid=(n // tm,),
        in_specs=[pl.BlockSpec((tm, D_MODEL), lambda i: (i, 0)), resident(wu), resident(wd), resident(g), resident(b)],
        out_specs=pl.BlockSpec((tm, D_MODEL), lambda i: (i, 0)),
        out_shape=jax.ShapeDtypeStruct((n, D_MODEL), F32),
        compiler_params=_cparams(("parallel",)),
        name="ffn",
    )(h2d, wu, wd, g, b)


def _bucket_np(dist):
    d = np.maximum(dist, 0)
    exact = N_BUCKETS // 2
    far = exact + (np.log(np.maximum(d, 1).astype(np.float32) / np.float32(exact))
                   / np.float32(math.log(MAX_DISTANCE / exact)) * (N_BUCKETS - exact)).astype(np.int32)
    return np.where(d < exact, d, np.minimum(far, N_BUCKETS - 1)).astype(np.int32)


def _bias_lookup(tbl, dist, mask=None):
    dist = np.asarray(dist)
    onehot = np.eye(N_BUCKETS, dtype=np.float32)[_bucket_np(dist).reshape(-1)]
    b = jnp.dot(jnp.asarray(onehot), tbl, precision=HIGHEST).T.reshape((ATT_HEADS,) + dist.shape)
    return b if mask is None else jnp.where(jnp.asarray(mask)[None], b, NEG)


def _toeplitz_tile(tbl, offset, mask):
    period = 2 * Q_BLOCK
    k = np.arange(period)
    vals = _bias_lookup(tbl, offset - np.where(k < Q_BLOCK, k, k - period))
    tiled = jnp.tile(vals, (1, Q_BLOCK))[:, :Q_BLOCK * (period - 1)]
    t = tiled.reshape(ATT_HEADS, Q_BLOCK, period - 1)[:, :, :Q_BLOCK]
    return jnp.where(jnp.asarray(mask)[None], t, NEG)


def _cmp_prompt_bias(tbl, t):
    n_sub = t // D_CMP
    back = (MAX_DISTANCE + L_CMP - 1) // D_CMP
    band = (np.arange(Q_BLOCK)[:, None] - D_CMP * np.arange(NEAR_W)[None, :] + D_CMP * back - (L_CMP - 1))
    near = jnp.pad(_bias_lookup(tbl, band), ((0, 0), (0, 0), (0, n_sub - NEAR_W)))
    far = tbl[N_BUCKETS - 1][:, None, None]
    blk = np.arange(n_sub)[None, :]
    tiles = []
    for qb in range(t // Q_BLOCK):
        near_lo = qb * (Q_BLOCK // D_CMP) - back
        qpos = qb * Q_BLOCK + np.arange(Q_BLOCK)[:, None]
        visible = (blk * D_CMP + (L_CMP - 1) <= qpos) & (blk < n_sub - 1)
        tile = jnp.where(jnp.asarray(blk >= near_lo)[None], jnp.roll(near, near_lo % n_sub, axis=2), far)
        tiles.append(jnp.where(jnp.asarray(visible)[None], tile, NEG))
    tiles = jnp.stack(tiles).reshape(t // Q_BLOCK, KV_HEADS, GQA * Q_BLOCK, n_sub)
    return jnp.moveaxis(tiles, 0, 1).reshape(KV_HEADS, -1, n_sub)


def _stack_gt(tab, t):
    return tab.reshape(KV_HEADS, GQA * t, tab.shape[-1])


def _far_rows(tbl):
    return jnp.broadcast_to(tbl[N_BUCKETS - 1][:, None], (ATT_HEADS, LANES))


def _overlap(n_cmp_pad, n_cmp, width, n_slc):
    i = np.arange(n_cmp_pad)[:, None]
    j = np.arange(width)[None, :]
    ov = (i * D_CMP < (j + 1) * L_SEL) & (i * D_CMP + L_CMP > j * L_SEL) & (i < n_cmp) & (j < n_slc)
    return jnp.asarray(ov.astype(np.float32))


def _prep_cmp_weights(w1, b1, w2, b2):
    eye = jnp.eye(2, dtype=F32)
    w1r = (w1[:, :, :, :, None, :] * eye[None, None, :, None, :, None]).transpose(1, 2, 3, 0, 4, 5)
    w1r = w1r.reshape(D_CMP // 2, 2 * KV_ROWS, 2 * KV_ROWS).astype(BF16)
    w2r = (w2[:, :, None, :] * eye[:, None, :, None]).reshape(2 * CMP_HID, KV_ROWS).astype(BF16)
    return (w1r, b1.reshape(1, 2 * CMP_HID), w2r, b2.reshape(1, KV_ROWS))


def _prep_w_in(w_in):
    sizes = (SSM_WIDTH, CONV_DIM, SSM_HEADS, ATT_WIDTH, KV_COLS, KV_COLS, KV_COLS)
    z, xbc, dt, q, kvc, kvs, kvw, gates = jnp.split(w_in, np.cumsum(sizes).tolist(), axis=1)
    small = jnp.concatenate([dt, gates], axis=1)
    small = jnp.pad(small, ((0, 0), (0, LANES - small.shape[1])))
    return jnp.concatenate([z, xbc, q * ATT_SCALE, kvc, kvs, kvw, small], axis=1).astype(BF16)


def _gate_expand():
    ex = np.zeros((N_BRANCH, LANES, ATT_WIDTH), np.float32)
    for br in range(N_BRANCH):
        for hd in range(ATT_HEADS):
            ex[br, SSM_HEADS + br * ATT_HEADS + hd, hd * HEAD_DIM:(hd + 1) * HEAD_DIM] = 1.0
    return jnp.asarray(ex, BF16)


def _feature_major(a):
    lead = a.shape[:-4]
    rows = a.shape[-4]
    return jnp.moveaxis(a.reshape(lead + (rows, KV_COLS)), -2, -1)


def _row_major6(a_t):
    bsz, _, rows = a_t.shape
    return jnp.moveaxis(a_t, 1, 2).reshape(1, bsz, rows, KV_HEADS, 2, HEAD_DIM)


def kernel(x_prompt, x_sample, cache_cmp_kv, cache_slc_kv, cache_win_kv, state_conv, state_ssm, page_table,
           rel_bias_table, emb_ln_g, emb_ln_b, w_in, conv_w, conv_b, dt_bias, a_log, d_skip, ssm_norm_g,
           cmp_w1, cmp_b1, cmp_w2, cmp_b2, att_norm_g, w_out, ln1_g, ln1_b, w_up, w_down, ln2_g, ln2_b):
    assert w_in.shape[0] == DEPTH
    bp, tp, _ = x_prompt.shape
    bs, ts, _ = x_sample.shape
    n_pages = page_table.shape[1]
    past = n_pages * PAGE_SIZE
    w_buf = cache_win_kv.shape[2]
    assert ts < D_CMP and ts % 8 == 0 and w_buf == WINDOW and past >= WINDOW and tp >= WINDOW
    tbl = rel_bias_table
    vec = lambda v: v.reshape(1, -1)

    w_proj = _prep_w_in(w_in[0])
    cmp_w = _prep_cmp_weights(cmp_w1[0], cmp_b1[0], cmp_w2[0], cmp_b2[0])
    wo = w_out[0].astype(BF16)
    wu = w_up[0].astype(BF16)
    wd = w_down[0].astype(BF16)
    eg, eb = vec(emb_ln_g), vec(emb_ln_b)
    gate_expand = _gate_expand()
    far = _far_rows(tbl)

    def trunk_tail(x2d, y, oc, os_, ow, sm):
        h = _combine(x2d, y, oc, os_, ow, sm, eg, eb, gate_expand, vec(att_norm_g[0]), wo, vec(ln1_g[0]),
                     vec(ln1_b[0]), COMBINE_ROWS)
        return _ffn(h, wu, wd, vec(ln2_g[0]), vec(ln2_b[0]), FFN_ROWS, FFN_SLAB)

    ssm_w = (conv_w[0], conv_b[0], dt_bias[0], a_log[0], d_skip[0], ssm_norm_g[0])

    xp2 = x_prompt.reshape(bp * tp, D_MODEL)
    z, xbc, q, kvc, kvc_t, kvs_t, kvw_t, sm = _proj(xp2, eg, eb, w_proj, BF16, PROJ_ROWS, seq=(bp, tp))
    r3 = lambda a: a.reshape(bp, tp, a.shape[-1])
    xbc3 = r3(xbc)
    y_ssm, h_new = _ssm(r3(z), xbc3, r3(sm), jnp.zeros((bp, CONV_WIDTH - 1, CONV_DIM), F32),
                        jnp.zeros((bp, SSM_HEADS, SSM_HEAD_DIM, D_STATE), F32), *ssm_w)
    n_sub = tp // D_CMP
    n_slc = tp // L_SEL
    oc, sel = _cmp_prompt(kvc, r3(q), _cmp_prompt_bias(tbl, tp), cmp_w, _overlap(n_sub, n_sub - 1, LANES, n_slc))
    ii = np.arange(Q_BLOCK)[:, None] - np.arange(Q_BLOCK)[None, :]
    tiles_gq = lambda rs, ok: jnp.concatenate(
        [_toeplitz_tile(tbl, Q_BLOCK * r, ok(ii + Q_BLOCK * r)) for r in rs], axis=2).reshape(
            KV_HEADS, GQA * Q_BLOCK, len(rs) * Q_BLOCK)
    near = tiles_gq((1, 0), lambda d: d >= 0)
    win_bias = tiles_gq(range(WINDOW // Q_BLOCK, -1, -1), lambda d: (d >= 0) & (d < WINDOW))
    far_gq = jnp.repeat(far, Q_BLOCK, axis=0).reshape(KV_HEADS, GQA * Q_BLOCK, LANES)
    assert n_slc <= SEL_PAD
    eneg = jnp.asarray(np.where(np.arange(SEL_PAD)[:, None] == (np.arange(tp) // L_SEL)[None, :], NEG, 0.0), BF16)
    os_, ow = _nsa_prompt(r3(q), kvs_t, kvw_t, sel, eneg, near, far_gq, win_bias)
    f2 = lambda a: a.reshape(bp * tp, a.shape[-1])
    y_prompt = trunk_tail(xp2, f2(y_ssm), f2(oc), f2(os_), f2(ow), sm).reshape(bp, tp, D_MODEL)
    w = min(WINDOW, tp)
    prompt_state = (_row_major6(kvc_t), _row_major6(kvs_t), _row_major6(kvw_t[:, :, tp - w:]),
                    xbc3[:, tp - (CONV_WIDTH - 1):][None], h_new[None])

    xs2 = x_sample.reshape(bs * ts, D_MODEL)
    z, xbc, q, kvc, kvs, kvw, sm = _proj(xs2, eg, eb, w_proj, F32, PROJ_ROWS)
    r3 = lambda a: a.reshape(bs, ts, a.shape[-1])
    xbc3 = r3(xbc)
    y_ssm, h_new = _ssm(r3(z), xbc3, r3(sm), state_conv[0], state_ssm[0], *ssm_w)
    n_sub = past // D_CMP
    n_cmp = n_sub - 1
    n_slc = -(-(past + ts) // L_SEL)
    width = -(-n_slc // LANES) * LANES
    qpos = past + np.arange(ts)
    dist_c = qpos[:, None] - (np.arange(n_sub) * D_CMP + L_CMP - 1)[None, :]
    bias_cs = _stack_gt(_bias_lookup(tbl, dist_c, (dist_c >= 0) & (np.arange(n_sub) < n_cmp)[None, :]), ts)
    oc, sel = _cmp_sample(_feature_major(cache_cmp_kv[0]), page_table, r3(q), bias_cs, cmp_w,
                          _overlap(n_sub, n_cmp, width, n_slc), n_slc)
    dist_last = qpos[:, None] - (past - PAGE_SIZE + np.arange(PAGE_SIZE))[None, :]
    dist_n = np.arange(ts)[:, None] - np.arange(LANES)[None, :]
    bias_new = _stack_gt(_bias_lookup(tbl, dist_n, (dist_n >= 0) & (np.arange(LANES) < ts)[None, :]), ts)
    os_ = _sel_sample(_feature_major(cache_slc_kv[0]), page_table, r3(kvs), r3(q), sel, far,
                      _stack_gt(_bias_lookup(tbl, dist_last), ts), bias_new)
    dist_w = qpos[:, None] - (past - w_buf + np.arange(w_buf))[None, :]
    bias_wb = _stack_gt(_bias_lookup(tbl, dist_w, (dist_w >= 0) & (dist_w < WINDOW)), ts)
    ow, win_new_t = _win_sample(_feature_major(cache_win_kv[0]), r3(kvw), r3(q), bias_wb, bias_new)
    f2 = lambda a: a.reshape(bs * ts, a.shape[-1])
    y_sample = trunk_tail(xs2, f2(y_ssm), f2(oc), f2(os_), f2(ow), sm).reshape(bs, ts, D_MODEL)
    kv6 = lambda a: a.reshape(1, bs, ts, KV_HEADS, 2, HEAD_DIM)
    sample_state = (kv6(kvc), kv6(kvs), _row_major6(win_new_t),
                    xbc3[:, ts - (CONV_WIDTH - 1):][None], h_new[None])

    return (y_prompt, y_sample) + prompt_state + sample_state
```

```python
import functools
import math

import numpy as np
import jax
import jax.numpy as jnp
from jax import lax
from jax.experimental import pallas as pl
from jax.experimental.pallas import tpu as pltpu

F32 = jnp.float32
BF16 = jnp.bfloat16
HIGHEST = lax.Precision.HIGHEST

D_MODEL = 1024
SSM_HEADS = 8
SSM_HEAD_DIM = 64
SSM_WIDTH = SSM_HEADS * SSM_HEAD_DIM
SSM_GROUPS = 2
D_STATE = 128
CONV_WIDTH = 4
CONV_DIM = SSM_WIDTH + 2 * SSM_GROUPS * D_STATE
SSD_CHUNK = 128
ATT_HEADS = 8
KV_HEADS = 2
GQA = ATT_HEADS // KV_HEADS
HEAD_DIM = 64
ATT_WIDTH = ATT_HEADS * HEAD_DIM
KV_COLS = KV_HEADS * 2 * HEAD_DIM
D_CMP = 16
L_CMP = 2 * D_CMP
CMP_HID = 64
L_SEL = 64
TOP_N = 16
WINDOW = 512
Q_BLOCK = 128
N_BRANCH = 3
FORCED_SCORE = 1e4
N_BUCKETS = 32
MAX_DISTANCE = 128
D_FF = 4 * D_MODEL
DEPTH = 1
ALPHA = (2 * DEPTH) ** 0.25
ATT_SCALE = HEAD_DIM ** -0.5
EPS = 1e-5
PAGE_SIZE = 128

LANES = 128
NEG = -1e30
MASKED_BELOW = -1e29
VMEM_LIMIT = 48 * 1024 * 1024
KV_ROWS = 2 * HEAD_DIM
SEL_PAD = 32
NEAR_W = 32
SEL_CHUNK = 512
PROJ_ROWS = 512
COMBINE_ROWS = 256
FFN_ROWS = 512
FFN_SLAB = D_FF // 4
ROWS_PER_STEP = 4

_OFF_Z = 0
_OFF_XBC = _OFF_Z + SSM_WIDTH
_OFF_Q = _OFF_XBC + CONV_DIM
_OFF_KVC = _OFF_Q + ATT_WIDTH
_OFF_KVS = _OFF_KVC + KV_COLS
_OFF_KVW = _OFF_KVS + KV_COLS
_OFF_SM = _OFF_KVW + KV_COLS
_N_PROJ = _OFF_SM + LANES


def _cparams(sem):
    return pltpu.CompilerParams(dimension_semantics=sem, vmem_limit_bytes=VMEM_LIMIT)


def _row_tile(n, preferred):
    tm = min(n, preferred)
    assert n % tm == 0 and tm % 8 == 0
    return tm


def _dot(a, b, precision=None):
    return jnp.dot(a, b, preferred_element_type=F32, precision=precision)


def _dot_nt(a, b):
    return lax.dot_general(a, b, (((1,), (1,)), ((), ())), preferred_element_type=F32)


def _layer_norm(x, g, b):
    mu = jnp.mean(x, -1, keepdims=True)
    xc = x - mu
    var = jnp.mean(xc * xc, -1, keepdims=True)
    return xc * lax.rsqrt(var + EPS) * g + b


def _sigmoid(x):
    return 1.0 / (1.0 + jnp.exp(-x))


def _softplus(x):
    return jnp.maximum(x, 0.0) + jnp.log(1.0 + jnp.exp(-jnp.abs(x)))


def _gelu_tanh(x):
    c = math.sqrt(2.0 / math.pi)
    return 0.5 * x * (1.0 + jnp.tanh(c * (x + 0.044715 * (x * x * x))))


def _proj_kernel(x_ref, g_ref, b_ref, w_ref, z_ref, xbc_ref, q_ref, kvc_ref, *rest, feature_major):
    xn = _layer_norm(x_ref[...], g_ref[...], b_ref[...]).astype(BF16)

    def mm(lo, hi):
        return _dot(xn, w_ref[:, lo:hi])

    z_ref[...] = mm(_OFF_Z, _OFF_XBC)
    xbc_ref[...] = mm(_OFF_XBC, _OFF_Q)
    q_ref[...] = mm(_OFF_Q, _OFF_KVC).astype(q_ref.dtype)
    kvc = mm(_OFF_KVC, _OFF_KVS)
    if feature_major:
        kvct_ref, kvst_ref, kvwt_ref, sm_ref = rest
        for h in range(KV_HEADS):
            kvc_ref[h] = kvc[:, h * KV_ROWS:(h + 1) * KV_ROWS]
        kvct_ref[0] = kvc.T
        kvst_ref[0] = mm(_OFF_KVS, _OFF_KVW).T
        kvwt_ref[0] = mm(_OFF_KVW, _OFF_SM).T
    else:
        kvs_ref, kvw_ref, sm_ref = rest
        kvc_ref[...] = kvc
        kvs_ref[...] = mm(_OFF_KVS, _OFF_KVW)
        kvw_ref[...] = mm(_OFF_KVW, _OFF_SM)
    sm_ref[...] = mm(_OFF_SM, _N_PROJ)


def _proj(x2d, g, b, w, q_dtype, tm, seq=None):
    n = x2d.shape[0]
    tm = _row_tile(n, tm)
    row = lambda i: (i, 0)
    fixed = lambda i: (0, 0)
    rm = lambda wd, dt: (pl.BlockSpec((tm, wd), row), jax.ShapeDtypeStruct((n, wd), dt))
    outs = [rm(SSM_WIDTH, F32), rm(CONV_DIM, F32), rm(ATT_WIDTH, q_dtype)]
    if seq is None:
        outs += [rm(KV_COLS, F32), rm(KV_COLS, F32), rm(KV_COLS, F32)]
    else:
        outs.append((pl.BlockSpec((KV_HEADS, tm, KV_ROWS), lambda i: (0, i, 0)),
                     jax.ShapeDtypeStruct((KV_HEADS, n, KV_ROWS), F32)))
        bsz, t = seq
        assert t % tm == 0 and tm % LANES == 0
        per = t // tm
        fm = (pl.BlockSpec((1, KV_COLS, tm), lambda i: (i // per, 0, i % per)),
              jax.ShapeDtypeStruct((bsz, KV_COLS, t), F32))
        outs += [fm, fm, fm]
    outs.append(rm(LANES, F32))
    return pl.pallas_call(
        functools.partial(_proj_kernel, feature_major=seq is not None),
        grid=(n // tm,),
        in_specs=[pl.BlockSpec((tm, D_MODEL), row), pl.BlockSpec((1, D_MODEL), fixed),
                  pl.BlockSpec((1, D_MODEL), fixed), pl.BlockSpec((D_MODEL, _N_PROJ), fixed)],
        out_specs=[o[0] for o in outs],
        out_shape=[o[1] for o in outs],
        compiler_params=_cparams(("parallel",)),
        name="proj",
    )(x2d, g, b, w)


def _ssm_kernel(z_ref, xbc_ref, sm_ref, hist_ref, h0_ref, cw_ref, cb_ref, dtb_ref, alog_ref, dskip_ref,
                ng_ref, y_ref, hfin_ref, xext, state, *, tb, l, nc, nb):
    for i in range(nb):
        _ssm_row(i, z_ref, xbc_ref, sm_ref, hist_ref, h0_ref, cw_ref, cb_ref, dtb_ref, alog_ref, dskip_ref,
                 ng_ref, y_ref, hfin_ref, xext.at[i], state.at[i], tb=tb, l=l, nc=nc)


def _ssm_row(i, z_ref, xbc_ref, sm_ref, hist_ref, h0_ref, cw_ref, cb_ref, dtb_ref, alog_ref, dskip_ref,
             ng_ref, y_ref, hfin_ref, xext, state, *, tb, l, nc):
    c = pl.program_id(1)

    @pl.when(c == 0)
    def _():
        xext[0:8, :] = jnp.zeros((8, CONV_DIM), F32)
        xext[8 - (CONV_WIDTH - 1):8, :] = hist_ref[i]
        if tb < l:
            xext[8 + tb:8 + l, :] = jnp.zeros((l - tb, CONV_DIM), F32)
        state[...] = h0_ref[i].reshape(SSM_WIDTH, D_STATE)

    xext[8:8 + tb, :] = xbc_ref[i]
    conv = cb_ref[...]
    for k in range(CONV_WIDTH):
        lo = 8 - (CONV_WIDTH - 1) + k
        conv = conv + cw_ref[k:k + 1, :] * xext[lo:lo + l, :]
    xc = conv * _sigmoid(conv)
    xext[0:8, :] = xext[tb:tb + 8, :]

    dt = _softplus(sm_ref[i] + dtb_ref[...])
    if tb < l:
        dt = jnp.concatenate([dt, jnp.zeros((l - tb, LANES), F32)], axis=0)
    a = dt * (-jnp.exp(alog_ref[...]))
    ri = lax.broadcasted_iota(jnp.int32, (l, l), 0)
    ci = lax.broadcasted_iota(jnp.int32, (l, l), 1)
    tril = ri >= ci
    a_cs = _dot(jnp.where(tril, 1.0, 0.0), a, HIGHEST)
    a_cs_t = a_cs.T
    dt_t = dt.T
    ea = jnp.exp(a_cs[:tb])
    wend = dt * jnp.exp(a_cs[l - 1:l, :] - a_cs)
    etot = jnp.exp(a_cs[l - 1:l, :])

    xs = xc[:, :SSM_WIDTH]
    lane = lax.broadcasted_iota(jnp.int32, (tb, LANES), 1)
    srow = lax.broadcasted_iota(jnp.int32, (LANES, D_STATE), 0)
    tril_q = tril[:tb]
    y_pairs = []
    for g in range(SSM_GROUPS):
        bg_f = xc[:, SSM_WIDTH + g * D_STATE:SSM_WIDTH + (g + 1) * D_STATE]
        bg = bg_f.astype(BF16)
        c_lo = SSM_WIDTH + SSM_GROUPS * D_STATE + g * D_STATE
        cg = xc[:tb, c_lo:c_lo + D_STATE].astype(BF16)
        cb = _dot_nt(cg, bg)
        for k in range(2):
            pair = 2 * g + k
            lo = pair * LANES
            h0, h1 = 2 * pair, 2 * pair + 1
            xs_pair = xs[:, lo:lo + LANES]
            xs_b = xs_pair.astype(BF16)
            xs_t = xs_pair.T
            ys, upd = [], []
            for r2, h in enumerate((h0, h1)):
                seg = a_cs[:tb, h:h + 1] - a_cs_t[h:h + 1, :]
                lm = jnp.where(tril_q, jnp.exp(jnp.where(tril_q, seg, 0.0)), 0.0) * dt_t[h:h + 1, :]
                ys.append(_dot((cb * lm).astype(BF16), xs_b))
                upd.append(_dot(xs_t[r2 * SSM_HEAD_DIM:(r2 + 1) * SSM_HEAD_DIM].astype(BF16),
                                (bg_f * wend[:, h:h + 1]).astype(BF16)))
            y_diag = jnp.where(lane < SSM_HEAD_DIM, ys[0], ys[1])
            sp = state[lo:lo + LANES, :]
            y_off = _dot_nt(cg, sp.astype(BF16)) * jnp.where(lane < SSM_HEAD_DIM, ea[:, h0:h0 + 1], ea[:, h1:h1 + 1])
            y_pairs.append(y_diag + y_off)
            keep = jnp.where(srow < SSM_HEAD_DIM, etot[:, h0:h0 + 1], etot[:, h1:h1 + 1])
            state[lo:lo + LANES, :] = sp * keep + jnp.concatenate(upd, axis=0)
    y = jnp.concatenate(y_pairs, axis=1) + dskip_ref[...] * xs[:tb]
    zz = z_ref[i]
    y = y * (zz * _sigmoid(zz))
    gw = SSM_WIDTH // SSM_GROUPS
    outs = []
    for g in range(SSM_GROUPS):
        yg = y[:, g * gw:(g + 1) * gw]
        ms = jnp.mean(yg * yg, -1, keepdims=True)
        outs.append(yg * lax.rsqrt(ms + EPS) * ng_ref[:, g * gw:(g + 1) * gw])
    y_ref[i] = jnp.concatenate(outs, axis=1).astype(y_ref.dtype)

    @pl.when(c == nc - 1)
    def _():
        hfin_ref[i] = state[...].reshape(SSM_HEADS, SSM_HEAD_DIM, D_STATE)


def _ssm(z, xbc, sm, hist, h0, conv_w, conv_b, dt_bias, a_log, d_skip, norm_g):
    bsz, t, _ = z.shape
    l = SSD_CHUNK
    tb = min(l, t)
    assert t % tb == 0 and tb % 8 == 0 and t >= CONV_WIDTH - 1
    nc = t // tb
    nb = ROWS_PER_STEP if (tb < l and bsz % ROWS_PER_STEP == 0) else 1
    pad8 = lambda v: jnp.pad(v.reshape(1, SSM_HEADS), ((0, 0), (0, LANES - SSM_HEADS)))
    blk = lambda b, c: (b, c, 0)
    per_b3 = lambda b, c: (b, 0, 0)
    per_b4 = lambda b, c: (b, 0, 0, 0)
    fixed = lambda b, c: (0, 0)
    return pl.pallas_call(
        functools.partial(_ssm_kernel, tb=tb, l=l, nc=nc, nb=nb),
        grid=(bsz // nb, nc),
        in_specs=[pl.BlockSpec((nb, tb, SSM_WIDTH), blk), pl.BlockSpec((nb, tb, CONV_DIM), blk),
                  pl.BlockSpec((nb, tb, LANES), blk),
                  pl.BlockSpec((nb, CONV_WIDTH - 1, CONV_DIM), per_b3),
                  pl.BlockSpec((nb, SSM_HEADS, SSM_HEAD_DIM, D_STATE), per_b4),
                  pl.BlockSpec((CONV_WIDTH, CONV_DIM), fixed), pl.BlockSpec((1, CONV_DIM), fixed),
                  pl.BlockSpec((1, LANES), fixed), pl.BlockSpec((1, LANES), fixed),
                  pl.BlockSpec((1, SSM_WIDTH), fixed), pl.BlockSpec((1, SSM_WIDTH), fixed)],
        out_specs=[pl.BlockSpec((nb, tb, SSM_WIDTH), blk),
                   pl.BlockSpec((nb, SSM_HEADS, SSM_HEAD_DIM, D_STATE), per_b4)],
        out_shape=[jax.ShapeDtypeStruct((bsz, t, SSM_WIDTH), BF16),
                   jax.ShapeDtypeStruct((bsz, SSM_HEADS, SSM_HEAD_DIM, D_STATE), F32)],
        scratch_shapes=[pltpu.VMEM((nb, 8 + l, CONV_DIM), F32), pltpu.VMEM((nb, SSM_WIDTH, D_STATE), F32)],
        compiler_params=_cparams(("parallel", "arbitrary")),
        name="ssm",
    )(z, xbc, sm, hist, h0, conv_w, conv_b.reshape(1, CONV_DIM), pad8(dt_bias), pad8(a_log),
      jnp.repeat(d_skip, SSM_HEAD_DIM).reshape(1, SSM_WIDTH), norm_g.reshape(1, SSM_WIDTH))


def _compress(load_pair, h, n_sub, cw):
    hid = jnp.zeros((n_sub, 2 * KV_ROWS), F32)
    for jp in range(D_CMP // 2):
        hid = hid + _dot(load_pair(h, jp), cw[0][jp])
    return _compress_finish(hid, n_sub, cw)


def _compress_finish(hid, n_sub, cw):
    _, b1_ref, w2_ref, b2_ref = cw
    pre = hid[:, :KV_ROWS] + pltpu.roll(hid[:, KV_ROWS:], n_sub - 1, 0) + b1_ref[...]
    return _dot(_gelu_tanh(pre).astype(BF16), w2_ref[...]) + b2_ref[...]


def _pad_q(q):
    return jnp.concatenate([q, jnp.zeros(q.shape, q.dtype)], axis=1)


def _select(score, qpos, n_slc):
    jj = lax.broadcasted_iota(jnp.int32, score.shape, 1)
    cur = qpos // L_SEL
    visible = jj * L_SEL <= qpos
    forced = (jj == 0) | (jj == cur) | (jj == cur - 1)
    sc = jnp.where(visible, jnp.where(forced, FORCED_SCORE, score), -1.0)
    sc = jnp.where(jj < n_slc, sc, -2.0)
    lane = lax.broadcasted_iota(jnp.int32, (1, score.shape[1]), 1)
    rank = jnp.zeros(score.shape, F32)
    for k in range(n_slc):
        ck = sc[:, k:k + 1]
        tie = jnp.where(lane > k, 1.0, 0.0)
        rank = rank + jnp.where(ck > sc, 1.0, jnp.where(ck == sc, tie, 0.0))
    return jnp.where(rank < min(TOP_N, n_slc), jnp.where(sc >= 0.0, 1.0, 0.0), 0.0)


def _select_blocks_on_rows(score_t, qpos, n_slc, n_visible):
    jj = lax.broadcasted_iota(jnp.int32, score_t.shape, 0)
    cur = qpos // L_SEL
    visible = jj * L_SEL <= qpos
    forced = (jj == 0) | (jj == cur) | (jj == cur - 1)
    sc = jnp.where(visible, jnp.where(forced, FORCED_SCORE, score_t), -1.0)

    def count(rank, ks):
        for k in ks:
            tie = jnp.where(jj > k, 1.0, 0.0)
            rank = rank + jnp.where(sc[k:k + 1, :] > sc, 1.0, jnp.where(sc[k:k + 1, :] == sc, tie, 0.0))
        return rank

    rank = jnp.zeros(score_t.shape, F32)
    group = 8
    for k0 in range(0, n_slc, group):
        ks = range(k0, min(k0 + group, n_slc))
        rank = lax.cond(k0 < n_visible, functools.partial(count, ks=ks), lambda r: r, rank)
    return jnp.where(rank < min(TOP_N, n_slc), jnp.where(sc >= 0.0, 1.0, 0.0), 0.0)


def _softmax_rows(s):
    m = jnp.max(s, -1, keepdims=True)
    e = jnp.where(s > MASKED_BELOW, jnp.exp(s - m), 0.0)
    return e / jnp.maximum(jnp.sum(e, -1, keepdims=True), 1e-30)


def _stack_heads(q_ref, h, i=0):
    return jnp.concatenate([q_ref[i, :, (h * GQA + g) * HEAD_DIM:(h * GQA + g + 1) * HEAD_DIM]
                            for g in range(GQA)], axis=0)


def _unstack_heads(o_ref, h, o, rows, i=0):
    for g in range(GQA):
        hd = h * GQA + g
        o_ref[i, :, hd * HEAD_DIM:(hd + 1) * HEAD_DIM] = o[g * rows:(g + 1) * rows]


def _cmp_prompt_kernel(kvc_ref, q_ref, bias_ref, w1_ref, b1_ref, w2_ref, b2_ref, ov_ref, o_ref, sel_ref,
                       kvcmp, *, n_sub, n_slc):
    qb = pl.program_id(1)

    @pl.when(qb == 0)
    def _():
        rows = lambda h, j: kvc_ref[h, pl.ds(j, n_sub, stride=D_CMP), :].astype(BF16)
        load = lambda h, jp: jnp.concatenate([rows(h, 2 * jp), rows(h, 2 * jp + 1)], axis=1)
        for h in range(KV_HEADS):
            kvcmp[h] = _compress(load, h, n_sub, (w1_ref, b1_ref, w2_ref, b2_ref)).astype(BF16)

    qpos = qb * Q_BLOCK + lax.broadcasted_iota(jnp.int32, (1, Q_BLOCK), 1)
    score_t = []
    for h in range(KV_HEADS):
        kv = kvcmp[h]
        s = _dot_nt(_pad_q(_stack_heads(q_ref, h)), kv) + bias_ref[h]
        p = _softmax_rows(s)
        _unstack_heads(o_ref, h, _dot(p.astype(BF16), kv)[:, HEAD_DIM:], Q_BLOCK)
        pg = p[0:Q_BLOCK]
        for g in range(1, GQA):
            pg = pg + p[g * Q_BLOCK:(g + 1) * Q_BLOCK]
        score_t.append(_dot(pg, ov_ref[...], HIGHEST).T)
    for h in range(KV_HEADS):
        sel_t = _select_blocks_on_rows(score_t[h][0:n_slc], qpos, n_slc, (qb + 1) * (Q_BLOCK // L_SEL))
        sel_ref[0, h] = jnp.concatenate([sel_t, jnp.zeros((LANES - n_slc, Q_BLOCK), F32)], axis=0).T


def _cmp_prompt(kvc, q, bias, cmp_w, overlap):
    bsz, t, _ = q.shape
    n_sub = t // D_CMP
    nqb = t // Q_BLOCK
    n_slc = t // L_SEL
    assert n_sub == LANES and n_slc <= LANES
    fixed = lambda a: pl.BlockSpec(a.shape, lambda b, i, _n=a.ndim: (0,) * _n)
    return pl.pallas_call(
        functools.partial(_cmp_prompt_kernel, n_sub=n_sub, n_slc=n_slc),
        grid=(bsz, nqb),
        in_specs=[pl.BlockSpec((KV_HEADS, t, KV_ROWS), lambda b, i: (0, b, 0)),
                  pl.BlockSpec((1, Q_BLOCK, ATT_WIDTH), lambda b, i: (b, i, 0)),
                  pl.BlockSpec((KV_HEADS, GQA * Q_BLOCK, n_sub), lambda b, i: (0, i, 0))]
        + [fixed(a) for a in cmp_w] + [fixed(overlap)],
        out_specs=[pl.BlockSpec((1, Q_BLOCK, ATT_WIDTH), lambda b, i: (b, i, 0)),
                   pl.BlockSpec((1, KV_HEADS, Q_BLOCK, LANES), lambda b, i: (b, 0, i, 0))],
        out_shape=[jax.ShapeDtypeStruct((bsz, t, ATT_WIDTH), F32),
                   jax.ShapeDtypeStruct((bsz, KV_HEADS, t, LANES), F32)],
        scratch_shapes=[pltpu.VMEM((KV_HEADS, n_sub, KV_ROWS), BF16)],
        compiler_params=_cparams(("parallel", "arbitrary")),
        name="cmp_prompt",
    )(kvc, q, bias, *cmp_w, overlap)


def _nsa_prompt_kernel(q_ref, kvs_ref, kvw_ref, sel_ref, eneg_ref, near_ref, far_ref, winb_ref, os_ref, ow_ref,
                       kaug, vsa, kwp, vwa, s_buf, mrun, acc, *, t):
    qb = pl.program_id(1)
    rows = GQA * Q_BLOCK
    near_w = 2 * Q_BLOCK
    win_w = WINDOW + Q_BLOCK

    @pl.when(qb == 0)
    def _():
        ones_row = jnp.where(lax.broadcasted_iota(jnp.int32, (KV_ROWS - HEAD_DIM, t), 0) == 0, 1.0, 0.0).astype(BF16)
        for h in range(KV_HEADS):
            lo = h * KV_ROWS
            kaug[h, :, 0:Q_BLOCK] = jnp.zeros((HEAD_DIM + SEL_PAD, Q_BLOCK), BF16)
            kaug[h, 0:HEAD_DIM, Q_BLOCK:] = kvs_ref[0, lo:lo + HEAD_DIM, :].astype(BF16)
            kaug[h, HEAD_DIM:, Q_BLOCK:] = eneg_ref[...]
            vsa[h, :, 0:Q_BLOCK] = jnp.zeros((KV_ROWS, Q_BLOCK), BF16)
            vsa[h, 0:HEAD_DIM, Q_BLOCK:] = kvs_ref[0, lo + HEAD_DIM:lo + KV_ROWS, :].astype(BF16)
            vsa[h, HEAD_DIM:, Q_BLOCK:] = ones_row
            kwp[h, :, 0:WINDOW] = jnp.zeros((HEAD_DIM, WINDOW), BF16)
            kwp[h, :, WINDOW:] = kvw_ref[0, lo:lo + HEAD_DIM, :].astype(BF16)
            vwa[h, :, 0:WINDOW] = jnp.zeros((KV_ROWS, WINDOW), BF16)
            vwa[h, 0:HEAD_DIM, WINDOW:] = kvw_ref[0, lo + HEAD_DIM:lo + KV_ROWS, :].astype(BF16)
            vwa[h, HEAD_DIM:, WINDOW:] = ones_row

    def normalise(a):
        return a[:, 0:HEAD_DIM] / a[:, HEAD_DIM:HEAD_DIM + 1]

    def tile_max(s):
        m = s[:, 0:LANES]
        for i in range(1, s.shape[1] // LANES):
            m = jnp.maximum(m, s[:, i * LANES:(i + 1) * LANES])
        return m

    start = pl.multiple_of(qb * Q_BLOCK, Q_BLOCK)
    n_chunk = t // SEL_CHUNK
    tiles_per_chunk = SEL_CHUNK // Q_BLOCK
    heads = range(KV_HEADS)
    q4 = [_stack_heads(q_ref, h) for h in heads]
    qa = [jnp.concatenate([q4[h], jnp.concatenate([(1.0 - sel_ref[0, h][:, 0:SEL_PAD]).astype(BF16)] * GQA,
                                                  axis=0)], axis=1) for h in heads]

    for h in heads:
        far = jnp.concatenate([far_ref[h]] * tiles_per_chunk, axis=1)
        mrun[h] = jnp.full((rows, LANES), NEG, F32)
        for c in range(n_chunk):
            @pl.when(c * tiles_per_chunk < qb - 1)
            def _(c=c, h=h, far=far):
                key = c * SEL_CHUNK + lax.broadcasted_iota(jnp.int32, (1, SEL_CHUNK), 1)
                late = jnp.where(key < (qb - 1) * Q_BLOCK, 0.0, NEG)
                lo = Q_BLOCK + c * SEL_CHUNK
                s = _dot(qa[h], kaug[h, :, lo:lo + SEL_CHUNK]) + far + late
                s_buf[h, :, c * SEL_CHUNK:(c + 1) * SEL_CHUNK] = s
                mrun[h] = jnp.maximum(mrun[h], tile_max(s))

    first = jnp.where(lax.broadcasted_iota(jnp.int32, (1, near_w), 1) < Q_BLOCK,
                      jnp.where(qb >= 1, 0.0, NEG), 0.0)
    pad = jnp.where(lax.broadcasted_iota(jnp.int32, (1, win_w), 1) < WINDOW - qb * Q_BLOCK, NEG, 0.0)
    s_near = [_dot(qa[h], kaug[h, :, pl.ds(start, near_w)]) + near_ref[h] + first for h in heads]
    s_w = [_dot(q4[h], kwp[h, :, pl.ds(start, win_w)]) + winb_ref[h] + pad for h in heads]
    m = []
    for h in heads:
        m.append(jnp.max(jnp.maximum(mrun[h], tile_max(s_near[h])), -1, keepdims=True))
        acc[h] = _dot_nt(jnp.exp(s_near[h] - m[h]).astype(BF16), vsa[h, :, pl.ds(start, near_w)])
        p_w = jnp.exp(s_w[h] - jnp.max(tile_max(s_w[h]), -1, keepdims=True))
        _unstack_heads(ow_ref, h, normalise(_dot_nt(p_w.astype(BF16), vwa[h, :, pl.ds(start, win_w)])), Q_BLOCK)

    for h in heads:
        for c in range(n_chunk):
            @pl.when(c * tiles_per_chunk < qb - 1)
            def _(c=c, h=h):
                lo = Q_BLOCK + c * SEL_CHUNK
                p = jnp.exp(s_buf[h, :, c * SEL_CHUNK:(c + 1) * SEL_CHUNK] - m[h])
                acc[h] = acc[h] + _dot_nt(p.astype(BF16), vsa[h, :, lo:lo + SEL_CHUNK])
    for h in heads:
        _unstack_heads(os_ref, h, normalise(acc[h]), Q_BLOCK)


def _nsa_prompt(q, kvs_t, kvw_t, sel, eneg, near, far, win_bias):
    bsz, _, t = kvs_t.shape
    assert t % SEL_CHUNK == 0
    nqb = t // Q_BLOCK
    qblk = lambda b, i: (b, i, 0)
    per_b = lambda b, i: (b, 0, 0)
    fixed = lambda a: pl.BlockSpec(a.shape, lambda b, i, _n=a.ndim: (0,) * _n)
    rows = GQA * Q_BLOCK
    return pl.pallas_call(
        functools.partial(_nsa_prompt_kernel, t=t),
        grid=(bsz, nqb),
        in_specs=[pl.BlockSpec((1, Q_BLOCK, ATT_WIDTH), qblk), pl.BlockSpec((1, KV_COLS, t), per_b),
                  pl.BlockSpec((1, KV_COLS, t), per_b),
                  pl.BlockSpec((1, KV_HEADS, Q_BLOCK, LANES), lambda b, i: (b, 0, i, 0)),
                  fixed(eneg), fixed(near), fixed(far), fixed(win_bias)],
        out_specs=[pl.BlockSpec((1, Q_BLOCK, ATT_WIDTH), qblk), pl.BlockSpec((1, Q_BLOCK, ATT_WIDTH), qblk)],
        out_shape=[jax.ShapeDtypeStruct((bsz, t, ATT_WIDTH), F32)] * 2,
        scratch_shapes=[pltpu.VMEM((KV_HEADS, HEAD_DIM + SEL_PAD, Q_BLOCK + t), BF16),
                        pltpu.VMEM((KV_HEADS, KV_ROWS, Q_BLOCK + t), BF16),
                        pltpu.VMEM((KV_HEADS, HEAD_DIM, WINDOW + t), BF16),
                        pltpu.VMEM((KV_HEADS, KV_ROWS, WINDOW + t), BF16),
                        pltpu.VMEM((KV_HEADS, rows, t), F32), pltpu.VMEM((KV_HEADS, rows, LANES), F32),
                        pltpu.VMEM((KV_HEADS, rows, KV_ROWS), F32)],
        compiler_params=_cparams(("parallel", "arbitrary")),
        name="nsa_prompt",
    )(q, kvs_t, kvw_t, sel, eneg, near, far, win_bias)


def _page_copy(pages_hbm, page, buf, sem, slot, k):
    return pltpu.make_async_copy(pages_hbm.at[page], buf.at[slot, k], sem.at[slot])


def _stream_pages(pt_ref, pages_hbm, buf, sem, n_pages):
    b = pl.program_id(0)
    slot = lax.rem(b, 2)

    def start(row, into):
        for k in range(n_pages):
            _page_copy(pages_hbm, pt_ref[row, k], buf, sem, into, k).start()

    @pl.when(b == 0)
    def _():
        start(0, 0)

    @pl.when(b + 1 < pl.num_programs(0))
    def _():
        start(b + 1, 1 - slot)

    for k in range(n_pages):
        _page_copy(pages_hbm, 0, buf, sem, slot, k).wait()
    return slot


def _cmp_sample_kernel(pt_ref, pages_hbm, q_ref, bias_ref, perm_ref, w1_ref, b1_ref, w2_ref, b2_ref, ov_ref, o_ref,
                       sel_ref, buf, sem, *xj, n_pages, n_slc, past, t):
    slot = _stream_pages(pt_ref, pages_hbm, buf, sem, n_pages)
    sub = PAGE_SIZE // D_CMP
    perm = perm_ref[...]
    per_tile = LANES // (2 * sub)
    n_sub = n_pages * sub
    n_pair = n_pages // 2
    cw = (w1_ref, b1_ref, w2_ref, b2_ref)
    qpos = past + lax.broadcasted_iota(jnp.int32, (t, 1), 0)

    def sort_pair(h, k2):
        lo = h * KV_ROWS
        pair = jnp.concatenate([buf[slot, 2 * k2, lo:lo + KV_ROWS, :].astype(BF16),
                                buf[slot, 2 * k2 + 1, lo:lo + KV_ROWS, :].astype(BF16)], axis=1)
        y = _dot(pair, perm)
        for c in range(2 * PAGE_SIZE // LANES):
            xt = y[:, c * LANES:(c + 1) * LANES].T
            for i in range(per_tile):
                j = c * per_tile + i
                xj[h][j // 2, 2 * sub * k2:2 * sub * (k2 + 1), (j % 2) * KV_ROWS:(j % 2 + 1) * KV_ROWS] = (
                    xt[2 * sub * i:2 * sub * (i + 1)].astype(BF16))

    def hidden_step(hid, h, jp):
        return hid + _dot(xj[h][jp], w1_ref[jp])

    def attend(h, hid):
        kv = _compress_finish(hid, n_sub, cw).astype(BF16)
        qh = _pad_q(_stack_heads(q_ref, h).astype(BF16))
        p = _softmax_rows(_dot_nt(qh, kv) + bias_ref[h])
        _unstack_heads(o_ref, h, _dot(p.astype(BF16), kv)[:, HEAD_DIM:], t)
        pg = p[0:t]
        for g in range(1, GQA):
            pg = pg + p[g * t:(g + 1) * t]
        return _dot(pg, ov_ref[...], HIGHEST)

    n_jp = D_CMP // 2
    zero = jnp.zeros((n_sub, 2 * KV_ROWS), F32)
    for k2 in range(n_pair):
        sort_pair(0, k2)
    hid0, hid1 = zero, zero
    every = n_pair // n_jp
    for k2 in range(n_pair):
        sort_pair(1, k2)
        if k2 % every == every - 1:
            hid0 = hidden_step(hid0, 0, k2 // every)
    for jp in range(n_jp // 2):
        hid1 = hidden_step(hid1, 1, jp)
    score0 = attend(0, hid0)
    for jp in range(n_jp // 2, n_jp):
        hid1 = hidden_step(hid1, 1, jp)
    sel_ref[0, 0] = _select(score0, qpos, n_slc)
    sel_ref[0, 1] = _select(attend(1, hid1), qpos, n_slc)


def _page_scratch(n_pages):
    return [pltpu.VMEM((2, n_pages, KV_COLS, PAGE_SIZE), F32), pltpu.SemaphoreType.DMA((2,))]


def _cmp_sample(pages, page_table, q, bias_cs, cmp_w, overlap, n_slc):
    bsz, n_pages = page_table.shape
    t = q.shape[1]
    width = overlap.shape[1]
    assert n_pages % 2 == 0
    sub = PAGE_SIZE // D_CMP
    col = np.arange(2 * PAGE_SIZE)
    page, row = col // PAGE_SIZE, col % PAGE_SIZE
    dest = (row % D_CMP) * 2 * sub + page * sub + row // D_CMP
    perm = jnp.asarray((dest[:, None] == col[None, :]).astype(np.float32), BF16)
    fixed = lambda a: pl.BlockSpec(a.shape, lambda b, pt, _n=a.ndim: (0,) * _n)
    grid_spec = pltpu.PrefetchScalarGridSpec(
        num_scalar_prefetch=1,
        grid=(bsz,),
        in_specs=[pl.BlockSpec(memory_space=pl.ANY), pl.BlockSpec((1, t, ATT_WIDTH), lambda b, pt: (b, 0, 0)),
                  fixed(bias_cs), fixed(perm)] + [fixed(a) for a in cmp_w] + [fixed(overlap)],
        out_specs=[pl.BlockSpec((1, t, ATT_WIDTH), lambda b, pt: (b, 0, 0)),
                   pl.BlockSpec((1, KV_HEADS, t, width), lambda b, pt: (b, 0, 0, 0))],
        scratch_shapes=_page_scratch(n_pages)
        + [pltpu.VMEM((D_CMP // 2, n_pages * sub, 2 * KV_ROWS), BF16)] * KV_HEADS)
    return pl.pallas_call(
        functools.partial(_cmp_sample_kernel, n_pages=n_pages, n_slc=n_slc, past=n_pages * PAGE_SIZE, t=t),
        grid_spec=grid_spec,
        out_shape=[jax.ShapeDtypeStruct((bsz, t, ATT_WIDTH), F32),
                   jax.ShapeDtypeStruct((bsz, KV_HEADS, t, width), F32)],
        compiler_params=_cparams(("arbitrary",)),
        name="cmp_sample",
    )(page_table, pages, q, bias_cs, perm, *cmp_w, overlap)


def _joint_attend(s_past, vt_past, s_new, v_new):
    m = jnp.maximum(jnp.max(s_past, -1, keepdims=True), jnp.max(s_new, -1, keepdims=True))
    e_past = jnp.exp(s_past - m)
    e_new = jnp.exp(s_new - m)
    den = jnp.sum(e_past, -1, keepdims=True) + jnp.sum(e_new, -1, keepdims=True)
    acc = _dot_nt(e_past.astype(BF16), vt_past) + _dot(e_new.astype(BF16), v_new)
    return acc / den


def _pad_new_rows(new_ref, t, i=0):
    return jnp.concatenate([new_ref[i], jnp.zeros((LANES - t, KV_COLS), F32)], axis=0)


def _sel_sample_kernel(pt_ref, pages_hbm, new_ref, q_ref, sel_ref, far_ref, near_ref, biasn_ref, o_ref, buf, sem,
                       kvb, mask, *, n_pages, t):
    slot = _stream_pages(pt_ref, pages_hbm, buf, sem, n_pages)
    for k in range(n_pages):
        kvb[:, k * PAGE_SIZE:(k + 1) * PAGE_SIZE] = buf[slot, k].astype(BF16)
    kv_new = _pad_new_rows(new_ref, t).astype(BF16)
    rows = GQA * t
    lane = lax.broadcasted_iota(jnp.int32, (rows, LANES), 1)
    per_tile = LANES // L_SEL
    for h in range(KV_HEADS):
        lo = h * KV_ROWS
        sel4 = jnp.concatenate([sel_ref[0, h]] * GQA, axis=0)

        def tile_mask(k):
            cols = [jnp.broadcast_to(sel4[:, per_tile * k + i:per_tile * k + i + 1], (rows, LANES))
                    for i in range(per_tile)]
            m = cols[-1]
            for i in range(per_tile - 2, -1, -1):
                m = jnp.where(lane < (i + 1) * L_SEL, cols[i], m)
            return (m - 1.0) * (-NEG)

        far = jnp.concatenate([jnp.broadcast_to(far_ref[h * GQA + g:h * GQA + g + 1, :], (t, LANES))
                               for g in range(GQA)], axis=0)
        for k in range(n_pages - 1):
            mask[:, k * LANES:(k + 1) * LANES] = tile_mask(k) + far
        mask[:, (n_pages - 1) * LANES:n_pages * LANES] = tile_mask(n_pages - 1) + near_ref[h]
        qh = _stack_heads(q_ref, h).astype(BF16)
        s_past = _dot(qh, kvb[lo:lo + HEAD_DIM, :]) + mask[...]
        s_new = _dot_nt(qh, kv_new[:, lo:lo + HEAD_DIM]) + biasn_ref[h] + tile_mask(n_pages)
        o = _joint_attend(s_past, kvb[lo + HEAD_DIM:lo + KV_ROWS, :], s_new, kv_new[:, lo + HEAD_DIM:lo + KV_ROWS])
        _unstack_heads(o_ref, h, o, t)


def _sel_sample(pages, page_table, kvs_new, q, sel, far, near, bias_new):
    bsz, n_pages = page_table.shape
    t = q.shape[1]
    past = n_pages * PAGE_SIZE
    width = sel.shape[-1]
    assert (n_pages + 1) * (LANES // L_SEL) <= width
    per_b = lambda b, pt: (b, 0, 0)
    fixed = lambda a: pl.BlockSpec(a.shape, lambda b, pt, _n=a.ndim: (0,) * _n)
    grid_spec = pltpu.PrefetchScalarGridSpec(
        num_scalar_prefetch=1,
        grid=(bsz,),
        in_specs=[pl.BlockSpec(memory_space=pl.ANY),
                  pl.BlockSpec((1, t, KV_COLS), per_b), pl.BlockSpec((1, t, ATT_WIDTH), per_b),
                  pl.BlockSpec((1, KV_HEADS, t, width), lambda b, pt: (b, 0, 0, 0)),
                  fixed(far), fixed(near), fixed(bias_new)],
        out_specs=pl.BlockSpec((1, t, ATT_WIDTH), per_b),
        scratch_shapes=_page_scratch(n_pages)
        + [pltpu.VMEM((KV_COLS, past), BF16), pltpu.VMEM((GQA * t, past), F32)])
    return pl.pallas_call(
        functools.partial(_sel_sample_kernel, n_pages=n_pages, t=t),
        grid_spec=grid_spec,
        out_shape=jax.ShapeDtypeStruct((bsz, t, ATT_WIDTH), F32),
        compiler_params=_cparams(("arbitrary",)),
        name="sel_sample",
    )(page_table, pages, kvs_new, q, sel, far, near, bias_new)


def _win_sample_kernel(buf_ref, new_ref, q_ref, bias_ref, biasn_ref, o_ref, win_ref, *, t, nb):
    lane = lax.broadcasted_iota(jnp.int32, (KV_COLS, LANES), 1)
    for i in range(nb):
        buf = buf_ref[i]
        w = buf.shape[1]
        new = _pad_new_rows(new_ref, t, i)
        shifted = pltpu.roll(buf, w - t, 1)
        tail = pltpu.roll(new.T, LANES - t, 1)
        win_ref[i, :, 0:w - LANES] = shifted[:, 0:w - LANES]
        win_ref[i, :, w - LANES:w] = jnp.where(lane >= LANES - t, tail, shifted[:, w - LANES:w])
        kvb = buf.astype(BF16)
        kv_new = new.astype(BF16)
        for h in range(KV_HEADS):
            lo = h * KV_ROWS
            qh = _stack_heads(q_ref, h, i).astype(BF16)
            s_past = _dot(qh, kvb[lo:lo + HEAD_DIM, :]) + bias_ref[h]
            s_new = _dot_nt(qh, kv_new[:, lo:lo + HEAD_DIM]) + biasn_ref[h]
            o = _joint_attend(s_past, kvb[lo + HEAD_DIM:lo + KV_ROWS, :], s_new,
                              kv_new[:, lo + HEAD_DIM:lo + KV_ROWS])
            _unstack_heads(o_ref, h, o, t, i)


def _win_sample(buf_t, kvw_new, q, bias_buf, bias_new):
    bsz, _, w = buf_t.shape
    t = q.shape[1]
    nb = ROWS_PER_STEP if bsz % ROWS_PER_STEP == 0 else 1
    per_b = lambda b: (b, 0, 0)
    fixed = lambda a: pl.BlockSpec(a.shape, lambda b, _n=a.ndim: (0,) * _n)
    return pl.pallas_call(
        functools.partial(_win_sample_kernel, t=t, nb=nb),
        grid=(bsz // nb,),
        in_specs=[pl.BlockSpec((nb, KV_COLS, w), per_b), pl.BlockSpec((nb, t, KV_COLS), per_b),
                  pl.BlockSpec((nb, t, ATT_WIDTH), per_b), fixed(bias_buf), fixed(bias_new)],
        out_specs=[pl.BlockSpec((nb, t, ATT_WIDTH), per_b), pl.BlockSpec((nb, KV_COLS, w), per_b)],
        out_shape=[jax.ShapeDtypeStruct((bsz, t, ATT_WIDTH), F32),
                   jax.ShapeDtypeStruct((bsz, KV_COLS, w), F32)],
        compiler_params=_cparams(("parallel",)),
        name="win_sample",
    )(buf_t, kvw_new, q, bias_buf, bias_new)


def _combine_kernel(x_ref, y_ref, oc_ref, os_ref, ow_ref, sm_ref, eg_ref, eb_ref, ex_ref, ag_ref, wo_ref, g1_ref,
                    b1_ref, h_ref):
    xn = _layer_norm(x_ref[...], eg_ref[...], eb_ref[...])
    gates = _sigmoid(sm_ref[...])
    g_hi = gates.astype(BF16)
    g_lo = (gates - g_hi.astype(F32)).astype(BF16)
    o = jnp.zeros(oc_ref.shape, F32)
    for br, ref in enumerate((oc_ref, os_ref, ow_ref)):
        o = o + (_dot(g_hi, ex_ref[br]) + _dot(g_lo, ex_ref[br])) * ref[...]
    rms = lax.rsqrt(jnp.mean(o * o, -1, keepdims=True) + EPS)
    att = (o * rms * ag_ref[...]).astype(BF16)
    mix = _dot(jnp.concatenate([y_ref[...], att], axis=1), wo_ref[...])
    h_ref[...] = _layer_norm(ALPHA * xn + mix, g1_ref[...], b1_ref[...])


def _combine(x2d, y, oc, os_, ow, sm, eg, eb, gate_expand, ag, wo, g1, b1, tm):
    n = x2d.shape[0]
    tm = _row_tile(n, tm)
    row = lambda i: (i, 0)
    fixed = lambda a: pl.BlockSpec(a.shape, lambda i, _n=a.ndim: (0,) * _n)
    att = pl.BlockSpec((tm, ATT_WIDTH), row)
    return pl.pallas_call(
        _combine_kernel,
        grid=(n // tm,),
        in_specs=[pl.BlockSpec((tm, D_MODEL), row), pl.BlockSpec((tm, SSM_WIDTH), row), att, att, att,
                  pl.BlockSpec((tm, LANES), row), fixed(eg), fixed(eb), fixed(gate_expand), fixed(ag), fixed(wo),
                  fixed(g1), fixed(b1)],
        out_specs=pl.BlockSpec((tm, D_MODEL), row),
        out_shape=jax.ShapeDtypeStruct((n, D_MODEL), F32),
        compiler_params=_cparams(("parallel",)),
        name="combine",
    )(x2d, y, oc, os_, ow, sm, eg, eb, gate_expand, ag, wo, g1, b1)


def _ffn_kernel(h_ref, wu_ref, wd_ref, g_ref, b_ref, o_ref, *, tf):
    h = h_ref[...]
    hb = h.astype(BF16)
    acc = None
    for k in range(D_FF // tf):
        u = jnp.maximum(_dot(hb, wu_ref[:, k * tf:(k + 1) * tf]), 0.0)
        part = _dot((u * u).astype(BF16), wd_ref[k * tf:(k + 1) * tf, :])
        acc = part if acc is None else acc + part
    o_ref[...] = _layer_norm(ALPHA * h + acc, g_ref[...], b_ref[...])


def _ffn(h2d, wu, wd, g, b, tm, tf):
    n = h2d.shape[0]
    tm = _row_tile(n, tm)
    resident = lambda a: pl.BlockSpec(a.shape, lambda i: (0, 0), pipeline_mode=pl.Buffered(1))
    return pl.pallas_call(
        functools.partial(_ffn_kernel, tf=tf),
        grid=(n // tm,),
        in_specs=[pl.BlockSpec((tm, D_MODEL), lambda i: (i, 0)), resident(wu), resident(wd), resident(g), resident(b)],
        out_specs=pl.BlockSpec((tm, D_MODEL), lambda i: (i, 0)),
        out_shape=jax.ShapeDtypeStruct((n, D_MODEL), F32),
        compiler_params=_cparams(("parallel",)),
        name="ffn",
    )(h2d, wu, wd, g, b)


def _bucket_np(dist):
    d = np.maximum(dist, 0)
    exact = N_BUCKETS // 2
    far = exact + (np.log(np.maximum(d, 1).astype(np.float32) / np.float32(exact))
                   / np.float32(math.log(MAX_DISTANCE / exact)) * (N_BUCKETS - exact)).astype(np.int32)
    return np.where(d < exact, d, np.minimum(far, N_BUCKETS - 1)).astype(np.int32)


def _bias_lookup(tbl, dist, mask=None):
    dist = np.asarray(dist)
    onehot = np.eye(N_BUCKETS, dtype=np.float32)[_bucket_np(dist).reshape(-1)]
    b = jnp.dot(jnp.asarray(onehot), tbl, precision=HIGHEST).T.reshape((ATT_HEADS,) + dist.shape)
    return b if mask is None else jnp.where(jnp.asarray(mask)[None], b, NEG)


def _toeplitz_tile(tbl, offset, mask):
    period = 2 * Q_BLOCK
    k = np.arange(period)
    vals = _bias_lookup(tbl, offset - np.where(k < Q_BLOCK, k, k - period))
    tiled = jnp.tile(vals, (1, Q_BLOCK))[:, :Q_BLOCK * (period - 1)]
    t = tiled.reshape(ATT_HEADS, Q_BLOCK, period - 1)[:, :, :Q_BLOCK]
    return jnp.where(jnp.asarray(mask)[None], t, NEG)


def _cmp_prompt_bias(tbl, t):
    n_sub = t // D_CMP
    back = (MAX_DISTANCE + L_CMP - 1) // D_CMP
    band = (np.arange(Q_BLOCK)[:, None] - D_CMP * np.arange(NEAR_W)[None, :] + D_CMP * back - (L_CMP - 1))
    near = jnp.pad(_bias_lookup(tbl, band), ((0, 0), (0, 0), (0, n_sub - NEAR_W)))
    far = tbl[N_BUCKETS - 1][:, None, None]
    blk = np.arange(n_sub)[None, :]
    tiles = []
    for qb in range(t // Q_BLOCK):
        near_lo = qb * (Q_BLOCK // D_CMP) - back
        qpos = qb * Q_BLOCK + np.arange(Q_BLOCK)[:, None]
        visible = (blk * D_CMP + (L_CMP - 1) <= qpos) & (blk < n_sub - 1)
        tile = jnp.where(jnp.asarray(blk >= near_lo)[None], jnp.roll(near, near_lo % n_sub, axis=2), far)
        tiles.append(jnp.where(jnp.asarray(visible)[None], tile, NEG))
    tiles = jnp.stack(tiles).reshape(t // Q_BLOCK, KV_HEADS, GQA * Q_BLOCK, n_sub)
    return jnp.moveaxis(tiles, 0, 1).reshape(KV_HEADS, -1, n_sub)


def _stack_gt(tab, t):
    return tab.reshape(KV_HEADS, GQA * t, tab.shape[-1])


def _far_rows(tbl):
    return jnp.broadcast_to(tbl[N_BUCKETS - 1][:, None], (ATT_HEADS, LANES))


def _overlap(n_cmp_pad, n_cmp, width, n_slc):
    i = np.arange(n_cmp_pad)[:, None]
    j = np.arange(width)[None, :]
    ov = (i * D_CMP < (j + 1) * L_SEL) & (i * D_CMP + L_CMP > j * L_SEL) & (i < n_cmp) & (j < n_slc)
    return jnp.asarray(ov.astype(np.float32))


def _prep_cmp_weights(w1, b1, w2, b2):
    eye = jnp.eye(2, dtype=F32)
    w1r = (w1[:, :, :, :, None, :] * eye[None, None, :, None, :, None]).transpose(1, 2, 3, 0, 4, 5)
    w1r = w1r.reshape(D_CMP // 2, 2 * KV_ROWS, 2 * KV_ROWS).astype(BF16)
    w2r = (w2[:, :, None, :] * eye[:, None, :, None]).reshape(2 * CMP_HID, KV_ROWS).astype(BF16)
    return (w1r, b1.reshape(1, 2 * CMP_HID), w2r, b2.reshape(1, KV_ROWS))


def _prep_w_in(w_in):
    sizes = (SSM_WIDTH, CONV_DIM, SSM_HEADS, ATT_WIDTH, KV_COLS, KV_COLS, KV_COLS)
    z, xbc, dt, q, kvc, kvs, kvw, gates = jnp.split(w_in, np.cumsum(sizes).tolist(), axis=1)
    small = jnp.concatenate([dt, gates], axis=1)
    small = jnp.pad(small, ((0, 0), (0, LANES - small.shape[1])))
    return jnp.concatenate([z, xbc, q * ATT_SCALE, kvc, kvs, kvw, small], axis=1).astype(BF16)


def _gate_expand():
    ex = np.zeros((N_BRANCH, LANES, ATT_WIDTH), np.float32)
    for br in range(N_BRANCH):
        for hd in range(ATT_HEADS):
            ex[br, SSM_HEADS + br * ATT_HEADS + hd, hd * HEAD_DIM:(hd + 1) * HEAD_DIM] = 1.0
    return jnp.asarray(ex, BF16)


def _feature_major(a):
    lead = a.shape[:-4]
    rows = a.shape[-4]
    return jnp.moveaxis(a.reshape(lead + (rows, KV_COLS)), -2, -1)


def _row_major6(a_t):
    bsz, _, rows = a_t.shape
    return jnp.moveaxis(a_t, 1, 2).reshape(1, bsz, rows, KV_HEADS, 2, HEAD_DIM)


def kernel(x_prompt, x_sample, cache_cmp_kv, cache_slc_kv, cache_win_kv, state_conv, state_ssm, page_table,
           rel_bias_table, emb_ln_g, emb_ln_b, w_in, conv_w, conv_b, dt_bias, a_log, d_skip, ssm_norm_g,
           cmp_w1, cmp_b1, cmp_w2, cmp_b2, att_norm_g, w_out, ln1_g, ln1_b, w_up, w_down, ln2_g, ln2_b):
    assert w_in.shape[0] == DEPTH
    bp, tp, _ = x_prompt.shape
    bs, ts, _ = x_sample.shape
    n_pages = page_table.shape[1]
    past = n_pages * PAGE_SIZE
    w_buf = cache_win_kv.shape[2]
    assert ts < D_CMP and ts % 8 == 0 and w_buf == WINDOW and past >= WINDOW and tp >= WINDOW
    tbl = rel_bias_table
    vec = lambda v: v.reshape(1, -1)

    w_proj = _prep_w_in(w_in[0])
    cmp_w = _prep_cmp_weights(cmp_w1[0], cmp_b1[0], cmp_w2[0], cmp_b2[0])
    wo = w_out[0].astype(BF16)
    wu = w_up[0].astype(BF16)
    wd = w_down[0].astype(BF16)
    eg, eb = vec(emb_ln_g), vec(emb_ln_b)
    gate_expand = _gate_expand()
    far = _far_rows(tbl)

    def trunk_tail(x2d, y, oc, os_, ow, sm):
        h = _combine(x2d, y, oc, os_, ow, sm, eg, eb, gate_expand, vec(att_norm_g[0]), wo, vec(ln1_g[0]),
                     vec(ln1_b[0]), COMBINE_ROWS)
        return _ffn(h, wu, wd, vec(ln2_g[0]), vec(ln2_b[0]), FFN_ROWS, FFN_SLAB)

    ssm_w = (conv_w[0], conv_b[0], dt_bias[0], a_log[0], d_skip[0], ssm_norm_g[0])

    xp2 = x_prompt.reshape(bp * tp, D_MODEL)
    z, xbc, q, kvc, kvc_t, kvs_t, kvw_t, sm = _proj(xp2, eg, eb, w_proj, BF16, PROJ_ROWS, seq=(bp, tp))
    r3 = lambda a: a.reshape(bp, tp, a.shape[-1])
    xbc3 = r3(xbc)
    y_ssm, h_new = _ssm(r3(z), xbc3, r3(sm), jnp.zeros((bp, CONV_WIDTH - 1, CONV_DIM), F32),
                        jnp.zeros((bp, SSM_HEADS, SSM_HEAD_DIM, D_STATE), F32), *ssm_w)
    n_sub = tp // D_CMP
    n_slc = tp // L_SEL
    oc, sel = _cmp_prompt(kvc, r3(q), _cmp_prompt_bias(tbl, tp), cmp_w, _overlap(n_sub, n_sub - 1, LANES, n_slc))
    ii = np.arange(Q_BLOCK)[:, None] - np.arange(Q_BLOCK)[None, :]
    tiles_gq = lambda rs, ok: jnp.concatenate(
        [_toeplitz_tile(tbl, Q_BLOCK * r, ok(ii + Q_BLOCK * r)) for r in rs], axis=2).reshape(
            KV_HEADS, GQA * Q_BLOCK, len(rs) * Q_BLOCK)
    near = tiles_gq((1, 0), lambda d: d >= 0)
    win_bias = tiles_gq(range(WINDOW // Q_BLOCK, -1, -1), lambda d: (d >= 0) & (d < WINDOW))
    far_gq = jnp.repeat(far, Q_BLOCK, axis=0).reshape(KV_HEADS, GQA * Q_BLOCK, LANES)
    assert n_slc <= SEL_PAD
    eneg = jnp.asarray(np.where(np.arange(SEL_PAD)[:, None] == (np.arange(tp) // L_SEL)[None, :], NEG, 0.0), BF16)
    os_, ow = _nsa_prompt(r3(q), kvs_t, kvw_t, sel, eneg, near, far_gq, win_bias)
    f2 = lambda a: a.reshape(bp * tp, a.shape[-1])
    y_prompt = trunk_tail(xp2, f2(y_ssm), f2(oc), f2(os_), f2(ow), sm).reshape(bp, tp, D_MODEL)
    w = min(WINDOW, tp)
    prompt_state = (_row_major6(kvc_t), _row_major6(kvs_t), _row_major6(kvw_t[:, :, tp - w:]),
                    xbc3[:, tp - (CONV_WIDTH - 1):][None], h_new[None])

    xs2 = x_sample.reshape(bs * ts, D_MODEL)
    z, xbc, q, kvc, kvs, kvw, sm = _proj(xs2, eg, eb, w_proj, F32, PROJ_ROWS)
    r3 = lambda a: a.reshape(bs, ts, a.shape[-1])
    xbc3 = r3(xbc)
    y_ssm, h_new = _ssm(r3(z), xbc3, r3(sm), state_conv[0], state_ssm[0], *ssm_w)
    n_sub = past // D_CMP
    n_cmp = n_sub - 1
    n_slc = -(-(past + ts) // L_SEL)
    width = -(-n_slc // LANES) * LANES
    qpos = past + np.arange(ts)
    dist_c = qpos[:, None] - (np.arange(n_sub) * D_CMP + L_CMP - 1)[None, :]
    bias_cs = _stack_gt(_bias_lookup(tbl, dist_c, (dist_c >= 0) & (np.arange(n_sub) < n_cmp)[None, :]), ts)
    oc, sel = _cmp_sample(_feature_major(cache_cmp_kv[0]), page_table, r3(q), bias_cs, cmp_w,
                          _overlap(n_sub, n_cmp, width, n_slc), n_slc)
    dist_last = qpos[:, None] - (past - PAGE_SIZE + np.arange(PAGE_SIZE))[None, :]
    dist_n = np.arange(ts)[:, None] - np.arange(LANES)[None, :]
    bias_new = _stack_gt(_bias_lookup(tbl, dist_n, (dist_n >= 0) & (np.arange(LANES) < ts)[None, :]), ts)
    os_ = _sel_sample(_feature_major(cache_slc_kv[0]), page_table, r3(kvs), r3(q), sel, far,
                      _stack_gt(_bias_lookup(tbl, dist_last), ts), bias_new)
    dist_w = qpos[:, None] - (past - w_buf + np.arange(w_buf))[None, :]
    bias_wb = _stack_gt(_bias_lookup(tbl, dist_w, (dist_w >= 0) & (dist_w < WINDOW)), ts)
    ow, win_new_t = _win_sample(_feature_major(cache_win_kv[0]), r3(kvw), r3(q), bias_wb, bias_new)
    f2 = lambda a: a.reshape(bs * ts, a.shape[-1])
    y_sample = trunk_tail(xs2, f2(y_ssm), f2(oc), f2(os_), f2(ow), sm).reshape(bs, ts, D_MODEL)
    kv6 = lambda a: a.reshape(1, bs, ts, KV_HEADS, 2, HEAD_DIM)
    sample_state = (kv6(kvc), kv6(kvs), _row_major6(win_new_t),
                    xbc3[:, ts - (CONV_WIDTH - 1):][None], h_new[None])

    return (y_prompt, y_sample) + prompt_state + sample_state
```

```python
import functools
import math

import numpy as np
import jax
import jax.numpy as jnp
from jax import lax
from jax.experimental import pallas as pl
from jax.experimental.pallas import tpu as pltpu

F32 = jnp.float32
BF16 = jnp.bfloat16
HIGHEST = lax.Precision.HIGHEST

D_MODEL = 1024
SSM_HEADS = 8
SSM_HEAD_DIM = 64
SSM_WIDTH = SSM_HEADS * SSM_HEAD_DIM
SSM_GROUPS = 2
D_STATE = 128
CONV_WIDTH = 4
CONV_DIM = SSM_WIDTH + 2 * SSM_GROUPS * D_STATE
SSD_CHUNK = 128
ATT_HEADS = 8
KV_HEADS = 2
GQA = ATT_HEADS // KV_HEADS
HEAD_DIM = 64
ATT_WIDTH = ATT_HEADS * HEAD_DIM
KV_COLS = KV_HEADS * 2 * HEAD_DIM
D_CMP = 16
L_CMP = 2 * D_CMP
CMP_HID = 64
L_SEL = 64
TOP_N = 16
WINDOW = 512
Q_BLOCK = 128
N_BRANCH = 3
FORCED_SCORE = 1e4
N_BUCKETS = 32
MAX_DISTANCE = 128
D_FF = 4 * D_MODEL
DEPTH = 1
ALPHA = (2 * DEPTH) ** 0.25
ATT_SCALE = HEAD_DIM ** -0.5
EPS = 1e-5
PAGE_SIZE = 128

LANES = 128
NEG = -1e30
MASKED_BELOW = -1e29
VMEM_LIMIT = 48 * 1024 * 1024
KV_ROWS = 2 * HEAD_DIM
SEL_PAD = 32
NEAR_W = 32
SEL_CHUNK = 512
PROJ_ROWS = 512
COMBINE_ROWS = 512
FFN_ROWS = 512
FFN_SLAB = D_FF // 4
ROWS_PER_STEP = 4

_OFF_Z = 0
_OFF_XBC = _OFF_Z + SSM_WIDTH
_OFF_Q = _OFF_XBC + CONV_DIM
_OFF_KVC = _OFF_Q + ATT_WIDTH
_OFF_KVS = _OFF_KVC + KV_COLS
_OFF_KVW = _OFF_KVS + KV_COLS
_OFF_SM = _OFF_KVW + KV_COLS
_N_PROJ = _OFF_SM + LANES


def _cparams(sem):
    return pltpu.CompilerParams(dimension_semantics=sem, vmem_limit_bytes=VMEM_LIMIT)


def _row_tile(n, preferred):
    tm = min(n, preferred)
    assert n % tm == 0 and tm % 8 == 0
    return tm


def _dot(a, b, precision=None):
    return jnp.dot(a, b, preferred_element_type=F32, precision=precision)


def _dot_nt(a, b):
    return lax.dot_general(a, b, (((1,), (1,)), ((), ())), preferred_element_type=F32)


def _layer_norm(x, g, b):
    mu = jnp.mean(x, -1, keepdims=True)
    xc = x - mu
    var = jnp.mean(xc * xc, -1, keepdims=True)
    return xc * lax.rsqrt(var + EPS) * g + b


def _sigmoid(x):
    return 1.0 / (1.0 + jnp.exp(-x))


def _softplus(x):
    return jnp.maximum(x, 0.0) + jnp.log(1.0 + jnp.exp(-jnp.abs(x)))


def _gelu_tanh(x):
    c = math.sqrt(2.0 / math.pi)
    return 0.5 * x * (1.0 + jnp.tanh(c * (x + 0.044715 * (x * x * x))))


def _proj_kernel(x_ref, g_ref, b_ref, w_ref, z_ref, xbc_ref, q_ref, kvc_ref, *rest, feature_major):
    xn = _layer_norm(x_ref[...], g_ref[...], b_ref[...]).astype(BF16)

    def mm(lo, hi):
        return _dot(xn, w_ref[:, lo:hi])

    z_ref[...] = mm(_OFF_Z, _OFF_XBC)
    xbc_ref[...] = mm(_OFF_XBC, _OFF_Q)
    q_ref[...] = mm(_OFF_Q, _OFF_KVC).astype(q_ref.dtype)
    kvc = mm(_OFF_KVC, _OFF_KVS)
    if feature_major:
        kvct_ref, kvst_ref, kvwt_ref, sm_ref = rest
        for h in range(KV_HEADS):
            kvc_ref[h] = kvc[:, h * KV_ROWS:(h + 1) * KV_ROWS]
        kvct_ref[0] = kvc.T
        kvst_ref[0] = mm(_OFF_KVS, _OFF_KVW).T
        kvwt_ref[0] = mm(_OFF_KVW, _OFF_SM).T
    else:
        kvs_ref, kvw_ref, sm_ref = rest
        kvc_ref[...] = kvc
        kvs_ref[...] = mm(_OFF_KVS, _OFF_KVW)
        kvw_ref[...] = mm(_OFF_KVW, _OFF_SM)
    sm_ref[...] = mm(_OFF_SM, _N_PROJ)


def _proj(x2d, g, b, w, q_dtype, tm, seq=None):
    n = x2d.shape[0]
    tm = _row_tile(n, tm)
    row = lambda i: (i, 0)
    fixed = lambda i: (0, 0)
    rm = lambda wd, dt: (pl.BlockSpec((tm, wd), row), jax.ShapeDtypeStruct((n, wd), dt))
    outs = [rm(SSM_WIDTH, F32), rm(CONV_DIM, F32), rm(ATT_WIDTH, q_dtype)]
    if seq is None:
        outs += [rm(KV_COLS, F32), rm(KV_COLS, F32), rm(KV_COLS, F32)]
    else:
        outs.append((pl.BlockSpec((KV_HEADS, tm, KV_ROWS), lambda i: (0, i, 0)),
                     jax.ShapeDtypeStruct((KV_HEADS, n, KV_ROWS), F32)))
        bsz, t = seq
        assert t % tm == 0 and tm % LANES == 0
        per = t // tm
        fm = (pl.BlockSpec((1, KV_COLS, tm), lambda i: (i // per, 0, i % per)),
              jax.ShapeDtypeStruct((bsz, KV_COLS, t), F32))
        outs += [fm, fm, fm]
    outs.append(rm(LANES, F32))
    return pl.pallas_call(
        functools.partial(_proj_kernel, feature_major=seq is not None),
        grid=(n // tm,),
        in_specs=[pl.BlockSpec((tm, D_MODEL), row), pl.BlockSpec((1, D_MODEL), fixed),
                  pl.BlockSpec((1, D_MODEL), fixed), pl.BlockSpec((D_MODEL, _N_PROJ), fixed)],
        out_specs=[o[0] for o in outs],
        out_shape=[o[1] for o in outs],
        compiler_params=_cparams(("parallel",)),
        name="proj",
    )(x2d, g, b, w)


def _ssm_kernel(z_ref, xbc_ref, sm_ref, hist_ref, h0_ref, cw_ref, cb_ref, dtb_ref, alog_ref, dskip_ref,
                ng_ref, y_ref, hfin_ref, xext, state, *, tb, l, nc, nb):
    for i in range(nb):
        _ssm_row(i, z_ref, xbc_ref, sm_ref, hist_ref, h0_ref, cw_ref, cb_ref, dtb_ref, alog_ref, dskip_ref,
                 ng_ref, y_ref, hfin_ref, xext.at[i], state.at[i], tb=tb, l=l, nc=nc)


def _ssm_row(i, z_ref, xbc_ref, sm_ref, hist_ref, h0_ref, cw_ref, cb_ref, dtb_ref, alog_ref, dskip_ref,
             ng_ref, y_ref, hfin_ref, xext, state, *, tb, l, nc):
    c = pl.program_id(1)

    @pl.when(c == 0)
    def _():
        xext[0:8, :] = jnp.zeros((8, CONV_DIM), F32)
        xext[8 - (CONV_WIDTH - 1):8, :] = hist_ref[i]
        if tb < l:
            xext[8 + tb:8 + l, :] = jnp.zeros((l - tb, CONV_DIM), F32)
        state[...] = h0_ref[i].reshape(SSM_WIDTH, D_STATE)

    xext[8:8 + tb, :] = xbc_ref[i]
    conv = cb_ref[...]
    for k in range(CONV_WIDTH):
        lo = 8 - (CONV_WIDTH - 1) + k
        conv = conv + cw_ref[k:k + 1, :] * xext[lo:lo + l, :]
    xc = conv * _sigmoid(conv)
    xext[0:8, :] = xext[tb:tb + 8, :]

    dt = _softplus(sm_ref[i] + dtb_ref[...])
    if tb < l:
        dt = jnp.concatenate([dt, jnp.zeros((l - tb, LANES), F32)], axis=0)
    a = dt * (-jnp.exp(alog_ref[...]))
    ri = lax.broadcasted_iota(jnp.int32, (l, l), 0)
    ci = lax.broadcasted_iota(jnp.int32, (l, l), 1)
    tril = ri >= ci
    a_cs = _dot(jnp.where(tril, 1.0, 0.0), a, HIGHEST)
    a_cs_t = a_cs.T
    dt_t = dt.T
    ea = jnp.exp(a_cs[:tb])
    wend = dt * jnp.exp(a_cs[l - 1:l, :] - a_cs)
    etot = jnp.exp(a_cs[l - 1:l, :])

    xs = xc[:, :SSM_WIDTH]
    lane = lax.broadcasted_iota(jnp.int32, (tb, LANES), 1)
    srow = lax.broadcasted_iota(jnp.int32, (LANES, D_STATE), 0)
    tril_q = tril[:tb]
    y_pairs = []
    for g in range(SSM_GROUPS):
        bg_f = xc[:, SSM_WIDTH + g * D_STATE:SSM_WIDTH + (g + 1) * D_STATE]
        bg = bg_f.astype(BF16)
        c_lo = SSM_WIDTH + SSM_GROUPS * D_STATE + g * D_STATE
        cg = xc[:tb, c_lo:c_lo + D_STATE].astype(BF16)
        cb = _dot_nt(cg, bg)
        for k in range(2):
            pair = 2 * g + k
            lo = pair * LANES
            h0, h1 = 2 * pair, 2 * pair + 1
            xs_pair = xs[:, lo:lo + LANES]
            xs_b = xs_pair.astype(BF16)
            xs_t = xs_pair.T
            ys, upd = [], []
            for r2, h in enumerate((h0, h1)):
                seg = a_cs[:tb, h:h + 1] - a_cs_t[h:h + 1, :]
                lm = jnp.where(tril_q, jnp.exp(jnp.where(tril_q, seg, 0.0)), 0.0) * dt_t[h:h + 1, :]
                ys.append(_dot((cb * lm).astype(BF16), xs_b))
                upd.append(_dot(xs_t[r2 * SSM_HEAD_DIM:(r2 + 1) * SSM_HEAD_DIM].astype(BF16),
                                (bg_f * wend[:, h:h + 1]).astype(BF16)))
            y_diag = jnp.where(lane < SSM_HEAD_DIM, ys[0], ys[1])
            sp = state[lo:lo + LANES, :]
            y_off = _dot_nt(cg, sp.astype(BF16)) * jnp.where(lane < SSM_HEAD_DIM, ea[:, h0:h0 + 1], ea[:, h1:h1 + 1])
            y_pairs.append(y_diag + y_off)
            keep = jnp.where(srow < SSM_HEAD_DIM, etot[:, h0:h0 + 1], etot[:, h1:h1 + 1])
            state[lo:lo + LANES, :] = sp * keep + jnp.concatenate(upd, axis=0)
    y = jnp.concatenate(y_pairs, axis=1) + dskip_ref[...] * xs[:tb]
    zz = z_ref[i]
    y = y * (zz * _sigmoid(zz))
    gw = SSM_WIDTH // SSM_GROUPS
    outs = []
    for g in range(SSM_GROUPS):
        yg = y[:, g * gw:(g + 1) * gw]
        ms = jnp.mean(yg * yg, -1, keepdims=True)
        outs.append(yg * lax.rsqrt(ms + EPS) * ng_ref[:, g * gw:(g + 1) * gw])
    y_ref[i] = jnp.concatenate(outs, axis=1).astype(y_ref.dtype)

    @pl.when(c == nc - 1)
    def _():
        hfin_ref[i] = state[...].reshape(SSM_HEADS, SSM_HEAD_DIM, D_STATE)


def _ssm(z, xbc, sm, hist, h0, conv_w, conv_b, dt_bias, a_log, d_skip, norm_g):
    bsz, t, _ = z.shape
    l = SSD_CHUNK
    tb = min(l, t)
    assert t % tb == 0 and tb % 8 == 0 and t >= CONV_WIDTH - 1
    nc = t // tb
    nb = ROWS_PER_STEP if (tb < l and bsz % ROWS_PER_STEP == 0) else 1
    pad8 = lambda v: jnp.pad(v.reshape(1, SSM_HEADS), ((0, 0), (0, LANES - SSM_HEADS)))
    blk = lambda b, c: (b, c, 0)
    per_b3 = lambda b, c: (b, 0, 0)
    per_b4 = lambda b, c: (b, 0, 0, 0)
    fixed = lambda b, c: (0, 0)
    return pl.pallas_call(
        functools.partial(_ssm_kernel, tb=tb, l=l, nc=nc, nb=nb),
        grid=(bsz // nb, nc),
        in_specs=[pl.BlockSpec((nb, tb, SSM_WIDTH), blk), pl.BlockSpec((nb, tb, CONV_DIM), blk),
                  pl.BlockSpec((nb, tb, LANES), blk),
                  pl.BlockSpec((nb, CONV_WIDTH - 1, CONV_DIM), per_b3),
                  pl.BlockSpec((nb, SSM_HEADS, SSM_HEAD_DIM, D_STATE), per_b4),
                  pl.BlockSpec((CONV_WIDTH, CONV_DIM), fixed), pl.BlockSpec((1, CONV_DIM), fixed),
                  pl.BlockSpec((1, LANES), fixed), pl.BlockSpec((1, LANES), fixed),
                  pl.BlockSpec((1, SSM_WIDTH), fixed), pl.BlockSpec((1, SSM_WIDTH), fixed)],
        out_specs=[pl.BlockSpec((nb, tb, SSM_WIDTH), blk),
                   pl.BlockSpec((nb, SSM_HEADS, SSM_HEAD_DIM, D_STATE), per_b4)],
        out_shape=[jax.ShapeDtypeStruct((bsz, t, SSM_WIDTH), BF16),
                   jax.ShapeDtypeStruct((bsz, SSM_HEADS, SSM_HEAD_DIM, D_STATE), F32)],
        scratch_shapes=[pltpu.VMEM((nb, 8 + l, CONV_DIM), F32), pltpu.VMEM((nb, SSM_WIDTH, D_STATE), F32)],
        compiler_params=_cparams(("parallel", "arbitrary")),
        name="ssm",
    )(z, xbc, sm, hist, h0, conv_w, conv_b.reshape(1, CONV_DIM), pad8(dt_bias), pad8(a_log),
      jnp.repeat(d_skip, SSM_HEAD_DIM).reshape(1, SSM_WIDTH), norm_g.reshape(1, SSM_WIDTH))


def _compress(load_pair, h, n_sub, cw):
    hid = jnp.zeros((n_sub, 2 * KV_ROWS), F32)
    for jp in range(D_CMP // 2):
        hid = hid + _dot(load_pair(h, jp), cw[0][jp])
    return _compress_finish(hid, n_sub, cw)


def _compress_finish(hid, n_sub, cw):
    _, b1_ref, w2_ref, b2_ref = cw
    pre = hid[:, :KV_ROWS] + pltpu.roll(hid[:, KV_ROWS:], n_sub - 1, 0) + b1_ref[...]
    return _dot(_gelu_tanh(pre).astype(BF16), w2_ref[...]) + b2_ref[...]


def _pad_q(q):
    return jnp.concatenate([q, jnp.zeros(q.shape, q.dtype)], axis=1)


def _select(score, qpos, n_slc):
    jj = lax.broadcasted_iota(jnp.int32, score.shape, 1)
    cur = qpos // L_SEL
    visible = jj * L_SEL <= qpos
    forced = (jj == 0) | (jj == cur) | (jj == cur - 1)
    sc = jnp.where(visible, jnp.where(forced, FORCED_SCORE, score), -1.0)
    sc = jnp.where(jj < n_slc, sc, -2.0)
    lane = lax.broadcasted_iota(jnp.int32, (1, score.shape[1]), 1)
    rank = jnp.zeros(score.shape, F32)
    for k in range(n_slc):
        ck = sc[:, k:k + 1]
        tie = jnp.where(lane > k, 1.0, 0.0)
        rank = rank + jnp.where(ck > sc, 1.0, jnp.where(ck == sc, tie, 0.0))
    return jnp.where(rank < min(TOP_N, n_slc), jnp.where(sc >= 0.0, 1.0, 0.0), 0.0)


def _select_blocks_on_rows(score_t, qpos, n_slc, n_visible):
    jj = lax.broadcasted_iota(jnp.int32, score_t.shape, 0)
    cur = qpos // L_SEL
    visible = jj * L_SEL <= qpos
    forced = (jj == 0) | (jj == cur) | (jj == cur - 1)
    sc = jnp.where(visible, jnp.where(forced, FORCED_SCORE, score_t), -1.0)

    def count(rank, ks):
        for k in ks:
            tie = jnp.where(jj > k, 1.0, 0.0)
            rank = rank + jnp.where(sc[k:k + 1, :] > sc, 1.0, jnp.where(sc[k:k + 1, :] == sc, tie, 0.0))
        return rank

    rank = jnp.zeros(score_t.shape, F32)
    group = 8
    for k0 in range(0, n_slc, group):
        ks = range(k0, min(k0 + group, n_slc))
        rank = lax.cond(k0 < n_visible, functools.partial(count, ks=ks), lambda r: r, rank)
    return jnp.where(rank < min(TOP_N, n_slc), jnp.where(sc >= 0.0, 1.0, 0.0), 0.0)


def _softmax_rows(s):
    m = jnp.max(s, -1, keepdims=True)
    e = jnp.where(s > MASKED_BELOW, jnp.exp(s - m), 0.0)
    return e / jnp.maximum(jnp.sum(e, -1, keepdims=True), 1e-30)


def _stack_heads(q_ref, h, i=0):
    return jnp.concatenate([q_ref[i, :, (h * GQA + g) * HEAD_DIM:(h * GQA + g + 1) * HEAD_DIM]
                            for g in range(GQA)], axis=0)


def _unstack_heads(o_ref, h, o, rows, i=0):
    for g in range(GQA):
        hd = h * GQA + g
        o_ref[i, :, hd * HEAD_DIM:(hd + 1) * HEAD_DIM] = o[g * rows:(g + 1) * rows]


def _cmp_prompt_kernel(kvc_ref, q_ref, bias_ref, w1_ref, b1_ref, w2_ref, b2_ref, ov_ref, o_ref, sel_ref,
                       kvcmp, *, n_sub, n_slc):
    qb = pl.program_id(1)

    @pl.when(qb == 0)
    def _():
        rows = lambda h, j: kvc_ref[h, pl.ds(j, n_sub, stride=D_CMP), :].astype(BF16)
        load = lambda h, jp: jnp.concatenate([rows(h, 2 * jp), rows(h, 2 * jp + 1)], axis=1)
        for h in range(KV_HEADS):
            kvcmp[h] = _compress(load, h, n_sub, (w1_ref, b1_ref, w2_ref, b2_ref)).astype(BF16)

    qpos = qb * Q_BLOCK + lax.broadcasted_iota(jnp.int32, (1, Q_BLOCK), 1)
    score_t = []
    for h in range(KV_HEADS):
        kv = kvcmp[h]
        s = _dot_nt(_pad_q(_stack_heads(q_ref, h)), kv) + bias_ref[h]
        p = _softmax_rows(s)
        _unstack_heads(o_ref, h, _dot(p.astype(BF16), kv)[:, HEAD_DIM:], Q_BLOCK)
        pg = p[0:Q_BLOCK]
        for g in range(1, GQA):
            pg = pg + p[g * Q_BLOCK:(g + 1) * Q_BLOCK]
        score_t.append(_dot(pg, ov_ref[...], HIGHEST).T)
    for h in range(KV_HEADS):
        sel_t = _select_blocks_on_rows(score_t[h][0:n_slc], qpos, n_slc, (qb + 1) * (Q_BLOCK // L_SEL))
        sel_ref[0, h] = jnp.concatenate([sel_t, jnp.zeros((LANES - n_slc, Q_BLOCK), F32)], axis=0).T


def _cmp_prompt(kvc, q, bias, cmp_w, overlap):
    bsz, t, _ = q.shape
    n_sub = t // D_CMP
    nqb = t // Q_BLOCK
    n_slc = t // L_SEL
    assert n_sub == LANES and n_slc <= LANES
    fixed = lambda a: pl.BlockSpec(a.shape, lambda b, i, _n=a.ndim: (0,) * _n)
    return pl.pallas_call(
        functools.partial(_cmp_prompt_kernel, n_sub=n_sub, n_slc=n_slc),
        grid=(bsz, nqb),
        in_specs=[pl.BlockSpec((KV_HEADS, t, KV_ROWS), lambda b, i: (0, b, 0)),
                  pl.BlockSpec((1, Q_BLOCK, ATT_WIDTH), lambda b, i: (b, i, 0)),
                  pl.BlockSpec((KV_HEADS, GQA * Q_BLOCK, n_sub), lambda b, i: (0, i, 0))]
        + [fixed(a) for a in cmp_w] + [fixed(overlap)],
        out_specs=[pl.BlockSpec((1, Q_BLOCK, ATT_WIDTH), lambda b, i: (b, i, 0)),
                   pl.BlockSpec((1, KV_HEADS, Q_BLOCK, LANES), lambda b, i: (b, 0, i, 0))],
        out_shape=[jax.ShapeDtypeStruct((bsz, t, ATT_WIDTH), F32),
                   jax.ShapeDtypeStruct((bsz, KV_HEADS, t, LANES), F32)],
        scratch_shapes=[pltpu.VMEM((KV_HEADS, n_sub, KV_ROWS), BF16)],
        compiler_params=_cparams(("parallel", "arbitrary")),
        name="cmp_prompt",
    )(kvc, q, bias, *cmp_w, overlap)


def _nsa_prompt_kernel(q_ref, kvs_ref, kvw_ref, sel_ref, eneg_ref, near_ref, far_ref, winb_ref, os_ref, ow_ref,
                       kaug, vsa, kwp, vwa, s_buf, mrun, acc, *, t):
    qb = pl.program_id(1)
    rows = GQA * Q_BLOCK
    near_w = 2 * Q_BLOCK
    win_w = WINDOW + Q_BLOCK

    @pl.when(qb == 0)
    def _():
        ones_row = jnp.where(lax.broadcasted_iota(jnp.int32, (KV_ROWS - HEAD_DIM, t), 0) == 0, 1.0, 0.0).astype(BF16)
        for h in range(KV_HEADS):
            lo = h * KV_ROWS
            kaug[h, :, 0:Q_BLOCK] = jnp.zeros((HEAD_DIM + SEL_PAD, Q_BLOCK), BF16)
            kaug[h, 0:HEAD_DIM, Q_BLOCK:] = kvs_ref[0, lo:lo + HEAD_DIM, :].astype(BF16)
            kaug[h, HEAD_DIM:, Q_BLOCK:] = eneg_ref[...]
            vsa[h, :, 0:Q_BLOCK] = jnp.zeros((KV_ROWS, Q_BLOCK), BF16)
            vsa[h, 0:HEAD_DIM, Q_BLOCK:] = kvs_ref[0, lo + HEAD_DIM:lo + KV_ROWS, :].astype(BF16)
            vsa[h, HEAD_DIM:, Q_BLOCK:] = ones_row
            kwp[h, :, 0:WINDOW] = jnp.zeros((HEAD_DIM, WINDOW), BF16)
            kwp[h, :, WINDOW:] = kvw_ref[0, lo:lo + HEAD_DIM, :].astype(BF16)
            vwa[h, :, 0:WINDOW] = jnp.zeros((KV_ROWS, WINDOW), BF16)
            vwa[h, 0:HEAD_DIM, WINDOW:] = kvw_ref[0, lo + HEAD_DIM:lo + KV_ROWS, :].astype(BF16)
            vwa[h, HEAD_DIM:, WINDOW:] = ones_row

    def normalise(a):
        return a[:, 0:HEAD_DIM] / a[:, HEAD_DIM:HEAD_DIM + 1]

    def tile_max(s):
        m = s[:, 0:LANES]
        for i in range(1, s.shape[1] // LANES):
            m = jnp.maximum(m, s[:, i * LANES:(i + 1) * LANES])
        return m

    start = pl.multiple_of(qb * Q_BLOCK, Q_BLOCK)
    n_chunk = t // SEL_CHUNK
    tiles_per_chunk = SEL_CHUNK // Q_BLOCK
    heads = range(KV_HEADS)
    q4 = [_stack_heads(q_ref, h) for h in heads]
    qa = [jnp.concatenate([q4[h], jnp.concatenate([(1.0 - sel_ref[0, h][:, 0:SEL_PAD]).astype(BF16)] * GQA,
                                                  axis=0)], axis=1) for h in heads]

    for h in heads:
        far = jnp.concatenate([far_ref[h]] * tiles_per_chunk, axis=1)
        mrun[h] = jnp.full((rows, LANES), NEG, F32)
        for c in range(n_chunk):
            @pl.when(c * tiles_per_chunk < qb - 1)
            def _(c=c, h=h, far=far):
                key = c * SEL_CHUNK + lax.broadcasted_iota(jnp.int32, (1, SEL_CHUNK), 1)
                late = jnp.where(key < (qb - 1) * Q_BLOCK, 0.0, NEG)
                lo = Q_BLOCK + c * SEL_CHUNK
                s = _dot(qa[h], kaug[h, :, lo:lo + SEL_CHUNK]) + far + late
                s_buf[h, :, c * SEL_CHUNK:(c + 1) * SEL_CHUNK] = s
                mrun[h] = jnp.maximum(mrun[h], tile_max(s))

    first = jnp.where(lax.broadcasted_iota(jnp.int32, (1, near_w), 1) < Q_BLOCK,
                      jnp.where(qb >= 1, 0.0, NEG), 0.0)
    pad = jnp.where(lax.broadcasted_iota(jnp.int32, (1, win_w), 1) < WINDOW - qb * Q_BLOCK, NEG, 0.0)
    s_near = [_dot(qa[h], kaug[h, :, pl.ds(start, near_w)]) + near_ref[h] + first for h in heads]
    s_w = [_dot(q4[h], kwp[h, :, pl.ds(start, win_w)]) + winb_ref[h] + pad for h in heads]
    m = []
    for h in heads:
        m.append(jnp.max(jnp.maximum(mrun[h], tile_max(s_near[h])), -1, keepdims=True))
        acc[h] = _dot_nt(jnp.exp(s_near[h] - m[h]).astype(BF16), vsa[h, :, pl.ds(start, near_w)])
        p_w = jnp.exp(s_w[h] - jnp.max(tile_max(s_w[h]), -1, keepdims=True))
        _unstack_heads(ow_ref, h, normalise(_dot_nt(p_w.astype(BF16), vwa[h, :, pl.ds(start, win_w)])), Q_BLOCK)

    for h in heads:
        for c in range(n_chunk):
            @pl.when(c * tiles_per_chunk < qb - 1)
            def _(c=c, h=h):
                lo = Q_BLOCK + c * SEL_CHUNK
                p = jnp.exp(s_buf[h, :, c * SEL_CHUNK:(c + 1) * SEL_CHUNK] - m[h])
                acc[h] = acc[h] + _dot_nt(p.astype(BF16), vsa[h, :, lo:lo + SEL_CHUNK])
    for h in heads:
        _unstack_heads(os_ref, h, normalise(acc[h]), Q_BLOCK)


def _nsa_prompt(q, kvs_t, kvw_t, sel, eneg, near, far, win_bias):
    bsz, _, t = kvs_t.shape
    assert t % SEL_CHUNK == 0
    nqb = t // Q_BLOCK
    qblk = lambda b, i: (b, i, 0)
    per_b = lambda b, i: (b, 0, 0)
    fixed = lambda a: pl.BlockSpec(a.shape, lambda b, i, _n=a.ndim: (0,) * _n)
    rows = GQA * Q_BLOCK
    return pl.pallas_call(
        functools.partial(_nsa_prompt_kernel, t=t),
        grid=(bsz, nqb),
        in_specs=[pl.BlockSpec((1, Q_BLOCK, ATT_WIDTH), qblk), pl.BlockSpec((1, KV_COLS, t), per_b),
                  pl.BlockSpec((1, KV_COLS, t), per_b),
                  pl.BlockSpec((1, KV_HEADS, Q_BLOCK, LANES), lambda b, i: (b, 0, i, 0)),
                  fixed(eneg), fixed(near), fixed(far), fixed(win_bias)],
        out_specs=[pl.BlockSpec((1, Q_BLOCK, ATT_WIDTH), qblk), pl.BlockSpec((1, Q_BLOCK, ATT_WIDTH), qblk)],
        out_shape=[jax.ShapeDtypeStruct((bsz, t, ATT_WIDTH), F32)] * 2,
        scratch_shapes=[pltpu.VMEM((KV_HEADS, HEAD_DIM + SEL_PAD, Q_BLOCK + t), BF16),
                        pltpu.VMEM((KV_HEADS, KV_ROWS, Q_BLOCK + t), BF16),
                        pltpu.VMEM((KV_HEADS, HEAD_DIM, WINDOW + t), BF16),
                        pltpu.VMEM((KV_HEADS, KV_ROWS, WINDOW + t), BF16),
                        pltpu.VMEM((KV_HEADS, rows, t), F32), pltpu.VMEM((KV_HEADS, rows, LANES), F32),
                        pltpu.VMEM((KV_HEADS, rows, KV_ROWS), F32)],
        compiler_params=_cparams(("parallel", "arbitrary")),
        name="nsa_prompt",
    )(q, kvs_t, kvw_t, sel, eneg, near, far, win_bias)


def _page_copy(pages_hbm, page, buf, sem, slot, k):
    return pltpu.make_async_copy(pages_hbm.at[page], buf.at[slot, k], sem.at[slot])


def _stream_pages(pt_ref, pages_hbm, buf, sem, n_pages):
    b = pl.program_id(0)
    slot = lax.rem(b, 2)

    def start(row, into):
        for k in range(n_pages):
            _page_copy(pages_hbm, pt_ref[row, k], buf, sem, into, k).start()

    @pl.when(b == 0)
    def _():
        start(0, 0)

    @pl.when(b + 1 < pl.num_programs(0))
    def _():
        start(b + 1, 1 - slot)

    for k in range(n_pages):
        _page_copy(pages_hbm, 0, buf, sem, slot, k).wait()
    return slot


def _cmp_sample_kernel(pt_ref, pages_hbm, q_ref, bias_ref, perm_ref, w1_ref, b1_ref, w2_ref, b2_ref, ov_ref, o_ref,
                       sel_ref, buf, sem, *xj, n_pages, n_slc, past, t):
    slot = _stream_pages(pt_ref, pages_hbm, buf, sem, n_pages)
    sub = PAGE_SIZE // D_CMP
    perm = perm_ref[...]
    per_tile = LANES // (2 * sub)
    n_sub = n_pages * sub
    n_pair = n_pages // 2
    cw = (w1_ref, b1_ref, w2_ref, b2_ref)
    qpos = past + lax.broadcasted_iota(jnp.int32, (t, 1), 0)

    def sort_pair(h, k2):
        lo = h * KV_ROWS
        pair = jnp.concatenate([buf[slot, 2 * k2, lo:lo + KV_ROWS, :].astype(BF16),
                                buf[slot, 2 * k2 + 1, lo:lo + KV_ROWS, :].astype(BF16)], axis=1)
        y = _dot(pair, perm)
        for c in range(2 * PAGE_SIZE // LANES):
            xt = y[:, c * LANES:(c + 1) * LANES].T
            for i in range(per_tile):
                j = c * per_tile + i
                xj[h][j // 2, 2 * sub * k2:2 * sub * (k2 + 1), (j % 2) * KV_ROWS:(j % 2 + 1) * KV_ROWS] = (
                    xt[2 * sub * i:2 * sub * (i + 1)].astype(BF16))

    def hidden_step(hid, h, jp):
        return hid + _dot(xj[h][jp], w1_ref[jp])

    def attend(h, hid):
        kv = _compress_finish(hid, n_sub, cw).astype(BF16)
        qh = _pad_q(_stack_heads(q_ref, h).astype(BF16))
        p = _softmax_rows(_dot_nt(qh, kv) + bias_ref[h])
        _unstack_heads(o_ref, h, _dot(p.astype(BF16), kv)[:, HEAD_DIM:], t)
        pg = p[0:t]
        for g in range(1, GQA):
            pg = pg + p[g * t:(g + 1) * t]
        return _dot(pg, ov_ref[...], HIGHEST)

    n_jp = D_CMP // 2
    zero = jnp.zeros((n_sub, 2 * KV_ROWS), F32)
    for k2 in range(n_pair):
        sort_pair(0, k2)
    hid0, hid1 = zero, zero
    every = n_pair // n_jp
    for k2 in range(n_pair):
        sort_pair(1, k2)
        if k2 % every == every - 1:
            hid0 = hidden_step(hid0, 0, k2 // every)
    for jp in range(n_jp // 2):
        hid1 = hidden_step(hid1, 1, jp)
    score0 = attend(0, hid0)
    for jp in range(n_jp // 2, n_jp):
        hid1 = hidden_step(hid1, 1, jp)
    sel_ref[0, 0] = _select(score0, qpos, n_slc)
    sel_ref[0, 1] = _select(attend(1, hid1), qpos, n_slc)


def _page_scratch(n_pages):
    return [pltpu.VMEM((2, n_pages, KV_COLS, PAGE_SIZE), F32), pltpu.SemaphoreType.DMA((2,))]


def _cmp_sample(pages, page_table, q, bias_cs, cmp_w, overlap, n_slc):
    bsz, n_pages = page_table.shape
    t = q.shape[1]
    width = overlap.shape[1]
    assert n_pages % 2 == 0
    sub = PAGE_SIZE // D_CMP
    col = np.arange(2 * PAGE_SIZE)
    page, row = col // PAGE_SIZE, col % PAGE_SIZE
    dest = (row % D_CMP) * 2 * sub + page * sub + row // D_CMP
    perm = jnp.asarray((dest[:, None] == col[None, :]).astype(np.float32), BF16)
    fixed = lambda a: pl.BlockSpec(a.shape, lambda b, pt, _n=a.ndim: (0,) * _n)
    grid_spec = pltpu.PrefetchScalarGridSpec(
        num_scalar_prefetch=1,
        grid=(bsz,),
        in_specs=[pl.BlockSpec(memory_space=pl.ANY), pl.BlockSpec((1, t, ATT_WIDTH), lambda b, pt: (b, 0, 0)),
                  fixed(bias_cs), fixed(perm)] + [fixed(a) for a in cmp_w] + [fixed(overlap)],
        out_specs=[pl.BlockSpec((1, t, ATT_WIDTH), lambda b, pt: (b, 0, 0)),
                   pl.BlockSpec((1, KV_HEADS, t, width), lambda b, pt: (b, 0, 0, 0))],
        scratch_shapes=_page_scratch(n_pages)
        + [pltpu.VMEM((D_CMP // 2, n_pages * sub, 2 * KV_ROWS), BF16)] * KV_HEADS)
    return pl.pallas_call(
        functools.partial(_cmp_sample_kernel, n_pages=n_pages, n_slc=n_slc, past=n_pages * PAGE_SIZE, t=t),
        grid_spec=grid_spec,
        out_shape=[jax.ShapeDtypeStruct((bsz, t, ATT_WIDTH), F32),
                   jax.ShapeDtypeStruct((bsz, KV_HEADS, t, width), F32)],
        compiler_params=_cparams(("arbitrary",)),
        name="cmp_sample",
    )(page_table, pages, q, bias_cs, perm, *cmp_w, overlap)


def _joint_attend(s_past, vt_past, s_new, v_new):
    m = jnp.maximum(jnp.max(s_past, -1, keepdims=True), jnp.max(s_new, -1, keepdims=True))
    e_past = jnp.exp(s_past - m)
    e_new = jnp.exp(s_new - m)
    den = jnp.sum(e_past, -1, keepdims=True) + jnp.sum(e_new, -1, keepdims=True)
    acc = _dot_nt(e_past.astype(BF16), vt_past) + _dot(e_new.astype(BF16), v_new)
    return acc / den


def _pad_new_rows(new_ref, t, i=0):
    return jnp.concatenate([new_ref[i], jnp.zeros((LANES - t, KV_COLS), F32)], axis=0)


def _sel_sample_kernel(pt_ref, pages_hbm, new_ref, q_ref, sel_ref, far_ref, near_ref, biasn_ref, o_ref, buf, sem,
                       kvb, mask, *, n_pages, t):
    slot = _stream_pages(pt_ref, pages_hbm, buf, sem, n_pages)
    for k in range(n_pages):
        kvb[:, k * PAGE_SIZE:(k + 1) * PAGE_SIZE] = buf[slot, k].astype(BF16)
    kv_new = _pad_new_rows(new_ref, t).astype(BF16)
    rows = GQA * t
    lane = lax.broadcasted_iota(jnp.int32, (rows, LANES), 1)
    per_tile = LANES // L_SEL
    for h in range(KV_HEADS):
        lo = h * KV_ROWS
        sel4 = jnp.concatenate([sel_ref[0, h]] * GQA, axis=0)

        def tile_mask(k):
            cols = [jnp.broadcast_to(sel4[:, per_tile * k + i:per_tile * k + i + 1], (rows, LANES))
                    for i in range(per_tile)]
            m = cols[-1]
            for i in range(per_tile - 2, -1, -1):
                m = jnp.where(lane < (i + 1) * L_SEL, cols[i], m)
            return (m - 1.0) * (-NEG)

        far = jnp.concatenate([jnp.broadcast_to(far_ref[h * GQA + g:h * GQA + g + 1, :], (t, LANES))
                               for g in range(GQA)], axis=0)
        for k in range(n_pages - 1):
            mask[:, k * LANES:(k + 1) * LANES] = tile_mask(k) + far
        mask[:, (n_pages - 1) * LANES:n_pages * LANES] = tile_mask(n_pages - 1) + near_ref[h]
        qh = _stack_heads(q_ref, h).astype(BF16)
        s_past = _dot(qh, kvb[lo:lo + HEAD_DIM, :]) + mask[...]
        s_new = _dot_nt(qh, kv_new[:, lo:lo + HEAD_DIM]) + biasn_ref[h] + tile_mask(n_pages)
        o = _joint_attend(s_past, kvb[lo + HEAD_DIM:lo + KV_ROWS, :], s_new, kv_new[:, lo + HEAD_DIM:lo + KV_ROWS])
        _unstack_heads(o_ref, h, o, t)


def _sel_sample(pages, page_table, kvs_new, q, sel, far, near, bias_new):
    bsz, n_pages = page_table.shape
    t = q.shape[1]
    past = n_pages * PAGE_SIZE
    width = sel.shape[-1]
    assert (n_pages + 1) * (LANES // L_SEL) <= width
    per_b = lambda b, pt: (b, 0, 0)
    fixed = lambda a: pl.BlockSpec(a.shape, lambda b, pt, _n=a.ndim: (0,) * _n)
    grid_spec = pltpu.PrefetchScalarGridSpec(
        num_scalar_prefetch=1,
        grid=(bsz,),
        in_specs=[pl.BlockSpec(memory_space=pl.ANY),
                  pl.BlockSpec((1, t, KV_COLS), per_b), pl.BlockSpec((1, t, ATT_WIDTH), per_b),
                  pl.BlockSpec((1, KV_HEADS, t, width), lambda b, pt: (b, 0, 0, 0)),
                  fixed(far), fixed(near), fixed(bias_new)],
        out_specs=pl.BlockSpec((1, t, ATT_WIDTH), per_b),
        scratch_shapes=_page_scratch(n_pages)
        + [pltpu.VMEM((KV_COLS, past), BF16), pltpu.VMEM((GQA * t, past), F32)])
    return pl.pallas_call(
        functools.partial(_sel_sample_kernel, n_pages=n_pages, t=t),
        grid_spec=grid_spec,
        out_shape=jax.ShapeDtypeStruct((bsz, t, ATT_WIDTH), F32),
        compiler_params=_cparams(("arbitrary",)),
        name="sel_sample",
    )(page_table, pages, kvs_new, q, sel, far, near, bias_new)


def _win_sample_kernel(buf_ref, new_ref, q_ref, bias_ref, biasn_ref, o_ref, win_ref, *, t, nb):
    lane = lax.broadcasted_iota(jnp.int32, (KV_COLS, LANES), 1)

    def update(i):
        buf = buf_ref[i]
        w = buf.shape[1]
        new = _pad_new_rows(new_ref, t, i)
        shifted = pltpu.roll(buf, w - t, 1)
        tail = pltpu.roll(new.T, LANES - t, 1)
        win_ref[i, :, 0:w - LANES] = shifted[:, 0:w - LANES]
        win_ref[i, :, w - LANES:w] = jnp.where(lane >= LANES - t, tail, shifted[:, w - LANES:w])

    def attend(i):
        kvb = buf_ref[i].astype(BF16)
        kv_new = _pad_new_rows(new_ref, t, i).astype(BF16)
        for h in range(KV_HEADS):
            lo = h * KV_ROWS
            qh = _stack_heads(q_ref, h, i).astype(BF16)
            s_past = _dot(qh, kvb[lo:lo + HEAD_DIM, :]) + bias_ref[h]
            s_new = _dot_nt(qh, kv_new[:, lo:lo + HEAD_DIM]) + biasn_ref[h]
            o = _joint_attend(s_past, kvb[lo + HEAD_DIM:lo + KV_ROWS, :], s_new,
                              kv_new[:, lo + HEAD_DIM:lo + KV_ROWS])
            _unstack_heads(o_ref, h, o, t, i)

    update(0)
    for i in range(nb):
        if i + 1 < nb:
            update(i + 1)
        attend(i)


def _win_sample(buf_t, kvw_new, q, bias_buf, bias_new):
    bsz, _, w = buf_t.shape
    t = q.shape[1]
    nb = ROWS_PER_STEP if bsz % ROWS_PER_STEP == 0 else 1
    per_b = lambda b: (b, 0, 0)
    fixed = lambda a: pl.BlockSpec(a.shape, lambda b, _n=a.ndim: (0,) * _n)
    return pl.pallas_call(
        functools.partial(_win_sample_kernel, t=t, nb=nb),
        grid=(bsz // nb,),
        in_specs=[pl.BlockSpec((nb, KV_COLS, w), per_b), pl.BlockSpec((nb, t, KV_COLS), per_b),
                  pl.BlockSpec((nb, t, ATT_WIDTH), per_b), fixed(bias_buf), fixed(bias_new)],
        out_specs=[pl.BlockSpec((nb, t, ATT_WIDTH), per_b), pl.BlockSpec((nb, KV_COLS, w), per_b)],
        out_shape=[jax.ShapeDtypeStruct((bsz, t, ATT_WIDTH), F32),
                   jax.ShapeDtypeStruct((bsz, KV_COLS, w), F32)],
        compiler_params=_cparams(("parallel",)),
        name="win_sample",
    )(buf_t, kvw_new, q, bias_buf, bias_new)


def _combine_kernel(x_ref, y_ref, oc_ref, os_ref, ow_ref, sm_ref, eg_ref, eb_ref, ex_ref, ag_ref, wo_ref, g1_ref,
                    b1_ref, h_ref):
    xn = _layer_norm(x_ref[...], eg_ref[...], eb_ref[...])
    gates = _sigmoid(sm_ref[...])
    g_hi = gates.astype(BF16)
    g_lo = (gates - g_hi.astype(F32)).astype(BF16)
    o = jnp.zeros(oc_ref.shape, F32)
    for br, ref in enumerate((oc_ref, os_ref, ow_ref)):
        o = o + (_dot(g_hi, ex_ref[br]) + _dot(g_lo, ex_ref[br])) * ref[...]
    rms = lax.rsqrt(jnp.mean(o * o, -1, keepdims=True) + EPS)
    att = (o * rms * ag_ref[...]).astype(BF16)
    mix = _dot(jnp.concatenate([y_ref[...], att], axis=1), wo_ref[...])
    h_ref[...] = _layer_norm(ALPHA * xn + mix, g1_ref[...], b1_ref[...])


def _combine(x2d, y, oc, os_, ow, sm, eg, eb, gate_expand, ag, wo, g1, b1, tm):
    n = x2d.shape[0]
    tm = _row_tile(n, tm)
    row = lambda i: (i, 0)
    fixed = lambda a: pl.BlockSpec(a.shape, lambda i, _n=a.ndim: (0,) * _n)
    att = pl.BlockSpec((tm, ATT_WIDTH), row)
    return pl.pallas_call(
        _combine_kernel,
        grid=(n // tm,),
        in_specs=[pl.BlockSpec((tm, D_MODEL), row), pl.BlockSpec((tm, SSM_WIDTH), row), att, att, att,
                  pl.BlockSpec((tm, LANES), row), fixed(eg), fixed(eb), fixed(gate_expand), fixed(ag), fixed(wo),
                  fixed(g1), fixed(b1)],
        out_specs=pl.BlockSpec((tm, D_MODEL), row),
        out_shape=jax.ShapeDtypeStruct((n, D_MODEL), F32),
        compiler_params=_cparams(("parallel",)),
        name="combine",
    )(x2d, y, oc, os_, ow, sm, eg, eb, gate_expand, ag, wo, g1, b1)


def _ffn_kernel(h_ref, wu_ref, wd_ref, g_ref, b_ref, o_ref, *, tf):
    h = h_ref[...]
    hb = h.astype(BF16)
    acc = None
    for k in range(D_FF // tf):
        u = jnp.maximum(_dot(hb, wu_ref[:, k * tf:(k + 1) * tf]), 0.0)
        part = _dot((u * u).astype(BF16), wd_ref[k * tf:(k + 1) * tf, :])
        acc = part if acc is None else acc + part
    o_ref[...] = _layer_norm(ALPHA * h + acc, g_ref[...], b_ref[...])


def _ffn(h2d, wu, wd, g, b, tm, tf):
    n = h2d.shape[0]
    tm = _row_tile(n, tm)
    resident = lambda a: pl.BlockSpec(a.shape, lambda i: (0, 0), pipeline_mode=pl.Buffered(1))
    return pl.pallas_call(
        functools.partial(_ffn_kernel, tf=tf),
        grid=(n // tm,),
        in_specs=[pl.BlockSpec((tm, D_MODEL), lambda i: (i, 0)), resident(wu), resident(wd), resident(g), resident(b)],
        out_specs=pl.BlockSpec((tm, D_MODEL), lambda i: (i, 0)),
        out_shape=jax.ShapeDtypeStruct((n, D_MODEL), F32),
        compiler_params=_cparams(("parallel",)),
        name="ffn",
    )(h2d, wu, wd, g, b)


def _bucket_np(dist):
    d = np.maximum(dist, 0)
    exact = N_BUCKETS // 2
    far = exact + (np.log(np.maximum(d, 1).astype(np.float32) / np.float32(exact))
                   / np.float32(math.log(MAX_DISTANCE / exact)) * (N_BUCKETS - exact)).astype(np.int32)
    return np.where(d < exact, d, np.minimum(far, N_BUCKETS - 1)).astype(np.int32)


def _bias_lookup(tbl, dist, mask=None):
    dist = np.asarray(dist)
    onehot = np.eye(N_BUCKETS, dtype=np.float32)[_bucket_np(dist).reshape(-1)]
    b = jnp.dot(jnp.asarray(onehot), tbl, precision=HIGHEST).T.reshape((ATT_HEADS,) + dist.shape)
    return b if mask is None else jnp.where(jnp.asarray(mask)[None], b, NEG)


def _toeplitz_tile(tbl, offset, mask):
    period = 2 * Q_BLOCK
    k = np.arange(period)
    vals = _bias_lookup(tbl, offset - np.where(k < Q_BLOCK, k, k - period))
    tiled = jnp.tile(vals, (1, Q_BLOCK))[:, :Q_BLOCK * (period - 1)]
    t = tiled.reshape(ATT_HEADS, Q_BLOCK, period - 1)[:, :, :Q_BLOCK]
    return jnp.where(jnp.asarray(mask)[None], t, NEG)


def _cmp_prompt_bias(tbl, t):
    n_sub = t // D_CMP
    back = (MAX_DISTANCE + L_CMP - 1) // D_CMP
    band = (np.arange(Q_BLOCK)[:, None] - D_CMP * np.arange(NEAR_W)[None, :] + D_CMP * back - (L_CMP - 1))
    near = jnp.pad(_bias_lookup(tbl, band), ((0, 0), (0, 0), (0, n_sub - NEAR_W)))
    far = tbl[N_BUCKETS - 1][:, None, None]
    blk = np.arange(n_sub)[None, :]
    tiles = []
    for qb in range(t // Q_BLOCK):
        near_lo = qb * (Q_BLOCK // D_CMP) - back
        qpos = qb * Q_BLOCK + np.arange(Q_BLOCK)[:, None]
        visible = (blk * D_CMP + (L_CMP - 1) <= qpos) & (blk < n_sub - 1)
        tile = jnp.where(jnp.asarray(blk >= near_lo)[None], jnp.roll(near, near_lo % n_sub, axis=2), far)
        tiles.append(jnp.where(jnp.asarray(visible)[None], tile, NEG))
    tiles = jnp.stack(tiles).reshape(t // Q_BLOCK, KV_HEADS, GQA * Q_BLOCK, n_sub)
    return jnp.moveaxis(tiles, 0, 1).reshape(KV_HEADS, -1, n_sub)


def _stack_gt(tab, t):
    return tab.reshape(KV_HEADS, GQA * t, tab.shape[-1])


def _far_rows(tbl):
    return jnp.broadcast_to(tbl[N_BUCKETS - 1][:, None], (ATT_HEADS, LANES))


def _overlap(n_cmp_pad, n_cmp, width, n_slc):
    i = np.arange(n_cmp_pad)[:, None]
    j = np.arange(width)[None, :]
    ov = (i * D_CMP < (j + 1) * L_SEL) & (i * D_CMP + L_CMP > j * L_SEL) & (i < n_cmp) & (j < n_slc)
    return jnp.asarray(ov.astype(np.float32))


def _prep_cmp_weights(w1, b1, w2, b2):
    eye = jnp.eye(2, dtype=F32)
    w1r = (w1[:, :, :, :, None, :] * eye[None, None, :, None, :, None]).transpose(1, 2, 3, 0, 4, 5)
    w1r = w1r.reshape(D_CMP // 2, 2 * KV_ROWS, 2 * KV_ROWS).astype(BF16)
    w2r = (w2[:, :, None, :] * eye[:, None, :, None]).reshape(2 * CMP_HID, KV_ROWS).astype(BF16)
    return (w1r, b1.reshape(1, 2 * CMP_HID), w2r, b2.reshape(1, KV_ROWS))


def _prep_w_in(w_in):
    sizes = (SSM_WIDTH, CONV_DIM, SSM_HEADS, ATT_WIDTH, KV_COLS, KV_COLS, KV_COLS)
    z, xbc, dt, q, kvc, kvs, kvw, gates = jnp.split(w_in, np.cumsum(sizes).tolist(), axis=1)
    small = jnp.concatenate([dt, gates], axis=1)
    small = jnp.pad(small, ((0, 0), (0, LANES - small.shape[1])))
    return jnp.concatenate([z, xbc, q * ATT_SCALE, kvc, kvs, kvw, small], axis=1).astype(BF16)


def _gate_expand():
    ex = np.zeros((N_BRANCH, LANES, ATT_WIDTH), np.float32)
    for br in range(N_BRANCH):
        for hd in range(ATT_HEADS):
            ex[br, SSM_HEADS + br * ATT_HEADS + hd, hd * HEAD_DIM:(hd + 1) * HEAD_DIM] = 1.0
    return jnp.asarray(ex, BF16)


def _feature_major(a):
    lead = a.shape[:-4]
    rows = a.shape[-4]
    return jnp.moveaxis(a.reshape(lead + (rows, KV_COLS)), -2, -1)


def _row_major6(a_t):
    bsz, _, rows = a_t.shape
    return jnp.moveaxis(a_t, 1, 2).reshape(1, bsz, rows, KV_HEADS, 2, HEAD_DIM)


def kernel(x_prompt, x_sample, cache_cmp_kv, cache_slc_kv, cache_win_kv, state_conv, state_ssm, page_table,
           rel_bias_table, emb_ln_g, emb_ln_b, w_in, conv_w, conv_b, dt_bias, a_log, d_skip, ssm_norm_g,
           cmp_w1, cmp_b1, cmp_w2, cmp_b2, att_norm_g, w_out, ln1_g, ln1_b, w_up, w_down, ln2_g, ln2_b):
    assert w_in.shape[0] == DEPTH
    bp, tp, _ = x_prompt.shape
    bs, ts, _ = x_sample.shape
    n_pages = page_table.shape[1]
    past = n_pages * PAGE_SIZE
    w_buf = cache_win_kv.shape[2]
    assert ts < D_CMP and ts % 8 == 0 and w_buf == WINDOW and past >= WINDOW and tp >= WINDOW
    tbl = rel_bias_table
    vec = lambda v: v.reshape(1, -1)

    w_proj = _prep_w_in(w_in[0])
    cmp_w = _prep_cmp_weights(cmp_w1[0], cmp_b1[0], cmp_w2[0], cmp_b2[0])
    wo = w_out[0].astype(BF16)
    wu = w_up[0].astype(BF16)
    wd = w_down[0].astype(BF16)
    eg, eb = vec(emb_ln_g), vec(emb_ln_b)
    gate_expand = _gate_expand()
    far = _far_rows(tbl)

    def trunk_tail(x2d, y, oc, os_, ow, sm):
        h = _combine(x2d, y, oc, os_, ow, sm, eg, eb, gate_expand, vec(att_norm_g[0]), wo, vec(ln1_g[0]),
                     vec(ln1_b[0]), COMBINE_ROWS)
        return _ffn(h, wu, wd, vec(ln2_g[0]), vec(ln2_b[0]), FFN_ROWS, FFN_SLAB)

    ssm_w = (conv_w[0], conv_b[0], dt_bias[0], a_log[0], d_skip[0], ssm_norm_g[0])

    xp2 = x_prompt.reshape(bp * tp, D_MODEL)
    z, xbc, q, kvc, kvc_t, kvs_t, kvw_t, sm = _proj(xp2, eg, eb, w_proj, BF16, PROJ_ROWS, seq=(bp, tp))
    r3 = lambda a: a.reshape(bp, tp, a.shape[-1])
    xbc3 = r3(xbc)
    y_ssm, h_new = _ssm(r3(z), xbc3, r3(sm), jnp.zeros((bp, CONV_WIDTH - 1, CONV_DIM), F32),
                        jnp.zeros((bp, SSM_HEADS, SSM_HEAD_DIM, D_STATE), F32), *ssm_w)
    n_sub = tp // D_CMP
    n_slc = tp // L_SEL
    oc, sel = _cmp_prompt(kvc, r3(q), _cmp_prompt_bias(tbl, tp), cmp_w, _overlap(n_sub, n_sub - 1, LANES, n_slc))
    ii = np.arange(Q_BLOCK)[:, None] - np.arange(Q_BLOCK)[None, :]
    tiles_gq = lambda rs, ok: jnp.concatenate(
        [_toeplitz_tile(tbl, Q_BLOCK * r, ok(ii + Q_BLOCK * r)) for r in rs], axis=2).reshape(
            KV_HEADS, GQA * Q_BLOCK, len(rs) * Q_BLOCK)
    near = tiles_gq((1, 0), lambda d: d >= 0)
    win_bias = tiles_gq(range(WINDOW // Q_BLOCK, -1, -1), lambda d: (d >= 0) & (d < WINDOW))
    far_gq = jnp.repeat(far, Q_BLOCK, axis=0).reshape(KV_HEADS, GQA * Q_BLOCK, LANES)
    assert n_slc <= SEL_PAD
    eneg = jnp.asarray(np.where(np.arange(SEL_PAD)[:, None] == (np.arange(tp) // L_SEL)[None, :], NEG, 0.0), BF16)
    os_, ow = _nsa_prompt(r3(q), kvs_t, kvw_t, sel, eneg, near, far_gq, win_bias)
    f2 = lambda a: a.reshape(bp * tp, a.shape[-1])
    y_prompt = trunk_tail(xp2, f2(y_ssm), f2(oc), f2(os_), f2(ow), sm).reshape(bp, tp, D_MODEL)
    w = min(WINDOW, tp)
    prompt_state = (_row_major6(kvc_t), _row_major6(kvs_t), _row_major6(kvw_t[:, :, tp - w:]),
                    xbc3[:, tp - (CONV_WIDTH - 1):][None], h_new[None])

    xs2 = x_sample.reshape(bs * ts, D_MODEL)
    z, xbc, q, kvc, kvs, kvw, sm = _proj(xs2, eg, eb, w_proj, F32, PROJ_ROWS)
    r3 = lambda a: a.reshape(bs, ts, a.shape[-1])
    xbc3 = r3(xbc)
    y_ssm, h_new = _ssm(r3(z), xbc3, r3(sm), state_conv[0], state_ssm[0], *ssm_w)
    n_sub = past // D_CMP
    n_cmp = n_sub - 1
    n_slc = -(-(past + ts) // L_SEL)
    width = -(-n_slc // LANES) * LANES
    qpos = past + np.arange(ts)
    dist_c = qpos[:, None] - (np.arange(n_sub) * D_CMP + L_CMP - 1)[None, :]
    bias_cs = _stack_gt(_bias_lookup(tbl, dist_c, (dist_c >= 0) & (np.arange(n_sub) < n_cmp)[None, :]), ts)
    oc, sel = _cmp_sample(_feature_major(cache_cmp_kv[0]), page_table, r3(q), bias_cs, cmp_w,
                          _overlap(n_sub, n_cmp, width, n_slc), n_slc)
    dist_last = qpos[:, None] - (past - PAGE_SIZE + np.arange(PAGE_SIZE))[None, :]
    dist_n = np.arange(ts)[:, None] - np.arange(LANES)[None, :]
    bias_new = _stack_gt(_bias_lookup(tbl, dist_n, (dist_n >= 0) & (np.arange(LANES) < ts)[None, :]), ts)
    os_ = _sel_sample(_feature_major(cache_slc_kv[0]), page_table, r3(kvs), r3(q), sel, far,
                      _stack_gt(_bias_lookup(tbl, dist_last), ts), bias_new)
    dist_w = qpos[:, None] - (past - w_buf + np.arange(w_buf))[None, :]
    bias_wb = _stack_gt(_bias_lookup(tbl, dist_w, (dist_w >= 0) & (dist_w < WINDOW)), ts)
    ow, win_new_t = _win_sample(_feature_major(cache_win_kv[0]), r3(kvw), r3(q), bias_wb, bias_new)
    f2 = lambda a: a.reshape(bs * ts, a.shape[-1])
    y_sample = trunk_tail(xs2, f2(y_ssm), f2(oc), f2(os_), f2(ow), sm).reshape(bs, ts, D_MODEL)
    kv6 = lambda a: a.reshape(1, bs, ts, KV_HEADS, 2, HEAD_DIM)
    sample_state = (kv6(kvc), kv6(kvs), _row_major6(win_new_t),
                    xbc3[:, ts - (CONV_WIDTH - 1):][None], h_new[None])

    return (y_prompt, y_sample) + prompt_state + sample_state
```

```python
import functools
import math

import numpy as np
import jax
import jax.numpy as jnp
from jax import lax
from jax.experimental import pallas as pl
from jax.experimental.pallas import tpu as pltpu

F32 = jnp.float32
BF16 = jnp.bfloat16
HIGHEST = lax.Precision.HIGHEST

D_MODEL = 1024
SSM_HEADS = 8
SSM_HEAD_DIM = 64
SSM_WIDTH = SSM_HEADS * SSM_HEAD_DIM
SSM_GROUPS = 2
D_STATE = 128
CONV_WIDTH = 4
CONV_DIM = SSM_WIDTH + 2 * SSM_GROUPS * D_STATE
SSD_CHUNK = 128
ATT_HEADS = 8
KV_HEADS = 2
GQA = ATT_HEADS // KV_HEADS
HEAD_DIM = 64
ATT_WIDTH = ATT_HEADS * HEAD_DIM
KV_COLS = KV_HEADS * 2 * HEAD_DIM
D_CMP = 16
L_CMP = 2 * D_CMP
CMP_HID = 64
L_SEL = 64
TOP_N = 16
WINDOW = 512
Q_BLOCK = 128
N_BRANCH = 3
FORCED_SCORE = 1e4
N_BUCKETS = 32
MAX_DISTANCE = 128
D_FF = 4 * D_MODEL
DEPTH = 1
ALPHA = (2 * DEPTH) ** 0.25
ATT_SCALE = HEAD_DIM ** -0.5
EPS = 1e-5
PAGE_SIZE = 128

LANES = 128
NEG = -1e30
MASKED_BELOW = -1e29
VMEM_LIMIT = 48 * 1024 * 1024
KV_ROWS = 2 * HEAD_DIM
SEL_PAD = 32
NEAR_W = 32
SEL_CHUNK = 512
PROJ_ROWS = 512
COMBINE_ROWS = 512
FFN_ROWS = 512
FFN_SLAB = D_FF // 4
ROWS_PER_STEP = 4

_OFF_Z = 0
_OFF_XBC = _OFF_Z + SSM_WIDTH
_OFF_Q = _OFF_XBC + CONV_DIM
_OFF_KVC = _OFF_Q + ATT_WIDTH
_OFF_KVS = _OFF_KVC + KV_COLS
_OFF_KVW = _OFF_KVS + KV_COLS
_OFF_SM = _OFF_KVW + KV_COLS
_N_PROJ = _OFF_SM + LANES


def _cparams(sem):
    return pltpu.CompilerParams(dimension_semantics=sem, vmem_limit_bytes=VMEM_LIMIT)


def _row_tile(n, preferred):
    tm = min(n, preferred)
    assert n % tm == 0 and tm % 8 == 0
    return tm


def _dot(a, b, precision=None):
    return jnp.dot(a, b, preferred_element_type=F32, precision=precision)


def _dot_nt(a, b):
    return lax.dot_general(a, b, (((1,), (1,)), ((), ())), preferred_element_type=F32)


def _layer_norm(x, g, b):
    mu = jnp.mean(x, -1, keepdims=True)
    xc = x - mu
    var = jnp.mean(xc * xc, -1, keepdims=True)
    return xc * lax.rsqrt(var + EPS) * g + b


def _sigmoid(x):
    return 1.0 / (1.0 + jnp.exp(-x))


def _softplus(x):
    return jnp.maximum(x, 0.0) + jnp.log(1.0 + jnp.exp(-jnp.abs(x)))


def _gelu_tanh(x):
    c = math.sqrt(2.0 / math.pi)
    return 0.5 * x * (1.0 + jnp.tanh(c * (x + 0.044715 * (x * x * x))))


def _proj_kernel(x_ref, g_ref, b_ref, w_ref, z_ref, xbc_ref, q_ref, kvc_ref, *rest, feature_major):
    xn = _layer_norm(x_ref[...], g_ref[...], b_ref[...]).astype(BF16)

    def mm(lo, hi):
        return _dot(xn, w_ref[:, lo:hi])

    z_ref[...] = mm(_OFF_Z, _OFF_XBC)
    xbc_ref[...] = mm(_OFF_XBC, _OFF_Q)
    q_ref[...] = mm(_OFF_Q, _OFF_KVC).astype(q_ref.dtype)
    kvc = mm(_OFF_KVC, _OFF_KVS)
    if feature_major:
        kvct_ref, kvst_ref, kvwt_ref, sm_ref = rest
        for h in range(KV_HEADS):
            kvc_ref[h] = kvc[:, h * KV_ROWS:(h + 1) * KV_ROWS]
        kvct_ref[0] = kvc.T
        kvst_ref[0] = mm(_OFF_KVS, _OFF_KVW).T
        kvwt_ref[0] = mm(_OFF_KVW, _OFF_SM).T
    else:
        kvs_ref, kvw_ref, sm_ref = rest
        kvc_ref[...] = kvc
        kvs_ref[...] = mm(_OFF_KVS, _OFF_KVW)
        kvw_ref[...] = mm(_OFF_KVW, _OFF_SM)
    sm_ref[...] = mm(_OFF_SM, _N_PROJ)


def _proj(x2d, g, b, w, q_dtype, tm, seq=None):
    n = x2d.shape[0]
    tm = _row_tile(n, tm)
    row = lambda i: (i, 0)
    fixed = lambda i: (0, 0)
    rm = lambda wd, dt: (pl.BlockSpec((tm, wd), row), jax.ShapeDtypeStruct((n, wd), dt))
    outs = [rm(SSM_WIDTH, F32), rm(CONV_DIM, F32), rm(ATT_WIDTH, q_dtype)]
    if seq is None:
        outs += [rm(KV_COLS, F32), rm(KV_COLS, F32), rm(KV_COLS, F32)]
    else:
        outs.append((pl.BlockSpec((KV_HEADS, tm, KV_ROWS), lambda i: (0, i, 0)),
                     jax.ShapeDtypeStruct((KV_HEADS, n, KV_ROWS), F32)))
        bsz, t = seq
        assert t % tm == 0 and tm % LANES == 0
        per = t // tm
        fm = (pl.BlockSpec((1, KV_COLS, tm), lambda i: (i // per, 0, i % per)),
              jax.ShapeDtypeStruct((bsz, KV_COLS, t), F32))
        outs += [fm, fm, fm]
    outs.append(rm(LANES, F32))
    return pl.pallas_call(
        functools.partial(_proj_kernel, feature_major=seq is not None),
        grid=(n // tm,),
        in_specs=[pl.BlockSpec((tm, D_MODEL), row), pl.BlockSpec((1, D_MODEL), fixed),
                  pl.BlockSpec((1, D_MODEL), fixed), pl.BlockSpec((D_MODEL, _N_PROJ), fixed)],
        out_specs=[o[0] for o in outs],
        out_shape=[o[1] for o in outs],
        compiler_params=_cparams(("parallel",)),
        name="proj",
    )(x2d, g, b, w)


def _ssm_kernel(z_ref, xbc_ref, sm_ref, hist_ref, h0_ref, cw_ref, cb_ref, dtb_ref, alog_ref, dskip_ref,
                ng_ref, y_ref, hfin_ref, xext, state, *, tb, l, nc, nb):
    for i in range(nb):
        _ssm_row(i, z_ref, xbc_ref, sm_ref, hist_ref, h0_ref, cw_ref, cb_ref, dtb_ref, alog_ref, dskip_ref,
                 ng_ref, y_ref, hfin_ref, xext.at[i], state.at[i], tb=tb, l=l, nc=nc)


def _ssm_row(i, z_ref, xbc_ref, sm_ref, hist_ref, h0_ref, cw_ref, cb_ref, dtb_ref, alog_ref, dskip_ref,
             ng_ref, y_ref, hfin_ref, xext, state, *, tb, l, nc):
    c = pl.program_id(1)

    @pl.when(c == 0)
    def _():
        xext[0:8, :] = jnp.zeros((8, CONV_DIM), F32)
        xext[8 - (CONV_WIDTH - 1):8, :] = hist_ref[i]
        if tb < l:
            xext[8 + tb:8 + l, :] = jnp.zeros((l - tb, CONV_DIM), F32)
        state[...] = h0_ref[i].reshape(SSM_WIDTH, D_STATE)

    xext[8:8 + tb, :] = xbc_ref[i]
    conv = cb_ref[...]
    for k in range(CONV_WIDTH):
        lo = 8 - (CONV_WIDTH - 1) + k
        conv = conv + cw_ref[k:k + 1, :] * xext[lo:lo + l, :]
    xc = conv * _sigmoid(conv)
    xext[0:8, :] = xext[tb:tb + 8, :]

    dt = _softplus(sm_ref[i] + dtb_ref[...])
    if tb < l:
        dt = jnp.concatenate([dt, jnp.zeros((l - tb, LANES), F32)], axis=0)
    a = dt * (-jnp.exp(alog_ref[...]))
    ri = lax.broadcasted_iota(jnp.int32, (l, l), 0)
    ci = lax.broadcasted_iota(jnp.int32, (l, l), 1)
    tril = ri >= ci
    a_cs = _dot(jnp.where(tril, 1.0, 0.0), a, HIGHEST)
    a_cs_t = a_cs.T
    dt_t = dt.T
    ea = jnp.exp(a_cs[:tb])
    wend = dt * jnp.exp(a_cs[l - 1:l, :] - a_cs)
    etot = jnp.exp(a_cs[l - 1:l, :])

    xs = xc[:, :SSM_WIDTH]
    lane = lax.broadcasted_iota(jnp.int32, (tb, LANES), 1)
    srow = lax.broadcasted_iota(jnp.int32, (LANES, D_STATE), 0)
    tril_q = tril[:tb]
    y_pairs = []
    for g in range(SSM_GROUPS):
        bg_f = xc[:, SSM_WIDTH + g * D_STATE:SSM_WIDTH + (g + 1) * D_STATE]
        bg = bg_f.astype(BF16)
        c_lo = SSM_WIDTH + SSM_GROUPS * D_STATE + g * D_STATE
        cg = xc[:tb, c_lo:c_lo + D_STATE].astype(BF16)
        cb = _dot_nt(cg, bg)
        for k in range(2):
            pair = 2 * g + k
            lo = pair * LANES
            h0, h1 = 2 * pair, 2 * pair + 1
            xs_pair = xs[:, lo:lo + LANES]
            xs_b = xs_pair.astype(BF16)
            xs_t = xs_pair.T
            ys, upd = [], []
            for r2, h in enumerate((h0, h1)):
                seg = a_cs[:tb, h:h + 1] - a_cs_t[h:h + 1, :]
                lm = jnp.where(tril_q, jnp.exp(jnp.where(tril_q, seg, 0.0)), 0.0) * dt_t[h:h + 1, :]
                ys.append(_dot((cb * lm).astype(BF16), xs_b))
                upd.append(_dot(xs_t[r2 * SSM_HEAD_DIM:(r2 + 1) * SSM_HEAD_DIM].astype(BF16),
                                (bg_f * wend[:, h:h + 1]).astype(BF16)))
            y_diag = jnp.where(lane < SSM_HEAD_DIM, ys[0], ys[1])
            sp = state[lo:lo + LANES, :]
            y_off = _dot_nt(cg, sp.astype(BF16)) * jnp.where(lane < SSM_HEAD_DIM, ea[:, h0:h0 + 1], ea[:, h1:h1 + 1])
            y_pairs.append(y_diag + y_off)
            keep = jnp.where(srow < SSM_HEAD_DIM, etot[:, h0:h0 + 1], etot[:, h1:h1 + 1])
            state[lo:lo + LANES, :] = sp * keep + jnp.concatenate(upd, axis=0)
    y = jnp.concatenate(y_pairs, axis=1) + dskip_ref[...] * xs[:tb]
    zz = z_ref[i]
    y = y * (zz * _sigmoid(zz))
    gw = SSM_WIDTH // SSM_GROUPS
    outs = []
    for g in range(SSM_GROUPS):
        yg = y[:, g * gw:(g + 1) * gw]
        ms = jnp.mean(yg * yg, -1, keepdims=True)
        outs.append(yg * lax.rsqrt(ms + EPS) * ng_ref[:, g * gw:(g + 1) * gw])
    y_ref[i] = jnp.concatenate(outs, axis=1).astype(y_ref.dtype)

    @pl.when(c == nc - 1)
    def _():
        hfin_ref[i] = state[...].reshape(SSM_HEADS, SSM_HEAD_DIM, D_STATE)


def _ssm(z, xbc, sm, hist, h0, conv_w, conv_b, dt_bias, a_log, d_skip, norm_g):
    bsz, t, _ = z.shape
    l = SSD_CHUNK
    tb = min(l, t)
    assert t % tb == 0 and tb % 8 == 0 and t >= CONV_WIDTH - 1
    nc = t // tb
    nb = ROWS_PER_STEP if (tb < l and bsz % ROWS_PER_STEP == 0) else 1
    pad8 = lambda v: jnp.pad(v.reshape(1, SSM_HEADS), ((0, 0), (0, LANES - SSM_HEADS)))
    blk = lambda b, c: (b, c, 0)
    per_b3 = lambda b, c: (b, 0, 0)
    per_b4 = lambda b, c: (b, 0, 0, 0)
    fixed = lambda b, c: (0, 0)
    return pl.pallas_call(
        functools.partial(_ssm_kernel, tb=tb, l=l, nc=nc, nb=nb),
        grid=(bsz // nb, nc),
        in_specs=[pl.BlockSpec((nb, tb, SSM_WIDTH), blk), pl.BlockSpec((nb, tb, CONV_DIM), blk),
                  pl.BlockSpec((nb, tb, LANES), blk),
                  pl.BlockSpec((nb, CONV_WIDTH - 1, CONV_DIM), per_b3),
                  pl.BlockSpec((nb, SSM_HEADS, SSM_HEAD_DIM, D_STATE), per_b4),
                  pl.BlockSpec((CONV_WIDTH, CONV_DIM), fixed), pl.BlockSpec((1, CONV_DIM), fixed),
                  pl.BlockSpec((1, LANES), fixed), pl.BlockSpec((1, LANES), fixed),
                  pl.BlockSpec((1, SSM_WIDTH), fixed), pl.BlockSpec((1, SSM_WIDTH), fixed)],
        out_specs=[pl.BlockSpec((nb, tb, SSM_WIDTH), blk),
                   pl.BlockSpec((nb, SSM_HEADS, SSM_HEAD_DIM, D_STATE), per_b4)],
        out_shape=[jax.ShapeDtypeStruct((bsz, t, SSM_WIDTH), BF16),
                   jax.ShapeDtypeStruct((bsz, SSM_HEADS, SSM_HEAD_DIM, D_STATE), F32)],
        scratch_shapes=[pltpu.VMEM((nb, 8 + l, CONV_DIM), F32), pltpu.VMEM((nb, SSM_WIDTH, D_STATE), F32)],
        compiler_params=_cparams(("parallel", "arbitrary")),
        name="ssm",
    )(z, xbc, sm, hist, h0, conv_w, conv_b.reshape(1, CONV_DIM), pad8(dt_bias), pad8(a_log),
      jnp.repeat(d_skip, SSM_HEAD_DIM).reshape(1, SSM_WIDTH), norm_g.reshape(1, SSM_WIDTH))


def _compress(load_pair, h, n_sub, cw):
    hid = jnp.zeros((n_sub, 2 * KV_ROWS), F32)
    for jp in range(D_CMP // 2):
        hid = hid + _dot(load_pair(h, jp), cw[0][jp])
    return _compress_finish(hid, n_sub, cw)


def _compress_finish(hid, n_sub, cw):
    _, b1_ref, w2_ref, b2_ref = cw
    pre = hid[:, :KV_ROWS] + pltpu.roll(hid[:, KV_ROWS:], n_sub - 1, 0) + b1_ref[...]
    return _dot(_gelu_tanh(pre).astype(BF16), w2_ref[...]) + b2_ref[...]


def _pad_q(q):
    return jnp.concatenate([q, jnp.zeros(q.shape, q.dtype)], axis=1)


def _select(score, qpos, n_slc):
    jj = lax.broadcasted_iota(jnp.int32, score.shape, 1)
    cur = qpos // L_SEL
    visible = jj * L_SEL <= qpos
    forced = (jj == 0) | (jj == cur) | (jj == cur - 1)
    sc = jnp.where(visible, jnp.where(forced, FORCED_SCORE, score), -1.0)
    sc = jnp.where(jj < n_slc, sc, -2.0)
    lane = lax.broadcasted_iota(jnp.int32, (1, score.shape[1]), 1)
    rank = jnp.zeros(score.shape, F32)
    for k in range(n_slc):
        ck = sc[:, k:k + 1]
        tie = jnp.where(lane > k, 1.0, 0.0)
        rank = rank + jnp.where(ck > sc, 1.0, jnp.where(ck == sc, tie, 0.0))
    return jnp.where(rank < min(TOP_N, n_slc), jnp.where(sc >= 0.0, 1.0, 0.0), 0.0)


def _select_blocks_on_rows(score_t, qpos, n_slc, n_visible):
    jj = lax.broadcasted_iota(jnp.int32, score_t.shape, 0)
    cur = qpos // L_SEL
    visible = jj * L_SEL <= qpos
    forced = (jj == 0) | (jj == cur) | (jj == cur - 1)
    sc = jnp.where(visible, jnp.where(forced, FORCED_SCORE, score_t), -1.0)

    def count(rank, ks):
        for k in ks:
            tie = jnp.where(jj > k, 1.0, 0.0)
            rank = rank + jnp.where(sc[k:k + 1, :] > sc, 1.0, jnp.where(sc[k:k + 1, :] == sc, tie, 0.0))
        return rank

    rank = jnp.zeros(score_t.shape, F32)
    group = 8
    for k0 in range(0, n_slc, group):
        ks = range(k0, min(k0 + group, n_slc))
        rank = lax.cond(k0 < n_visible, functools.partial(count, ks=ks), lambda r: r, rank)
    return jnp.where(rank < min(TOP_N, n_slc), jnp.where(sc >= 0.0, 1.0, 0.0), 0.0)


def _softmax_rows(s):
    m = jnp.max(s, -1, keepdims=True)
    e = jnp.where(s > MASKED_BELOW, jnp.exp(s - m), 0.0)
    return e / jnp.maximum(jnp.sum(e, -1, keepdims=True), 1e-30)


def _stack_heads(q_ref, h, i=0):
    return jnp.concatenate([q_ref[i, :, (h * GQA + g) * HEAD_DIM:(h * GQA + g + 1) * HEAD_DIM]
                            for g in range(GQA)], axis=0)


def _unstack_heads(o_ref, h, o, rows, i=0):
    for g in range(GQA):
        hd = h * GQA + g
        o_ref[i, :, hd * HEAD_DIM:(hd + 1) * HEAD_DIM] = o[g * rows:(g + 1) * rows]


def _cmp_prompt_kernel(kvc_ref, q_ref, bias_ref, w1_ref, b1_ref, w2_ref, b2_ref, ov_ref, o_ref, sel_ref,
                       kvcmp, *, n_sub, n_slc):
    qb = pl.program_id(1)

    @pl.when(qb == 0)
    def _():
        rows = lambda h, j: kvc_ref[h, pl.ds(j, n_sub, stride=D_CMP), :].astype(BF16)
        load = lambda h, jp: jnp.concatenate([rows(h, 2 * jp), rows(h, 2 * jp + 1)], axis=1)
        for h in range(KV_HEADS):
            kvcmp[h] = _compress(load, h, n_sub, (w1_ref, b1_ref, w2_ref, b2_ref)).astype(BF16)

    qpos = qb * Q_BLOCK + lax.broadcasted_iota(jnp.int32, (1, Q_BLOCK), 1)
    score_t = []
    for h in range(KV_HEADS):
        kv = kvcmp[h]
        s = _dot_nt(_pad_q(_stack_heads(q_ref, h)), kv) + bias_ref[h]
        p = _softmax_rows(s)
        _unstack_heads(o_ref, h, _dot(p.astype(BF16), kv)[:, HEAD_DIM:], Q_BLOCK)
        pg = p[0:Q_BLOCK]
        for g in range(1, GQA):
            pg = pg + p[g * Q_BLOCK:(g + 1) * Q_BLOCK]
        score_t.append(_dot(pg, ov_ref[...], HIGHEST).T)
    for h in range(KV_HEADS):
        sel_t = _select_blocks_on_rows(score_t[h][0:n_slc], qpos, n_slc, (qb + 1) * (Q_BLOCK // L_SEL))
        sel_ref[0, h] = jnp.concatenate([sel_t, jnp.zeros((LANES - n_slc, Q_BLOCK), F32)], axis=0).T


def _cmp_prompt(kvc, q, bias, cmp_w, overlap):
    bsz, t, _ = q.shape
    n_sub = t // D_CMP
    nqb = t // Q_BLOCK
    n_slc = t // L_SEL
    assert n_sub == LANES and n_slc <= LANES
    fixed = lambda a: pl.BlockSpec(a.shape, lambda b, i, _n=a.ndim: (0,) * _n)
    return pl.pallas_call(
        functools.partial(_cmp_prompt_kernel, n_sub=n_sub, n_slc=n_slc),
        grid=(bsz, nqb),
        in_specs=[pl.BlockSpec((KV_HEADS, t, KV_ROWS), lambda b, i: (0, b, 0)),
                  pl.BlockSpec((1, Q_BLOCK, ATT_WIDTH), lambda b, i: (b, i, 0)),
                  pl.BlockSpec((KV_HEADS, GQA * Q_BLOCK, n_sub), lambda b, i: (0, i, 0))]
        + [fixed(a) for a in cmp_w] + [fixed(overlap)],
        out_specs=[pl.BlockSpec((1, Q_BLOCK, ATT_WIDTH), lambda b, i: (b, i, 0)),
                   pl.BlockSpec((1, KV_HEADS, Q_BLOCK, LANES), lambda b, i: (b, 0, i, 0))],
        out_shape=[jax.ShapeDtypeStruct((bsz, t, ATT_WIDTH), F32),
                   jax.ShapeDtypeStruct((bsz, KV_HEADS, t, LANES), F32)],
        scratch_shapes=[pltpu.VMEM((KV_HEADS, n_sub, KV_ROWS), BF16)],
        compiler_params=_cparams(("parallel", "arbitrary")),
        name="cmp_prompt",
    )(kvc, q, bias, *cmp_w, overlap)


def _nsa_prompt_kernel(q_ref, kvs_ref, kvw_ref, sel_ref, eneg_ref, near_ref, far_ref, winb_ref, os_ref, ow_ref,
                       kaug, vsa, kwp, vwa, s_buf, mrun, acc, *, t):
    qb = pl.program_id(1)
    rows = GQA * Q_BLOCK
    near_w = 2 * Q_BLOCK
    win_w = WINDOW + Q_BLOCK

    @pl.when(qb == 0)
    def _():
        ones_row = jnp.where(lax.broadcasted_iota(jnp.int32, (KV_ROWS - HEAD_DIM, t), 0) == 0, 1.0, 0.0).astype(BF16)
        for h in range(KV_HEADS):
            lo = h * KV_ROWS
            kaug[h, :, 0:Q_BLOCK] = jnp.zeros((HEAD_DIM + SEL_PAD, Q_BLOCK), BF16)
            kaug[h, 0:HEAD_DIM, Q_BLOCK:] = kvs_ref[0, lo:lo + HEAD_DIM, :].astype(BF16)
            kaug[h, HEAD_DIM:, Q_BLOCK:] = eneg_ref[...]
            vsa[h, :, 0:Q_BLOCK] = jnp.zeros((KV_ROWS, Q_BLOCK), BF16)
            vsa[h, 0:HEAD_DIM, Q_BLOCK:] = kvs_ref[0, lo + HEAD_DIM:lo + KV_ROWS, :].astype(BF16)
            vsa[h, HEAD_DIM:, Q_BLOCK:] = ones_row
            kwp[h, :, 0:WINDOW] = jnp.zeros((HEAD_DIM, WINDOW), BF16)
            kwp[h, :, WINDOW:] = kvw_ref[0, lo:lo + HEAD_DIM, :].astype(BF16)
            vwa[h, :, 0:WINDOW] = jnp.zeros((KV_ROWS, WINDOW), BF16)
            vwa[h, 0:HEAD_DIM, WINDOW:] = kvw_ref[0, lo + HEAD_DIM:lo + KV_ROWS, :].astype(BF16)
            vwa[h, HEAD_DIM:, WINDOW:] = ones_row

    def normalise(a):
        return a[:, 0:HEAD_DIM] / a[:, HEAD_DIM:HEAD_DIM + 1]

    def tile_max(s):
        m = s[:, 0:LANES]
        for i in range(1, s.shape[1] // LANES):
            m = jnp.maximum(m, s[:, i * LANES:(i + 1) * LANES])
        return m

    start = pl.multiple_of(qb * Q_BLOCK, Q_BLOCK)
    n_chunk = t // SEL_CHUNK
    tiles_per_chunk = SEL_CHUNK // Q_BLOCK
    heads = range(KV_HEADS)
    q4 = [_stack_heads(q_ref, h) for h in heads]
    qa = [jnp.concatenate([q4[h], jnp.concatenate([(1.0 - sel_ref[0, h][:, 0:SEL_PAD]).astype(BF16)] * GQA,
                                                  axis=0)], axis=1) for h in heads]

    for h in heads:
        far = jnp.concatenate([far_ref[h]] * tiles_per_chunk, axis=1)
        mrun[h] = jnp.full((rows, LANES), NEG, F32)
        for c in range(n_chunk):
            @pl.when(c * tiles_per_chunk < qb - 1)
            def _(c=c, h=h, far=far):
                key = c * SEL_CHUNK + lax.broadcasted_iota(jnp.int32, (1, SEL_CHUNK), 1)
                late = jnp.where(key < (qb - 1) * Q_BLOCK, 0.0, NEG)
                lo = Q_BLOCK + c * SEL_CHUNK
                s = _dot(qa[h], kaug[h, :, lo:lo + SEL_CHUNK]) + far + late
                s_buf[h, :, c * SEL_CHUNK:(c + 1) * SEL_CHUNK] = s
                mrun[h] = jnp.maximum(mrun[h], tile_max(s))

    first = jnp.where(lax.broadcasted_iota(jnp.int32, (1, near_w), 1) < Q_BLOCK,
                      jnp.where(qb >= 1, 0.0, NEG), 0.0)
    pad = jnp.where(lax.broadcasted_iota(jnp.int32, (1, win_w), 1) < WINDOW - qb * Q_BLOCK, NEG, 0.0)
    s_near = [_dot(qa[h], kaug[h, :, pl.ds(start, near_w)]) + near_ref[h] + first for h in heads]
    s_w = [_dot(q4[h], kwp[h, :, pl.ds(start, win_w)]) + winb_ref[h] + pad for h in heads]
    m = []
    for h in heads:
        m.append(jnp.max(jnp.maximum(mrun[h], tile_max(s_near[h])), -1, keepdims=True))
        acc[h] = _dot_nt(jnp.exp(s_near[h] - m[h]).astype(BF16), vsa[h, :, pl.ds(start, near_w)])
        p_w = jnp.exp(s_w[h] - jnp.max(tile_max(s_w[h]), -1, keepdims=True))
        _unstack_heads(ow_ref, h, normalise(_dot_nt(p_w.astype(BF16), vwa[h, :, pl.ds(start, win_w)])), Q_BLOCK)

    for h in heads:
        for c in range(n_chunk):
            @pl.when(c * tiles_per_chunk < qb - 1)
            def _(c=c, h=h):
                lo = Q_BLOCK + c * SEL_CHUNK
                p = jnp.exp(s_buf[h, :, c * SEL_CHUNK:(c + 1) * SEL_CHUNK] - m[h])
                acc[h] = acc[h] + _dot_nt(p.astype(BF16), vsa[h, :, lo:lo + SEL_CHUNK])
    for h in heads:
        _unstack_heads(os_ref, h, normalise(acc[h]), Q_BLOCK)


def _nsa_prompt(q, kvs_t, kvw_t, sel, eneg, near, far, win_bias):
    bsz, _, t = kvs_t.shape
    assert t % SEL_CHUNK == 0
    nqb = t // Q_BLOCK
    qblk = lambda b, i: (b, i, 0)
    per_b = lambda b, i: (b, 0, 0)
    fixed = lambda a: pl.BlockSpec(a.shape, lambda b, i, _n=a.ndim: (0,) * _n)
    rows = GQA * Q_BLOCK
    return pl.pallas_call(
        functools.partial(_nsa_prompt_kernel, t=t),
        grid=(bsz, nqb),
        in_specs=[pl.BlockSpec((1, Q_BLOCK, ATT_WIDTH), qblk), pl.BlockSpec((1, KV_COLS, t), per_b),
                  pl.BlockSpec((1, KV_COLS, t), per_b),
                  pl.BlockSpec((1, KV_HEADS, Q_BLOCK, LANES), lambda b, i: (b, 0, i, 0)),
                  fixed(eneg), fixed(near), fixed(far), fixed(win_bias)],
        out_specs=[pl.BlockSpec((1, Q_BLOCK, ATT_WIDTH), qblk), pl.BlockSpec((1, Q_BLOCK, ATT_WIDTH), qblk)],
        out_shape=[jax.ShapeDtypeStruct((bsz, t, ATT_WIDTH), F32)] * 2,
        scratch_shapes=[pltpu.VMEM((KV_HEADS, HEAD_DIM + SEL_PAD, Q_BLOCK + t), BF16),
                        pltpu.VMEM((KV_HEADS, KV_ROWS, Q_BLOCK + t), BF16),
                        pltpu.VMEM((KV_HEADS, HEAD_DIM, WINDOW + t), BF16),
                        pltpu.VMEM((KV_HEADS, KV_ROWS, WINDOW + t), BF16),
                        pltpu.VMEM((KV_HEADS, rows, t), F32), pltpu.VMEM((KV_HEADS, rows, LANES), F32),
                        pltpu.VMEM((KV_HEADS, rows, KV_ROWS), F32)],
        compiler_params=_cparams(("parallel", "arbitrary")),
        name="nsa_prompt",
    )(q, kvs_t, kvw_t, sel, eneg, near, far, win_bias)


def _page_copy(pages_hbm, page, buf, sem, slot, k):
    return pltpu.make_async_copy(pages_hbm.at[page], buf.at[slot, k], sem.at[slot])


def _stream_pages(pt_ref, pages_hbm, buf, sem, n_pages):
    b = pl.program_id(0)
    slot = lax.rem(b, 2)

    def start(row, into):
        for k in range(n_pages):
            _page_copy(pages_hbm, pt_ref[row, k], buf, sem, into, k).start(priority=k % 2)

    @pl.when(b == 0)
    def _():
        start(0, 0)

    @pl.when(b + 1 < pl.num_programs(0))
    def _():
        start(b + 1, 1 - slot)

    for k in range(n_pages):
        _page_copy(pages_hbm, 0, buf, sem, slot, k).wait()
    return slot


def _cmp_sample_kernel(pt_ref, pages_hbm, q_ref, bias_ref, perm_ref, w1_ref, b1_ref, w2_ref, b2_ref, ov_ref, o_ref,
                       sel_ref, buf, sem, *xj, n_pages, n_slc, past, t):
    slot = _stream_pages(pt_ref, pages_hbm, buf, sem, n_pages)
    sub = PAGE_SIZE // D_CMP
    perm = perm_ref[...]
    per_tile = LANES // (2 * sub)
    n_sub = n_pages * sub
    n_pair = n_pages // 2
    cw = (w1_ref, b1_ref, w2_ref, b2_ref)
    qpos = past + lax.broadcasted_iota(jnp.int32, (t, 1), 0)

    def sort_pair(h, k2):
        lo = h * KV_ROWS
        pair = jnp.concatenate([buf[slot, 2 * k2, lo:lo + KV_ROWS, :].astype(BF16),
                                buf[slot, 2 * k2 + 1, lo:lo + KV_ROWS, :].astype(BF16)], axis=1)
        y = _dot(pair, perm)
        for c in range(2 * PAGE_SIZE // LANES):
            xt = y[:, c * LANES:(c + 1) * LANES].T
            for i in range(per_tile):
                j = c * per_tile + i
                xj[h][j // 2, 2 * sub * k2:2 * sub * (k2 + 1), (j % 2) * KV_ROWS:(j % 2 + 1) * KV_ROWS] = (
                    xt[2 * sub * i:2 * sub * (i + 1)].astype(BF16))

    def hidden_step(hid, h, jp):
        return hid + _dot(xj[h][jp], w1_ref[jp])

    def attend(h, hid):
        kv = _compress_finish(hid, n_sub, cw).astype(BF16)
        qh = _pad_q(_stack_heads(q_ref, h).astype(BF16))
        p = _softmax_rows(_dot_nt(qh, kv) + bias_ref[h])
        _unstack_heads(o_ref, h, _dot(p.astype(BF16), kv)[:, HEAD_DIM:], t)
        pg = p[0:t]
        for g in range(1, GQA):
            pg = pg + p[g * t:(g + 1) * t]
        return _dot(pg, ov_ref[...], HIGHEST)

    n_jp = D_CMP // 2
    zero = jnp.zeros((n_sub, 2 * KV_ROWS), F32)
    for k2 in range(n_pair):
        sort_pair(0, k2)
    hid0, hid1 = zero, zero
    every = n_pair // n_jp
    for k2 in range(n_pair):
        sort_pair(1, k2)
        if k2 % every == every - 1:
            hid0 = hidden_step(hid0, 0, k2 // every)
    for jp in range(n_jp // 2):
        hid1 = hidden_step(hid1, 1, jp)
    score0 = attend(0, hid0)
    for jp in range(n_jp // 2, n_jp):
        hid1 = hidden_step(hid1, 1, jp)
    sel_ref[0, 0] = _select(score0, qpos, n_slc)
    sel_ref[0, 1] = _select(attend(1, hid1), qpos, n_slc)


def _page_scratch(n_pages):
    return [pltpu.VMEM((2, n_pages, KV_COLS, PAGE_SIZE), F32), pltpu.SemaphoreType.DMA((2,))]


def _cmp_sample(pages, page_table, q, bias_cs, cmp_w, overlap, n_slc):
    bsz, n_pages = page_table.shape
    t = q.shape[1]
    width = overlap.shape[1]
    assert n_pages % 2 == 0
    sub = PAGE_SIZE // D_CMP
    col = np.arange(2 * PAGE_SIZE)
    page, row = col // PAGE_SIZE, col % PAGE_SIZE
    dest = (row % D_CMP) * 2 * sub + page * sub + row // D_CMP
    perm = jnp.asarray((dest[:, None] == col[None, :]).astype(np.float32), BF16)
    fixed = lambda a: pl.BlockSpec(a.shape, lambda b, pt, _n=a.ndim: (0,) * _n)
    grid_spec = pltpu.PrefetchScalarGridSpec(
        num_scalar_prefetch=1,
        grid=(bsz,),
        in_specs=[pl.BlockSpec(memory_space=pl.ANY), pl.BlockSpec((1, t, ATT_WIDTH), lambda b, pt: (b, 0, 0)),
                  fixed(bias_cs), fixed(perm)] + [fixed(a) for a in cmp_w] + [fixed(overlap)],
        out_specs=[pl.BlockSpec((1, t, ATT_WIDTH), lambda b, pt: (b, 0, 0)),
                   pl.BlockSpec((1, KV_HEADS, t, width), lambda b, pt: (b, 0, 0, 0))],
        scratch_shapes=_page_scratch(n_pages)
        + [pltpu.VMEM((D_CMP // 2, n_pages * sub, 2 * KV_ROWS), BF16)] * KV_HEADS)
    return pl.pallas_call(
        functools.partial(_cmp_sample_kernel, n_pages=n_pages, n_slc=n_slc, past=n_pages * PAGE_SIZE, t=t),
        grid_spec=grid_spec,
        out_shape=[jax.ShapeDtypeStruct((bsz, t, ATT_WIDTH), F32),
                   jax.ShapeDtypeStruct((bsz, KV_HEADS, t, width), F32)],
        compiler_params=_cparams(("arbitrary",)),
        name="cmp_sample",
    )(page_table, pages, q, bias_cs, perm, *cmp_w, overlap)


def _joint_attend(s_past, vt_past, s_new, v_new):
    m = jnp.maximum(jnp.max(s_past, -1, keepdims=True), jnp.max(s_new, -1, keepdims=True))
    e_past = jnp.exp(s_past - m)
    e_new = jnp.exp(s_new - m)
    den = jnp.sum(e_past, -1, keepdims=True) + jnp.sum(e_new, -1, keepdims=True)
    acc = _dot_nt(e_past.astype(BF16), vt_past) + _dot(e_new.astype(BF16), v_new)
    return acc / den


def _pad_new_rows(new_ref, t, i=0):
    return jnp.concatenate([new_ref[i], jnp.zeros((LANES - t, KV_COLS), F32)], axis=0)


def _sel_sample_kernel(pt_ref, pages_hbm, new_ref, q_ref, sel_ref, far_ref, near_ref, biasn_ref, o_ref, buf, sem,
                       kvb, mask, *, n_pages, t):
    slot = _stream_pages(pt_ref, pages_hbm, buf, sem, n_pages)
    for k in range(n_pages):
        kvb[:, k * PAGE_SIZE:(k + 1) * PAGE_SIZE] = buf[slot, k].astype(BF16)
    kv_new = _pad_new_rows(new_ref, t).astype(BF16)
    rows = GQA * t
    lane = lax.broadcasted_iota(jnp.int32, (rows, LANES), 1)
    per_tile = LANES // L_SEL
    for h in range(KV_HEADS):
        lo = h * KV_ROWS
        sel4 = jnp.concatenate([sel_ref[0, h]] * GQA, axis=0)

        def tile_mask(k):
            cols = [jnp.broadcast_to(sel4[:, per_tile * k + i:per_tile * k + i + 1], (rows, LANES))
                    for i in range(per_tile)]
            m = cols[-1]
            for i in range(per_tile - 2, -1, -1):
                m = jnp.where(lane < (i + 1) * L_SEL, cols[i], m)
            return (m - 1.0) * (-NEG)

        far = jnp.concatenate([jnp.broadcast_to(far_ref[h * GQA + g:h * GQA + g + 1, :], (t, LANES))
                               for g in range(GQA)], axis=0)
        for k in range(n_pages - 1):
            mask[:, k * LANES:(k + 1) * LANES] = tile_mask(k) + far
        mask[:, (n_pages - 1) * LANES:n_pages * LANES] = tile_mask(n_pages - 1) + near_ref[h]
        qh = _stack_heads(q_ref, h).astype(BF16)
        s_past = _dot(qh, kvb[lo:lo + HEAD_DIM, :]) + mask[...]
        s_new = _dot_nt(qh, kv_new[:, lo:lo + HEAD_DIM]) + biasn_ref[h] + tile_mask(n_pages)
        o = _joint_attend(s_past, kvb[lo + HEAD_DIM:lo + KV_ROWS, :], s_new, kv_new[:, lo + HEAD_DIM:lo + KV_ROWS])
        _unstack_heads(o_ref, h, o, t)


def _sel_sample(pages, page_table, kvs_new, q, sel, far, near, bias_new):
    bsz, n_pages = page_table.shape
    t = q.shape[1]
    past = n_pages * PAGE_SIZE
    width = sel.shape[-1]
    assert (n_pages + 1) * (LANES // L_SEL) <= width
    per_b = lambda b, pt: (b, 0, 0)
    fixed = lambda a: pl.BlockSpec(a.shape, lambda b, pt, _n=a.ndim: (0,) * _n)
    grid_spec = pltpu.PrefetchScalarGridSpec(
        num_scalar_prefetch=1,
        grid=(bsz,),
        in_specs=[pl.BlockSpec(memory_space=pl.ANY),
                  pl.BlockSpec((1, t, KV_COLS), per_b), pl.BlockSpec((1, t, ATT_WIDTH), per_b),
                  pl.BlockSpec((1, KV_HEADS, t, width), lambda b, pt: (b, 0, 0, 0)),
                  fixed(far), fixed(near), fixed(bias_new)],
        out_specs=pl.BlockSpec((1, t, ATT_WIDTH), per_b),
        scratch_shapes=_page_scratch(n_pages)
        + [pltpu.VMEM((KV_COLS, past), BF16), pltpu.VMEM((GQA * t, past), F32)])
    return pl.pallas_call(
        functools.partial(_sel_sample_kernel, n_pages=n_pages, t=t),
        grid_spec=grid_spec,
        out_shape=jax.ShapeDtypeStruct((bsz, t, ATT_WIDTH), F32),
        compiler_params=_cparams(("arbitrary",)),
        name="sel_sample",
    )(page_table, pages, kvs_new, q, sel, far, near, bias_new)


def _win_sample_kernel(buf_ref, new_ref, q_ref, bias_ref, biasn_ref, o_ref, win_ref, *, t, nb):
    lane = lax.broadcasted_iota(jnp.int32, (KV_COLS, LANES), 1)

    def update(i):
        buf = buf_ref[i]
        w = buf.shape[1]
        new = _pad_new_rows(new_ref, t, i)
        shifted = pltpu.roll(buf, w - t, 1)
        tail = pltpu.roll(new.T, LANES - t, 1)
        win_ref[i, :, 0:w - LANES] = shifted[:, 0:w - LANES]
        win_ref[i, :, w - LANES:w] = jnp.where(lane >= LANES - t, tail, shifted[:, w - LANES:w])

    def attend(i):
        kvb = buf_ref[i].astype(BF16)
        kv_new = _pad_new_rows(new_ref, t, i).astype(BF16)
        for h in range(KV_HEADS):
            lo = h * KV_ROWS
            qh = _stack_heads(q_ref, h, i).astype(BF16)
            s_past = _dot(qh, kvb[lo:lo + HEAD_DIM, :]) + bias_ref[h]
            s_new = _dot_nt(qh, kv_new[:, lo:lo + HEAD_DIM]) + biasn_ref[h]
            o = _joint_attend(s_past, kvb[lo + HEAD_DIM:lo + KV_ROWS, :], s_new,
                              kv_new[:, lo + HEAD_DIM:lo + KV_ROWS])
            _unstack_heads(o_ref, h, o, t, i)

    update(0)
    for i in range(nb):
        if i + 1 < nb:
            update(i + 1)
        attend(i)


def _win_sample(buf_t, kvw_new, q, bias_buf, bias_new):
    bsz, _, w = buf_t.shape
    t = q.shape[1]
    nb = ROWS_PER_STEP if bsz % ROWS_PER_STEP == 0 else 1
    per_b = lambda b: (b, 0, 0)
    fixed = lambda a: pl.BlockSpec(a.shape, lambda b, _n=a.ndim: (0,) * _n)
    return pl.pallas_call(
        functools.partial(_win_sample_kernel, t=t, nb=nb),
        grid=(bsz // nb,),
        in_specs=[pl.BlockSpec((nb, KV_COLS, w), per_b), pl.BlockSpec((nb, t, KV_COLS), per_b),
                  pl.BlockSpec((nb, t, ATT_WIDTH), per_b), fixed(bias_buf), fixed(bias_new)],
        out_specs=[pl.BlockSpec((nb, t, ATT_WIDTH), per_b), pl.BlockSpec((nb, KV_COLS, w), per_b)],
        out_shape=[jax.ShapeDtypeStruct((bsz, t, ATT_WIDTH), F32),
                   jax.ShapeDtypeStruct((bsz, KV_COLS, w), F32)],
        compiler_params=_cparams(("parallel",)),
        name="win_sample",
    )(buf_t, kvw_new, q, bias_buf, bias_new)


def _combine_kernel(x_ref, y_ref, oc_ref, os_ref, ow_ref, sm_ref, eg_ref, eb_ref, ex_ref, ag_ref, wo_ref, g1_ref,
                    b1_ref, h_ref):
    xn = _layer_norm(x_ref[...], eg_ref[...], eb_ref[...])
    gates = _sigmoid(sm_ref[...])
    g_hi = gates.astype(BF16)
    g_lo = (gates - g_hi.astype(F32)).astype(BF16)
    o = jnp.zeros(oc_ref.shape, F32)
    for br, ref in enumerate((oc_ref, os_ref, ow_ref)):
        o = o + (_dot(g_hi, ex_ref[br]) + _dot(g_lo, ex_ref[br])) * ref[...]
    rms = lax.rsqrt(jnp.mean(o * o, -1, keepdims=True) + EPS)
    att = (o * rms * ag_ref[...]).astype(BF16)
    mix = _dot(jnp.concatenate([y_ref[...], att], axis=1), wo_ref[...])
    h_ref[...] = _layer_norm(ALPHA * xn + mix, g1_ref[...], b1_ref[...])


def _combine(x2d, y, oc, os_, ow, sm, eg, eb, gate_expand, ag, wo, g1, b1, tm):
    n = x2d.shape[0]
    tm = _row_tile(n, tm)
    row = lambda i: (i, 0)
    fixed = lambda a: pl.BlockSpec(a.shape, lambda i, _n=a.ndim: (0,) * _n)
    att = pl.BlockSpec((tm, ATT_WIDTH), row)
    return pl.pallas_call(
        _combine_kernel,
        grid=(n // tm,),
        in_specs=[pl.BlockSpec((tm, D_MODEL), row), pl.BlockSpec((tm, SSM_WIDTH), row), att, att, att,
                  pl.BlockSpec((tm, LANES), row), fixed(eg), fixed(eb), fixed(gate_expand), fixed(ag), fixed(wo),
                  fixed(g1), fixed(b1)],
        out_specs=pl.BlockSpec((tm, D_MODEL), row),
        out_shape=jax.ShapeDtypeStruct((n, D_MODEL), F32),
        compiler_params=_cparams(("parallel",)),
        name="combine",
    )(x2d, y, oc, os_, ow, sm, eg, eb, gate_expand, ag, wo, g1, b1)


def _ffn_kernel(h_ref, wu_ref, wd_ref, g_ref, b_ref, o_ref, *, tf):
    h = h_ref[...]
    hb = h.astype(BF16)
    acc = None
    for k in range(D_FF // tf):
        u = jnp.maximum(_dot(hb, wu_ref[:, k * tf:(k + 1) * tf]), 0.0)
        part = _dot((u * u).astype(BF16), wd_ref[k * tf:(k + 1) * tf, :])
        acc = part if acc is None else acc + part
    o_ref[...] = _layer_norm(ALPHA * h + acc, g_ref[...], b_ref[...])


def _ffn(h2d, wu, wd, g, b, tm, tf):
    n = h2d.shape[0]
    tm = _row_tile(n, tm)
    resident = lambda a: pl.BlockSpec(a.shape, lambda i: (0, 0), pipeline_mode=pl.Buffered(1))
    return pl.pallas_call(
        functools.partial(_ffn_kernel, tf=tf),
        grid=(n // tm,),
        in_specs=[pl.BlockSpec((tm, D_MODEL), lambda i: (i, 0)), resident(wu), resident(wd), resident(g), resident(b)],
        out_specs=pl.BlockSpec((tm, D_MODEL), lambda i: (i, 0)),
        out_shape=jax.ShapeDtypeStruct((n, D_MODEL), F32),
        compiler_params=_cparams(("parallel",)),
        name="ffn",
    )(h2d, wu, wd, g, b)


def _bucket_np(dist):
    d = np.maximum(dist, 0)
    exact = N_BUCKETS // 2
    far = exact + (np.log(np.maximum(d, 1).astype(np.float32) / np.float32(exact))
                   / np.float32(math.log(MAX_DISTANCE / exact)) * (N_BUCKETS - exact)).astype(np.int32)
    return np.where(d < exact, d, np.minimum(far, N_BUCKETS - 1)).astype(np.int32)


def _bias_lookup(tbl, dist, mask=None):
    dist = np.asarray(dist)
    onehot = np.eye(N_BUCKETS, dtype=np.float32)[_bucket_np(dist).reshape(-1)]
    b = jnp.dot(jnp.asarray(onehot), tbl, precision=HIGHEST).T.reshape((ATT_HEADS,) + dist.shape)
    return b if mask is None else jnp.where(jnp.asarray(mask)[None], b, NEG)


def _toeplitz_tile(tbl, offset, mask):
    period = 2 * Q_BLOCK
    k = np.arange(period)
    vals = _bias_lookup(tbl, offset - np.where(k < Q_BLOCK, k, k - period))
    tiled = jnp.tile(vals, (1, Q_BLOCK))[:, :Q_BLOCK * (period - 1)]
    t = tiled.reshape(ATT_HEADS, Q_BLOCK, period - 1)[:, :, :Q_BLOCK]
    return jnp.where(jnp.asarray(mask)[None], t, NEG)


def _cmp_prompt_bias(tbl, t):
    n_sub = t // D_CMP
    back = (MAX_DISTANCE + L_CMP - 1) // D_CMP
    band = (np.arange(Q_BLOCK)[:, None] - D_CMP * np.arange(NEAR_W)[None, :] + D_CMP * back - (L_CMP - 1))
    near = jnp.pad(_bias_lookup(tbl, band), ((0, 0), (0, 0), (0, n_sub - NEAR_W)))
    far = tbl[N_BUCKETS - 1][:, None, None]
    blk = np.arange(n_sub)[None, :]
    tiles = []
    for qb in range(t // Q_BLOCK):
        near_lo = qb * (Q_BLOCK // D_CMP) - back
        qpos = qb * Q_BLOCK + np.arange(Q_BLOCK)[:, None]
        visible = (blk * D_CMP + (L_CMP - 1) <= qpos) & (blk < n_sub - 1)
        tile = jnp.where(jnp.asarray(blk >= near_lo)[None], jnp.roll(near, near_lo % n_sub, axis=2), far)
        tiles.append(jnp.where(jnp.asarray(visible)[None], tile, NEG))
    tiles = jnp.stack(tiles).reshape(t // Q_BLOCK, KV_HEADS, GQA * Q_BLOCK, n_sub)
    return jnp.moveaxis(tiles, 0, 1).reshape(KV_HEADS, -1, n_sub)


def _stack_gt(tab, t):
    return tab.reshape(KV_HEADS, GQA * t, tab.shape[-1])


def _far_rows(tbl):
    return jnp.broadcast_to(tbl[N_BUCKETS - 1][:, None], (ATT_HEADS, LANES))


def _overlap(n_cmp_pad, n_cmp, width, n_slc):
    i = np.arange(n_cmp_pad)[:, None]
    j = np.arange(width)[None, :]
    ov = (i * D_CMP < (j + 1) * L_SEL) & (i * D_CMP + L_CMP > j * L_SEL) & (i < n_cmp) & (j < n_slc)
    return jnp.asarray(ov.astype(np.float32))


def _prep_cmp_weights(w1, b1, w2, b2):
    eye = jnp.eye(2, dtype=F32)
    w1r = (w1[:, :, :, :, None, :] * eye[None, None, :, None, :, None]).transpose(1, 2, 3, 0, 4, 5)
    w1r = w1r.reshape(D_CMP // 2, 2 * KV_ROWS, 2 * KV_ROWS).astype(BF16)
    w2r = (w2[:, :, None, :] * eye[:, None, :, None]).reshape(2 * CMP_HID, KV_ROWS).astype(BF16)
    return (w1r, b1.reshape(1, 2 * CMP_HID), w2r, b2.reshape(1, KV_ROWS))


def _prep_w_in(w_in):
    sizes = (SSM_WIDTH, CONV_DIM, SSM_HEADS, ATT_WIDTH, KV_COLS, KV_COLS, KV_COLS)
    z, xbc, dt, q, kvc, kvs, kvw, gates = jnp.split(w_in, np.cumsum(sizes).tolist(), axis=1)
    small = jnp.concatenate([dt, gates], axis=1)
    small = jnp.pad(small, ((0, 0), (0, LANES - small.shape[1])))
    return jnp.concatenate([z, xbc, q * ATT_SCALE, kvc, kvs, kvw, small], axis=1).astype(BF16)


def _gate_expand():
    ex = np.zeros((N_BRANCH, LANES, ATT_WIDTH), np.float32)
    for br in range(N_BRANCH):
        for hd in range(ATT_HEADS):
            ex[br, SSM_HEADS + br * ATT_HEADS + hd, hd * HEAD_DIM:(hd + 1) * HEAD_DIM] = 1.0
    return jnp.asarray(ex, BF16)


def _feature_major(a):
    lead = a.shape[:-4]
    rows = a.shape[-4]
    return jnp.moveaxis(a.reshape(lead + (rows, KV_COLS)), -2, -1)


def _row_major6(a_t):
    bsz, _, rows = a_t.shape
    return jnp.moveaxis(a_t, 1, 2).reshape(1, bsz, rows, KV_HEADS, 2, HEAD_DIM)


def kernel(x_prompt, x_sample, cache_cmp_kv, cache_slc_kv, cache_win_kv, state_conv, state_ssm, page_table,
           rel_bias_table, emb_ln_g, emb_ln_b, w_in, conv_w, conv_b, dt_bias, a_log, d_skip, ssm_norm_g,
           cmp_w1, cmp_b1, cmp_w2, cmp_b2, att_norm_g, w_out, ln1_g, ln1_b, w_up, w_down, ln2_g, ln2_b):
    assert w_in.shape[0] == DEPTH
    bp, tp, _ = x_prompt.shape
    bs, ts, _ = x_sample.shape
    n_pages = page_table.shape[1]
    past = n_pages * PAGE_SIZE
    w_buf = cache_win_kv.shape[2]
    assert ts < D_CMP and ts % 8 == 0 and w_buf == WINDOW and past >= WINDOW and tp >= WINDOW
    tbl = rel_bias_table
    vec = lambda v: v.reshape(1, -1)

    w_proj = _prep_w_in(w_in[0])
    cmp_w = _prep_cmp_weights(cmp_w1[0], cmp_b1[0], cmp_w2[0], cmp_b2[0])
    wo = w_out[0].astype(BF16)
    wu = w_up[0].astype(BF16)
    wd = w_down[0].astype(BF16)
    eg, eb = vec(emb_ln_g), vec(emb_ln_b)
    gate_expand = _gate_expand()
    far = _far_rows(tbl)

    def trunk_tail(x2d, y, oc, os_, ow, sm):
        h = _combine(x2d, y, oc, os_, ow, sm, eg, eb, gate_expand, vec(att_norm_g[0]), wo, vec(ln1_g[0]),
                     vec(ln1_b[0]), COMBINE_ROWS)
        return _ffn(h, wu, wd, vec(ln2_g[0]), vec(ln2_b[0]), FFN_ROWS, FFN_SLAB)

    ssm_w = (conv_w[0], conv_b[0], dt_bias[0], a_log[0], d_skip[0], ssm_norm_g[0])

    xp2 = x_prompt.reshape(bp * tp, D_MODEL)
    z, xbc, q, kvc, kvc_t, kvs_t, kvw_t, sm = _proj(xp2, eg, eb, w_proj, BF16, PROJ_ROWS, seq=(bp, tp))
    r3 = lambda a: a.reshape(bp, tp, a.shape[-1])
    xbc3 = r3(xbc)
    y_ssm, h_new = _ssm(r3(z), xbc3, r3(sm), jnp.zeros((bp, CONV_WIDTH - 1, CONV_DIM), F32),
                        jnp.zeros((bp, SSM_HEADS, SSM_HEAD_DIM, D_STATE), F32), *ssm_w)
    n_sub = tp // D_CMP
    n_slc = tp // L_SEL
    oc, sel = _cmp_prompt(kvc, r3(q), _cmp_prompt_bias(tbl, tp), cmp_w, _overlap(n_sub, n_sub - 1, LANES, n_slc))
    ii = np.arange(Q_BLOCK)[:, None] - np.arange(Q_BLOCK)[None, :]
    tiles_gq = lambda rs, ok: jnp.concatenate(
        [_toeplitz_tile(tbl, Q_BLOCK * r, ok(ii + Q_BLOCK * r)) for r in rs], axis=2).reshape(
            KV_HEADS, GQA * Q_BLOCK, len(rs) * Q_BLOCK)
    near = tiles_gq((1, 0), lambda d: d >= 0)
    win_bias = tiles_gq(range(WINDOW // Q_BLOCK, -1, -1), lambda d: (d >= 0) & (d < WINDOW))
    far_gq = jnp.repeat(far, Q_BLOCK, axis=0).reshape(KV_HEADS, GQA * Q_BLOCK, LANES)
    assert n_slc <= SEL_PAD
    eneg = jnp.asarray(np.where(np.arange(SEL_PAD)[:, None] == (np.arange(tp) // L_SEL)[None, :], NEG, 0.0), BF16)
    os_, ow = _nsa_prompt(r3(q), kvs_t, kvw_t, sel, eneg, near, far_gq, win_bias)
    f2 = lambda a: a.reshape(bp * tp, a.shape[-1])
    y_prompt = trunk_tail(xp2, f2(y_ssm), f2(oc), f2(os_), f2(ow), sm).reshape(bp, tp, D_MODEL)
    w = min(WINDOW, tp)
    prompt_state = (_row_major6(kvc_t), _row_major6(kvs_t), _row_major6(kvw_t[:, :, tp - w:]),
                    xbc3[:, tp - (CONV_WIDTH - 1):][None], h_new[None])

    xs2 = x_sample.reshape(bs * ts, D_MODEL)
    z, xbc, q, kvc, kvs, kvw, sm = _proj(xs2, eg, eb, w_proj, F32, PROJ_ROWS)
    r3 = lambda a: a.reshape(bs, ts, a.shape[-1])
    xbc3 = r3(xbc)
    y_ssm, h_new = _ssm(r3(z), xbc3, r3(sm), state_conv[0], state_ssm[0], *ssm_w)
    n_sub = past // D_CMP
    n_cmp = n_sub - 1
    n_slc = -(-(past + ts) // L_SEL)
    width = -(-n_slc // LANES) * LANES
    qpos = past + np.arange(ts)
    dist_c = qpos[:, None] - (np.arange(n_sub) * D_CMP + L_CMP - 1)[None, :]
    bias_cs = _stack_gt(_bias_lookup(tbl, dist_c, (dist_c >= 0) & (np.arange(n_sub) < n_cmp)[None, :]), ts)
    oc, sel = _cmp_sample(_feature_major(cache_cmp_kv[0]), page_table, r3(q), bias_cs, cmp_w,
                          _overlap(n_sub, n_cmp, width, n_slc), n_slc)
    dist_last = qpos[:, None] - (past - PAGE_SIZE + np.arange(PAGE_SIZE))[None, :]
    dist_n = np.arange(ts)[:, None] - np.arange(LANES)[None, :]
    bias_new = _stack_gt(_bias_lookup(tbl, dist_n, (dist_n >= 0) & (np.arange(LANES) < ts)[None, :]), ts)
    os_ = _sel_sample(_feature_major(cache_slc_kv[0]), page_table, r3(kvs), r3(q), sel, far,
                      _stack_gt(_bias_lookup(tbl, dist_last), ts), bias_new)
    dist_w = qpos[:, None] - (past - w_buf + np.arange(w_buf))[None, :]
    bias_wb = _stack_gt(_bias_lookup(tbl, dist_w, (dist_w >= 0) & (dist_w < WINDOW)), ts)
    ow, win_new_t = _win_sample(_feature_major(cache_win_kv[0]), r3(kvw), r3(q), bias_wb, bias_new)
    f2 = lambda a: a.reshape(bs * ts, a.shape[-1])
    y_sample = trunk_tail(xs2, f2(y_ssm), f2(oc), f2(os_), f2(ow), sm).reshape(bs, ts, D_MODEL)
    kv6 = lambda a: a.reshape(1, bs, ts, KV_HEADS, 2, HEAD_DIM)
    sample_state = (kv6(kvc), kv6(kvs), _row_major6(win_new_t),
                    xbc3[:, ts - (CONV_WIDTH - 1):][None], h_new[None])

    return (y_prompt, y_sample) + prompt_state + sample_state
```
